```python
import jax
import jax.numpy as jnp
from jax import lax
import numpy as np

D_MODEL = 1024
BATCH = 8
SEQ = 4096
DEPTH = 2

CTX_LEN = 256
GRID_W = 64
HEAD_DIM = 64
ROPE_THETA = 10000.0
QBLK = 128
EPS = 1e-6

A_HEADS = (D_MODEL // 2) // HEAD_DIM
A_KV_HEADS = A_HEADS // 4
A_WIDTH = A_HEADS * HEAD_DIM
B_WIDTH = D_MODEL - A_WIDTH
CONV_WIDTH = 31
AB_IN = A_WIDTH + 2 * A_KV_HEADS * HEAD_DIM + 2 * B_WIDTH

C_HEADS = (D_MODEL // 2) // HEAD_DIM
C_KV_HEADS = C_HEADS // 4
C_WIDTH = C_HEADS * HEAD_DIM
WINDOW = 128
D_WIDTH = D_MODEL - C_WIDTH
POOL_SIZES = (2, 4, 8, 16)
POOL_GROUP_WIDTH = D_WIDTH // len(POOL_SIZES)
CD_IN = C_WIDTH + 2 * C_KV_HEADS * HEAD_DIM + D_WIDTH

N_EXPERT_GROUPS = 4
EXPERTS_PER_GROUP = 8
N_EXPERTS = N_EXPERT_GROUPS * EXPERTS_PER_GROUP
TOP_K_INNER = 2
D_EXPERT = D_MODEL // 2
MOE_BLOCK = 128

kernel_name = 'hybrid_dit_gqa_conformer_swa_pool_hmoe'


def _rmsnorm(x, g):
    xf = x.astype(jnp.float32)
    y = xf * lax.rsqrt(jnp.mean(xf * xf, axis=-1, keepdims=True) + EPS)
    return (y * g.astype(jnp.float32)).astype(x.dtype)


def _modulate(h, shift, scale):
    return h * (1 + scale) + shift


def _rope_tables(n_lat):
    rows = n_lat // GRID_W
    row = jnp.repeat(jnp.arange(rows, dtype=jnp.float32), GRID_W)
    col = jnp.tile(jnp.arange(GRID_W, dtype=jnp.float32), rows)
    n_freq = HEAD_DIM // 4
    inv = ROPE_THETA ** (-jnp.arange(n_freq, dtype=jnp.float32) / n_freq)
    ang = jnp.concatenate([row[:, None] * inv, col[:, None] * inv], axis=-1)
    return jnp.cos(ang), jnp.sin(ang)


def _rope(x, cos, sin):
    half = HEAD_DIM // 2
    x1, x2 = x[..., :half], x[..., half:]
    c = cos.astype(x.dtype)
    s = sin.astype(x.dtype)
    return jnp.concatenate([x1 * c - x2 * s, x2 * c + x1 * s], axis=-1)


def _heads(t, n):
    b, s, _ = t.shape
    return t.reshape(b, s, n, HEAD_DIM).transpose(0, 2, 1, 3)


def _group(q, n_kv):
    b, h, s, d = q.shape
    return q.reshape(b, n_kv, h // n_kv, s, d)


def _merge(o):
    b, k, g, s, d = o.shape
    return o.transpose(0, 3, 1, 2, 4).reshape(b, s, k * g * d)


def _attend(q, k, v, mask, sink):
    s = jnp.einsum('bkgqd,bksd->bkgqs', q, k).astype(jnp.float32) * (HEAD_DIM ** -0.5)
    if mask is not None:
        s = jnp.where(mask, s, -jnp.inf)
    if sink is None:
        p = jax.nn.softmax(s, axis=-1)
    else:
        sk = jnp.broadcast_to(sink.astype(jnp.float32)[None, :, :, None, None], s.shape[:-1] + (1,))
        p = jax.nn.softmax(jnp.concatenate([s, sk], axis=-1), axis=-1)[..., :-1]
    return jnp.einsum('bkgqs,bksd->bkgqd', p.astype(v.dtype), v)


def _dense_attn_blocks(q, k, v):
    b, hk, g, s, d = q.shape
    nb = s // QBLK
    qb = jnp.moveaxis(q.reshape(b, hk, g, nb, QBLK, d), 3, 0)
    ob = lax.map(lambda qi: _attend(qi, k, v, None, None), qb)
    return jnp.moveaxis(ob, 0, 3).reshape(b, hk, g, s, d)


def _window_attn_blocks(q, k_lat, v_lat, k_ctx, v_ctx, sink):
    b, hk, g, s, d = q.shape
    nb = s // QBLK
    span = QBLK + 2 * WINDOW
    n_ctx = k_ctx.shape[2]
    pad = ((0, 0), (0, 0), (WINDOW, WINDOW), (0, 0))
    kp = jnp.pad(k_lat, pad)
    vp = jnp.pad(v_lat, pad)
    qb = jnp.moveaxis(q.reshape(b, hk, g, nb, QBLK, d), 3, 0)
    r = jnp.arange(QBLK)
    j = jnp.arange(span)
    ctx_mask = jnp.ones((QBLK, n_ctx), dtype=bool)

    def block(args):
        qi, i = args
        start = i * QBLK
        kw = lax.dynamic_slice_in_dim(kp, start, span, axis=2)
        vw = lax.dynamic_slice_in_dim(vp, start, span, axis=2)
        qpos = start + r
        kpos = start - WINDOW + j
        band = (jnp.abs(qpos[:, None] - kpos[None, :]) <= WINDOW) & ((kpos >= 0) & (kpos < s))[None, :]
        mask = jnp.concatenate([ctx_mask, band], axis=1)
        return _attend(qi, jnp.concatenate([k_ctx, kw], axis=2), jnp.concatenate([v_ctx, vw], axis=2), mask, sink)

    ob = lax.map(block, (qb, jnp.arange(nb)))
    return jnp.moveaxis(ob, 0, 3).reshape(b, hk, g, s, d)


def _conv_module(u, conv_w, conv_b, ln_g, ln_b):
    a, gate = jnp.split(u, 2, axis=-1)
    glu = a * jax.nn.sigmoid(gate)
    y = lax.conv_general_dilated(glu, conv_w[:, None, :], window_strides=(1,),
                                 padding=[(CONV_WIDTH // 2, CONV_WIDTH // 2)],
                                 dimension_numbers=('NWC', 'WIO', 'NWC'),
                                 feature_group_count=B_WIDTH) + conv_b
    yf = y.astype(jnp.float32)
    mu = jnp.mean(yf, axis=-1, keepdims=True)
    var = jnp.mean((yf - mu) ** 2, axis=-1, keepdims=True)
    yn = ((yf - mu) * lax.rsqrt(var + EPS) * ln_g.astype(jnp.float32) + ln_b.astype(jnp.float32)).astype(u.dtype)
    return jax.nn.silu(yn)


def _pool_mixer(u, pool_w, pool_scale):
    b, s, _ = u.shape
    uf = u.astype(jnp.float32)
    cs = jnp.pad(jnp.cumsum(uf, axis=1), ((0, 0), (1, 0), (0, 0)))
    t = jnp.arange(s)
    outs = []
    for gi, w in enumerate(POOL_SIZES):
        lo_c, hi_c = gi * POOL_GROUP_WIDTH, (gi + 1) * POOL_GROUP_WIDTH
        lo = jnp.clip(t - w // 2, 0, s)
        hi = jnp.clip(t - w // 2 + w, 0, s)
        csg = cs[..., lo_c:hi_c]
        mean = (jnp.take(csg, hi, axis=1) - jnp.take(csg, lo, axis=1)) / (hi - lo).astype(jnp.float32)[None, :, None]
        outs.append(mean - uf[..., lo_c:hi_c])
    p = jnp.stack(outs, axis=2).astype(u.dtype)
    y = jnp.einsum('bsgc,gcd->bsgd', p, pool_w).reshape(b, s, D_WIDTH)
    return y * pool_scale


def _mixer_ab(hx, hc, w_in, w_out, q_g, k_g, conv_w, conv_b, ln_g, ln_b, cos, sin, ctx_out):
    kvw = A_KV_HEADS * HEAD_DIM
    o_k, o_v, o_u = A_WIDTH, A_WIDTH + kvw, A_WIDTH + 2 * kvw
    px = hx @ w_in
    qx = _rope(_rmsnorm(_heads(px[..., :o_k], A_HEADS), q_g), cos, sin)
    kx = _rope(_rmsnorm(_heads(px[..., o_k:o_v], A_KV_HEADS), k_g), cos, sin)
    vx = _heads(px[..., o_v:o_u], A_KV_HEADS)
    if ctx_out:
        pc = hc @ w_in
        kvc = pc[..., o_k:o_u]
    else:
        kvc = hc @ w_in[:, o_k:o_u]
    kc = _rmsnorm(_heads(kvc[..., :kvw], A_KV_HEADS), k_g)
    vc = _heads(kvc[..., kvw:], A_KV_HEADS)
    ox = _dense_attn_blocks(_group(qx, A_KV_HEADS), jnp.concatenate([kc, kx], axis=2), jnp.concatenate([vc, vx], axis=2))
    bx = _conv_module(px[..., o_u:], conv_w, conv_b, ln_g, ln_b)
    yx = jnp.concatenate([_merge(ox), bx], axis=-1) @ w_out
    if not ctx_out:
        return yx, None
    qc = _rmsnorm(_heads(pc[..., :o_k], A_HEADS), q_g)
    oc = _attend(_group(qc, A_KV_HEADS), kc, vc, None, None)
    bc = _conv_module(pc[..., o_u:], conv_w, conv_b, ln_g, ln_b)
    yc = jnp.concatenate([_merge(oc), bc], axis=-1) @ w_out
    return yx, yc


def _mixer_cd(hx, hc, w_in, w_out, q_g, k_g, sink, pool_w, pool_scale, cos, sin, ctx_out):
    kvw = C_KV_HEADS * HEAD_DIM
    o_k, o_v, o_u = C_WIDTH, C_WIDTH + kvw, C_WIDTH + 2 * kvw
    sink_g = sink.reshape(C_KV_HEADS, C_HEADS // C_KV_HEADS)
    px = hx @ w_in
    qx = _rope(_rmsnorm(_heads(px[..., :o_k], C_HEADS), q_g), cos, sin)
    kx = _rope(_rmsnorm(_heads(px[..., o_k:o_v], C_KV_HEADS), k_g), cos, sin)
    vx = _heads(px[..., o_v:o_u], C_KV_HEADS)
    if ctx_out:
        pc = hc @ w_in
        kvc = pc[..., o_k:o_u]
    else:
        kvc = hc @ w_in[:, o_k:o_u]
    kc = _rmsnorm(_heads(kvc[..., :kvw], C_KV_HEADS), k_g)
    vc = _heads(kvc[..., kvw:], C_KV_HEADS)
    ox = _window_attn_blocks(_group(qx, C_KV_HEADS), kx, vx, kc, vc, sink_g)
    dx = _pool_mixer(px[..., o_u:], pool_w, pool_scale)
    yx = jnp.concatenate([_merge(ox), dx], axis=-1) @ w_out
    if not ctx_out:
        return yx, None
    qc = _rmsnorm(_heads(pc[..., :o_k], C_HEADS), q_g)
    oc = _attend(_group(qc, C_KV_HEADS), kc, vc, None, sink_g)
    dc = _pool_mixer(pc[..., o_u:], pool_w, pool_scale)
    yc = jnp.concatenate([_merge(oc), dc], axis=-1) @ w_out
    return yx, yc


def _moe(h, grp_w, grp_b, exp_w, exp_b, w_gate, w_up, w_down):
    n_tok, d = h.shape
    tok_idx = jnp.arange(n_tok)
    g_logits = (h @ grp_w).astype(jnp.float32) + grp_b.astype(jnp.float32)
    g_sel = jnp.argmax(g_logits, axis=-1)
    g_w = jax.nn.softmax(g_logits, axis=-1)[tok_idx, g_sel]
    e_logits = ((h @ exp_w).astype(jnp.float32) + exp_b.astype(jnp.float32)).reshape(n_tok, N_EXPERT_GROUPS, EXPERTS_PER_GROUP)[tok_idx, g_sel]
    e_p, e_i = lax.top_k(jax.nn.softmax(e_logits, axis=-1), TOP_K_INNER)
    wts = (g_w[:, None] * e_p / jnp.sum(e_p, axis=-1, keepdims=True)).reshape(-1)
    eid = (g_sel[:, None] * EXPERTS_PER_GROUP + e_i).reshape(-1).astype(jnp.int32)
    n_asg = n_tok * TOP_K_INNER
    flat_tok = jnp.repeat(tok_idx, TOP_K_INNER).astype(jnp.int32)
    order = jnp.argsort(eid)
    se, stok, sw = eid[order], flat_tok[order], wts[order]
    counts = jnp.bincount(eid, length=N_EXPERTS).astype(jnp.int32)
    padded = (counts + MOE_BLOCK - 1) // MOE_BLOCK * MOE_BLOCK
    start = jnp.cumsum(counts) - counts
    pend = jnp.cumsum(padded)
    dest = (pend - padded)[se] + jnp.arange(n_asg, dtype=jnp.int32) - start[se]
    n_blk = -(-(n_asg + N_EXPERTS * (MOE_BLOCK - 1)) // MOE_BLOCK)
    cap = n_blk * MOE_BLOCK
    buf_tok = jnp.full((cap,), n_tok, jnp.int32).at[dest].set(stok)
    buf_w = jnp.zeros((cap,), h.dtype).at[dest].set(sw.astype(h.dtype))
    blk_e = jnp.minimum(jnp.searchsorted(pend, jnp.arange(n_blk, dtype=jnp.int32) * MOE_BLOCK, side='right'), N_EXPERTS - 1)
    h_pad = jnp.concatenate([h, jnp.zeros((1, d), h.dtype)], axis=0)

    def run(args):
        tok, wt, e = args
        xb = h_pad[tok]
        hid = jax.nn.silu(xb @ w_gate[e]) * (xb @ w_up[e])
        return (hid @ w_down[e]) * wt[:, None]

    ys = lax.map(run, (buf_tok.reshape(n_blk, MOE_BLOCK), buf_w.reshape(n_blk, MOE_BLOCK), blk_e))
    out = jnp.zeros((n_tok + 1, d), h.dtype).at[buf_tok].add(ys.reshape(cap, d))
    return out[:n_tok]


def setup_inputs(seed: int = 0) -> dict:
    key = jax.random.key(seed)
    keys = jax.random.split(key, 48)
    counter = iter(range(48))

    def nrm(shape, scale):
        return jax.random.normal(keys[next(counter)], shape, jnp.float32) * scale

    n_even = (DEPTH + 1) // 2
    n_odd = DEPTH // 2
    dm = D_MODEL
    return {
        'x': nrm((BATCH, SEQ, dm), 1.0),
        'c': nrm((BATCH, dm), 1.0),
        'ctx': nrm((BATCH, CTX_LEN, dm), 1.0),
        'c_ctx': nrm((dm,), 1.0),
        'mod_w': nrm((DEPTH, dm, 6 * dm), 0.5 * dm ** -0.5),
        'mod_b': nrm((DEPTH, 6 * dm), 0.02),
        'ln1_g': 1.0 + nrm((DEPTH, dm), 0.02),
        'ln2_g': 1.0 + nrm((DEPTH, dm), 0.02),
        'w_in_ab': nrm((n_even, dm, AB_IN), dm ** -0.5),
        'w_out_ab': nrm((n_even, A_WIDTH + B_WIDTH, dm), dm ** -0.5),
        'q_norm_a': 1.0 + nrm((n_even, HEAD_DIM), 0.02),
        'k_norm_a': 1.0 + nrm((n_even, HEAD_DIM), 0.02),
        'conv_w': nrm((n_even, CONV_WIDTH, B_WIDTH), CONV_WIDTH ** -0.5),
        'conv_b': nrm((n_even, B_WIDTH), 0.02),
        'conv_ln_g': 1.0 + nrm((n_even, B_WIDTH), 0.02),
        'conv_ln_b': nrm((n_even, B_WIDTH), 0.02),
        'w_in_cd': nrm((n_odd, dm, CD_IN), dm ** -0.5),
        'w_out_cd': nrm((n_odd, C_WIDTH + D_WIDTH, dm), dm ** -0.5),
        'q_norm_c': 1.0 + nrm((n_odd, HEAD_DIM), 0.02),
        'k_norm_c': 1.0 + nrm((n_odd, HEAD_DIM), 0.02),
        'sink_c': nrm((n_odd, C_HEADS), 0.5),
        'pool_w': nrm((n_odd, len(POOL_SIZES), POOL_GROUP_WIDTH, POOL_GROUP_WIDTH), POOL_GROUP_WIDTH ** -0.5),
        'pool_scale': 1.0 + nrm((n_odd, D_WIDTH), 0.02),
        'rt_grp_w': nrm((DEPTH, dm, N_EXPERT_GROUPS), dm ** -0.5),
        'rt_grp_b': nrm((DEPTH, N_EXPERT_GROUPS), 0.01),
        'rt_exp_w': nrm((DEPTH, dm, N_EXPERTS), dm ** -0.5),
        'rt_exp_b': nrm((DEPTH, N_EXPERTS), 0.01),
        'ex_gate': nrm((DEPTH, N_EXPERTS, dm, D_EXPERT), dm ** -0.5),
        'ex_up': nrm((DEPTH, N_EXPERTS, dm, D_EXPERT), dm ** -0.5),
        'ex_down': nrm((DEPTH, N_EXPERTS, D_EXPERT, dm), D_EXPERT ** -0.5),
    }


def reference(x, c, ctx, c_ctx, mod_w, mod_b, ln1_g, ln2_g,
              w_in_ab, w_out_ab, q_norm_a, k_norm_a, conv_w, conv_b, conv_ln_g, conv_ln_b,
              w_in_cd, w_out_cd, q_norm_c, k_norm_c, sink_c, pool_w, pool_scale,
              rt_grp_w, rt_grp_b, rt_exp_w, rt_exp_b, ex_gate, ex_up, ex_down):
    b, s, dm = x.shape
    cos, sin = _rope_tables(s)
    for i in range(DEPTH):
        last = i == DEPTH - 1
        li = i // 2
        mod = jax.nn.silu(c) @ mod_w[i] + mod_b[i]
        modc = jax.nn.silu(c_ctx) @ mod_w[i] + mod_b[i]
        sh1, sc1, g1, sh2, sc2, g2 = jnp.split(mod[:, None, :], 6, axis=-1)
        sh1c, sc1c, g1c, sh2c, sc2c, g2c = jnp.split(modc[None, None, :], 6, axis=-1)
        hx = _modulate(_rmsnorm(x, ln1_g[i]), sh1, sc1)
        hc = _modulate(_rmsnorm(ctx, ln1_g[i]), sh1c, sc1c)
        if i % 2 == 0:
            yx, yc = _mixer_ab(hx, hc, w_in_ab[li], w_out_ab[li], q_norm_a[li], k_norm_a[li],
                               conv_w[li], conv_b[li], conv_ln_g[li], conv_ln_b[li], cos, sin, not last)
        else:
            yx, yc = _mixer_cd(hx, hc, w_in_cd[li], w_out_cd[li], q_norm_c[li], k_norm_c[li],
                               sink_c[li], pool_w[li], pool_scale[li], cos, sin, not last)
        x = x + g1 * yx
        hx = _modulate(_rmsnorm(x, ln2_g[i]), sh2, sc2)
        if last:
            ox = _moe(hx.reshape(b * s, dm), rt_grp_w[i], rt_grp_b[i], rt_exp_w[i], rt_exp_b[i],
                      ex_gate[i], ex_up[i], ex_down[i]).reshape(b, s, dm)
            x = x + g2 * ox
        else:
            ctx = ctx + g1c * yc
            hc = _modulate(_rmsnorm(ctx, ln2_g[i]), sh2c, sc2c)
            n_ctx = ctx.shape[1]
            tokens = jnp.concatenate([hc, hx], axis=1).reshape(-1, dm)
            out = _moe(tokens, rt_grp_w[i], rt_grp_b[i], rt_exp_w[i], rt_exp_b[i],
                       ex_gate[i], ex_up[i], ex_down[i]).reshape(b, n_ctx + s, dm)
            ctx = ctx + g2c * out[:, :n_ctx]
            x = x + g2 * out[:, n_ctx:]
    return x
```

```python
import functools

import jax
import jax.numpy as jnp
from jax import lax
from jax.experimental import pallas as pl
from jax.experimental.pallas import tpu as pltpu

F32 = jnp.float32
BF16 = jnp.bfloat16

D = 1024
HD = 64
NH = 8
NKV = 2
GQ = NH // NKV
QW = NH * HD
KW = NKV * HD
UW = 512
EPS = 1e-6
ROPE_THETA = 10000.0
GRID_W = 64
CONV_K = 31
WINDOW = 128
POOL_SIZES = (2, 4, 8, 16)
N_GROUPS = 4
PER_GROUP = 8
N_EXP = N_GROUPS * PER_GROUP
D_EXP = D // 2

TM = 256
TQ = 128
TK = 256
SUB = 64
HALO = 16
BLK = 256
VMEM_LIMIT = 56 * 1024 * 1024


def _sigmoid(x):
    return 1.0 / (1.0 + jnp.exp(-x))


def _dot(a, b):
    return jnp.dot(a, b, preferred_element_type=F32)


def _dot_nt(a, b):
    return lax.dot_general(a, b, (((1,), (1,)), ((), ())), preferred_element_type=F32)


def _split(a):
    hi = a.astype(BF16)
    lo = (a - hi.astype(F32)).astype(BF16)
    return hi, lo


def _dot3(a, w):
    a_hi, a_lo = _split(a)
    w_hi, w_lo = _split(w)
    return _dot(a_hi, w_hi) + _dot(a_lo, w_hi) + _dot(a_hi, w_lo)


def _rmsnorm(x, g):
    return x * lax.rsqrt(jnp.mean(x * x, axis=-1, keepdims=True) + EPS) * g


def _params(sem, vmem=VMEM_LIMIT):
    return pltpu.CompilerParams(dimension_semantics=sem, vmem_limit_bytes=vmem)


def _mod_kernel(c_ref, w_ref, b_ref, o_ref):
    a = c_ref[...]
    a = a * _sigmoid(a)
    o_ref[0] = _dot3(a, w_ref[0]) + b_ref[0]


def _modulation(cc, mod_w, mod_b):
    depth = mod_w.shape[0]
    tn = 1024
    return pl.pallas_call(
        _mod_kernel,
        grid=(depth, 6 * D // tn),
        in_specs=[
            pl.BlockSpec((16, D), lambda l, j: (0, 0)),
            pl.BlockSpec((1, D, tn), lambda l, j: (l, 0, j)),
            pl.BlockSpec((1, 1, tn), lambda l, j: (l, 0, j)),
        ],
        out_specs=pl.BlockSpec((1, 16, tn), lambda l, j: (l, 0, j)),
        out_shape=jax.ShapeDtypeStruct((depth, 16, 6 * D), F32),
        compiler_params=_params(("arbitrary", "arbitrary")),
        name="modulation",
    )(cc, mod_w, mod_b.reshape(depth, 1, 6 * D))


def _swap_halves(t):
    w = t.shape[1]
    lane = lax.broadcasted_iota(jnp.int32, t.shape, 1)
    first = (lane & (HD - 1)) < (HD // 2)
    return jnp.where(first, pltpu.roll(t, w - HD // 2, 1), pltpu.roll(t, HD // 2, 1))


def _head_norm_rope(t, bd, g, cos, sin, scale):
    hi, lo = _split(t * t)
    ssq = _dot(hi, bd) + _dot(lo, bd)
    tn = t * lax.rsqrt(ssq * (1.0 / HD) + EPS) * g
    n = t.shape[1] // 128
    if n > 1:
        cos = jnp.concatenate([cos] * n, axis=1)
        sin = jnp.concatenate([sin] * n, axis=1)
    out = tn * cos + _swap_halves(tn) * sin
    return out * scale if scale != 1.0 else out


def _proj_kernel(x_ref, mod_ref, lng_ref, w_ref, cos_ref, sin_ref, qg_ref, kg_ref, bdq_ref, bdk_ref,
                 q_ref, k_ref, v_ref, u_ref, *, glu):
    mod = mod_ref[0, 0]
    h = _rmsnorm(x_ref[0], lng_ref[...]) * (1.0 + mod[:, D:2 * D]) + mod[:, 0:D]
    px = _dot(h.astype(BF16), w_ref[...])
    cos = cos_ref[...]
    sin = sin_ref[...]
    q = _head_norm_rope(px[:, 0:QW], bdq_ref[...], qg_ref[...], cos, sin, HD ** -0.5)
    k = _head_norm_rope(px[:, QW:QW + KW], bdk_ref[...], kg_ref[...], cos, sin, 1.0)
    v = px[:, QW + KW:QW + 2 * KW]
    for hh in range(NH):
        q_ref[0, hh] = q[:, hh * HD:(hh + 1) * HD].astype(BF16)
    for j in range(NKV):
        k_ref[0, j] = k[:, j * HD:(j + 1) * HD].astype(BF16)
        v_ref[0, j] = v[:, j * HD:(j + 1) * HD].astype(BF16)
    o_u = QW + 2 * KW
    if glu:
        u_ref[0] = px[:, o_u:o_u + UW] * _sigmoid(px[:, o_u + UW:o_u + 2 * UW])
    else:
        u_ref[0] = px[:, o_u:o_u + UW]


def _project(xc, modsel, ln_g, w_in, cos, sin, q_g, k_g, bdq, bdk, *, glu):
    b, s_tot, _ = xc.shape
    n_in = w_in.shape[1]
    nt = s_tot // TM
    const = lambda bi, i: (0, 0)
    return pl.pallas_call(
        functools.partial(_proj_kernel, glu=glu),
        grid=(b, nt),
        in_specs=[
            pl.BlockSpec((1, TM, D), lambda bi, i: (bi, i, 0)),
            pl.BlockSpec((1, 1, 1, 6 * D), lambda bi, i: (bi, jnp.minimum(i, 1), 0, 0)),
            pl.BlockSpec((1, D), const),
            pl.BlockSpec((D, n_in), const),
            pl.BlockSpec((TM, 128), lambda bi, i: (i, 0)),
            pl.BlockSpec((TM, 128), lambda bi, i: (i, 0)),
            pl.BlockSpec((1, QW), const),
            pl.BlockSpec((1, KW), const),
            pl.BlockSpec((QW, QW), const),
            pl.BlockSpec((KW, KW), const),
        ],
        out_specs=[
            pl.BlockSpec((1, NH, TM, HD), lambda bi, i: (bi, 0, i, 0)),
            pl.BlockSpec((1, NKV, TM, HD), lambda bi, i: (bi, 0, i, 0)),
            pl.BlockSpec((1, NKV, TM, HD), lambda bi, i: (bi, 0, i, 0)),
            pl.BlockSpec((1, TM, UW), lambda bi, i: (bi, i, 0)),
        ],
        out_shape=[
            jax.ShapeDtypeStruct((b, NH, s_tot, HD), BF16),
            jax.ShapeDtypeStruct((b, NKV, s_tot, HD), BF16),
            jax.ShapeDtypeStruct((b, NKV, s_tot, HD), BF16),
            jax.ShapeDtypeStruct((b, s_tot, UW), F32),
        ],
        compiler_params=_params(("arbitrary", "arbitrary")),
        name="in_projection",
    )(xc, modsel, ln_g, w_in, cos, sin, q_g, k_g, bdq, bdk)


def _merge_heads(o):
    return jnp.concatenate([o[g * TQ:(g + 1) * TQ] for g in range(GQ)], axis=1)


def _dense_attn_kernel(q_ref, k_ref, v_ref, o_ref, m_sc, l_sc, acc_sc, *, n_ctx, s_tot):
    qi = pl.program_id(2)
    q = q_ref[0].reshape(GQ * TQ, HD)
    m_sc[...] = jnp.full(m_sc.shape, -jnp.inf, F32)
    l_sc[...] = jnp.zeros(l_sc.shape, F32)
    acc_sc[...] = jnp.zeros(acc_sc.shape, F32)
    n_chunks = jnp.where(qi * TQ < n_ctx, n_ctx // TK, s_tot // TK)

    def chunk(c, carry):
        start = pl.multiple_of(c * TK, TK)
        k = k_ref[0, 0, pl.ds(start, TK), :]
        v = v_ref[0, 0, pl.ds(start, TK), :]
        s = _dot_nt(q, k)
        m_prev = m_sc[...]
        m_new = jnp.maximum(m_prev, jnp.max(s, axis=1, keepdims=True))
        alpha = jnp.exp(m_prev - m_new)
        p = jnp.exp(s - m_new[:, 0:1])
        l_sc[...] = alpha * l_sc[...] + jnp.sum(p, axis=1, keepdims=True)
        acc_sc[...] = acc_sc[...] * alpha[:, 0:HD] + _dot(p.astype(BF16), v)
        m_sc[...] = m_new
        return carry

    lax.fori_loop(0, n_chunks, chunk, 0)
    o = acc_sc[...] / l_sc[:, 0:HD]
    o_ref[0] = _merge_heads(o).astype(BF16)


def _dense_attention(q, k, v, n_ctx):
    b, _, s_tot, _ = q.shape
    return pl.pallas_call(
        functools.partial(_dense_attn_kernel, n_ctx=n_ctx, s_tot=s_tot),
        grid=(b, NKV, s_tot // TQ),
        in_specs=[
            pl.BlockSpec((1, GQ, TQ, HD), lambda bi, j, i: (bi, j, i, 0)),
            pl.BlockSpec((1, 1, s_tot, HD), lambda bi, j, i: (bi, j, 0, 0)),
            pl.BlockSpec((1, 1, s_tot, HD), lambda bi, j, i: (bi, j, 0, 0)),
        ],
        out_specs=pl.BlockSpec((1, TQ, GQ * HD), lambda bi, j, i: (bi, i, j)),
        out_shape=jax.ShapeDtypeStruct((b, s_tot, QW), BF16),
        scratch_shapes=[
            pltpu.VMEM((GQ * TQ, 128), F32),
            pltpu.VMEM((GQ * TQ, 128), F32),
            pltpu.VMEM((GQ * TQ, HD), F32),
        ],
        compiler_params=_params(("arbitrary", "arbitrary", "arbitrary")),
        name="dense_attention",
    )(q, k, v)


def _window_attn_kernel(sink_ref, q_ref, k_ref, v_ref, o_ref, *, n_ctx, s_tot):
    j = pl.program_id(1)
    qi = pl.program_id(2)
    span = TQ + 2 * WINDOW
    q = q_ref[0].reshape(GQ * TQ, HD)
    q0 = n_ctx + qi * TQ
    start = pl.multiple_of(jnp.clip(q0 - WINDOW, n_ctx, s_tot - span), 128)
    kc = k_ref[0, 0, 0:n_ctx, :]
    vc = v_ref[0, 0, 0:n_ctx, :]
    kw = k_ref[0, 0, pl.ds(start, span), :]
    vw = v_ref[0, 0, pl.ds(start, span), :]
    s_c = _dot_nt(q, kc)
    s_w = _dot_nt(q, kw)
    row = lax.broadcasted_iota(jnp.int32, (GQ * TQ, span), 0)
    col = lax.broadcasted_iota(jnp.int32, (GQ * TQ, span), 1)
    dist = (q0 + (row & (TQ - 1))) - (start + col)
    s_w = jnp.where(jnp.abs(dist) <= WINDOW, s_w, -jnp.inf)
    sink = jnp.concatenate([jnp.full((TQ, 1), sink_ref[j * GQ + g], F32) for g in range(GQ)], axis=0)
    m = jnp.maximum(jnp.maximum(jnp.max(s_c, axis=1, keepdims=True), jnp.max(s_w, axis=1, keepdims=True)), sink)
    p_c = jnp.exp(s_c - m)
    p_w = jnp.exp(s_w - m)
    l = jnp.sum(p_c, axis=1, keepdims=True) + jnp.sum(p_w, axis=1, keepdims=True) + jnp.exp(sink - m)
    o = (_dot(p_c.astype(BF16), vc) + _dot(p_w.astype(BF16), vw)) / l
    o_ref[0] = _merge_heads(o).astype(BF16)


def _window_attention(sink, q, k, v, n_ctx):
    b, _, s_tot, _ = q.shape
    s_lat = s_tot - n_ctx
    off = n_ctx // TQ
    return pl.pallas_call(
        functools.partial(_window_attn_kernel, n_ctx=n_ctx, s_tot=s_tot),
        grid_spec=pltpu.PrefetchScalarGridSpec(
            num_scalar_prefetch=1,
            grid=(b, NKV, s_lat // TQ),
            in_specs=[
                pl.BlockSpec((1, GQ, TQ, HD), lambda bi, j, i, sk: (bi, j, i + off, 0)),
                pl.BlockSpec((1, 1, s_tot, HD), lambda bi, j, i, sk: (bi, j, 0, 0)),
                pl.BlockSpec((1, 1, s_tot, HD), lambda bi, j, i, sk: (bi, j, 0, 0)),
            ],
            out_specs=pl.BlockSpec((1, TQ, GQ * HD), lambda bi, j, i, sk: (bi, i, j)),
        ),
        out_shape=jax.ShapeDtypeStruct((b, s_lat, QW), BF16),
        compiler_params=_params(("arbitrary", "arbitrary", "arbitrary")),
        name="window_attention",
    )(sink, q, k, v)


def _conv_kernel(g_ref, cw_ref, cb_ref, lg_ref, lb_ref, o_ref, pad_sc, *, n_ctx, s_tot):
    zeros = jnp.zeros((HALO, UW), F32)
    pad_sc[0:HALO, :] = zeros
    pad_sc[HALO + n_ctx:2 * HALO + n_ctx, :] = zeros
    pad_sc[2 * HALO + s_tot:3 * HALO + s_tot, :] = zeros

    def fill(i, carry):
        src = pl.multiple_of(i * TM, TM)
        dst = pl.multiple_of(src + HALO + jnp.where(src >= n_ctx, HALO, 0), 8)
        pad_sc[pl.ds(dst, TM), :] = g_ref[0, pl.ds(src, TM), :]
        return carry

    lax.fori_loop(0, s_tot // TM, fill, 0)
    half = CONV_K // 2

    def tile(i, carry):
        src = pl.multiple_of(i * SUB, SUB)
        base = pl.multiple_of(src + jnp.where(src >= n_ctx, HALO, 0), 8)
        win = pad_sc[pl.ds(base, SUB + 2 * HALO), :]
        acc = jnp.zeros((SUB, UW), F32) + cb_ref[...]
        for t in range(CONV_K):
            o = HALO - half + t
            acc = acc + win[o:o + SUB, :] * cw_ref[t:t + 1, :]
        mu = jnp.mean(acc, axis=-1, keepdims=True)
        xc = acc - mu
        var = jnp.mean(xc * xc, axis=-1, keepdims=True)
        yn = xc * lax.rsqrt(var + EPS) * lg_ref[...] + lb_ref[...]
        o_ref[0, pl.ds(src, SUB), :] = (yn * _sigmoid(yn)).astype(BF16)
        return carry

    lax.fori_loop(0, s_tot // SUB, tile, 0)


def _conv_module(glu, conv_w, conv_b, ln_g, ln_b, n_ctx):
    b, s_tot, _ = glu.shape
    const = lambda bi: (0, 0)
    return pl.pallas_call(
        functools.partial(_conv_kernel, n_ctx=n_ctx, s_tot=s_tot),
        grid=(b,),
        in_specs=[
            pl.BlockSpec((1, s_tot, UW), lambda bi: (bi, 0, 0)),
            pl.BlockSpec((CONV_K, UW), const),
            pl.BlockSpec((1, UW), const),
            pl.BlockSpec((1, UW), const),
            pl.BlockSpec((1, UW), const),
        ],
        out_specs=pl.BlockSpec((1, s_tot, UW), lambda bi: (bi, 0, 0)),
        out_shape=jax.ShapeDtypeStruct((b, s_tot, UW), BF16),
        scratch_shapes=[pltpu.VMEM((s_tot + 3 * HALO, UW), F32)],
        compiler_params=_params(("arbitrary",)),
        name="conv_module",
    )(glu, conv_w, conv_b, ln_g, ln_b)


def _pool_kernel(u_ref, pw_ref, ps_ref, o_ref, pad_sc, *, n_ctx, s_lat):
    zeros = jnp.zeros((HALO, UW), F32)
    pad_sc[0:HALO, :] = zeros
    pad_sc[HALO + s_lat:2 * HALO + s_lat, :] = zeros

    def fill(i, carry):
        src = pl.multiple_of(i * TM, TM)
        pad_sc[pl.ds(pl.multiple_of(src + HALO, 8), TM), :] = u_ref[0, pl.ds(pl.multiple_of(src + n_ctx, 8), TM), :]
        return carry

    lax.fori_loop(0, s_lat // TM, fill, 0)
    gw = UW // len(POOL_SIZES)

    def tile(i, carry):
        src = pl.multiple_of(i * SUB, SUB)
        win = pad_sc[pl.ds(src, SUB + 2 * HALO), :]
        t = src + lax.broadcasted_iota(jnp.int32, (SUB, 1), 0)
        outs = []
        for gi, w in enumerate(POOL_SIZES):
            lanes = slice(gi * gw, (gi + 1) * gw)
            tot = jnp.zeros((SUB, gw), F32)
            for d in range(-(w // 2), w - w // 2):
                tot = tot + win[HALO + d:HALO + d + SUB, lanes]
            lo = jnp.clip(t - w // 2, 0, s_lat)
            hi = jnp.clip(t - w // 2 + w, 0, s_lat)
            p = tot / (hi - lo).astype(F32) - win[HALO:HALO + SUB, lanes]
            outs.append(_dot(p.astype(BF16), pw_ref[gi]))
        y = jnp.concatenate(outs, axis=1) * ps_ref[...]
        o_ref[0, pl.ds(src, SUB), :] = y.astype(BF16)
        return carry

    lax.fori_loop(0, s_lat // SUB, tile, 0)


def _pool_mixer(u, pool_w, pool_scale, n_ctx):
    b, s_tot, _ = u.shape
    s_lat = s_tot - n_ctx
    gw = UW // len(POOL_SIZES)
    return pl.pallas_call(
        functools.partial(_pool_kernel, n_ctx=n_ctx, s_lat=s_lat),
        grid=(b,),
        in_specs=[
            pl.BlockSpec((1, s_tot, UW), lambda bi: (bi, 0, 0)),
            pl.BlockSpec((len(POOL_SIZES), gw, gw), lambda bi: (0, 0, 0)),
            pl.BlockSpec((1, UW), lambda bi: (0, 0)),
        ],
        out_specs=pl.BlockSpec((1, s_lat, UW), lambda bi: (bi, 0, 0)),
        out_shape=jax.ShapeDtypeStruct((b, s_lat, UW), BF16),
        scratch_shapes=[pltpu.VMEM((s_lat + 2 * HALO, UW), F32)],
        compiler_params=_params(("arbitrary",)),
        name="pool_mixer",
    )(u, pool_w, pool_scale)


def _route(logits):
    shape = logits.shape
    lane = lax.broadcasted_iota(jnp.int32, shape, 1)
    lanef = lane.astype(F32)
    is_g = (lane >= N_EXP) & (lane < N_EXP + N_GROUPS)
    big = 1e9
    lg = jnp.where(is_g, logits, -jnp.inf)
    gmax = jnp.max(lg, axis=1, keepdims=True)
    gsel = jnp.min(jnp.where(is_g & (lg == gmax), lanef, big), axis=1, keepdims=True) - float(N_EXP)
    gsum = jnp.sum(jnp.exp(lg - gmax), axis=1, keepdims=True)
    g_w = 1.0 / gsum
    in_grp = (lane < N_EXP) & ((lane >> 3).astype(F32) == gsel)
    el = jnp.where(in_grp, logits, -jnp.inf)
    emax = jnp.max(el, axis=1, keepdims=True)
    ee = jnp.exp(el - emax)
    p = ee / jnp.sum(ee, axis=1, keepdims=True)
    p1 = jnp.max(jnp.where(in_grp, p, -1.0), axis=1, keepdims=True)
    i1 = jnp.min(jnp.where(in_grp & (p == p1), lanef, big), axis=1, keepdims=True)
    rest = in_grp & (lanef != i1)
    p2 = jnp.max(jnp.where(rest, p, -1.0), axis=1, keepdims=True)
    i2 = jnp.min(jnp.where(rest & (p == p2), lanef, big), axis=1, keepdims=True)
    w1 = g_w * p1 / (p1 + p2)
    w2 = g_w * p2 / (p1 + p2)
    onehot = jnp.where((lanef == i1) | (lanef == i2), 1.0, 0.0)
    return lanef, i1, i2, w1, w2, onehot


def _outproj_kernel(a_ref, b_ref, x_ref, mod_ref, w_ref, lng_ref, rw_ref, rb_ref, tri_ref,
                    x1_ref, h2_ref, r_ref, cnt_ref, cnt_sc):
    @pl.when((pl.program_id(0) == 0) & (pl.program_id(1) == 0))
    def _():
        cnt_sc[...] = jnp.zeros(cnt_sc.shape, F32)

    mod = mod_ref[0, 0]
    y = _dot(a_ref[0], w_ref[0:QW, :]) + _dot(b_ref[0], w_ref[QW:QW + UW, :])
    x1 = x_ref[0] + mod[:, 2 * D:3 * D] * y
    x1_ref[0] = x1
    h2 = _rmsnorm(x1, lng_ref[...]) * (1.0 + mod[:, 4 * D:5 * D]) + mod[:, 3 * D:4 * D]
    h2_ref[0] = h2
    logits = _dot3(h2, rw_ref[...]) + rb_ref[...]
    lanef, i1, i2, w1, w2, onehot = _route(logits)
    before = _dot(tri_ref[...], onehot.astype(BF16)) + cnt_sc[...]
    r1 = jnp.sum(jnp.where(lanef == i1, before, 0.0), axis=1, keepdims=True)
    r2 = jnp.sum(jnp.where(lanef == i2, before, 0.0), axis=1, keepdims=True)
    cnt_sc[...] = cnt_sc[...] + jnp.sum(onehot, axis=0, keepdims=True)
    cnt_ref[...] = cnt_sc[...]
    info = jnp.where(lanef == 0.0, i1, jnp.where(lanef == 1.0, i2, jnp.where(lanef == 2.0, w1, jnp.where(
        lanef == 3.0, w2, jnp.where(lanef == 4.0, r1, jnp.where(lanef == 5.0, r2, 0.0))))))
    r_ref[0] = info


def _out_project(a, bmix, xc, modsel, w_out, ln_g, rw, rb, tri, *, tile_off):
    b, s_out, _ = a.shape
    nt = s_out // TM
    const = lambda bi, i: (0, 0)
    sel = (lambda i: jnp.minimum(i, 1)) if tile_off == 0 else (lambda i: 1)
    tok = lambda bi, i: (bi, i, 0)
    return pl.pallas_call(
        _outproj_kernel,
        grid=(b, nt),
        in_specs=[
            pl.BlockSpec((1, TM, QW), tok),
            pl.BlockSpec((1, TM, UW), tok),
            pl.BlockSpec((1, TM, D), lambda bi, i: (bi, i + tile_off, 0)),
            pl.BlockSpec((1, 1, 1, 6 * D), lambda bi, i: (bi, sel(i), 0, 0)),
            pl.BlockSpec((D, D), const),
            pl.BlockSpec((1, D), const),
            pl.BlockSpec((D, 128), const),
            pl.BlockSpec((1, 128), const),
            pl.BlockSpec((TM, TM), const),
        ],
        out_specs=[
            pl.BlockSpec((1, TM, D), tok),
            pl.BlockSpec((1, TM, D), tok),
            pl.BlockSpec((1, TM, 128), tok),
            pl.BlockSpec((1, 128), const),
        ],
        out_shape=[
            jax.ShapeDtypeStruct((b, s_out, D), F32),
            jax.ShapeDtypeStruct((b, s_out, D), F32),
            jax.ShapeDtypeStruct((b, s_out, 128), F32),
            jax.ShapeDtypeStruct((1, 128), F32),
        ],
        scratch_shapes=[pltpu.VMEM((1, 128), F32)],
        compiler_params=_params(("arbitrary", "arbitrary")),
        name="out_projection_router",
    )(a, bmix, xc, modsel, w_out, ln_g, rw, rb, tri)


def _row_copy(src_ref, src_row, dst_ref, dst_row, sem):
    return pltpu.make_async_copy(src_ref.at[pl.ds(src_row, 1), :], dst_ref.at[pl.ds(dst_row, 1), :], sem)


def _dispatch_kernel(pos_ref, h_ref, xs_in_ref, xs_ref, sem):
    del xs_in_ref
    i = pl.program_id(0)

    def issue(r, carry):
        for k in range(2):
            _row_copy(h_ref, r, xs_ref, pos_ref[(i * TM + r) * 2 + k], sem).start()
        return carry

    lax.fori_loop(0, TM, issue, 0)

    def wait(r, carry):
        _row_copy(h_ref, 0, xs_ref, 0, sem).wait()
        return carry

    lax.fori_loop(0, 2 * TM, wait, 0)


def _dispatch(pos, h2, cap):
    t = h2.shape[0]
    xs0 = jnp.zeros((cap, D), F32)
    return pl.pallas_call(
        _dispatch_kernel,
        grid_spec=pltpu.PrefetchScalarGridSpec(
            num_scalar_prefetch=1,
            grid=(t // TM,),
            in_specs=[
                pl.BlockSpec((TM, D), lambda i, pos: (i, 0)),
                pl.BlockSpec(memory_space=pl.ANY),
            ],
            out_specs=pl.BlockSpec(memory_space=pl.ANY),
            scratch_shapes=[pltpu.SemaphoreType.DMA],
        ),
        out_shape=jax.ShapeDtypeStruct((cap, D), F32),
        input_output_aliases={2: 0},
        compiler_params=_params(("arbitrary",)),
        name="moe_dispatch",
    )(pos, h2, xs0)


def _expert_kernel(be_ref, nu_ref, x_ref, wg_ref, wu_ref, wd_ref, y_ref):
    del be_ref

    used = pl.program_id(0) < nu_ref[0]

    @pl.when(used)
    def _():
        xb = x_ref[...].astype(BF16)
        g = _dot(xb, wg_ref[0])
        u = _dot(xb, wu_ref[0])
        hid = (g * _sigmoid(g)) * u
        y_ref[...] = _dot(hid.astype(BF16), wd_ref[0])

    @pl.when(jnp.logical_not(used))
    def _():
        y_ref[...] = jnp.zeros(y_ref.shape, F32)


def _experts(blk_e, n_used, xs, wg, wu, wd):
    cap = xs.shape[0]
    row = lambda i, be, nu: (jnp.minimum(i, nu[0] - 1), 0)
    wsel = lambda i, be, nu: (be[i], 0, 0)
    return pl.pallas_call(
        _expert_kernel,
        grid_spec=pltpu.PrefetchScalarGridSpec(
            num_scalar_prefetch=2,
            grid=(cap // BLK,),
            in_specs=[
                pl.BlockSpec((BLK, D), row),
                pl.BlockSpec((1, D, D_EXP), wsel),
                pl.BlockSpec((1, D, D_EXP), wsel),
                pl.BlockSpec((1, D_EXP, D), wsel),
            ],
            out_specs=pl.BlockSpec((BLK, D), lambda i, be, nu: (i, 0)),
        ),
        out_shape=jax.ShapeDtypeStruct((cap, D), F32),
        compiler_params=_params(("arbitrary",)),
        name="moe_experts",
    )(blk_e, n_used, xs, wg, wu, wd)


def _combine_kernel(pos_ref, y_ref, x1_ref, r_ref, mod_ref, o_ref, buf, sem):
    i = pl.program_id(0)

    def issue(r, carry):
        for k in range(2):
            _row_copy(y_ref, pos_ref[(i * TM + r) * 2 + k], buf.at[k], r, sem).start()
        return carry

    lax.fori_loop(0, TM, issue, 0)

    def wait(r, carry):
        _row_copy(y_ref, 0, buf.at[0], 0, sem).wait()
        return carry

    lax.fori_loop(0, 2 * TM, wait, 0)
    info = r_ref[...]
    g2 = mod_ref[0, 0][:, 5 * D:6 * D]
    o_ref[...] = x1_ref[...] + g2 * (info[:, 2:3] * buf[0] + info[:, 3:4] * buf[1])


def _combine(pos, ys, x1, rinfo, modsel, *, tiles_per_batch, with_ctx):
    t = x1.shape[0]
    if with_ctx:
        msel = lambda i, pos: (i // tiles_per_batch, jnp.minimum(i % tiles_per_batch, 1), 0, 0)
    else:
        msel = lambda i, pos: (i // tiles_per_batch, 1, 0, 0)
    tok = lambda i, pos: (i, 0)
    return pl.pallas_call(
        _combine_kernel,
        grid_spec=pltpu.PrefetchScalarGridSpec(
            num_scalar_prefetch=1,
            grid=(t // TM,),
            in_specs=[
                pl.BlockSpec(memory_space=pl.ANY),
                pl.BlockSpec((TM, D), tok),
                pl.BlockSpec((TM, 128), tok),
                pl.BlockSpec((1, 1, 1, 6 * D), msel),
            ],
            out_specs=pl.BlockSpec((TM, D), tok),
            scratch_shapes=[pltpu.VMEM((2, TM, D), F32), pltpu.SemaphoreType.DMA],
        ),
        out_shape=jax.ShapeDtypeStruct((t, D), F32),
        compiler_params=_params(("arbitrary",)),
        name="moe_combine",
    )(pos, ys, x1, rinfo, modsel)


def _moe(x1, h2, rinfo, cnt, modsel, wg, wu, wd, *, with_ctx):
    b, s, _ = x1.shape
    t = b * s
    n_asg = 2 * t
    n_blk = -(-(n_asg + N_EXP * (BLK - 1)) // BLK)
    cap = n_blk * BLK
    info = rinfo.reshape(t, 128)
    eid = info[:, 0:2].astype(jnp.int32)
    rank = info[:, 4:6].astype(jnp.int32)
    counts = cnt[0, 0:N_EXP].astype(jnp.int32)
    padded = (counts + BLK - 1) // BLK * BLK
    pend = jnp.cumsum(padded)
    pstart = pend - padded
    pos = (pstart[eid] + rank).reshape(n_asg)
    blk_e = jnp.minimum(jnp.searchsorted(pend, jnp.arange(n_blk, dtype=jnp.int32) * BLK, side="right"),
                        N_EXP - 1).astype(jnp.int32)
    n_used = (pend[N_EXP - 1:N_EXP] // BLK).astype(jnp.int32)
    xs = _dispatch(pos, h2.reshape(t, D), cap)
    ys = _experts(blk_e, n_used, xs, wg, wu, wd)
    out = _combine(pos, ys, x1.reshape(t, D), info, modsel, tiles_per_batch=s // TM, with_ctx=with_ctx)
    return out.reshape(b, s, D)


def _rope_tables(n_ctx, s_lat):
    rows = s_lat // GRID_W
    row = jnp.repeat(jnp.arange(rows, dtype=F32), GRID_W)
    col = jnp.tile(jnp.arange(GRID_W, dtype=F32), rows)
    n_freq = HD // 4
    inv = ROPE_THETA ** (-jnp.arange(n_freq, dtype=F32) / n_freq)
    ang = jnp.concatenate([row[:, None] * inv, col[:, None] * inv], axis=-1)
    cos = jnp.concatenate([jnp.ones((n_ctx, HD // 2), F32), jnp.cos(ang)], axis=0)
    sin = jnp.concatenate([jnp.zeros((n_ctx, HD // 2), F32), jnp.sin(ang)], axis=0)
    cos = jnp.concatenate([cos, cos, cos, cos], axis=1)
    sin = jnp.concatenate([-sin, sin, -sin, sin], axis=1)
    return cos, sin


def _block_diag_ones(n):
    i = jnp.arange(n) // HD
    return (i[:, None] == i[None, :]).astype(BF16)


def kernel(x, c, ctx, c_ctx, mod_w, mod_b, ln1_g, ln2_g, w_in_ab, w_out_ab, q_norm_a, k_norm_a, conv_w, conv_b,
           conv_ln_g, conv_ln_b, w_in_cd, w_out_cd, q_norm_c, k_norm_c, sink_c, pool_w, pool_scale,
           rt_grp_w, rt_grp_b, rt_exp_w, rt_exp_b, ex_gate, ex_up, ex_down):
    b, s_lat, _ = x.shape
    n_ctx = ctx.shape[1]
    assert n_ctx == TM and s_lat % TM == 0 and b <= 8

    cc = jnp.zeros((16, D), F32).at[0:b].set(c).at[8].set(c_ctx)
    mod = _modulation(cc, mod_w, mod_b)

    def modsel(l):
        return jnp.stack([jnp.broadcast_to(mod[l, 8], (b, 6 * D)), mod[l, 0:b]], axis=1).reshape(b, 2, 1, 6 * D)

    cos, sin = _rope_tables(n_ctx, s_lat)
    bdq = _block_diag_ones(QW)
    bdk = _block_diag_ones(KW)
    tri = (jnp.arange(TM)[:, None] > jnp.arange(TM)[None, :]).astype(BF16)

    def router(l):
        rw = jnp.zeros((D, 128), F32).at[:, 0:N_EXP].set(rt_exp_w[l]).at[:, N_EXP:N_EXP + N_GROUPS].set(rt_grp_w[l])
        rb = jnp.zeros((1, 128), F32).at[0, 0:N_EXP].set(rt_exp_b[l]).at[0, N_EXP:N_EXP + N_GROUPS].set(rt_grp_b[l])
        return rw, rb

    def tile_gain(g, n):
        return jnp.tile(g, n).reshape(1, n * HD)

    xc = jnp.concatenate([ctx, x], axis=1)

    ms = modsel(0)
    q, k, v, glu = _project(xc, ms, ln1_g[0:1], w_in_ab[0].astype(BF16), cos, sin,
                            tile_gain(q_norm_a[0], NH), tile_gain(k_norm_a[0], NKV), bdq, bdk, glu=True)
    att = _dense_attention(q, k, v, n_ctx)
    cv = _conv_module(glu, conv_w[0], conv_b[0:1], conv_ln_g[0:1], conv_ln_b[0:1], n_ctx)
    rw, rb = router(0)
    x1, h2, rinfo, cnt = _out_project(att, cv, xc, ms, w_out_ab[0].astype(BF16), ln2_g[0:1], rw, rb, tri, tile_off=0)
    xc = _moe(x1, h2, rinfo, cnt, ms, ex_gate[0].astype(BF16), ex_up[0].astype(BF16), ex_down[0].astype(BF16),
              with_ctx=True)

    ms = modsel(1)
    q, k, v, u = _project(xc, ms, ln1_g[1:2], w_in_cd[0].astype(BF16), cos, sin,
                          tile_gain(q_norm_c[0], NH), tile_gain(k_norm_c[0], NKV), bdq, bdk, glu=False)
    att = _window_attention(sink_c[0], q, k, v, n_ctx)
    pm = _pool_mixer(u, pool_w[0].astype(BF16), pool_scale[0:1], n_ctx)
    rw, rb = router(1)
    x1, h2, rinfo, cnt = _out_project(att, pm, xc, ms, w_out_cd[0].astype(BF16), ln2_g[1:2], rw, rb, tri,
                                      tile_off=n_ctx // TM)
    return _moe(x1, h2, rinfo, cnt, ms, ex_gate[1].astype(BF16), ex_up[1].astype(BF16), ex_down[1].astype(BF16),
                with_ctx=False)
```

```python
import functools

import jax
import jax.numpy as jnp
from jax import lax
from jax.experimental import pallas as pl
from jax.experimental.pallas import tpu as pltpu

F32 = jnp.float32
BF16 = jnp.bfloat16

D = 1024
HD = 64
NH = 8
NKV = 2
GQ = NH // NKV
QW = NH * HD
KW = NKV * HD
UW = 512
EPS = 1e-6
ROPE_THETA = 10000.0
GRID_W = 64
CONV_K = 31
WINDOW = 128
POOL_SIZES = (2, 4, 8, 16)
N_GROUPS = 4
PER_GROUP = 8
N_EXP = N_GROUPS * PER_GROUP
D_EXP = D // 2

TM = 256
TQ = 128
TKL = 512
SUB = 64
HALO = 16
BLK = 256
VMEM_LIMIT = 56 * 1024 * 1024


def _sigmoid(x):
    return 1.0 / (1.0 + jnp.exp(-x))


def _dot(a, b):
    return jnp.dot(a, b, preferred_element_type=F32)


def _dot_nt(a, b):
    return lax.dot_general(a, b, (((1,), (1,)), ((), ())), preferred_element_type=F32)


def _split(a):
    hi = a.astype(BF16)
    lo = (a - hi.astype(F32)).astype(BF16)
    return hi, lo


def _dot3(a, w):
    a_hi, a_lo = _split(a)
    w_hi, w_lo = _split(w)
    return _dot(a_hi, w_hi) + _dot(a_lo, w_hi) + _dot(a_hi, w_lo)


def _rmsnorm(x, g):
    return x * lax.rsqrt(jnp.mean(x * x, axis=-1, keepdims=True) + EPS) * g


def _params(sem, vmem=VMEM_LIMIT):
    return pltpu.CompilerParams(dimension_semantics=sem, vmem_limit_bytes=vmem)


def _mod_kernel(c_ref, w_ref, b_ref, o_ref):
    a = c_ref[...]
    a = a * _sigmoid(a)
    o_ref[0] = _dot3(a, w_ref[0]) + b_ref[0]


def _modulation(cc, mod_w, mod_b):
    depth = mod_w.shape[0]
    tn = 1024
    return pl.pallas_call(
        _mod_kernel,
        grid=(depth, 6 * D // tn),
        in_specs=[
            pl.BlockSpec((16, D), lambda l, j: (0, 0)),
            pl.BlockSpec((1, D, tn), lambda l, j: (l, 0, j)),
            pl.BlockSpec((1, 1, tn), lambda l, j: (l, 0, j)),
        ],
        out_specs=pl.BlockSpec((1, 16, tn), lambda l, j: (l, 0, j)),
        out_shape=jax.ShapeDtypeStruct((depth, 16, 6 * D), F32),
        compiler_params=_params(("arbitrary", "arbitrary")),
        name="modulation",
    )(cc, mod_w, mod_b.reshape(depth, 1, 6 * D))


def _swap_halves(t):
    w = t.shape[1]
    lane = lax.broadcasted_iota(jnp.int32, t.shape, 1)
    first = (lane & (HD - 1)) < (HD // 2)
    return jnp.where(first, pltpu.roll(t, w - HD // 2, 1), pltpu.roll(t, HD // 2, 1))


def _head_norm_rope(t, bd, g, cos, sin, scale):
    hi, lo = _split(t * t)
    ssq = _dot(hi, bd) + _dot(lo, bd)
    tn = t * lax.rsqrt(ssq * (1.0 / HD) + EPS) * g
    n = t.shape[1] // 128
    if n > 1:
        cos = jnp.concatenate([cos] * n, axis=1)
        sin = jnp.concatenate([sin] * n, axis=1)
    out = tn * cos + _swap_halves(tn) * sin
    return out * scale if scale != 1.0 else out


def _proj_kernel(x_ref, mod_ref, lng_ref, w_ref, cos_ref, sin_ref, qg_ref, kg_ref, bdq_ref, bdk_ref,
                 q_ref, k_ref, v_ref, u_ref, *, glu):
    mod = mod_ref[0, 0]
    h = _rmsnorm(x_ref[0], lng_ref[...]) * (1.0 + mod[:, D:2 * D]) + mod[:, 0:D]
    px = _dot(h.astype(BF16), w_ref[...])
    cos = cos_ref[...]
    sin = sin_ref[...]
    q = _head_norm_rope(px[:, 0:QW], bdq_ref[...], qg_ref[...], cos, sin, HD ** -0.5)
    k = _head_norm_rope(px[:, QW:QW + KW], bdk_ref[...], kg_ref[...], cos, sin, 1.0)
    v = px[:, QW + KW:QW + 2 * KW]
    for hh in range(NH):
        q_ref[0, hh] = q[:, hh * HD:(hh + 1) * HD].astype(BF16)
    lane = lax.broadcasted_iota(jnp.int32, v.shape, 1)
    ones_col = jnp.where(lane == HD, 1.0, 0.0)
    for j in range(NKV):
        k_ref[0, j] = k[:, j * HD:(j + 1) * HD].astype(BF16)
        vj = v if j == 0 else pltpu.roll(v, KW - j * HD, 1)
        v_ref[0, j] = jnp.where(lane < HD, vj, ones_col).astype(BF16)
    o_u = QW + 2 * KW
    if glu:
        u_ref[0] = px[:, o_u:o_u + UW] * _sigmoid(px[:, o_u + UW:o_u + 2 * UW])
    else:
        u_ref[0] = px[:, o_u:o_u + UW]


def _project(xc, modsel, ln_g, w_in, cos, sin, q_g, k_g, bdq, bdk, *, glu):
    b, s_tot, _ = xc.shape
    n_in = w_in.shape[1]
    nt = s_tot // TM
    const = lambda bi, i: (0, 0)
    return pl.pallas_call(
        functools.partial(_proj_kernel, glu=glu),
        grid=(b, nt),
        in_specs=[
            pl.BlockSpec((1, TM, D), lambda bi, i: (bi, i, 0)),
            pl.BlockSpec((1, 1, 1, 6 * D), lambda bi, i: (bi, jnp.minimum(i, 1), 0, 0)),
            pl.BlockSpec((1, D), const),
            pl.BlockSpec((D, n_in), const),
            pl.BlockSpec((TM, 128), lambda bi, i: (i, 0)),
            pl.BlockSpec((TM, 128), lambda bi, i: (i, 0)),
            pl.BlockSpec((1, QW), const),
            pl.BlockSpec((1, KW), const),
            pl.BlockSpec((QW, QW), const),
            pl.BlockSpec((KW, KW), const),
        ],
        out_specs=[
            pl.BlockSpec((1, NH, TM, HD), lambda bi, i: (bi, 0, i, 0)),
            pl.BlockSpec((1, NKV, TM, HD), lambda bi, i: (bi, 0, i, 0)),
            pl.BlockSpec((1, NKV, TM, 2 * HD), lambda bi, i: (bi, 0, i, 0)),
            pl.BlockSpec((1, TM, UW), lambda bi, i: (bi, i, 0)),
        ],
        out_shape=[
            jax.ShapeDtypeStruct((b, NH, s_tot, HD), BF16),
            jax.ShapeDtypeStruct((b, NKV, s_tot, HD), BF16),
            jax.ShapeDtypeStruct((b, NKV, s_tot, 2 * HD), BF16),
            jax.ShapeDtypeStruct((b, s_tot, UW), F32),
        ],
        compiler_params=_params(("arbitrary", "arbitrary")),
        name="in_projection",
    )(xc, modsel, ln_g, w_in, cos, sin, q_g, k_g, bdq, bdk)


def _merge_heads(o):
    return jnp.concatenate([o[g * TQ:(g + 1) * TQ] for g in range(GQ)], axis=1)


def _lane_max(s):
    return functools.reduce(jnp.maximum, [s[:, j * 128:(j + 1) * 128] for j in range(s.shape[1] // 128)])


def _dense_attn_kernel(q_ref, k_ref, v_ref, o_ref, sc_sc, sl_sc, m_sc, *, n_ctx, s_tot):
    qi = pl.program_id(2)
    q = q_ref[0].reshape(GQ * TQ, HD)
    n_lat = (s_tot - n_ctx) // TKL

    def chunks(n):
        spans = [(0, n_ctx)] + [(n_ctx + c * TKL, n_ctx + (c + 1) * TKL) for c in range(n)]
        return list(zip(spans, [sc_sc] + [sl_sc.at[c] for c in range(n)]))

    def scores(n):
        for i, ((lo, hi), slot) in enumerate(chunks(n)):
            s = _dot_nt(q, k_ref[0, 0, lo:hi, :])
            slot[...] = s
            m_sc[i] = _lane_max(s)
        m_lane = functools.reduce(jnp.maximum, [m_sc[i] for i in range(n + 1)])
        m_sc[0] = jnp.broadcast_to(jnp.max(m_lane, axis=1, keepdims=True), m_lane.shape)

    def weighted(n):
        acc = None
        for (lo, hi), slot in chunks(n):
            m = jnp.concatenate([m_sc[0]] * ((hi - lo) // 128), axis=1)
            part = _dot(jnp.exp(slot[...] - m).astype(BF16), v_ref[0, 0, lo:hi, :])
            acc = part if acc is None else acc + part
        o_ref[0] = _merge_heads(acc[:, 0:HD] / acc[:, HD:HD + 1]).astype(BF16)

    is_lat = qi * TQ >= n_ctx
    pl.when(is_lat)(lambda: scores(n_lat))
    pl.when(jnp.logical_not(is_lat))(lambda: scores(0))
    pl.when(is_lat)(lambda: weighted(n_lat))
    pl.when(jnp.logical_not(is_lat))(lambda: weighted(0))


def _dense_attention(q, k, v, n_ctx):
    b, _, s_tot, _ = q.shape
    rows = GQ * TQ
    return pl.pallas_call(
        functools.partial(_dense_attn_kernel, n_ctx=n_ctx, s_tot=s_tot),
        grid=(b, NKV, s_tot // TQ),
        in_specs=[
            pl.BlockSpec((1, GQ, TQ, HD), lambda bi, j, i: (bi, j, i, 0)),
            pl.BlockSpec((1, 1, s_tot, HD), lambda bi, j, i: (bi, j, 0, 0)),
            pl.BlockSpec((1, 1, s_tot, 2 * HD), lambda bi, j, i: (bi, j, 0, 0)),
        ],
        out_specs=pl.BlockSpec((1, TQ, GQ * HD), lambda bi, j, i: (bi, i, j)),
        out_shape=jax.ShapeDtypeStruct((b, s_tot, QW), BF16),
        scratch_shapes=[
            pltpu.VMEM((rows, n_ctx), F32),
            pltpu.VMEM(((s_tot - n_ctx) // TKL, rows, TKL), F32),
            pltpu.VMEM(((s_tot - n_ctx) // TKL + 1, rows, 128), F32),
        ],
        compiler_params=_params(("arbitrary", "arbitrary", "arbitrary")),
        name="dense_attention",
    )(q, k, v)


def _window_attn_kernel(sink_ref, q_ref, k_ref, v_ref, o_ref, *, n_ctx, s_tot):
    qi = pl.program_id(1)
    span = TQ + 2 * WINDOW
    q0 = n_ctx + qi * TQ
    start = pl.multiple_of(jnp.clip(q0 - WINDOW, n_ctx, s_tot - span), 128)
    row = lax.broadcasted_iota(jnp.int32, (GQ * TQ, span), 0)
    col = lax.broadcasted_iota(jnp.int32, (GQ * TQ, span), 1)
    keep = jnp.abs((q0 - start) + (row & (TQ - 1)) - col) <= WINDOW
    outs = []
    for j in range(NKV):
        q = q_ref[0, j * GQ:(j + 1) * GQ].reshape(GQ * TQ, HD)
        s_c = _dot_nt(q, k_ref[0, j, 0:n_ctx, :])
        s_w = jnp.where(keep, _dot_nt(q, k_ref[0, j, pl.ds(start, span), :]), -jnp.inf)
        sink = jnp.concatenate([jnp.full((TQ, 1), sink_ref[j * GQ + g], F32) for g in range(GQ)], axis=0)
        m_lane = jnp.maximum(_lane_max(s_c), _lane_max(s_w))
        m = jnp.maximum(jnp.max(m_lane, axis=1, keepdims=True), sink)
        p_c = jnp.exp(s_c - m).astype(BF16)
        p_w = jnp.exp(s_w - m).astype(BF16)
        acc = _dot(p_c, v_ref[0, j, 0:n_ctx, :]) + _dot(p_w, v_ref[0, j, pl.ds(start, span), :])
        l = acc[:, HD:HD + 1] + jnp.exp(sink - m)
        outs.append(_merge_heads(acc[:, 0:HD] / l))
    o_ref[0] = jnp.concatenate(outs, axis=1).astype(BF16)


def _window_attention(sink, q, k, v, n_ctx):
    b, _, s_tot, _ = q.shape
    s_lat = s_tot - n_ctx
    off = n_ctx // TQ
    return pl.pallas_call(
        functools.partial(_window_attn_kernel, n_ctx=n_ctx, s_tot=s_tot),
        grid_spec=pltpu.PrefetchScalarGridSpec(
            num_scalar_prefetch=1,
            grid=(b, s_lat // TQ),
            in_specs=[
                pl.BlockSpec((1, NH, TQ, HD), lambda bi, i, sk: (bi, 0, i + off, 0)),
                pl.BlockSpec((1, NKV, s_tot, HD), lambda bi, i, sk: (bi, 0, 0, 0)),
                pl.BlockSpec((1, NKV, s_tot, 2 * HD), lambda bi, i, sk: (bi, 0, 0, 0)),
            ],
            out_specs=pl.BlockSpec((1, TQ, QW), lambda bi, i, sk: (bi, i, 0)),
        ),
        out_shape=jax.ShapeDtypeStruct((b, s_lat, QW), BF16),
        compiler_params=_params(("arbitrary", "arbitrary")),
        name="window_attention",
    )(sink, q, k, v)


def _conv_kernel(g_ref, cw_ref, cb_ref, lg_ref, lb_ref, o_ref, pad_sc, *, n_ctx, s_tot):
    zeros = jnp.zeros((HALO, UW), F32)
    pad_sc[0:HALO, :] = zeros
    pad_sc[HALO + n_ctx:2 * HALO + n_ctx, :] = zeros
    pad_sc[2 * HALO + s_tot:3 * HALO + s_tot, :] = zeros

    def fill(i, carry):
        src = pl.multiple_of(i * TM, TM)
        dst = pl.multiple_of(src + HALO + jnp.where(src >= n_ctx, HALO, 0), 8)
        pad_sc[pl.ds(dst, TM), :] = g_ref[0, pl.ds(src, TM), :]
        return carry

    lax.fori_loop(0, s_tot // TM, fill, 0)
    half = CONV_K // 2

    def tile(i, carry):
        src = pl.multiple_of(i * SUB, SUB)
        base = pl.multiple_of(src + jnp.where(src >= n_ctx, HALO, 0), 8)
        win = pad_sc[pl.ds(base, SUB + 2 * HALO), :]
        acc = jnp.zeros((SUB, UW), F32) + cb_ref[...]
        for t in range(CONV_K):
            o = HALO - half + t
            acc = acc + win[o:o + SUB, :] * cw_ref[t:t + 1, :]
        mu = jnp.mean(acc, axis=-1, keepdims=True)
        xc = acc - mu
        var = jnp.mean(xc * xc, axis=-1, keepdims=True)
        yn = xc * lax.rsqrt(var + EPS) * lg_ref[...] + lb_ref[...]
        o_ref[0, pl.ds(src, SUB), :] = (yn * _sigmoid(yn)).astype(BF16)
        return carry

    lax.fori_loop(0, s_tot // SUB, tile, 0)


def _conv_module(glu, conv_w, conv_b, ln_g, ln_b, n_ctx):
    b, s_tot, _ = glu.shape
    const = lambda bi: (0, 0)
    return pl.pallas_call(
        functools.partial(_conv_kernel, n_ctx=n_ctx, s_tot=s_tot),
        grid=(b,),
        in_specs=[
            pl.BlockSpec((1, s_tot, UW), lambda bi: (bi, 0, 0)),
            pl.BlockSpec((CONV_K, UW), const),
            pl.BlockSpec((1, UW), const),
            pl.BlockSpec((1, UW), const),
            pl.BlockSpec((1, UW), const),
        ],
        out_specs=pl.BlockSpec((1, s_tot, UW), lambda bi: (bi, 0, 0)),
        out_shape=jax.ShapeDtypeStruct((b, s_tot, UW), BF16),
        scratch_shapes=[pltpu.VMEM((s_tot + 3 * HALO, UW), F32)],
        compiler_params=_params(("arbitrary",)),
        name="conv_module",
    )(glu, conv_w, conv_b, ln_g, ln_b)


def _pool_kernel(u_ref, pw_ref, ps_ref, o_ref, pad_sc, *, n_ctx, s_lat):
    zeros = jnp.zeros((HALO, UW), F32)
    pad_sc[0:HALO, :] = zeros
    pad_sc[HALO + s_lat:2 * HALO + s_lat, :] = zeros

    def fill(i, carry):
        src = pl.multiple_of(i * TM, TM)
        pad_sc[pl.ds(pl.multiple_of(src + HALO, 8), TM), :] = u_ref[0, pl.ds(pl.multiple_of(src + n_ctx, 8), TM), :]
        return carry

    lax.fori_loop(0, s_lat // TM, fill, 0)
    gw = UW // len(POOL_SIZES)

    def tile(i, carry):
        src = pl.multiple_of(i * SUB, SUB)
        win = pad_sc[pl.ds(src, SUB + 2 * HALO), :]
        t = src + lax.broadcasted_iota(jnp.int32, (SUB, 1), 0)
        outs = []
        for gi, w in enumerate(POOL_SIZES):
            lanes = slice(gi * gw, (gi + 1) * gw)
            tot = jnp.zeros((SUB, gw), F32)
            for d in range(-(w // 2), w - w // 2):
                tot = tot + win[HALO + d:HALO + d + SUB, lanes]
            lo = jnp.clip(t - w // 2, 0, s_lat)
            hi = jnp.clip(t - w // 2 + w, 0, s_lat)
            p = tot / (hi - lo).astype(F32) - win[HALO:HALO + SUB, lanes]
            outs.append(_dot(p.astype(BF16), pw_ref[gi]))
        y = jnp.concatenate(outs, axis=1) * ps_ref[...]
        o_ref[0, pl.ds(src, SUB), :] = y.astype(BF16)
        return carry

    lax.fori_loop(0, s_lat // SUB, tile, 0)


def _pool_mixer(u, pool_w, pool_scale, n_ctx):
    b, s_tot, _ = u.shape
    s_lat = s_tot - n_ctx
    gw = UW // len(POOL_SIZES)
    return pl.pallas_call(
        functools.partial(_pool_kernel, n_ctx=n_ctx, s_lat=s_lat),
        grid=(b,),
        in_specs=[
            pl.BlockSpec((1, s_tot, UW), lambda bi: (bi, 0, 0)),
            pl.BlockSpec((len(POOL_SIZES), gw, gw), lambda bi: (0, 0, 0)),
            pl.BlockSpec((1, UW), lambda bi: (0, 0)),
        ],
        out_specs=pl.BlockSpec((1, s_lat, UW), lambda bi: (bi, 0, 0)),
        out_shape=jax.ShapeDtypeStruct((b, s_lat, UW), BF16),
        scratch_shapes=[pltpu.VMEM((s_lat + 2 * HALO, UW), F32)],
        compiler_params=_params(("arbitrary",)),
        name="pool_mixer",
    )(u, pool_w, pool_scale)


def _route(logits):
    shape = logits.shape
    lane = lax.broadcasted_iota(jnp.int32, shape, 1)
    lanef = lane.astype(F32)
    is_g = (lane >= N_EXP) & (lane < N_EXP + N_GROUPS)
    big = 1e9
    lg = jnp.where(is_g, logits, -jnp.inf)
    gmax = jnp.max(lg, axis=1, keepdims=True)
    gsel = jnp.min(jnp.where(is_g & (lg == gmax), lanef, big), axis=1, keepdims=True) - float(N_EXP)
    gsum = jnp.sum(jnp.exp(lg - gmax), axis=1, keepdims=True)
    g_w = 1.0 / gsum
    in_grp = (lane < N_EXP) & ((lane >> 3).astype(F32) == gsel)
    el = jnp.where(in_grp, logits, -jnp.inf)
    emax = jnp.max(el, axis=1, keepdims=True)
    ee = jnp.exp(el - emax)
    p = ee / jnp.sum(ee, axis=1, keepdims=True)
    p1 = jnp.max(jnp.where(in_grp, p, -1.0), axis=1, keepdims=True)
    i1 = jnp.min(jnp.where(in_grp & (p == p1), lanef, big), axis=1, keepdims=True)
    rest = in_grp & (lanef != i1)
    p2 = jnp.max(jnp.where(rest, p, -1.0), axis=1, keepdims=True)
    i2 = jnp.min(jnp.where(rest & (p == p2), lanef, big), axis=1, keepdims=True)
    w1 = g_w * p1 / (p1 + p2)
    w2 = g_w * p2 / (p1 + p2)
    onehot = jnp.where((lanef == i1) | (lanef == i2), 1.0, 0.0)
    return lanef, i1, i2, w1, w2, onehot


def _outproj_kernel(a_ref, b_ref, x_ref, mod_ref, w_ref, lng_ref, rw_ref, rb_ref, tri_ref,
                    x1_ref, h2_ref, r_ref, cnt_ref, cnt_sc):
    @pl.when((pl.program_id(0) == 0) & (pl.program_id(1) == 0))
    def _():
        cnt_sc[...] = jnp.zeros(cnt_sc.shape, F32)

    mod = mod_ref[0, 0]
    y = _dot(a_ref[0], w_ref[0:QW, :]) + _dot(b_ref[0], w_ref[QW:QW + UW, :])
    x1 = x_ref[0] + mod[:, 2 * D:3 * D] * y
    x1_ref[0] = x1
    h2 = _rmsnorm(x1, lng_ref[...]) * (1.0 + mod[:, 4 * D:5 * D]) + mod[:, 3 * D:4 * D]
    h2_ref[0] = h2
    logits = _dot3(h2, rw_ref[...]) + rb_ref[...]
    lanef, i1, i2, w1, w2, onehot = _route(logits)
    before = _dot(tri_ref[...], onehot.astype(BF16)) + cnt_sc[...]
    r1 = jnp.sum(jnp.where(lanef == i1, before, 0.0), axis=1, keepdims=True)
    r2 = jnp.sum(jnp.where(lanef == i2, before, 0.0), axis=1, keepdims=True)
    cnt_sc[...] = cnt_sc[...] + jnp.sum(onehot, axis=0, keepdims=True)
    cnt_ref[...] = cnt_sc[...]
    info = jnp.where(lanef == 0.0, i1, jnp.where(lanef == 1.0, i2, jnp.where(lanef == 2.0, w1, jnp.where(
        lanef == 3.0, w2, jnp.where(lanef == 4.0, r1, jnp.where(lanef == 5.0, r2, 0.0))))))
    r_ref[0] = info


def _out_project(a, bmix, xc, modsel, w_out, ln_g, rw, rb, tri, *, tile_off):
    b, s_out, _ = a.shape
    nt = s_out // TM
    const = lambda bi, i: (0, 0)
    sel = (lambda i: jnp.minimum(i, 1)) if tile_off == 0 else (lambda i: 1)
    tok = lambda bi, i: (bi, i, 0)
    return pl.pallas_call(
        _outproj_kernel,
        grid=(b, nt),
        in_specs=[
            pl.BlockSpec((1, TM, QW), tok),
            pl.BlockSpec((1, TM, UW), tok),
            pl.BlockSpec((1, TM, D), lambda bi, i: (bi, i + tile_off, 0)),
            pl.BlockSpec((1, 1, 1, 6 * D), lambda bi, i: (bi, sel(i), 0, 0)),
            pl.BlockSpec((D, D), const),
            pl.BlockSpec((1, D), const),
            pl.BlockSpec((D, 128), const),
            pl.BlockSpec((1, 128), const),
            pl.BlockSpec((TM, TM), const),
        ],
        out_specs=[
            pl.BlockSpec((1, TM, D), tok),
            pl.BlockSpec((1, TM, D), tok),
            pl.BlockSpec((1, TM, 128), tok),
            pl.BlockSpec((1, 128), const),
        ],
        out_shape=[
            jax.ShapeDtypeStruct((b, s_out, D), F32),
            jax.ShapeDtypeStruct((b, s_out, D), F32),
            jax.ShapeDtypeStruct((b, s_out, 128), F32),
            jax.ShapeDtypeStruct((1, 128), F32),
        ],
        scratch_shapes=[pltpu.VMEM((1, 128), F32)],
        compiler_params=_params(("arbitrary", "arbitrary")),
        name="out_projection_router",
    )(a, bmix, xc, modsel, w_out, ln_g, rw, rb, tri)


def _row_copy(src_ref, src_row, dst_ref, dst_row, sem):
    return pltpu.make_async_copy(src_ref.at[pl.ds(src_row, 1), :], dst_ref.at[pl.ds(dst_row, 1), :], sem)


def _dispatch_kernel(pos_ref, h_ref, xs_in_ref, xs_ref, sem):
    del xs_in_ref
    i = pl.program_id(0)

    def issue(r, carry):
        for k in range(2):
            _row_copy(h_ref, r, xs_ref, pos_ref[(i * TM + r) * 2 + k], sem).start()
        return carry

    lax.fori_loop(0, TM, issue, 0)

    def wait(r, carry):
        _row_copy(h_ref, 0, xs_ref, 0, sem).wait()
        return carry

    lax.fori_loop(0, 2 * TM, wait, 0)


def _dispatch(pos, h2, cap):
    t = h2.shape[0]
    xs0 = jnp.zeros((cap, D), F32)
    return pl.pallas_call(
        _dispatch_kernel,
        grid_spec=pltpu.PrefetchScalarGridSpec(
            num_scalar_prefetch=1,
            grid=(t // TM,),
            in_specs=[
                pl.BlockSpec((TM, D), lambda i, pos: (i, 0)),
                pl.BlockSpec(memory_space=pl.ANY),
            ],
            out_specs=pl.BlockSpec(memory_space=pl.ANY),
            scratch_shapes=[pltpu.SemaphoreType.DMA],
        ),
        out_shape=jax.ShapeDtypeStruct((cap, D), F32),
        input_output_aliases={2: 0},
        compiler_params=_params(("arbitrary",)),
        name="moe_dispatch",
    )(pos, h2, xs0)


def _expert_kernel(be_ref, nu_ref, x_ref, wg_ref, wu_ref, wd_ref, y_ref):
    del be_ref

    used = pl.program_id(0) < nu_ref[0]

    @pl.when(used)
    def _():
        xb = x_ref[...].astype(BF16)
        g = _dot(xb, wg_ref[0])
        u = _dot(xb, wu_ref[0])
        hid = (g * _sigmoid(g)) * u
        y_ref[...] = _dot(hid.astype(BF16), wd_ref[0])

    @pl.when(jnp.logical_not(used))
    def _():
        y_ref[...] = jnp.zeros(y_ref.shape, F32)


def _experts(blk_e, n_used, xs, wg, wu, wd):
    cap = xs.shape[0]
    row = lambda i, be, nu: (jnp.minimum(i, nu[0] - 1), 0)
    wsel = lambda i, be, nu: (be[i], 0, 0)
    return pl.pallas_call(
        _expert_kernel,
        grid_spec=pltpu.PrefetchScalarGridSpec(
            num_scalar_prefetch=2,
            grid=(cap // BLK,),
            in_specs=[
                pl.BlockSpec((BLK, D), row),
                pl.BlockSpec((1, D, D_EXP), wsel),
                pl.BlockSpec((1, D, D_EXP), wsel),
                pl.BlockSpec((1, D_EXP, D), wsel),
            ],
            out_specs=pl.BlockSpec((BLK, D), lambda i, be, nu: (i, 0)),
        ),
        out_shape=jax.ShapeDtypeStruct((cap, D), F32),
        compiler_params=_params(("arbitrary",)),
        name="moe_experts",
    )(blk_e, n_used, xs, wg, wu, wd)


def _combine_kernel(pos_ref, y_ref, x1_ref, r_ref, mod_ref, o_ref, buf, sem):
    i = pl.program_id(0)

    def issue(r, carry):
        for k in range(2):
            _row_copy(y_ref, pos_ref[(i * TM + r) * 2 + k], buf.at[k], r, sem).start()
        return carry

    lax.fori_loop(0, TM, issue, 0)

    def wait(r, carry):
        _row_copy(y_ref, 0, buf.at[0], 0, sem).wait()
        return carry

    lax.fori_loop(0, 2 * TM, wait, 0)
    info = r_ref[...]
    g2 = mod_ref[0, 0][:, 5 * D:6 * D]
    o_ref[...] = x1_ref[...] + g2 * (info[:, 2:3] * buf[0] + info[:, 3:4] * buf[1])


def _combine(pos, ys, x1, rinfo, modsel, *, tiles_per_batch, with_ctx):
    t = x1.shape[0]
    if with_ctx:
        msel = lambda i, pos: (i // tiles_per_batch, jnp.minimum(i % tiles_per_batch, 1), 0, 0)
    else:
        msel = lambda i, pos: (i // tiles_per_batch, 1, 0, 0)
    tok = lambda i, pos: (i, 0)
    return pl.pallas_call(
        _combine_kernel,
        grid_spec=pltpu.PrefetchScalarGridSpec(
            num_scalar_prefetch=1,
            grid=(t // TM,),
            in_specs=[
                pl.BlockSpec(memory_space=pl.ANY),
                pl.BlockSpec((TM, D), tok),
                pl.BlockSpec((TM, 128), tok),
                pl.BlockSpec((1, 1, 1, 6 * D), msel),
            ],
            out_specs=pl.BlockSpec((TM, D), tok),
            scratch_shapes=[pltpu.VMEM((2, TM, D), F32), pltpu.SemaphoreType.DMA],
        ),
        out_shape=jax.ShapeDtypeStruct((t, D), F32),
        compiler_params=_params(("arbitrary",)),
        name="moe_combine",
    )(pos, ys, x1, rinfo, modsel)


def _moe(x1, h2, rinfo, cnt, modsel, wg, wu, wd, *, with_ctx):
    b, s, _ = x1.shape
    t = b * s
    n_asg = 2 * t
    n_blk = -(-(n_asg + N_EXP * (BLK - 1)) // BLK)
    cap = n_blk * BLK
    info = rinfo.reshape(t, 128)
    eid = info[:, 0:2].astype(jnp.int32)
    rank = info[:, 4:6].astype(jnp.int32)
    counts = cnt[0, 0:N_EXP].astype(jnp.int32)
    padded = (counts + BLK - 1) // BLK * BLK
    pend = jnp.cumsum(padded)
    pstart = pend - padded
    pos = (pstart[eid] + rank).reshape(n_asg)
    blk_e = jnp.minimum(jnp.searchsorted(pend, jnp.arange(n_blk, dtype=jnp.int32) * BLK, side="right"),
                        N_EXP - 1).astype(jnp.int32)
    n_used = (pend[N_EXP - 1:N_EXP] // BLK).astype(jnp.int32)
    xs = _dispatch(pos, h2.reshape(t, D), cap)
    ys = _experts(blk_e, n_used, xs, wg, wu, wd)
    out = _combine(pos, ys, x1.reshape(t, D), info, modsel, tiles_per_batch=s // TM, with_ctx=with_ctx)
    return out.reshape(b, s, D)


def _rope_tables(n_ctx, s_lat):
    rows = s_lat // GRID_W
    row = jnp.repeat(jnp.arange(rows, dtype=F32), GRID_W)
    col = jnp.tile(jnp.arange(GRID_W, dtype=F32), rows)
    n_freq = HD // 4
    inv = ROPE_THETA ** (-jnp.arange(n_freq, dtype=F32) / n_freq)
    ang = jnp.concatenate([row[:, None] * inv, col[:, None] * inv], axis=-1)
    cos = jnp.concatenate([jnp.ones((n_ctx, HD // 2), F32), jnp.cos(ang)], axis=0)
    sin = jnp.concatenate([jnp.zeros((n_ctx, HD // 2), F32), jnp.sin(ang)], axis=0)
    cos = jnp.concatenate([cos, cos, cos, cos], axis=1)
    sin = jnp.concatenate([-sin, sin, -sin, sin], axis=1)
    return cos, sin


def _block_diag_ones(n):
    i = jnp.arange(n) // HD
    return (i[:, None] == i[None, :]).astype(BF16)


def kernel(x, c, ctx, c_ctx, mod_w, mod_b, ln1_g, ln2_g, w_in_ab, w_out_ab, q_norm_a, k_norm_a, conv_w, conv_b,
           conv_ln_g, conv_ln_b, w_in_cd, w_out_cd, q_norm_c, k_norm_c, sink_c, pool_w, pool_scale,
           rt_grp_w, rt_grp_b, rt_exp_w, rt_exp_b, ex_gate, ex_up, ex_down):
    b, s_lat, _ = x.shape
    n_ctx = ctx.shape[1]
    assert n_ctx == TM and s_lat % TM == 0 and b <= 8

    cc = jnp.zeros((16, D), F32).at[0:b].set(c).at[8].set(c_ctx)
    mod = _modulation(cc, mod_w, mod_b)

    def modsel(l):
        return jnp.stack([jnp.broadcast_to(mod[l, 8], (b, 6 * D)), mod[l, 0:b]], axis=1).reshape(b, 2, 1, 6 * D)

    cos, sin = _rope_tables(n_ctx, s_lat)
    bdq = _block_diag_ones(QW)
    bdk = _block_diag_ones(KW)
    tri = (jnp.arange(TM)[:, None] > jnp.arange(TM)[None, :]).astype(BF16)

    def router(l):
        rw = jnp.zeros((D, 128), F32).at[:, 0:N_EXP].set(rt_exp_w[l]).at[:, N_EXP:N_EXP + N_GROUPS].set(rt_grp_w[l])
        rb = jnp.zeros((1, 128), F32).at[0, 0:N_EXP].set(rt_exp_b[l]).at[0, N_EXP:N_EXP + N_GROUPS].set(rt_grp_b[l])
        return rw, rb

    def tile_gain(g, n):
        return jnp.tile(g, n).reshape(1, n * HD)

    xc = jnp.concatenate([ctx, x], axis=1)

    ms = modsel(0)
    q, k, v, glu = _project(xc, ms, ln1_g[0:1], w_in_ab[0].astype(BF16), cos, sin,
                            tile_gain(q_norm_a[0], NH), tile_gain(k_norm_a[0], NKV), bdq, bdk, glu=True)
    att = _dense_attention(q, k, v, n_ctx)
    cv = _conv_module(glu, conv_w[0], conv_b[0:1], conv_ln_g[0:1], conv_ln_b[0:1], n_ctx)
    rw, rb = router(0)
    x1, h2, rinfo, cnt = _out_project(att, cv, xc, ms, w_out_ab[0].astype(BF16), ln2_g[0:1], rw, rb, tri, tile_off=0)
    xc = _moe(x1, h2, rinfo, cnt, ms, ex_gate[0].astype(BF16), ex_up[0].astype(BF16), ex_down[0].astype(BF16),
              with_ctx=True)

    ms = modsel(1)
    q, k, v, u = _project(xc, ms, ln1_g[1:2], w_in_cd[0].astype(BF16), cos, sin,
                          tile_gain(q_norm_c[0], NH), tile_gain(k_norm_c[0], NKV), bdq, bdk, glu=False)
    att = _window_attention(sink_c[0], q, k, v, n_ctx)
    pm = _pool_mixer(u, pool_w[0].astype(BF16), pool_scale[0:1], n_ctx)
    rw, rb = router(1)
    x1, h2, rinfo, cnt = _out_project(att, pm, xc, ms, w_out_cd[0].astype(BF16), ln2_g[1:2], rw, rb, tri,
                                      tile_off=n_ctx // TM)
    return _moe(x1, h2, rinfo, cnt, ms, ex_gate[1].astype(BF16), ex_up[1].astype(BF16), ex_down[1].astype(BF16),
                with_ctx=False)
```

```python
import functools

import jax
import jax.numpy as jnp
from jax import lax
from jax.experimental import pallas as pl
from jax.experimental.pallas import tpu as pltpu

F32 = jnp.float32
BF16 = jnp.bfloat16

D = 1024
HD = 64
NH = 8
NKV = 2
GQ = NH // NKV
QW = NH * HD
KW = NKV * HD
UW = 512
EPS = 1e-6
ROPE_THETA = 10000.0
GRID_W = 64
CONV_K = 31
WINDOW = 128
POOL_SIZES = (2, 4, 8, 16)
N_GROUPS = 4
PER_GROUP = 8
N_EXP = N_GROUPS * PER_GROUP
D_EXP = D // 2

TM = 256
TQ = 128
TKL = 512
SUB = 64
HALO = 16
BLK = 256
GRP = 8
NG = 96
LR = NG * GRP
VMEM_LIMIT = 56 * 1024 * 1024


def _sigmoid(x):
    return 1.0 / (1.0 + jnp.exp(-x))


def _dot(a, b):
    return jnp.dot(a, b, preferred_element_type=F32)


def _dot_nt(a, b):
    return lax.dot_general(a, b, (((1,), (1,)), ((), ())), preferred_element_type=F32)


def _split(a):
    hi = a.astype(BF16)
    lo = (a - hi.astype(F32)).astype(BF16)
    return hi, lo


def _dot3(a, w):
    a_hi, a_lo = _split(a)
    w_hi, w_lo = _split(w)
    return _dot(a_hi, w_hi) + _dot(a_lo, w_hi) + _dot(a_hi, w_lo)


def _rmsnorm(x, g):
    return x * lax.rsqrt(jnp.mean(x * x, axis=-1, keepdims=True) + EPS) * g


def _params(sem, vmem=VMEM_LIMIT):
    return pltpu.CompilerParams(dimension_semantics=sem, vmem_limit_bytes=vmem)


def _mod_kernel(c_ref, w_ref, b_ref, o_ref):
    a = c_ref[...]
    a = a * _sigmoid(a)
    o_ref[0] = _dot3(a, w_ref[0]) + b_ref[0]


def _modulation(cc, mod_w, mod_b):
    depth = mod_w.shape[0]
    tn = 1024
    return pl.pallas_call(
        _mod_kernel,
        grid=(depth, 6 * D // tn),
        in_specs=[
            pl.BlockSpec((16, D), lambda l, j: (0, 0)),
            pl.BlockSpec((1, D, tn), lambda l, j: (l, 0, j)),
            pl.BlockSpec((1, 1, tn), lambda l, j: (l, 0, j)),
        ],
        out_specs=pl.BlockSpec((1, 16, tn), lambda l, j: (l, 0, j)),
        out_shape=jax.ShapeDtypeStruct((depth, 16, 6 * D), F32),
        compiler_params=_params(("arbitrary", "arbitrary")),
        name="modulation",
    )(cc, mod_w, mod_b.reshape(depth, 1, 6 * D))


def _swap_halves(t):
    w = t.shape[1]
    lane = lax.broadcasted_iota(jnp.int32, t.shape, 1)
    first = (lane & (HD - 1)) < (HD // 2)
    return jnp.where(first, pltpu.roll(t, w - HD // 2, 1), pltpu.roll(t, HD // 2, 1))


def _head_norm_rope(t, bd, g, cos, sin, scale):
    hi, lo = _split(t * t)
    ssq = _dot(hi, bd) + _dot(lo, bd)
    tn = t * lax.rsqrt(ssq * (1.0 / HD) + EPS) * g
    n = t.shape[1] // 128
    if n > 1:
        cos = jnp.concatenate([cos] * n, axis=1)
        sin = jnp.concatenate([sin] * n, axis=1)
    out = tn * cos + _swap_halves(tn) * sin
    return out * scale if scale != 1.0 else out


def _proj_kernel(x_ref, mod_ref, lng_ref, w_ref, cos_ref, sin_ref, qg_ref, kg_ref, bdq_ref, bdk_ref,
                 q_ref, k_ref, v_ref, u_ref, *, glu):
    mod = mod_ref[0, 0]
    h = _rmsnorm(x_ref[0], lng_ref[...]) * (1.0 + mod[:, D:2 * D]) + mod[:, 0:D]
    px = _dot(h.astype(BF16), w_ref[...])
    cos = cos_ref[...]
    sin = sin_ref[...]
    q = _head_norm_rope(px[:, 0:QW], bdq_ref[...], qg_ref[...], cos, sin, HD ** -0.5)
    k = _head_norm_rope(px[:, QW:QW + KW], bdk_ref[...], kg_ref[...], cos, sin, 1.0)
    v = px[:, QW + KW:QW + 2 * KW]
    for hh in range(NH):
        q_ref[0, hh] = q[:, hh * HD:(hh + 1) * HD].astype(BF16)
    lane = lax.broadcasted_iota(jnp.int32, v.shape, 1)
    ones_col = jnp.where(lane == HD, 1.0, 0.0)
    for j in range(NKV):
        k_ref[0, j] = k[:, j * HD:(j + 1) * HD].astype(BF16)
        vj = v if j == 0 else pltpu.roll(v, KW - j * HD, 1)
        v_ref[0, j] = jnp.where(lane < HD, vj, ones_col).astype(BF16)
    o_u = QW + 2 * KW
    if glu:
        u_ref[0] = px[:, o_u:o_u + UW] * _sigmoid(px[:, o_u + UW:o_u + 2 * UW])
    else:
        u_ref[0] = px[:, o_u:o_u + UW]


def _project(xc, modsel, ln_g, w_in, cos, sin, q_g, k_g, bdq, bdk, *, glu):
    b, s_tot, _ = xc.shape
    n_in = w_in.shape[1]
    nt = s_tot // TM
    const = lambda bi, i: (0, 0)
    return pl.pallas_call(
        functools.partial(_proj_kernel, glu=glu),
        grid=(b, nt),
        in_specs=[
            pl.BlockSpec((1, TM, D), lambda bi, i: (bi, i, 0)),
            pl.BlockSpec((1, 1, 1, 6 * D), lambda bi, i: (bi, jnp.minimum(i, 1), 0, 0)),
            pl.BlockSpec((1, D), const),
            pl.BlockSpec((D, n_in), const),
            pl.BlockSpec((TM, 128), lambda bi, i: (i, 0)),
            pl.BlockSpec((TM, 128), lambda bi, i: (i, 0)),
            pl.BlockSpec((1, QW), const),
            pl.BlockSpec((1, KW), const),
            pl.BlockSpec((QW, QW), const),
            pl.BlockSpec((KW, KW), const),
        ],
        out_specs=[
            pl.BlockSpec((1, NH, TM, HD), lambda bi, i: (bi, 0, i, 0)),
            pl.BlockSpec((1, NKV, TM, HD), lambda bi, i: (bi, 0, i, 0)),
            pl.BlockSpec((1, NKV, TM, 2 * HD), lambda bi, i: (bi, 0, i, 0)),
            pl.BlockSpec((1, TM, UW), lambda bi, i: (bi, i, 0)),
        ],
        out_shape=[
            jax.ShapeDtypeStruct((b, NH, s_tot, HD), BF16),
            jax.ShapeDtypeStruct((b, NKV, s_tot, HD), BF16),
            jax.ShapeDtypeStruct((b, NKV, s_tot, 2 * HD), BF16),
            jax.ShapeDtypeStruct((b, s_tot, UW), F32),
        ],
        compiler_params=_params(("arbitrary", "arbitrary")),
        name="in_projection",
    )(xc, modsel, ln_g, w_in, cos, sin, q_g, k_g, bdq, bdk)


def _merge_heads(o):
    return jnp.concatenate([o[g * TQ:(g + 1) * TQ] for g in range(GQ)], axis=1)


def _lane_max(s):
    return functools.reduce(jnp.maximum, [s[:, j * 128:(j + 1) * 128] for j in range(s.shape[1] // 128)])


def _dense_attn_kernel(q_ref, k_ref, v_ref, o_ref, sc_sc, sl_sc, m_sc, *, n_ctx, s_tot):
    qi = pl.program_id(2)
    q = q_ref[0].reshape(GQ * TQ, HD)
    n_lat = (s_tot - n_ctx) // TKL

    def chunks(n):
        spans = [(0, n_ctx)] + [(n_ctx + c * TKL, n_ctx + (c + 1) * TKL) for c in range(n)]
        return list(zip(spans, [sc_sc] + [sl_sc.at[c] for c in range(n)]))

    def scores(n):
        for i, ((lo, hi), slot) in enumerate(chunks(n)):
            s = _dot_nt(q, k_ref[0, 0, lo:hi, :])
            slot[...] = s
            m_sc[i] = _lane_max(s)
        m_lane = functools.reduce(jnp.maximum, [m_sc[i] for i in range(n + 1)])
        m_sc[0] = jnp.broadcast_to(jnp.max(m_lane, axis=1, keepdims=True), m_lane.shape)

    def weighted(n):
        acc = None
        for (lo, hi), slot in chunks(n):
            m = jnp.concatenate([m_sc[0]] * ((hi - lo) // 128), axis=1)
            part = _dot(jnp.exp(slot[...] - m).astype(BF16), v_ref[0, 0, lo:hi, :])
            acc = part if acc is None else acc + part
        o_ref[0] = _merge_heads(acc[:, 0:HD] / acc[:, HD:HD + 1]).astype(BF16)

    is_lat = qi * TQ >= n_ctx
    pl.when(is_lat)(lambda: scores(n_lat))
    pl.when(jnp.logical_not(is_lat))(lambda: scores(0))
    pl.when(is_lat)(lambda: weighted(n_lat))
    pl.when(jnp.logical_not(is_lat))(lambda: weighted(0))


def _dense_attention(q, k, v, n_ctx):
    b, _, s_tot, _ = q.shape
    rows = GQ * TQ
    return pl.pallas_call(
        functools.partial(_dense_attn_kernel, n_ctx=n_ctx, s_tot=s_tot),
        grid=(b, NKV, s_tot // TQ),
        in_specs=[
            pl.BlockSpec((1, GQ, TQ, HD), lambda bi, j, i: (bi, j, i, 0)),
            pl.BlockSpec((1, 1, s_tot, HD), lambda bi, j, i: (bi, j, 0, 0)),
            pl.BlockSpec((1, 1, s_tot, 2 * HD), lambda bi, j, i: (bi, j, 0, 0)),
        ],
        out_specs=pl.BlockSpec((1, TQ, GQ * HD), lambda bi, j, i: (bi, i, j)),
        out_shape=jax.ShapeDtypeStruct((b, s_tot, QW), BF16),
        scratch_shapes=[
            pltpu.VMEM((rows, n_ctx), F32),
            pltpu.VMEM(((s_tot - n_ctx) // TKL, rows, TKL), F32),
            pltpu.VMEM(((s_tot - n_ctx) // TKL + 1, rows, 128), F32),
        ],
        compiler_params=_params(("arbitrary", "arbitrary", "arbitrary")),
        name="dense_attention",
    )(q, k, v)


def _window_attn_kernel(sink_ref, q_ref, k_ref, v_ref, o_ref, *, n_ctx, s_tot):
    qi = pl.program_id(1)
    span = TQ + 2 * WINDOW
    q0 = n_ctx + qi * TQ
    start = pl.multiple_of(jnp.clip(q0 - WINDOW, n_ctx, s_tot - span), 128)
    row = lax.broadcasted_iota(jnp.int32, (GQ * TQ, span), 0)
    col = lax.broadcasted_iota(jnp.int32, (GQ * TQ, span), 1)
    keep = jnp.abs((q0 - start) + (row & (TQ - 1)) - col) <= WINDOW
    outs = []
    for j in range(NKV):
        q = q_ref[0, j * GQ:(j + 1) * GQ].reshape(GQ * TQ, HD)
        s_c = _dot_nt(q, k_ref[0, j, 0:n_ctx, :])
        s_w = jnp.where(keep, _dot_nt(q, k_ref[0, j, pl.ds(start, span), :]), -jnp.inf)
        sink = jnp.concatenate([jnp.full((TQ, 1), sink_ref[j * GQ + g], F32) for g in range(GQ)], axis=0)
        m_lane = jnp.maximum(_lane_max(s_c), _lane_max(s_w))
        m = jnp.maximum(jnp.max(m_lane, axis=1, keepdims=True), sink)
        p_c = jnp.exp(s_c - m).astype(BF16)
        p_w = jnp.exp(s_w - m).astype(BF16)
        acc = _dot(p_c, v_ref[0, j, 0:n_ctx, :]) + _dot(p_w, v_ref[0, j, pl.ds(start, span), :])
        l = acc[:, HD:HD + 1] + jnp.exp(sink - m)
        outs.append(_merge_heads(acc[:, 0:HD] / l))
    o_ref[0] = jnp.concatenate(outs, axis=1).astype(BF16)


def _window_attention(sink, q, k, v, n_ctx):
    b, _, s_tot, _ = q.shape
    s_lat = s_tot - n_ctx
    off = n_ctx // TQ
    return pl.pallas_call(
        functools.partial(_window_attn_kernel, n_ctx=n_ctx, s_tot=s_tot),
        grid_spec=pltpu.PrefetchScalarGridSpec(
            num_scalar_prefetch=1,
            grid=(b, s_lat // TQ),
            in_specs=[
                pl.BlockSpec((1, NH, TQ, HD), lambda bi, i, sk: (bi, 0, i + off, 0)),
                pl.BlockSpec((1, NKV, s_tot, HD), lambda bi, i, sk: (bi, 0, 0, 0)),
                pl.BlockSpec((1, NKV, s_tot, 2 * HD), lambda bi, i, sk: (bi, 0, 0, 0)),
            ],
            out_specs=pl.BlockSpec((1, TQ, QW), lambda bi, i, sk: (bi, i, 0)),
        ),
        out_shape=jax.ShapeDtypeStruct((b, s_lat, QW), BF16),
        compiler_params=_params(("arbitrary", "arbitrary")),
        name="window_attention",
    )(sink, q, k, v)


def _conv_kernel(g_ref, cw_ref, cb_ref, lg_ref, lb_ref, o_ref, pad_sc, *, n_ctx, s_tot):
    zeros = jnp.zeros((HALO, UW), F32)
    pad_sc[0:HALO, :] = zeros
    pad_sc[HALO + n_ctx:2 * HALO + n_ctx, :] = zeros
    pad_sc[2 * HALO + s_tot:3 * HALO + s_tot, :] = zeros

    def fill(i, carry):
        src = pl.multiple_of(i * TM, TM)
        dst = pl.multiple_of(src + HALO + jnp.where(src >= n_ctx, HALO, 0), 8)
        pad_sc[pl.ds(dst, TM), :] = g_ref[0, pl.ds(src, TM), :]
        return carry

    lax.fori_loop(0, s_tot // TM, fill, 0)
    half = CONV_K // 2

    def tile(i, carry):
        src = pl.multiple_of(i * SUB, SUB)
        base = pl.multiple_of(src + jnp.where(src >= n_ctx, HALO, 0), 8)
        win = pad_sc[pl.ds(base, SUB + 2 * HALO), :]
        acc = jnp.zeros((SUB, UW), F32) + cb_ref[...]
        for t in range(CONV_K):
            o = HALO - half + t
            acc = acc + win[o:o + SUB, :] * cw_ref[t:t + 1, :]
        mu = jnp.mean(acc, axis=-1, keepdims=True)
        xc = acc - mu
        var = jnp.mean(xc * xc, axis=-1, keepdims=True)
        yn = xc * lax.rsqrt(var + EPS) * lg_ref[...] + lb_ref[...]
        o_ref[0, pl.ds(src, SUB), :] = (yn * _sigmoid(yn)).astype(BF16)
        return carry

    lax.fori_loop(0, s_tot // SUB, tile, 0)


def _conv_module(glu, conv_w, conv_b, ln_g, ln_b, n_ctx):
    b, s_tot, _ = glu.shape
    const = lambda bi: (0, 0)
    return pl.pallas_call(
        functools.partial(_conv_kernel, n_ctx=n_ctx, s_tot=s_tot),
        grid=(b,),
        in_specs=[
            pl.BlockSpec((1, s_tot, UW), lambda bi: (bi, 0, 0)),
            pl.BlockSpec((CONV_K, UW), const),
            pl.BlockSpec((1, UW), const),
            pl.BlockSpec((1, UW), const),
            pl.BlockSpec((1, UW), const),
        ],
        out_specs=pl.BlockSpec((1, s_tot, UW), lambda bi: (bi, 0, 0)),
        out_shape=jax.ShapeDtypeStruct((b, s_tot, UW), BF16),
        scratch_shapes=[pltpu.VMEM((s_tot + 3 * HALO, UW), F32)],
        compiler_params=_params(("arbitrary",)),
        name="conv_module",
    )(glu, conv_w, conv_b, ln_g, ln_b)


def _pool_kernel(u_ref, pw_ref, ps_ref, o_ref, pad_sc, *, n_ctx, s_lat):
    zeros = jnp.zeros((HALO, UW), F32)
    pad_sc[0:HALO, :] = zeros
    pad_sc[HALO + s_lat:2 * HALO + s_lat, :] = zeros

    def fill(i, carry):
        src = pl.multiple_of(i * TM, TM)
        pad_sc[pl.ds(pl.multiple_of(src + HALO, 8), TM), :] = u_ref[0, pl.ds(pl.multiple_of(src + n_ctx, 8), TM), :]
        return carry

    lax.fori_loop(0, s_lat // TM, fill, 0)
    gw = UW // len(POOL_SIZES)

    def tile(i, carry):
        src = pl.multiple_of(i * SUB, SUB)
        win = pad_sc[pl.ds(src, SUB + 2 * HALO), :]
        t = src + lax.broadcasted_iota(jnp.int32, (SUB, 1), 0)
        outs = []
        for gi, w in enumerate(POOL_SIZES):
            lanes = slice(gi * gw, (gi + 1) * gw)
            tot = jnp.zeros((SUB, gw), F32)
            for d in range(-(w // 2), w - w // 2):
                tot = tot + win[HALO + d:HALO + d + SUB, lanes]
            lo = jnp.clip(t - w // 2, 0, s_lat)
            hi = jnp.clip(t - w // 2 + w, 0, s_lat)
            p = tot / (hi - lo).astype(F32) - win[HALO:HALO + SUB, lanes]
            outs.append(_dot(p.astype(BF16), pw_ref[gi]))
        y = jnp.concatenate(outs, axis=1) * ps_ref[...]
        o_ref[0, pl.ds(src, SUB), :] = y.astype(BF16)
        return carry

    lax.fori_loop(0, s_lat // SUB, tile, 0)


def _pool_mixer(u, pool_w, pool_scale, n_ctx):
    b, s_tot, _ = u.shape
    s_lat = s_tot - n_ctx
    gw = UW // len(POOL_SIZES)
    return pl.pallas_call(
        functools.partial(_pool_kernel, n_ctx=n_ctx, s_lat=s_lat),
        grid=(b,),
        in_specs=[
            pl.BlockSpec((1, s_tot, UW), lambda bi: (bi, 0, 0)),
            pl.BlockSpec((len(POOL_SIZES), gw, gw), lambda bi: (0, 0, 0)),
            pl.BlockSpec((1, UW), lambda bi: (0, 0)),
        ],
        out_specs=pl.BlockSpec((1, s_lat, UW), lambda bi: (bi, 0, 0)),
        out_shape=jax.ShapeDtypeStruct((b, s_lat, UW), BF16),
        scratch_shapes=[pltpu.VMEM((s_lat + 2 * HALO, UW), F32)],
        compiler_params=_params(("arbitrary",)),
        name="pool_mixer",
    )(u, pool_w, pool_scale)


def _route(logits):
    shape = logits.shape
    lane = lax.broadcasted_iota(jnp.int32, shape, 1)
    lanef = lane.astype(F32)
    is_g = (lane >= N_EXP) & (lane < N_EXP + N_GROUPS)
    big = 1e9
    lg = jnp.where(is_g, logits, -jnp.inf)
    gmax = jnp.max(lg, axis=1, keepdims=True)
    gsel = jnp.min(jnp.where(is_g & (lg == gmax), lanef, big), axis=1, keepdims=True) - float(N_EXP)
    gsum = jnp.sum(jnp.exp(lg - gmax), axis=1, keepdims=True)
    g_w = 1.0 / gsum
    in_grp = (lane < N_EXP) & ((lane >> 3).astype(F32) == gsel)
    el = jnp.where(in_grp, logits, -jnp.inf)
    emax = jnp.max(el, axis=1, keepdims=True)
    ee = jnp.exp(el - emax)
    p = ee / jnp.sum(ee, axis=1, keepdims=True)
    p1 = jnp.max(jnp.where(in_grp, p, -1.0), axis=1, keepdims=True)
    i1 = jnp.min(jnp.where(in_grp & (p == p1), lanef, big), axis=1, keepdims=True)
    rest = in_grp & (lanef != i1)
    p2 = jnp.max(jnp.where(rest, p, -1.0), axis=1, keepdims=True)
    i2 = jnp.min(jnp.where(rest & (p == p2), lanef, big), axis=1, keepdims=True)
    w1 = g_w * p1 / (p1 + p2)
    w2 = g_w * p2 / (p1 + p2)
    onehot = jnp.where((lanef == i1) | (lanef == i2), 1.0, 0.0)
    return lanef, i1, i2, w1, w2, onehot


def _outproj_kernel(a_ref, b_ref, x_ref, mod_ref, w_ref, lng_ref, rw_ref, rb_ref, tri_ref, upper_ref,
                    x1_ref, h2_ref, r_ref, g8_ref):
    mod = mod_ref[0, 0]
    y = _dot(a_ref[0], w_ref[0:QW, :]) + _dot(b_ref[0], w_ref[QW:QW + UW, :])
    x1 = x_ref[0] + mod[:, 2 * D:3 * D] * y
    x1_ref[0] = x1
    h2 = _rmsnorm(x1, lng_ref[...]) * (1.0 + mod[:, 4 * D:5 * D]) + mod[:, 3 * D:4 * D]
    h2_ref[0] = h2.astype(BF16)
    logits = _dot3(h2, rw_ref[...]) + rb_ref[...]
    lanef, i1, i2, w1, w2, onehot = _route(logits)
    before = _dot(tri_ref[...], onehot.astype(BF16))
    cnt = jnp.sum(onehot, axis=0, keepdims=True)
    g8 = jnp.floor((cnt + float(GRP - 1)) * (1.0 / GRP))
    goff = _dot(jnp.broadcast_to(g8, (8, 128)).astype(BF16), upper_ref[...])[0:1, :]
    local = before + float(GRP) * goff
    l1 = jnp.sum(jnp.where(lanef == i1, local, 0.0), axis=1, keepdims=True)
    l2 = jnp.sum(jnp.where(lanef == i2, local, 0.0), axis=1, keepdims=True)
    info = jnp.where(lanef == 0.0, i1, jnp.where(lanef == 1.0, i2, jnp.where(lanef == 2.0, w1, jnp.where(
        lanef == 3.0, w2, jnp.where(lanef == 4.0, l1, jnp.where(lanef == 5.0, l2, 0.0))))))
    r_ref[0] = info
    g8_ref[0, 0] = g8


def _out_project(a, bmix, xc, modsel, w_out, ln_g, rw, rb, tri, upper, *, tile_off):
    b, s_out, _ = a.shape
    nt = s_out // TM
    const = lambda bi, i: (0, 0)
    sel = (lambda i: jnp.minimum(i, 1)) if tile_off == 0 else (lambda i: 1)
    tok = lambda bi, i: (bi, i, 0)
    return pl.pallas_call(
        _outproj_kernel,
        grid=(b, nt),
        in_specs=[
            pl.BlockSpec((1, TM, QW), tok),
            pl.BlockSpec((1, TM, UW), tok),
            pl.BlockSpec((1, TM, D), lambda bi, i: (bi, i + tile_off, 0)),
            pl.BlockSpec((1, 1, 1, 6 * D), lambda bi, i: (bi, sel(i), 0, 0)),
            pl.BlockSpec((D, D), const),
            pl.BlockSpec((1, D), const),
            pl.BlockSpec((D, 128), const),
            pl.BlockSpec((1, 128), const),
            pl.BlockSpec((TM, TM), const),
            pl.BlockSpec((128, 128), const),
        ],
        out_specs=[
            pl.BlockSpec((1, TM, D), tok),
            pl.BlockSpec((1, TM, D), tok),
            pl.BlockSpec((1, TM, 128), tok),
            pl.BlockSpec((1, 1, 1, 128), lambda bi, i: (bi, i, 0, 0)),
        ],
        out_shape=[
            jax.ShapeDtypeStruct((b, s_out, D), F32),
            jax.ShapeDtypeStruct((b, s_out, D), BF16),
            jax.ShapeDtypeStruct((b, s_out, 128), F32),
            jax.ShapeDtypeStruct((b, nt, 1, 128), F32),
        ],
        compiler_params=_params(("arbitrary", "arbitrary")),
        name="out_projection_router",
    )(a, bmix, xc, modsel, w_out, ln_g, rw, rb, tri, upper)


def _group_copy(src_ref, src_row, dst_ref, dst_row, sem):
    return pltpu.make_async_copy(src_ref.at[pl.ds(pl.multiple_of(src_row, GRP), GRP), :],
                                 dst_ref.at[pl.ds(pl.multiple_of(dst_row, GRP), GRP), :], sem)


def _wait_groups(count, src_ref, dst_ref, sem):
    def wait(g, carry):
        _group_copy(src_ref, 0, dst_ref, 0, sem).wait()
        return carry

    lax.fori_loop(0, count, wait, 0)


def _dispatch_kernel(gdst_ref, ng_ref, zdst_ref, nz_ref, h_ref, r_ref, xs_ref, loc, zbuf, sems, zsem):
    i = pl.program_id(0)
    n = pl.num_programs(0)

    @pl.when(i == 0)
    def _():
        zbuf[...] = jnp.zeros(zbuf.shape, F32)

        def per_expert(e, carry):
            def zero(j, c):
                _group_copy(zbuf, 0, xs_ref, zdst_ref[e] + j * GRP, zsem).start()
                return c

            lax.fori_loop(0, nz_ref[e], zero, 0)
            return carry

        lax.fori_loop(0, N_EXP, per_expert, 0)
        _wait_groups(nz_ref[N_EXP], zbuf, xs_ref, zsem)

    def run(slot):
        buf = loc.at[slot]
        sem = sems.at[slot]

        @pl.when(i >= 2)
        def _():
            _wait_groups(ng_ref[i - 2], buf, xs_ref, sem)

        rt = r_ref[...].T
        rows = lax.broadcasted_iota(jnp.int32, (LR, TM), 0).astype(F32)
        perm = jnp.where((rows == rt[4:5, :]) | (rows == rt[5:6, :]), 1.0, 0.0).astype(BF16)
        buf[...] = _dot(perm, h_ref[...])

        def issue(g, carry):
            _group_copy(buf, g * GRP, xs_ref, gdst_ref[i * NG + g], sem).start()
            return carry

        lax.fori_loop(0, ng_ref[i], issue, 0)

        @pl.when(i == n - 1)
        def _():
            _wait_groups(ng_ref[i], buf, xs_ref, sem)

            @pl.when(i >= 1)
            def _():
                _wait_groups(ng_ref[i - 1], loc.at[1 - slot], xs_ref, sems.at[1 - slot])

    pl.when(i % 2 == 0)(lambda: run(0))
    pl.when(i % 2 == 1)(lambda: run(1))


def _dispatch(gdst, ng, zdst, nz, h2, rinfo, cap):
    t = h2.shape[0]
    tok = lambda i, *_: (i, 0)
    return pl.pallas_call(
        _dispatch_kernel,
        grid_spec=pltpu.PrefetchScalarGridSpec(
            num_scalar_prefetch=4,
            grid=(t // TM,),
            in_specs=[
                pl.BlockSpec((TM, D), tok),
                pl.BlockSpec((TM, 128), tok),
            ],
            out_specs=pl.BlockSpec(memory_space=pl.ANY),
            scratch_shapes=[
                pltpu.VMEM((2, LR, D), F32),
                pltpu.VMEM((GRP, D), F32),
                pltpu.SemaphoreType.DMA((2,)),
                pltpu.SemaphoreType.DMA,
            ],
        ),
        out_shape=jax.ShapeDtypeStruct((cap, D), F32),
        compiler_params=_params(("arbitrary",)),
        name="moe_dispatch",
    )(gdst, ng, zdst, nz, h2, rinfo)


def _expert_kernel(be_ref, nu_ref, x_ref, wg_ref, wu_ref, wd_ref, y_ref, wg_sc, wu_sc, wd_sc):
    i = pl.program_id(0)
    used = i < nu_ref[0]
    fresh = (i == 0) | (be_ref[i] != be_ref[jnp.maximum(i - 1, 0)])

    @pl.when(used & fresh)
    def _():
        wg_sc[...] = wg_ref[0, 0].astype(BF16)
        wu_sc[...] = wu_ref[0, 0].astype(BF16)
        wd_sc[...] = wd_ref[0, 0].astype(BF16)

    @pl.when(used)
    def _():
        xb = x_ref[...].astype(BF16)
        g = _dot(xb, wg_sc[...])
        u = _dot(xb, wu_sc[...])
        hid = (g * _sigmoid(g)) * u
        y_ref[...] = _dot(hid.astype(BF16), wd_sc[...])

    @pl.when(jnp.logical_not(used))
    def _():
        y_ref[...] = jnp.zeros(y_ref.shape, F32)


def _experts(blk_e, n_used, xs, wg, wu, wd, layer):
    cap = xs.shape[0]
    row = lambda i, be, nu: (jnp.minimum(i, nu[0] - 1), 0)
    wsel = lambda i, be, nu: (layer, be[i], 0, 0)
    return pl.pallas_call(
        _expert_kernel,
        grid_spec=pltpu.PrefetchScalarGridSpec(
            num_scalar_prefetch=2,
            grid=(cap // BLK,),
            in_specs=[
                pl.BlockSpec((BLK, D), row),
                pl.BlockSpec((1, 1, D, D_EXP), wsel),
                pl.BlockSpec((1, 1, D, D_EXP), wsel),
                pl.BlockSpec((1, 1, D_EXP, D), wsel),
            ],
            out_specs=pl.BlockSpec((BLK, D), lambda i, be, nu: (i, 0)),
            scratch_shapes=[
                pltpu.VMEM((D, D_EXP), BF16),
                pltpu.VMEM((D, D_EXP), BF16),
                pltpu.VMEM((D_EXP, D), BF16),
            ],
        ),
        out_shape=jax.ShapeDtypeStruct((cap, D), F32),
        compiler_params=_params(("arbitrary",)),
        name="moe_experts",
    )(blk_e, n_used, xs, wg, wu, wd)


def _combine_kernel(gdst_ref, ng_ref, y_ref, x1_ref, r_ref, mod_ref, o_ref, loc, sems):
    i = pl.program_id(0)
    n = pl.num_programs(0)

    def fetch(tile, slot):
        def issue(g, carry):
            _group_copy(y_ref, gdst_ref[tile * NG + g], loc.at[slot], g * GRP, sems.at[slot]).start()
            return carry

        lax.fori_loop(0, ng_ref[tile], issue, 0)

    @pl.when(i == 0)
    def _():
        loc[...] = jnp.zeros(loc.shape, F32)
        fetch(0, 0)

    def run(slot):
        @pl.when(i + 1 < n)
        def _():
            fetch(i + 1, 1 - slot)

        _wait_groups(ng_ref[i], y_ref, loc.at[slot], sems.at[slot])
        ysb = loc[slot].astype(BF16)
        info = r_ref[...]
        col = lax.broadcasted_iota(jnp.int32, (TM, LR), 1).astype(F32)
        y1 = _dot(jnp.where(col == info[:, 4:5], 1.0, 0.0).astype(BF16), ysb)
        y2 = _dot(jnp.where(col == info[:, 5:6], 1.0, 0.0).astype(BF16), ysb)
        g2 = mod_ref[0, 0][:, 5 * D:6 * D]
        o_ref[...] = x1_ref[...] + g2 * (info[:, 2:3] * y1 + info[:, 3:4] * y2)

    pl.when(i % 2 == 0)(lambda: run(0))
    pl.when(i % 2 == 1)(lambda: run(1))


def _combine(gdst, ng, ys, x1, rinfo, modsel, *, tiles_per_batch, with_ctx):
    t = x1.shape[0]
    if with_ctx:
        msel = lambda i, *_: (i // tiles_per_batch, jnp.minimum(i % tiles_per_batch, 1), 0, 0)
    else:
        msel = lambda i, *_: (i // tiles_per_batch, 1, 0, 0)
    tok = lambda i, *_: (i, 0)
    return pl.pallas_call(
        _combine_kernel,
        grid_spec=pltpu.PrefetchScalarGridSpec(
            num_scalar_prefetch=2,
            grid=(t // TM,),
            in_specs=[
                pl.BlockSpec(memory_space=pl.ANY),
                pl.BlockSpec((TM, D), tok),
                pl.BlockSpec((TM, 128), tok),
                pl.BlockSpec((1, 1, 1, 6 * D), msel),
            ],
            out_specs=pl.BlockSpec((TM, D), tok),
            scratch_shapes=[pltpu.VMEM((2, LR, D), F32), pltpu.SemaphoreType.DMA((2,))],
        ),
        out_shape=jax.ShapeDtypeStruct((t, D), F32),
        compiler_params=_params(("arbitrary",)),
        name="moe_combine",
    )(gdst, ng, ys, x1, rinfo, modsel)


def _moe(x1, h2, rinfo, g8, modsel, wg, wu, wd, layer, *, with_ctx):
    b, s, _ = x1.shape
    t = b * s
    n_tiles = t // TM
    i32 = jnp.int32
    grp = g8.reshape(n_tiles, 128)[:, 0:N_EXP].astype(i32)
    n_blk = -(-(2 * t + n_tiles * N_EXP * (GRP - 1) + N_EXP * (BLK - 1)) // BLK)
    cap = n_blk * BLK
    goff_end = jnp.cumsum(grp, axis=1)
    ng = goff_end[:, N_EXP - 1]
    rows_e = GRP * jnp.sum(grp, axis=0)
    padded = (rows_e + BLK - 1) // BLK * BLK
    pend = jnp.cumsum(padded)
    pstart = pend - padded
    run_dst = pstart[None, :] + GRP * (jnp.cumsum(grp, axis=0) - grp)
    gidx = jnp.arange(NG, dtype=i32)
    e_of_g = jnp.minimum(jnp.sum((goff_end[:, None, :] <= gidx[None, :, None]).astype(i32), axis=2), N_EXP - 1)
    first_g = jnp.take_along_axis(goff_end - grp, e_of_g, axis=1)
    gdst = jnp.take_along_axis(run_dst, e_of_g, axis=1) + GRP * (gidx[None, :] - first_g)
    gdst = jnp.where(gidx[None, :] < ng[:, None], gdst, 0).reshape(n_tiles * NG).astype(i32)
    nz = (padded - rows_e) // GRP
    nz = jnp.concatenate([nz, jnp.sum(nz, keepdims=True)]).astype(i32)
    zdst = (pstart + rows_e).astype(i32)
    blk_e = jnp.minimum(jnp.sum((pend[None, :] <= (jnp.arange(n_blk, dtype=i32) * BLK)[:, None]).astype(i32), axis=1),
                        N_EXP - 1).astype(i32)
    n_used = (pend[N_EXP - 1:N_EXP] // BLK).astype(i32)
    info = rinfo.reshape(t, 128)
    xs = _dispatch(gdst, ng.astype(i32), zdst, nz, h2.reshape(t, D), info, cap)
    ys = _experts(blk_e, n_used, xs, wg, wu, wd, layer)
    out = _combine(gdst, ng.astype(i32), ys, x1.reshape(t, D), info, modsel, tiles_per_batch=s // TM,
                   with_ctx=with_ctx)
    return out.reshape(b, s, D)


def _rope_tables(n_ctx, s_lat):
    rows = s_lat // GRID_W
    row = jnp.repeat(jnp.arange(rows, dtype=F32), GRID_W)
    col = jnp.tile(jnp.arange(GRID_W, dtype=F32), rows)
    n_freq = HD // 4
    inv = ROPE_THETA ** (-jnp.arange(n_freq, dtype=F32) / n_freq)
    ang = jnp.concatenate([row[:, None] * inv, col[:, None] * inv], axis=-1)
    cos = jnp.concatenate([jnp.ones((n_ctx, HD // 2), F32), jnp.cos(ang)], axis=0)
    sin = jnp.concatenate([jnp.zeros((n_ctx, HD // 2), F32), jnp.sin(ang)], axis=0)
    cos = jnp.concatenate([cos, cos, cos, cos], axis=1)
    sin = jnp.concatenate([-sin, sin, -sin, sin], axis=1)
    return cos, sin


def _block_diag_ones(n):
    i = jnp.arange(n) // HD
    return (i[:, None] == i[None, :]).astype(BF16)


def kernel(x, c, ctx, c_ctx, mod_w, mod_b, ln1_g, ln2_g, w_in_ab, w_out_ab, q_norm_a, k_norm_a, conv_w, conv_b,
           conv_ln_g, conv_ln_b, w_in_cd, w_out_cd, q_norm_c, k_norm_c, sink_c, pool_w, pool_scale,
           rt_grp_w, rt_grp_b, rt_exp_w, rt_exp_b, ex_gate, ex_up, ex_down):
    b, s_lat, _ = x.shape
    n_ctx = ctx.shape[1]
    assert n_ctx == TM and s_lat % TM == 0 and b <= 8

    cc = jnp.zeros((16, D), F32).at[0:b].set(c).at[8].set(c_ctx)
    mod = _modulation(cc, mod_w, mod_b)

    def modsel(l):
        return jnp.stack([jnp.broadcast_to(mod[l, 8], (b, 6 * D)), mod[l, 0:b]], axis=1).reshape(b, 2, 1, 6 * D)

    cos, sin = _rope_tables(n_ctx, s_lat)
    bdq = _block_diag_ones(QW)
    bdk = _block_diag_ones(KW)
    tri = (jnp.arange(TM)[:, None] > jnp.arange(TM)[None, :]).astype(BF16)
    upper = (jnp.arange(128)[:, None] < jnp.arange(128)[None, :]).astype(BF16)

    def router(l):
        rw = jnp.zeros((D, 128), F32).at[:, 0:N_EXP].set(rt_exp_w[l]).at[:, N_EXP:N_EXP + N_GROUPS].set(rt_grp_w[l])
        rb = jnp.zeros((1, 128), F32).at[0, 0:N_EXP].set(rt_exp_b[l]).at[0, N_EXP:N_EXP + N_GROUPS].set(rt_grp_b[l])
        return rw, rb

    def tile_gain(g, n):
        return jnp.tile(g, n).reshape(1, n * HD)

    xc = jnp.concatenate([ctx, x], axis=1)

    ms = modsel(0)
    q, k, v, glu = _project(xc, ms, ln1_g[0:1], w_in_ab[0].astype(BF16), cos, sin,
                            tile_gain(q_norm_a[0], NH), tile_gain(k_norm_a[0], NKV), bdq, bdk, glu=True)
    att = _dense_attention(q, k, v, n_ctx)
    cv = _conv_module(glu, conv_w[0], conv_b[0:1], conv_ln_g[0:1], conv_ln_b[0:1], n_ctx)
    rw, rb = router(0)
    x1, h2, rinfo, g8 = _out_project(att, cv, xc, ms, w_out_ab[0].astype(BF16), ln2_g[0:1], rw, rb, tri, upper,
                                     tile_off=0)
    xc = _moe(x1, h2, rinfo, g8, ms, ex_gate, ex_up, ex_down, 0, with_ctx=True)

    ms = modsel(1)
    q, k, v, u = _project(xc, ms, ln1_g[1:2], w_in_cd[0].astype(BF16), cos, sin,
                          tile_gain(q_norm_c[0], NH), tile_gain(k_norm_c[0], NKV), bdq, bdk, glu=False)
    att = _window_attention(sink_c[0], q, k, v, n_ctx)
    pm = _pool_mixer(u, pool_w[0].astype(BF16), pool_scale[0:1], n_ctx)
    rw, rb = router(1)
    x1, h2, rinfo, g8 = _out_project(att, pm, xc, ms, w_out_cd[0].astype(BF16), ln2_g[1:2], rw, rb, tri, upper,
                                     tile_off=n_ctx // TM)
    return _moe(x1, h2, rinfo, g8, ms, ex_gate, ex_up, ex_down, 1, with_ctx=False)
```

```python
import functools

import jax
import jax.numpy as jnp
from jax import lax
from jax.experimental import pallas as pl
from jax.experimental.pallas import tpu as pltpu

F32 = jnp.float32
BF16 = jnp.bfloat16

D = 1024
HD = 64
NH = 8
NKV = 2
GQ = NH // NKV
QW = NH * HD
KW = NKV * HD
UW = 512
EPS = 1e-6
ROPE_THETA = 10000.0
GRID_W = 64
CONV_K = 31
WINDOW = 128
POOL_SIZES = (2, 4, 8, 16)
N_GROUPS = 4
PER_GROUP = 8
N_EXP = N_GROUPS * PER_GROUP
D_EXP = D // 2

TM = 256
TQ = 128
TKL = 512
SUB = 64
HALO = 16
BLK = 512
GRP = 8
NG = 96
LR = NG * GRP
VMEM_LIMIT = 56 * 1024 * 1024


def _sigmoid(x):
    return 1.0 / (1.0 + jnp.exp(-x))


def _dot(a, b):
    return jnp.dot(a, b, preferred_element_type=F32)


def _dot_nt(a, b):
    return lax.dot_general(a, b, (((1,), (1,)), ((), ())), preferred_element_type=F32)


def _split(a):
    hi = a.astype(BF16)
    lo = (a - hi.astype(F32)).astype(BF16)
    return hi, lo


def _dot3(a, w):
    a_hi, a_lo = _split(a)
    w_hi, w_lo = _split(w)
    return _dot(a_hi, w_hi) + _dot(a_lo, w_hi) + _dot(a_hi, w_lo)


def _rmsnorm(x, g):
    return x * lax.rsqrt(jnp.mean(x * x, axis=-1, keepdims=True) + EPS) * g


def _params(sem, vmem=VMEM_LIMIT):
    return pltpu.CompilerParams(dimension_semantics=sem, vmem_limit_bytes=vmem)


def _mod_kernel(c_ref, w_ref, b_ref, o_ref):
    a = c_ref[...]
    a = a * _sigmoid(a)
    o_ref[0] = _dot3(a, w_ref[0]) + b_ref[0]


def _modulation(cc, mod_w, mod_b):
    depth = mod_w.shape[0]
    tn = 1024
    return pl.pallas_call(
        _mod_kernel,
        grid=(depth, 6 * D // tn),
        in_specs=[
            pl.BlockSpec((16, D), lambda l, j: (0, 0)),
            pl.BlockSpec((1, D, tn), lambda l, j: (l, 0, j)),
            pl.BlockSpec((1, 1, tn), lambda l, j: (l, 0, j)),
        ],
        out_specs=pl.BlockSpec((1, 16, tn), lambda l, j: (l, 0, j)),
        out_shape=jax.ShapeDtypeStruct((depth, 16, 6 * D), F32),
        compiler_params=_params(("arbitrary", "arbitrary")),
        name="modulation",
    )(cc, mod_w, mod_b.reshape(depth, 1, 6 * D))


def _swap_halves(t):
    w = t.shape[1]
    lane = lax.broadcasted_iota(jnp.int32, t.shape, 1)
    first = (lane & (HD - 1)) < (HD // 2)
    return jnp.where(first, pltpu.roll(t, w - HD // 2, 1), pltpu.roll(t, HD // 2, 1))


def _head_norm_rope(t, bd, g, cos, sin, scale):
    hi, lo = _split(t * t)
    ssq = _dot(hi, bd) + _dot(lo, bd)
    tn = t * lax.rsqrt(ssq * (1.0 / HD) + EPS) * g
    n = t.shape[1] // 128
    if n > 1:
        cos = jnp.concatenate([cos] * n, axis=1)
        sin = jnp.concatenate([sin] * n, axis=1)
    out = tn * cos + _swap_halves(tn) * sin
    return out * scale if scale != 1.0 else out


def _tile_rows(c_ref, x_ref):
    return jnp.where(pl.program_id(1) == 0, c_ref[0], x_ref[0])


def _proj_kernel(c_ref, x_ref, mod_ref, lng_ref, w_ref, cos_ref, sin_ref, qg_ref, kg_ref, bdq_ref, bdk_ref,
                 q_ref, k_ref, v_ref, u_ref, *, glu):
    mod = mod_ref[0, 0]
    h = _rmsnorm(_tile_rows(c_ref, x_ref), lng_ref[...]) * (1.0 + mod[:, D:2 * D]) + mod[:, 0:D]
    px = _dot(h.astype(BF16), w_ref[...])
    cos = cos_ref[...]
    sin = sin_ref[...]
    q = _head_norm_rope(px[:, 0:QW], bdq_ref[...], qg_ref[...], cos, sin, HD ** -0.5)
    k = _head_norm_rope(px[:, QW:QW + KW], bdk_ref[...], kg_ref[...], cos, sin, 1.0)
    v = px[:, QW + KW:QW + 2 * KW]
    for hh in range(NH):
        q_ref[0, hh] = q[:, hh * HD:(hh + 1) * HD].astype(BF16)
    lane = lax.broadcasted_iota(jnp.int32, v.shape, 1)
    ones_col = jnp.where(lane == HD, 1.0, 0.0)
    for j in range(NKV):
        k_ref[0, j] = k[:, j * HD:(j + 1) * HD].astype(BF16)
        vj = v if j == 0 else pltpu.roll(v, KW - j * HD, 1)
        v_ref[0, j] = jnp.where(lane < HD, vj, ones_col).astype(BF16)
    o_u = QW + 2 * KW
    if glu:
        u_ref[0] = px[:, o_u:o_u + UW] * _sigmoid(px[:, o_u + UW:o_u + 2 * UW])
    else:
        u_ref[0] = px[:, o_u:o_u + UW]


def _token_specs(lat_first):
    return [pl.BlockSpec((1, TM, D), lambda bi, i: (bi, 0, 0)),
            pl.BlockSpec((1, TM, D), lambda bi, i: (bi, jnp.maximum(i - 1, 0) + lat_first, 0))]


def _project(ctx_rows, lat_rows, lat_first, s_tot, modsel, ln_g, w_in, cos, sin, q_g, k_g, bdq, bdk, *, glu):
    b = ctx_rows.shape[0]
    n_in = w_in.shape[1]
    nt = s_tot // TM
    const = lambda bi, i: (0, 0)
    return pl.pallas_call(
        functools.partial(_proj_kernel, glu=glu),
        grid=(b, nt),
        in_specs=_token_specs(lat_first) + [
            pl.BlockSpec((1, 1, 1, 6 * D), lambda bi, i: (bi, jnp.minimum(i, 1), 0, 0)),
            pl.BlockSpec((1, D), const),
            pl.BlockSpec((D, n_in), const),
            pl.BlockSpec((TM, 128), lambda bi, i: (i, 0)),
            pl.BlockSpec((TM, 128), lambda bi, i: (i, 0)),
            pl.BlockSpec((1, QW), const),
            pl.BlockSpec((1, KW), const),
            pl.BlockSpec((QW, QW), const),
            pl.BlockSpec((KW, KW), const),
        ],
        out_specs=[
            pl.BlockSpec((1, NH, TM, HD), lambda bi, i: (bi, 0, i, 0)),
            pl.BlockSpec((1, NKV, TM, HD), lambda bi, i: (bi, 0, i, 0)),
            pl.BlockSpec((1, NKV, TM, 2 * HD), lambda bi, i: (bi, 0, i, 0)),
            pl.BlockSpec((1, TM, UW), lambda bi, i: (bi, i, 0)),
        ],
        out_shape=[
            jax.ShapeDtypeStruct((b, NH, s_tot, HD), BF16),
            jax.ShapeDtypeStruct((b, NKV, s_tot, HD), BF16),
            jax.ShapeDtypeStruct((b, NKV, s_tot, 2 * HD), BF16),
            jax.ShapeDtypeStruct((b, s_tot, UW), F32),
        ],
        compiler_params=_params(("arbitrary", "arbitrary")),
        name="in_projection",
    )(ctx_rows, lat_rows, modsel, ln_g, w_in, cos, sin, q_g, k_g, bdq, bdk)


def _merge_heads(o):
    return jnp.concatenate([o[g * TQ:(g + 1) * TQ] for g in range(GQ)], axis=1)


def _lane_max(s):
    return functools.reduce(jnp.maximum, [s[:, j * 128:(j + 1) * 128] for j in range(s.shape[1] // 128)])


def _dense_attn_kernel(q_ref, k_ref, v_ref, o_ref, sc_sc, sl_sc, m_sc, *, n_ctx, s_tot):
    qi = pl.program_id(2)
    q = q_ref[0].reshape(GQ * TQ, HD)
    n_lat = (s_tot - n_ctx) // TKL

    def chunks(n):
        spans = [(0, n_ctx)] + [(n_ctx + c * TKL, n_ctx + (c + 1) * TKL) for c in range(n)]
        return list(zip(spans, [sc_sc] + [sl_sc.at[c] for c in range(n)]))

    def scores(n):
        for i, ((lo, hi), slot) in enumerate(chunks(n)):
            s = _dot_nt(q, k_ref[0, 0, lo:hi, :])
            slot[...] = s
            m_sc[i] = _lane_max(s)
        m_lane = functools.reduce(jnp.maximum, [m_sc[i] for i in range(n + 1)])
        m_sc[0] = jnp.broadcast_to(jnp.max(m_lane, axis=1, keepdims=True), m_lane.shape)

    def weighted(n):
        acc = None
        for (lo, hi), slot in chunks(n):
            m = jnp.concatenate([m_sc[0]] * ((hi - lo) // 128), axis=1)
            part = _dot(jnp.exp(slot[...] - m).astype(BF16), v_ref[0, 0, lo:hi, :])
            acc = part if acc is None else acc + part
        o_ref[0] = _merge_heads(acc[:, 0:HD] / acc[:, HD:HD + 1]).astype(BF16)

    is_lat = qi * TQ >= n_ctx
    pl.when(is_lat)(lambda: scores(n_lat))
    pl.when(jnp.logical_not(is_lat))(lambda: scores(0))
    pl.when(is_lat)(lambda: weighted(n_lat))
    pl.when(jnp.logical_not(is_lat))(lambda: weighted(0))


def _dense_attention(q, k, v, n_ctx):
    b, _, s_tot, _ = q.shape
    rows = GQ * TQ
    return pl.pallas_call(
        functools.partial(_dense_attn_kernel, n_ctx=n_ctx, s_tot=s_tot),
        grid=(b, NKV, s_tot // TQ),
        in_specs=[
            pl.BlockSpec((1, GQ, TQ, HD), lambda bi, j, i: (bi, j, i, 0)),
            pl.BlockSpec((1, 1, s_tot, HD), lambda bi, j, i: (bi, j, 0, 0)),
            pl.BlockSpec((1, 1, s_tot, 2 * HD), lambda bi, j, i: (bi, j, 0, 0)),
        ],
        out_specs=pl.BlockSpec((1, TQ, GQ * HD), lambda bi, j, i: (bi, i, j)),
        out_shape=jax.ShapeDtypeStruct((b, s_tot, QW), BF16),
        scratch_shapes=[
            pltpu.VMEM((rows, n_ctx), F32),
            pltpu.VMEM(((s_tot - n_ctx) // TKL, rows, TKL), F32),
            pltpu.VMEM(((s_tot - n_ctx) // TKL + 1, rows, 128), F32),
        ],
        compiler_params=_params(("arbitrary", "arbitrary", "arbitrary")),
        name="dense_attention",
    )(q, k, v)


def _window_attn_kernel(sink_ref, q_ref, k_ref, v_ref, o_ref, sc_sc, sw_sc, m_sc, e_sc, *, n_ctx, s_tot):
    qi = pl.program_id(1)
    span = TQ + 2 * WINDOW
    q0 = n_ctx + qi * TQ
    start = pl.multiple_of(jnp.clip(q0 - WINDOW, n_ctx, s_tot - span), 128)
    row = lax.broadcasted_iota(jnp.int32, (TQ, span), 0)
    col = lax.broadcasted_iota(jnp.int32, (TQ, span), 1)
    band = jnp.where(jnp.abs((q0 - start) + row - col) <= WINDOW, 0.0, -jnp.inf)
    band = jnp.concatenate([band] * GQ, axis=0)
    for j in range(NKV):
        q = q_ref[0, j * GQ:(j + 1) * GQ].reshape(GQ * TQ, HD)
        s_c = _dot_nt(q, k_ref[0, j, 0:n_ctx, :])
        s_w = _dot_nt(q, k_ref[0, j, pl.ds(start, span), :]) + band
        sc_sc[j] = s_c
        sw_sc[j] = s_w
        sink = jnp.concatenate([jnp.full((TQ, 128), sink_ref[j * GQ + g], F32) for g in range(GQ)], axis=0)
        m_lane = jnp.maximum(_lane_max(s_c), _lane_max(s_w))
        m_sc[j] = jnp.maximum(jnp.broadcast_to(jnp.max(m_lane, axis=1, keepdims=True), m_lane.shape), sink)
        e_sc[j] = jnp.exp(sink - m_sc[j])
    for j in range(NKV):
        m = m_sc[j]
        p_c = jnp.exp(sc_sc[j] - jnp.concatenate([m] * (n_ctx // 128), axis=1)).astype(BF16)
        p_w = jnp.exp(sw_sc[j] - jnp.concatenate([m] * (span // 128), axis=1)).astype(BF16)
        acc = _dot(p_c, v_ref[0, j, 0:n_ctx, :]) + _dot(p_w, v_ref[0, j, pl.ds(start, span), :])
        l = acc[:, HD:HD + 1] + e_sc[j][:, 0:1]
        o_ref[0, :, j * GQ * HD:(j + 1) * GQ * HD] = _merge_heads(acc[:, 0:HD] / l).astype(BF16)


def _window_attention(sink, q, k, v, n_ctx):
    b, _, s_tot, _ = q.shape
    s_lat = s_tot - n_ctx
    off = n_ctx // TQ
    return pl.pallas_call(
        functools.partial(_window_attn_kernel, n_ctx=n_ctx, s_tot=s_tot),
        grid_spec=pltpu.PrefetchScalarGridSpec(
            num_scalar_prefetch=1,
            grid=(b, s_lat // TQ),
            in_specs=[
                pl.BlockSpec((1, NH, TQ, HD), lambda bi, i, sk: (bi, 0, i + off, 0)),
                pl.BlockSpec((1, NKV, s_tot, HD), lambda bi, i, sk: (bi, 0, 0, 0)),
                pl.BlockSpec((1, NKV, s_tot, 2 * HD), lambda bi, i, sk: (bi, 0, 0, 0)),
            ],
            out_specs=pl.BlockSpec((1, TQ, QW), lambda bi, i, sk: (bi, i, 0)),
            scratch_shapes=[
                pltpu.VMEM((NKV, GQ * TQ, n_ctx), F32),
                pltpu.VMEM((NKV, GQ * TQ, TQ + 2 * WINDOW), F32),
                pltpu.VMEM((NKV, GQ * TQ, 128), F32),
                pltpu.VMEM((NKV, GQ * TQ, 128), F32),
            ],
        ),
        out_shape=jax.ShapeDtypeStruct((b, s_lat, QW), BF16),
        compiler_params=_params(("arbitrary", "arbitrary")),
        name="window_attention",
    )(sink, q, k, v)


def _conv_kernel(g_ref, cw_ref, cb_ref, lg_ref, lb_ref, o_ref, pad_sc, win_sc, *, n_ctx, s_tot):
    zeros = jnp.zeros((HALO, UW), F32)
    pad_sc[0:HALO, :] = zeros
    pad_sc[HALO + n_ctx:2 * HALO + n_ctx, :] = zeros
    pad_sc[2 * HALO + s_tot:3 * HALO + s_tot, :] = zeros

    def fill(i, carry):
        src = pl.multiple_of(i * TM, TM)
        dst = pl.multiple_of(src + HALO + jnp.where(src >= n_ctx, HALO, 0), 8)
        pad_sc[pl.ds(dst, TM), :] = g_ref[0, pl.ds(src, TM), :]
        return carry

    lax.fori_loop(0, s_tot // TM, fill, 0)
    half = CONV_K // 2

    def tile(i, carry):
        src = pl.multiple_of(i * SUB, SUB)
        base = pl.multiple_of(src + jnp.where(src >= n_ctx, HALO, 0), 8)
        win = pad_sc[pl.ds(base, SUB + 2 * HALO), :]
        keep = SUB + 2 * HALO - 8
        for r in range(8):
            win_sc[r, 0:keep, :] = win[r:r + keep, :]
        acc = jnp.zeros((SUB, UW), F32) + cb_ref[...]
        for t in range(CONV_K):
            a, r = divmod(HALO - half + t, 8)
            acc = acc + win_sc[r, 8 * a:8 * a + SUB, :] * cw_ref[t:t + 1, :]
        mu = jnp.mean(acc, axis=-1, keepdims=True)
        xc = acc - mu
        var = jnp.mean(xc * xc, axis=-1, keepdims=True)
        yn = xc * lax.rsqrt(var + EPS) * lg_ref[...] + lb_ref[...]
        o_ref[0, pl.ds(src, SUB), :] = (yn * _sigmoid(yn)).astype(BF16)
        return carry

    lax.fori_loop(0, s_tot // SUB, tile, 0)


def _conv_module(glu, conv_w, conv_b, ln_g, ln_b, n_ctx):
    b, s_tot, _ = glu.shape
    const = lambda bi: (0, 0)
    return pl.pallas_call(
        functools.partial(_conv_kernel, n_ctx=n_ctx, s_tot=s_tot),
        grid=(b,),
        in_specs=[
            pl.BlockSpec((1, s_tot, UW), lambda bi: (bi, 0, 0)),
            pl.BlockSpec((CONV_K, UW), const),
            pl.BlockSpec((1, UW), const),
            pl.BlockSpec((1, UW), const),
            pl.BlockSpec((1, UW), const),
        ],
        out_specs=pl.BlockSpec((1, s_tot, UW), lambda bi: (bi, 0, 0)),
        out_shape=jax.ShapeDtypeStruct((b, s_tot, UW), BF16),
        scratch_shapes=[pltpu.VMEM((s_tot + 3 * HALO, UW), F32), pltpu.VMEM((8, SUB + 2 * HALO, UW), F32)],
        compiler_params=_params(("arbitrary",)),
        name="conv_module",
    )(glu, conv_w, conv_b, ln_g, ln_b)


def _pool_kernel(u_ref, pw_ref, ps_ref, o_ref, pad_sc, *, n_ctx, s_lat):
    zeros = jnp.zeros((HALO, UW), F32)
    pad_sc[0:HALO, :] = zeros
    pad_sc[HALO + s_lat:2 * HALO + s_lat, :] = zeros

    def fill(i, carry):
        src = pl.multiple_of(i * TM, TM)
        pad_sc[pl.ds(pl.multiple_of(src + HALO, 8), TM), :] = u_ref[0, pl.ds(pl.multiple_of(src + n_ctx, 8), TM), :]
        return carry

    lax.fori_loop(0, s_lat // TM, fill, 0)
    gw = UW // len(POOL_SIZES)

    def tile(i, carry):
        src = pl.multiple_of(i * SUB, SUB)
        win = pad_sc[pl.ds(src, SUB + 2 * HALO), :]
        t = src + lax.broadcasted_iota(jnp.int32, (SUB, 1), 0)
        outs = []
        for gi, w in enumerate(POOL_SIZES):
            lanes = slice(gi * gw, (gi + 1) * gw)
            tot = jnp.zeros((SUB, gw), F32)
            for d in range(-(w // 2), w - w // 2):
                tot = tot + win[HALO + d:HALO + d + SUB, lanes]
            lo = jnp.clip(t - w // 2, 0, s_lat)
            hi = jnp.clip(t - w // 2 + w, 0, s_lat)
            p = tot / (hi - lo).astype(F32) - win[HALO:HALO + SUB, lanes]
            outs.append(_dot(p.astype(BF16), pw_ref[gi]))
        y = jnp.concatenate(outs, axis=1) * ps_ref[...]
        o_ref[0, pl.ds(src, SUB), :] = y.astype(BF16)
        return carry

    lax.fori_loop(0, s_lat // SUB, tile, 0)


def _pool_mixer(u, pool_w, pool_scale, n_ctx):
    b, s_tot, _ = u.shape
    s_lat = s_tot - n_ctx
    gw = UW // len(POOL_SIZES)
    return pl.pallas_call(
        functools.partial(_pool_kernel, n_ctx=n_ctx, s_lat=s_lat),
        grid=(b,),
        in_specs=[
            pl.BlockSpec((1, s_tot, UW), lambda bi: (bi, 0, 0)),
            pl.BlockSpec((len(POOL_SIZES), gw, gw), lambda bi: (0, 0, 0)),
            pl.BlockSpec((1, UW), lambda bi: (0, 0)),
        ],
        out_specs=pl.BlockSpec((1, s_lat, UW), lambda bi: (bi, 0, 0)),
        out_shape=jax.ShapeDtypeStruct((b, s_lat, UW), BF16),
        scratch_shapes=[pltpu.VMEM((s_lat + 2 * HALO, UW), F32)],
        compiler_params=_params(("arbitrary",)),
        name="pool_mixer",
    )(u, pool_w, pool_scale)


def _route(logits):
    shape = logits.shape
    lane = lax.broadcasted_iota(jnp.int32, shape, 1)
    lanef = lane.astype(F32)
    is_g = (lane >= N_EXP) & (lane < N_EXP + N_GROUPS)
    big = 1e9
    lg = jnp.where(is_g, logits, -jnp.inf)
    gmax = jnp.max(lg, axis=1, keepdims=True)
    gsel = jnp.min(jnp.where(is_g & (lg == gmax), lanef, big), axis=1, keepdims=True) - float(N_EXP)
    gsum = jnp.sum(jnp.exp(lg - gmax), axis=1, keepdims=True)
    g_w = 1.0 / gsum
    in_grp = (lane < N_EXP) & ((lane >> 3).astype(F32) == gsel)
    el = jnp.where(in_grp, logits, -jnp.inf)
    emax = jnp.max(el, axis=1, keepdims=True)
    ee = jnp.exp(el - emax)
    p = ee / jnp.sum(ee, axis=1, keepdims=True)
    p1 = jnp.max(jnp.where(in_grp, p, -1.0), axis=1, keepdims=True)
    i1 = jnp.min(jnp.where(in_grp & (p == p1), lanef, big), axis=1, keepdims=True)
    rest = in_grp & (lanef != i1)
    p2 = jnp.max(jnp.where(rest, p, -1.0), axis=1, keepdims=True)
    i2 = jnp.min(jnp.where(rest & (p == p2), lanef, big), axis=1, keepdims=True)
    w1 = g_w * p1 / (p1 + p2)
    w2 = g_w * p2 / (p1 + p2)
    onehot = jnp.where((lanef == i1) | (lanef == i2), 1.0, 0.0)
    return lanef, i1, i2, w1, w2, onehot


def _outproj_kernel(a_ref, b_ref, *refs, with_ctx):
    if with_ctx:
        c_ref, x_ref = refs[0:2]
        x_in = _tile_rows(c_ref, x_ref)
    else:
        x_in = refs[0][0]
    mod_ref, w_ref, lng_ref, rw_ref, rb_ref, tri_ref, upper_ref, x1_ref, h2_ref, r_ref, g8_ref = refs[-11:]
    mod = mod_ref[0, 0]
    y = _dot(a_ref[0], w_ref[0:QW, :]) + _dot(b_ref[0], w_ref[QW:QW + UW, :])
    x1 = x_in + mod[:, 2 * D:3 * D] * y
    x1_ref[0] = x1
    h2 = _rmsnorm(x1, lng_ref[...]) * (1.0 + mod[:, 4 * D:5 * D]) + mod[:, 3 * D:4 * D]
    h2_ref[0] = h2.astype(BF16)
    logits = _dot3(h2, rw_ref[...]) + rb_ref[...]
    lanef, i1, i2, w1, w2, onehot = _route(logits)
    before = _dot(tri_ref[...], onehot.astype(BF16))
    cnt = jnp.sum(onehot, axis=0, keepdims=True)
    g8 = jnp.floor((cnt + float(GRP - 1)) * (1.0 / GRP))
    goff = _dot(jnp.broadcast_to(g8, (8, 128)).astype(BF16), upper_ref[...])[0:1, :]
    local = before + float(GRP) * goff
    l1 = jnp.sum(jnp.where(lanef == i1, local, 0.0), axis=1, keepdims=True)
    l2 = jnp.sum(jnp.where(lanef == i2, local, 0.0), axis=1, keepdims=True)
    info = jnp.where(lanef == 0.0, i1, jnp.where(lanef == 1.0, i2, jnp.where(lanef == 2.0, w1, jnp.where(
        lanef == 3.0, w2, jnp.where(lanef == 4.0, l1, jnp.where(lanef == 5.0, l2, 0.0))))))
    r_ref[0] = info
    g8_ref[0, 0] = g8


def _out_project(a, bmix, residual, modsel, w_out, ln_g, rw, rb, tri, upper):
    b, s_out, _ = a.shape
    nt = s_out // TM
    const = lambda bi, i: (0, 0)
    with_ctx = len(residual) == 2
    sel = (lambda i: jnp.minimum(i, 1)) if with_ctx else (lambda i: 1)
    tok = lambda bi, i: (bi, i, 0)
    first_lat = residual[0].shape[1] // TM - nt
    res_specs = _token_specs(0) if with_ctx else [pl.BlockSpec((1, TM, D), lambda bi, i: (bi, i + first_lat, 0))]
    return pl.pallas_call(
        functools.partial(_outproj_kernel, with_ctx=with_ctx),
        grid=(b, nt),
        in_specs=[
            pl.BlockSpec((1, TM, QW), tok),
            pl.BlockSpec((1, TM, UW), tok),
        ] + res_specs + [
            pl.BlockSpec((1, 1, 1, 6 * D), lambda bi, i: (bi, sel(i), 0, 0)),
            pl.BlockSpec((D, D), const),
            pl.BlockSpec((1, D), const),
            pl.BlockSpec((D, 128), const),
            pl.BlockSpec((1, 128), const),
            pl.BlockSpec((TM, TM), const),
            pl.BlockSpec((128, 128), const),
        ],
        out_specs=[
            pl.BlockSpec((1, TM, D), tok),
            pl.BlockSpec((1, TM, D), tok),
            pl.BlockSpec((1, TM, 128), tok),
            pl.BlockSpec((1, 1, 1, 128), lambda bi, i: (bi, i, 0, 0)),
        ],
        out_shape=[
            jax.ShapeDtypeStruct((b, s_out, D), F32),
            jax.ShapeDtypeStruct((b, s_out, D), BF16),
            jax.ShapeDtypeStruct((b, s_out, 128), F32),
            jax.ShapeDtypeStruct((b, nt, 1, 128), F32),
        ],
        compiler_params=_params(("arbitrary", "arbitrary")),
        name="out_projection_router",
    )(a, bmix, *residual, modsel, w_out, ln_g, rw, rb, tri, upper)


def _group_copy(src_ref, src_row, dst_ref, dst_row, sem):
    return pltpu.make_async_copy(src_ref.at[pl.ds(pl.multiple_of(src_row, GRP), GRP), :],
                                 dst_ref.at[pl.ds(pl.multiple_of(dst_row, GRP), GRP), :], sem)


def _wait_groups(count, src_ref, dst_ref, sem):
    def wait(g, carry):
        _group_copy(src_ref, 0, dst_ref, 0, sem).wait()
        return carry

    lax.fori_loop(0, count, wait, 0)


def _dispatch_kernel(gdst_ref, ng_ref, zdst_ref, nz_ref, h_ref, r_ref, xs_ref, loc, zbuf, sems, zsem):
    i = pl.program_id(0)
    n = pl.num_programs(0)

    @pl.when(i == 0)
    def _():
        zbuf[...] = jnp.zeros(zbuf.shape, F32)

        def per_expert(e, carry):
            def zero(j, c):
                _group_copy(zbuf, 0, xs_ref, zdst_ref[e] + j * GRP, zsem).start()
                return c

            lax.fori_loop(0, nz_ref[e], zero, 0)
            return carry

        lax.fori_loop(0, N_EXP, per_expert, 0)
        _wait_groups(nz_ref[N_EXP], zbuf, xs_ref, zsem)

    def run(slot):
        buf = loc.at[slot]
        sem = sems.at[slot]

        @pl.when(i >= 2)
        def _():
            _wait_groups(ng_ref[i - 2], buf, xs_ref, sem)

        rt = r_ref[...].T
        rows = lax.broadcasted_iota(jnp.int32, (LR, TM), 0).astype(F32)
        perm = jnp.where((rows == rt[4:5, :]) | (rows == rt[5:6, :]), 1.0, 0.0).astype(BF16)
        buf[...] = _dot(perm, h_ref[...])

        def issue(g, carry):
            _group_copy(buf, g * GRP, xs_ref, gdst_ref[i * NG + g], sem).start()
            return carry

        lax.fori_loop(0, ng_ref[i], issue, 0)

        @pl.when(i == n - 1)
        def _():
            _wait_groups(ng_ref[i], buf, xs_ref, sem)

            @pl.when(i >= 1)
            def _():
                _wait_groups(ng_ref[i - 1], loc.at[1 - slot], xs_ref, sems.at[1 - slot])

    pl.when(i % 2 == 0)(lambda: run(0))
    pl.when(i % 2 == 1)(lambda: run(1))


def _dispatch(gdst, ng, zdst, nz, h2, rinfo, cap):
    t = h2.shape[0]
    tok = lambda i, *_: (i, 0)
    return pl.pallas_call(
        _dispatch_kernel,
        grid_spec=pltpu.PrefetchScalarGridSpec(
            num_scalar_prefetch=4,
            grid=(t // TM,),
            in_specs=[
                pl.BlockSpec((TM, D), tok),
                pl.BlockSpec((TM, 128), tok),
            ],
            out_specs=pl.BlockSpec(memory_space=pl.ANY),
            scratch_shapes=[
                pltpu.VMEM((2, LR, D), F32),
                pltpu.VMEM((GRP, D), F32),
                pltpu.SemaphoreType.DMA((2,)),
                pltpu.SemaphoreType.DMA,
            ],
        ),
        out_shape=jax.ShapeDtypeStruct((cap, D), F32),
        compiler_params=_params(("arbitrary",)),
        name="moe_dispatch",
    )(gdst, ng, zdst, nz, h2, rinfo)


def _expert_kernel(be_ref, nu_ref, x_ref, wg_ref, wu_ref, wd_ref, y_ref, wg_sc, wu_sc, wd_sc):
    i = pl.program_id(0)
    used = i < nu_ref[0]
    fresh = (i == 0) | (be_ref[i] != be_ref[jnp.maximum(i - 1, 0)])

    @pl.when(used & fresh)
    def _():
        wg_sc[...] = wg_ref[0, 0].astype(BF16)
        wu_sc[...] = wu_ref[0, 0].astype(BF16)
        wd_sc[...] = wd_ref[0, 0].astype(BF16)

    @pl.when(used)
    def _():
        xb = x_ref[...].astype(BF16)
        g = _dot(xb, wg_sc[...])
        u = _dot(xb, wu_sc[...])
        hid = (g * _sigmoid(g)) * u
        y_ref[...] = _dot(hid.astype(BF16), wd_sc[...])

    @pl.when(jnp.logical_not(used))
    def _():
        y_ref[...] = jnp.zeros(y_ref.shape, F32)


def _experts(blk_e, n_used, xs, wg, wu, wd, layer):
    cap = xs.shape[0]
    row = lambda i, be, nu: (jnp.minimum(i, nu[0] - 1), 0)
    wsel = lambda i, be, nu: (layer, be[i], 0, 0)
    return pl.pallas_call(
        _expert_kernel,
        grid_spec=pltpu.PrefetchScalarGridSpec(
            num_scalar_prefetch=2,
            grid=(cap // BLK,),
            in_specs=[
                pl.BlockSpec((BLK, D), row),
                pl.BlockSpec((1, 1, D, D_EXP), wsel),
                pl.BlockSpec((1, 1, D, D_EXP), wsel),
                pl.BlockSpec((1, 1, D_EXP, D), wsel),
            ],
            out_specs=pl.BlockSpec((BLK, D), lambda i, be, nu: (i, 0)),
            scratch_shapes=[
                pltpu.VMEM((D, D_EXP), BF16),
                pltpu.VMEM((D, D_EXP), BF16),
                pltpu.VMEM((D_EXP, D), BF16),
            ],
        ),
        out_shape=jax.ShapeDtypeStruct((cap, D), F32),
        compiler_params=_params(("arbitrary",)),
        name="moe_experts",
    )(blk_e, n_used, xs, wg, wu, wd)


def _combine_kernel(gdst_ref, ng_ref, y_ref, x1_ref, r_ref, mod_ref, o_ref, loc, sems):
    i = pl.program_id(0)
    n = pl.num_programs(0)

    def fetch(tile, slot):
        def issue(g, carry):
            _group_copy(y_ref, gdst_ref[tile * NG + g], loc.at[slot], g * GRP, sems.at[slot]).start()
            return carry

        lax.fori_loop(0, ng_ref[tile], issue, 0)

    @pl.when(i == 0)
    def _():
        loc[...] = jnp.zeros(loc.shape, F32)
        fetch(0, 0)

    def run(slot):
        @pl.when(i + 1 < n)
        def _():
            fetch(i + 1, 1 - slot)

        _wait_groups(ng_ref[i], y_ref, loc.at[slot], sems.at[slot])
        ysb = loc[slot].astype(BF16)
        info = r_ref[...]
        col = lax.broadcasted_iota(jnp.int32, (TM, LR), 1).astype(F32)
        y1 = _dot(jnp.where(col == info[:, 4:5], 1.0, 0.0).astype(BF16), ysb)
        y2 = _dot(jnp.where(col == info[:, 5:6], 1.0, 0.0).astype(BF16), ysb)
        g2 = mod_ref[0, 0][:, 5 * D:6 * D]
        o_ref[...] = x1_ref[...] + g2 * (info[:, 2:3] * y1 + info[:, 3:4] * y2)

    pl.when(i % 2 == 0)(lambda: run(0))
    pl.when(i % 2 == 1)(lambda: run(1))


def _combine(gdst, ng, ys, x1, rinfo, modsel, *, tiles_per_batch, with_ctx):
    t = x1.shape[0]
    if with_ctx:
        msel = lambda i, *_: (i // tiles_per_batch, jnp.minimum(i % tiles_per_batch, 1), 0, 0)
    else:
        msel = lambda i, *_: (i // tiles_per_batch, 1, 0, 0)
    tok = lambda i, *_: (i, 0)
    return pl.pallas_call(
        _combine_kernel,
        grid_spec=pltpu.PrefetchScalarGridSpec(
            num_scalar_prefetch=2,
            grid=(t // TM,),
            in_specs=[
                pl.BlockSpec(memory_space=pl.ANY),
                pl.BlockSpec((TM, D), tok),
                pl.BlockSpec((TM, 128), tok),
                pl.BlockSpec((1, 1, 1, 6 * D), msel),
            ],
            out_specs=pl.BlockSpec((TM, D), tok),
            scratch_shapes=[pltpu.VMEM((2, LR, D), F32), pltpu.SemaphoreType.DMA((2,))],
        ),
        out_shape=jax.ShapeDtypeStruct((t, D), F32),
        compiler_params=_params(("arbitrary",)),
        name="moe_combine",
    )(gdst, ng, ys, x1, rinfo, modsel)


def _moe(x1, h2, rinfo, g8, modsel, wg, wu, wd, layer, *, with_ctx):
    b, s, _ = x1.shape
    t = b * s
    n_tiles = t // TM
    i32 = jnp.int32
    grp = g8.reshape(n_tiles, 128)[:, 0:N_EXP].astype(i32)
    n_blk = -(-(2 * t + n_tiles * N_EXP * (GRP - 1) + N_EXP * (BLK - 1)) // BLK)
    cap = n_blk * BLK
    goff_end = jnp.cumsum(grp, axis=1)
    ng = goff_end[:, N_EXP - 1]
    rows_e = GRP * jnp.sum(grp, axis=0)
    padded = (rows_e + BLK - 1) // BLK * BLK
    pend = jnp.cumsum(padded)
    pstart = pend - padded
    run_dst = pstart[None, :] + GRP * (jnp.cumsum(grp, axis=0) - grp)
    gidx = jnp.arange(NG, dtype=i32)[None, :, None]
    first_g = (goff_end - grp)[:, None, :]
    in_run = (gidx >= first_g) & (gidx < goff_end[:, None, :])
    gdst = jnp.sum(jnp.where(in_run, run_dst[:, None, :] + GRP * (gidx - first_g), 0), axis=2)
    gdst = gdst.reshape(n_tiles * NG).astype(i32)
    nz = (padded - rows_e) // GRP
    nz = jnp.concatenate([nz, jnp.sum(nz, keepdims=True)]).astype(i32)
    zdst = (pstart + rows_e).astype(i32)
    blk_e = jnp.minimum(jnp.sum((pend[None, :] <= (jnp.arange(n_blk, dtype=i32) * BLK)[:, None]).astype(i32), axis=1),
                        N_EXP - 1).astype(i32)
    n_used = (pend[N_EXP - 1:N_EXP] // BLK).astype(i32)
    info = rinfo.reshape(t, 128)
    xs = _dispatch(gdst, ng.astype(i32), zdst, nz, h2.reshape(t, D), info, cap)
    ys = _experts(blk_e, n_used, xs, wg, wu, wd, layer)
    out = _combine(gdst, ng.astype(i32), ys, x1.reshape(t, D), info, modsel, tiles_per_batch=s // TM,
                   with_ctx=with_ctx)
    return out.reshape(b, s, D)


def _rope_tables(n_ctx, s_lat):
    rows = s_lat // GRID_W
    row = jnp.repeat(jnp.arange(rows, dtype=F32), GRID_W)
    col = jnp.tile(jnp.arange(GRID_W, dtype=F32), rows)
    n_freq = HD // 4
    inv = ROPE_THETA ** (-jnp.arange(n_freq, dtype=F32) / n_freq)
    ang = jnp.concatenate([row[:, None] * inv, col[:, None] * inv], axis=-1)
    cos = jnp.concatenate([jnp.ones((n_ctx, HD // 2), F32), jnp.cos(ang)], axis=0)
    sin = jnp.concatenate([jnp.zeros((n_ctx, HD // 2), F32), jnp.sin(ang)], axis=0)
    cos = jnp.concatenate([cos, cos, cos, cos], axis=1)
    sin = jnp.concatenate([-sin, sin, -sin, sin], axis=1)
    return cos, sin


def _block_diag_ones(n):
    i = jnp.arange(n) // HD
    return (i[:, None] == i[None, :]).astype(BF16)


def kernel(x, c, ctx, c_ctx, mod_w, mod_b, ln1_g, ln2_g, w_in_ab, w_out_ab, q_norm_a, k_norm_a, conv_w, conv_b,
           conv_ln_g, conv_ln_b, w_in_cd, w_out_cd, q_norm_c, k_norm_c, sink_c, pool_w, pool_scale,
           rt_grp_w, rt_grp_b, rt_exp_w, rt_exp_b, ex_gate, ex_up, ex_down):
    b, s_lat, _ = x.shape
    n_ctx = ctx.shape[1]
    assert n_ctx == TM and s_lat % TM == 0 and b <= 8

    cc = jnp.zeros((16, D), F32).at[0:b].set(c).at[8].set(c_ctx)
    mod = _modulation(cc, mod_w, mod_b)

    def modsel(l):
        return jnp.stack([jnp.broadcast_to(mod[l, 8], (b, 6 * D)), mod[l, 0:b]], axis=1).reshape(b, 2, 1, 6 * D)

    cos, sin = _rope_tables(n_ctx, s_lat)
    bdq = _block_diag_ones(QW)
    bdk = _block_diag_ones(KW)
    tri = (jnp.arange(TM)[:, None] > jnp.arange(TM)[None, :]).astype(BF16)
    upper = (jnp.arange(128)[:, None] < jnp.arange(128)[None, :]).astype(BF16)

    def router(l):
        rw = jnp.zeros((D, 128), F32).at[:, 0:N_EXP].set(rt_exp_w[l]).at[:, N_EXP:N_EXP + N_GROUPS].set(rt_grp_w[l])
        rb = jnp.zeros((1, 128), F32).at[0, 0:N_EXP].set(rt_exp_b[l]).at[0, N_EXP:N_EXP + N_GROUPS].set(rt_grp_b[l])
        return rw, rb

    def tile_gain(g, n):
        return jnp.tile(g, n).reshape(1, n * HD)

    s_tot = n_ctx + s_lat

    ms = modsel(0)
    q, k, v, glu = _project(ctx, x, 0, s_tot, ms, ln1_g[0:1], w_in_ab[0].astype(BF16), cos, sin,
                            tile_gain(q_norm_a[0], NH), tile_gain(k_norm_a[0], NKV), bdq, bdk, glu=True)
    att = _dense_attention(q, k, v, n_ctx)
    cv = _conv_module(glu, conv_w[0], conv_b[0:1], conv_ln_g[0:1], conv_ln_b[0:1], n_ctx)
    rw, rb = router(0)
    x1, h2, rinfo, g8 = _out_project(att, cv, (ctx, x), ms, w_out_ab[0].astype(BF16), ln2_g[0:1], rw, rb, tri, upper)
    xc = _moe(x1, h2, rinfo, g8, ms, ex_gate, ex_up, ex_down, 0, with_ctx=True)

    ms = modsel(1)
    q, k, v, u = _project(xc, xc, n_ctx // TM, s_tot, ms, ln1_g[1:2], w_in_cd[0].astype(BF16), cos, sin,
                          tile_gain(q_norm_c[0], NH), tile_gain(k_norm_c[0], NKV), bdq, bdk, glu=False)
    att = _window_attention(sink_c[0], q, k, v, n_ctx)
    pm = _pool_mixer(u, pool_w[0].astype(BF16), pool_scale[0:1], n_ctx)
    rw, rb = router(1)
    x1, h2, rinfo, g8 = _out_project(att, pm, (xc,), ms, w_out_cd[0].astype(BF16), ln2_g[1:2], rw, rb, tri, upper)
    return _moe(x1, h2, rinfo, g8, ms, ex_gate, ex_up, ex_down, 1, with_ctx=False)
```

```python
import functools

import jax
import jax.numpy as jnp
from jax import lax
from jax.experimental import pallas as pl
from jax.experimental.pallas import tpu as pltpu

F32 = jnp.float32
BF16 = jnp.bfloat16
U32 = jnp.uint32

D = 1024
HD = 64
NH = 8
NKV = 2
GQ = NH // NKV
QW = NH * HD
KW = NKV * HD
UW = 512
EPS = 1e-6
ROPE_THETA = 10000.0
GRID_W = 64
CONV_K = 31
WINDOW = 128
POOL_SIZES = (2, 4, 8, 16)
N_GROUPS = 4
PER_GROUP = 8
N_EXP = N_GROUPS * PER_GROUP
D_EXP = D // 2

TM = 256
TQ = 128
TQD = 256
TKL = 512
SUB = 64
HALO = 16
BLK = 512
GRP = 8
NG = 96
LR = NG * GRP
VMEM_LIMIT = 56 * 1024 * 1024


def _sigmoid(x):
    return 1.0 / (1.0 + jnp.exp(-x))


def _dot(a, b):
    return jnp.dot(a, b, preferred_element_type=F32)


def _dot_nt(a, b):
    return lax.dot_general(a, b, (((1,), (1,)), ((), ())), preferred_element_type=F32)


def _split(a):
    hi = a.astype(BF16)
    lo = (a - hi.astype(F32)).astype(BF16)
    return hi, lo


def _dot3(a, w):
    a_hi, a_lo = _split(a)
    w_hi, w_lo = _split(w)
    return _dot(a_hi, w_hi) + _dot(a_lo, w_hi) + _dot(a_hi, w_lo)


def _rmsnorm(x, g):
    return x * lax.rsqrt(jnp.mean(x * x, axis=-1, keepdims=True) + EPS) * g


def _params(sem, vmem=VMEM_LIMIT):
    return pltpu.CompilerParams(dimension_semantics=sem, vmem_limit_bytes=vmem)


def _mod_kernel(c_ref, w_ref, b_ref, o_ref):
    a = c_ref[...]
    a = a * _sigmoid(a)
    o_ref[0] = _dot3(a, w_ref[0]) + b_ref[0]


def _modulation(cc, mod_w, mod_b):
    depth = mod_w.shape[0]
    tn = 1024
    return pl.pallas_call(
        _mod_kernel,
        grid=(depth, 6 * D // tn),
        in_specs=[
            pl.BlockSpec((16, D), lambda l, j: (0, 0)),
            pl.BlockSpec((1, D, tn), lambda l, j: (l, 0, j)),
            pl.BlockSpec((1, 1, tn), lambda l, j: (l, 0, j)),
        ],
        out_specs=pl.BlockSpec((1, 16, tn), lambda l, j: (l, 0, j)),
        out_shape=jax.ShapeDtypeStruct((depth, 16, 6 * D), F32),
        compiler_params=_params(("arbitrary", "arbitrary")),
        name="modulation",
    )(cc, mod_w, mod_b.reshape(depth, 1, 6 * D))


def _swap_halves(t):
    w = t.shape[1]
    lane = lax.broadcasted_iota(jnp.int32, t.shape, 1)
    first = (lane & (HD - 1)) < (HD // 2)
    return jnp.where(first, pltpu.roll(t, w - HD // 2, 1), pltpu.roll(t, HD // 2, 1))


def _head_norm_rope(t, bd, g, cos, sin, scale):
    hi, lo = _split(t * t)
    ssq = _dot(hi, bd) + _dot(lo, bd)
    tn = t * lax.rsqrt(ssq * (1.0 / HD) + EPS) * g
    n = t.shape[1] // 128
    if n > 1:
        cos = jnp.concatenate([cos] * n, axis=1)
        sin = jnp.concatenate([sin] * n, axis=1)
    out = tn * cos + _swap_halves(tn) * sin
    return out * scale if scale != 1.0 else out


def _tile_rows(c_ref, x_ref):
    return jnp.where(pl.program_id(1) == 0, c_ref[0], x_ref[0])


def _proj_kernel(c_ref, x_ref, mod_ref, lng_ref, w_ref, cos_ref, sin_ref, qg_ref, kg_ref, bdq_ref, bdk_ref,
                 q_ref, k_ref, v_ref, u_ref, *, glu):
    mod = mod_ref[0, 0]
    h = _rmsnorm(_tile_rows(c_ref, x_ref), lng_ref[...]) * (1.0 + mod[:, D:2 * D]) + mod[:, 0:D]
    px = _dot(h.astype(BF16), w_ref[...])
    cos = cos_ref[...]
    sin = sin_ref[...]
    q = _head_norm_rope(px[:, 0:QW], bdq_ref[...], qg_ref[...], cos, sin, HD ** -0.5)
    k = _head_norm_rope(px[:, QW:QW + KW], bdk_ref[...], kg_ref[...], cos, sin, 1.0)
    v = px[:, QW + KW:QW + 2 * KW]
    for hh in range(NH):
        q_ref[0, hh] = q[:, hh * HD:(hh + 1) * HD].astype(BF16)
    lane = lax.broadcasted_iota(jnp.int32, v.shape, 1)
    ones_col = jnp.where(lane == HD, 1.0, 0.0)
    for j in range(NKV):
        k_ref[0, j] = k[:, j * HD:(j + 1) * HD].astype(BF16)
        vj = v if j == 0 else pltpu.roll(v, KW - j * HD, 1)
        v_ref[0, j] = jnp.where(lane < HD, vj, ones_col).astype(BF16)
    o_u = QW + 2 * KW
    if glu:
        u_ref[0] = px[:, o_u:o_u + UW] * _sigmoid(px[:, o_u + UW:o_u + 2 * UW])
    else:
        u_ref[0] = px[:, o_u:o_u + UW]


def _token_specs(lat_first):
    return [pl.BlockSpec((1, TM, D), lambda bi, i: (bi, 0, 0)),
            pl.BlockSpec((1, TM, D), lambda bi, i: (bi, jnp.maximum(i - 1, 0) + lat_first, 0))]


def _project(ctx_rows, lat_rows, lat_first, s_tot, modsel, ln_g, w_in, cos, sin, q_g, k_g, bdq, bdk, *, glu):
    b = ctx_rows.shape[0]
    n_in = w_in.shape[1]
    nt = s_tot // TM
    const = lambda bi, i: (0, 0)
    return pl.pallas_call(
        functools.partial(_proj_kernel, glu=glu),
        grid=(b, nt),
        in_specs=_token_specs(lat_first) + [
            pl.BlockSpec((1, 1, 1, 6 * D), lambda bi, i: (bi, jnp.minimum(i, 1), 0, 0)),
            pl.BlockSpec((1, D), const),
            pl.BlockSpec((D, n_in), const),
            pl.BlockSpec((TM, 128), lambda bi, i: (i, 0)),
            pl.BlockSpec((TM, 128), lambda bi, i: (i, 0)),
            pl.BlockSpec((1, QW), const),
            pl.BlockSpec((1, KW), const),
            pl.BlockSpec((QW, QW), const),
            pl.BlockSpec((KW, KW), const),
        ],
        out_specs=[
            pl.BlockSpec((1, NH, TM, HD), lambda bi, i: (bi, 0, i, 0)),
            pl.BlockSpec((1, NKV, TM, HD), lambda bi, i: (bi, 0, i, 0)),
            pl.BlockSpec((1, NKV, TM, 2 * HD), lambda bi, i: (bi, 0, i, 0)),
            pl.BlockSpec((1, TM, UW), lambda bi, i: (bi, i, 0)),
        ],
        out_shape=[
            jax.ShapeDtypeStruct((b, NH, s_tot, HD), BF16),
            jax.ShapeDtypeStruct((b, NKV, s_tot, HD), BF16),
            jax.ShapeDtypeStruct((b, NKV, s_tot, 2 * HD), BF16),
            jax.ShapeDtypeStruct((b, s_tot, UW), F32),
        ],
        compiler_params=_params(("arbitrary", "arbitrary")),
        name="in_projection",
    )(ctx_rows, lat_rows, modsel, ln_g, w_in, cos, sin, q_g, k_g, bdq, bdk)


def _merge_heads(o):
    tq = o.shape[0] // GQ
    return jnp.concatenate([o[g * tq:(g + 1) * tq] for g in range(GQ)], axis=1)


def _lane_max(s):
    return functools.reduce(jnp.maximum, [s[:, j * 128:(j + 1) * 128] for j in range(s.shape[1] // 128)])


def _dense_attn_kernel(q_ref, k_ref, v_ref, o_ref, sc_sc, sl_sc, m_sc, *, n_ctx, s_tot):
    qi = pl.program_id(2)
    q = q_ref[0].reshape(GQ * TQD, HD)
    n_lat = (s_tot - n_ctx) // TKL

    def chunks(n):
        spans = [(0, n_ctx)] + [(n_ctx + c * TKL, n_ctx + (c + 1) * TKL) for c in range(n)]
        return list(zip(spans, [sc_sc] + [sl_sc.at[c] for c in range(n)]))

    def scores(n):
        for i, ((lo, hi), slot) in enumerate(chunks(n)):
            s = _dot_nt(q, k_ref[0, 0, lo:hi, :])
            slot[...] = s
            m_sc[i] = _lane_max(s)
        m_lane = functools.reduce(jnp.maximum, [m_sc[i] for i in range(n + 1)])
        m_sc[0] = jnp.broadcast_to(jnp.max(m_lane, axis=1, keepdims=True), m_lane.shape)

    def weighted(n):
        acc = None
        for (lo, hi), slot in chunks(n):
            m = jnp.concatenate([m_sc[0]] * ((hi - lo) // 128), axis=1)
            part = _dot(jnp.exp(slot[...] - m).astype(BF16), v_ref[0, 0, lo:hi, :])
            acc = part if acc is None else acc + part
        o_ref[0] = _merge_heads(acc[:, 0:HD] / acc[:, HD:HD + 1]).astype(BF16)

    is_lat = qi * TQD >= n_ctx
    pl.when(is_lat)(lambda: scores(n_lat))
    pl.when(jnp.logical_not(is_lat))(lambda: scores(0))
    pl.when(is_lat)(lambda: weighted(n_lat))
    pl.when(jnp.logical_not(is_lat))(lambda: weighted(0))


def _dense_attention(q, k, v, n_ctx):
    b, _, s_tot, _ = q.shape
    rows = GQ * TQD
    return pl.pallas_call(
        functools.partial(_dense_attn_kernel, n_ctx=n_ctx, s_tot=s_tot),
        grid=(b, NKV, s_tot // TQD),
        in_specs=[
            pl.BlockSpec((1, GQ, TQD, HD), lambda bi, j, i: (bi, j, i, 0)),
            pl.BlockSpec((1, 1, s_tot, HD), lambda bi, j, i: (bi, j, 0, 0)),
            pl.BlockSpec((1, 1, s_tot, 2 * HD), lambda bi, j, i: (bi, j, 0, 0)),
        ],
        out_specs=pl.BlockSpec((1, TQD, GQ * HD), lambda bi, j, i: (bi, i, j)),
        out_shape=jax.ShapeDtypeStruct((b, s_tot, QW), BF16),
        scratch_shapes=[
            pltpu.VMEM((rows, n_ctx), F32),
            pltpu.VMEM(((s_tot - n_ctx) // TKL, rows, TKL), F32),
            pltpu.VMEM(((s_tot - n_ctx) // TKL + 1, rows, 128), F32),
        ],
        compiler_params=_params(("arbitrary", "arbitrary", "arbitrary")),
        name="dense_attention",
    )(q, k, v)


def _window_attn_kernel(sink_ref, q_ref, k_ref, v_ref, o_ref, sc_sc, sw_sc, m_sc, e_sc, *, n_ctx, s_tot):
    qi = pl.program_id(1)
    span = TQ + 2 * WINDOW
    q0 = n_ctx + qi * TQ
    start = pl.multiple_of(jnp.clip(q0 - WINDOW, n_ctx, s_tot - span), 128)
    row = lax.broadcasted_iota(jnp.int32, (TQ, span), 0)
    col = lax.broadcasted_iota(jnp.int32, (TQ, span), 1)
    band = jnp.where(jnp.abs((q0 - start) + row - col) <= WINDOW, 0.0, -jnp.inf)
    band = jnp.concatenate([band] * GQ, axis=0)
    for j in range(NKV):
        q = q_ref[0, j * GQ:(j + 1) * GQ].reshape(GQ * TQ, HD)
        s_c = _dot_nt(q, k_ref[0, j, 0:n_ctx, :])
        s_w = _dot_nt(q, k_ref[0, j, pl.ds(start, span), :]) + band
        sc_sc[j] = s_c
        sw_sc[j] = s_w
        sink = jnp.concatenate([jnp.full((TQ, 128), sink_ref[j * GQ + g], F32) for g in range(GQ)], axis=0)
        m_lane = jnp.maximum(_lane_max(s_c), _lane_max(s_w))
        m_sc[j] = jnp.maximum(jnp.broadcast_to(jnp.max(m_lane, axis=1, keepdims=True), m_lane.shape), sink)
        e_sc[j] = jnp.exp(sink - m_sc[j])
    for j in range(NKV):
        m = m_sc[j]
        p_c = jnp.exp(sc_sc[j] - jnp.concatenate([m] * (n_ctx // 128), axis=1)).astype(BF16)
        p_w = jnp.exp(sw_sc[j] - jnp.concatenate([m] * (span // 128), axis=1)).astype(BF16)
        acc = _dot(p_c, v_ref[0, j, 0:n_ctx, :]) + _dot(p_w, v_ref[0, j, pl.ds(start, span), :])
        l = acc[:, HD:HD + 1] + e_sc[j][:, 0:1]
        o_ref[0, :, j * GQ * HD:(j + 1) * GQ * HD] = _merge_heads(acc[:, 0:HD] / l).astype(BF16)


def _window_attention(sink, q, k, v, n_ctx):
    b, _, s_tot, _ = q.shape
    s_lat = s_tot - n_ctx
    off = n_ctx // TQ
    return pl.pallas_call(
        functools.partial(_window_attn_kernel, n_ctx=n_ctx, s_tot=s_tot),
        grid_spec=pltpu.PrefetchScalarGridSpec(
            num_scalar_prefetch=1,
            grid=(b, s_lat // TQ),
            in_specs=[
                pl.BlockSpec((1, NH, TQ, HD), lambda bi, i, sk: (bi, 0, i + off, 0)),
                pl.BlockSpec((1, NKV, s_tot, HD), lambda bi, i, sk: (bi, 0, 0, 0)),
                pl.BlockSpec((1, NKV, s_tot, 2 * HD), lambda bi, i, sk: (bi, 0, 0, 0)),
            ],
            out_specs=pl.BlockSpec((1, TQ, QW), lambda bi, i, sk: (bi, i, 0)),
            scratch_shapes=[
                pltpu.VMEM((NKV, GQ * TQ, n_ctx), F32),
                pltpu.VMEM((NKV, GQ * TQ, TQ + 2 * WINDOW), F32),
                pltpu.VMEM((NKV, GQ * TQ, 128), F32),
                pltpu.VMEM((NKV, GQ * TQ, 128), F32),
            ],
        ),
        out_shape=jax.ShapeDtypeStruct((b, s_lat, QW), BF16),
        compiler_params=_params(("arbitrary", "arbitrary")),
        name="window_attention",
    )(sink, q, k, v)


def _conv_kernel(g_ref, cw_ref, cb_ref, lg_ref, lb_ref, o_ref, pad_sc, win_sc, *, n_ctx, s_tot):
    zeros = jnp.zeros((HALO, UW), F32)
    pad_sc[0:HALO, :] = zeros
    pad_sc[HALO + n_ctx:2 * HALO + n_ctx, :] = zeros
    pad_sc[2 * HALO + s_tot:3 * HALO + s_tot, :] = zeros

    def fill(i, carry):
        src = pl.multiple_of(i * TM, TM)
        dst = pl.multiple_of(src + HALO + jnp.where(src >= n_ctx, HALO, 0), 8)
        pad_sc[pl.ds(dst, TM), :] = g_ref[0, pl.ds(src, TM), :]
        return carry

    lax.fori_loop(0, s_tot // TM, fill, 0)
    half = CONV_K // 2

    def tile(i, carry):
        src = pl.multiple_of(i * SUB, SUB)
        base = pl.multiple_of(src + jnp.where(src >= n_ctx, HALO, 0), 8)
        win = pad_sc[pl.ds(base, SUB + 2 * HALO), :]
        keep = SUB + 2 * HALO - 8
        for r in range(8):
            win_sc[r, 0:keep, :] = win[r:r + keep, :]
        acc = jnp.zeros((SUB, UW), F32) + cb_ref[...]
        for t in range(CONV_K):
            a, r = divmod(HALO - half + t, 8)
            acc = acc + win_sc[r, 8 * a:8 * a + SUB, :] * cw_ref[t:t + 1, :]
        mu = jnp.mean(acc, axis=-1, keepdims=True)
        xc = acc - mu
        var = jnp.mean(xc * xc, axis=-1, keepdims=True)
        yn = xc * lax.rsqrt(var + EPS) * lg_ref[...] + lb_ref[...]
        o_ref[0, pl.ds(src, SUB), :] = (yn * _sigmoid(yn)).astype(BF16)
        return carry

    lax.fori_loop(0, s_tot // SUB, tile, 0)


def _conv_module(glu, conv_w, conv_b, ln_g, ln_b, n_ctx):
    b, s_tot, _ = glu.shape
    const = lambda bi: (0, 0)
    return pl.pallas_call(
        functools.partial(_conv_kernel, n_ctx=n_ctx, s_tot=s_tot),
        grid=(b,),
        in_specs=[
            pl.BlockSpec((1, s_tot, UW), lambda bi: (bi, 0, 0)),
            pl.BlockSpec((CONV_K, UW), const),
            pl.BlockSpec((1, UW), const),
            pl.BlockSpec((1, UW), const),
            pl.BlockSpec((1, UW), const),
        ],
        out_specs=pl.BlockSpec((1, s_tot, UW), lambda bi: (bi, 0, 0)),
        out_shape=jax.ShapeDtypeStruct((b, s_tot, UW), BF16),
        scratch_shapes=[pltpu.VMEM((s_tot + 3 * HALO, UW), F32), pltpu.VMEM((8, SUB + 2 * HALO, UW), F32)],
        compiler_params=_params(("arbitrary",)),
        name="conv_module",
    )(glu, conv_w, conv_b, ln_g, ln_b)


def _pool_kernel(u_ref, pw_ref, ps_ref, o_ref, pad_sc, *, n_ctx, s_lat):
    zeros = jnp.zeros((HALO, UW), F32)
    pad_sc[0:HALO, :] = zeros
    pad_sc[HALO + s_lat:2 * HALO + s_lat, :] = zeros

    def fill(i, carry):
        src = pl.multiple_of(i * TM, TM)
        pad_sc[pl.ds(pl.multiple_of(src + HALO, 8), TM), :] = u_ref[0, pl.ds(pl.multiple_of(src + n_ctx, 8), TM), :]
        return carry

    lax.fori_loop(0, s_lat // TM, fill, 0)
    gw = UW // len(POOL_SIZES)

    def tile(i, carry):
        src = pl.multiple_of(i * SUB, SUB)
        win = pad_sc[pl.ds(src, SUB + 2 * HALO), :]
        t = src + lax.broadcasted_iota(jnp.int32, (SUB, 1), 0)
        outs = []
        for gi, w in enumerate(POOL_SIZES):
            lanes = slice(gi * gw, (gi + 1) * gw)
            tot = jnp.zeros((SUB, gw), F32)
            for d in range(-(w // 2), w - w // 2):
                tot = tot + win[HALO + d:HALO + d + SUB, lanes]
            lo = jnp.clip(t - w // 2, 0, s_lat)
            hi = jnp.clip(t - w // 2 + w, 0, s_lat)
            p = tot / (hi - lo).astype(F32) - win[HALO:HALO + SUB, lanes]
            outs.append(_dot(p.astype(BF16), pw_ref[gi]))
        y = jnp.concatenate(outs, axis=1) * ps_ref[...]
        o_ref[0, pl.ds(src, SUB), :] = y.astype(BF16)
        return carry

    lax.fori_loop(0, s_lat // SUB, tile, 0)


def _pool_mixer(u, pool_w, pool_scale, n_ctx):
    b, s_tot, _ = u.shape
    s_lat = s_tot - n_ctx
    gw = UW // len(POOL_SIZES)
    return pl.pallas_call(
        functools.partial(_pool_kernel, n_ctx=n_ctx, s_lat=s_lat),
        grid=(b,),
        in_specs=[
            pl.BlockSpec((1, s_tot, UW), lambda bi: (bi, 0, 0)),
            pl.BlockSpec((len(POOL_SIZES), gw, gw), lambda bi: (0, 0, 0)),
            pl.BlockSpec((1, UW), lambda bi: (0, 0)),
        ],
        out_specs=pl.BlockSpec((1, s_lat, UW), lambda bi: (bi, 0, 0)),
        out_shape=jax.ShapeDtypeStruct((b, s_lat, UW), BF16),
        scratch_shapes=[pltpu.VMEM((s_lat + 2 * HALO, UW), F32)],
        compiler_params=_params(("arbitrary",)),
        name="pool_mixer",
    )(u, pool_w, pool_scale)


def _route(logits):
    shape = logits.shape
    lane = lax.broadcasted_iota(jnp.int32, shape, 1)
    lanef = lane.astype(F32)
    is_g = (lane >= N_EXP) & (lane < N_EXP + N_GROUPS)
    big = 1e9
    lg = jnp.where(is_g, logits, -jnp.inf)
    gmax = jnp.max(lg, axis=1, keepdims=True)
    gsel = jnp.min(jnp.where(is_g & (lg == gmax), lanef, big), axis=1, keepdims=True) - float(N_EXP)
    gsum = jnp.sum(jnp.exp(lg - gmax), axis=1, keepdims=True)
    g_w = 1.0 / gsum
    in_grp = (lane < N_EXP) & ((lane >> 3).astype(F32) == gsel)
    el = jnp.where(in_grp, logits, -jnp.inf)
    emax = jnp.max(el, axis=1, keepdims=True)
    ee = jnp.exp(el - emax)
    p = ee / jnp.sum(ee, axis=1, keepdims=True)
    p1 = jnp.max(jnp.where(in_grp, p, -1.0), axis=1, keepdims=True)
    i1 = jnp.min(jnp.where(in_grp & (p == p1), lanef, big), axis=1, keepdims=True)
    rest = in_grp & (lanef != i1)
    p2 = jnp.max(jnp.where(rest, p, -1.0), axis=1, keepdims=True)
    i2 = jnp.min(jnp.where(rest & (p == p2), lanef, big), axis=1, keepdims=True)
    w1 = g_w * p1 / (p1 + p2)
    w2 = g_w * p2 / (p1 + p2)
    onehot = jnp.where((lanef == i1) | (lanef == i2), 1.0, 0.0)
    return lanef, i1, i2, w1, w2, onehot


def _outproj_kernel(a_ref, b_ref, *refs, with_ctx):
    if with_ctx:
        c_ref, x_ref = refs[0:2]
        x_in = _tile_rows(c_ref, x_ref)
    else:
        x_in = refs[0][0]
    mod_ref, w_ref, lng_ref, rw_ref, rb_ref, tri_ref, upper_ref, x1_ref, h2_ref, r_ref, g8_ref = refs[-11:]
    mod = mod_ref[0, 0]
    y = _dot(a_ref[0], w_ref[0:QW, :]) + _dot(b_ref[0], w_ref[QW:QW + UW, :])
    x1 = x_in + mod[:, 2 * D:3 * D] * y
    x1_ref[0] = x1
    h2 = _rmsnorm(x1, lng_ref[...]) * (1.0 + mod[:, 4 * D:5 * D]) + mod[:, 3 * D:4 * D]
    h2_ref[0] = h2.astype(BF16)
    logits = _dot3(h2, rw_ref[...]) + rb_ref[...]
    lanef, i1, i2, w1, w2, onehot = _route(logits)
    before = _dot(tri_ref[...], onehot.astype(BF16))
    cnt = jnp.sum(onehot, axis=0, keepdims=True)
    g8 = jnp.floor((cnt + float(GRP - 1)) * (1.0 / GRP))
    goff = _dot(jnp.broadcast_to(g8, (8, 128)).astype(BF16), upper_ref[...])[0:1, :]
    local = before + float(GRP) * goff
    l1 = jnp.sum(jnp.where(lanef == i1, local, 0.0), axis=1, keepdims=True)
    l2 = jnp.sum(jnp.where(lanef == i2, local, 0.0), axis=1, keepdims=True)
    info = jnp.where(lanef == 0.0, i1, jnp.where(lanef == 1.0, i2, jnp.where(lanef == 2.0, w1, jnp.where(
        lanef == 3.0, w2, jnp.where(lanef == 4.0, l1, jnp.where(lanef == 5.0, l2, 0.0))))))
    r_ref[0] = info
    g8_ref[0, 0] = g8


def _out_project(a, bmix, residual, modsel, w_out, ln_g, rw, rb, tri, upper):
    b, s_out, _ = a.shape
    nt = s_out // TM
    const = lambda bi, i: (0, 0)
    with_ctx = len(residual) == 2
    sel = (lambda i: jnp.minimum(i, 1)) if with_ctx else (lambda i: 1)
    tok = lambda bi, i: (bi, i, 0)
    first_lat = residual[0].shape[1] // TM - nt
    res_specs = _token_specs(0) if with_ctx else [pl.BlockSpec((1, TM, D), lambda bi, i: (bi, i + first_lat, 0))]
    return pl.pallas_call(
        functools.partial(_outproj_kernel, with_ctx=with_ctx),
        grid=(b, nt),
        in_specs=[
            pl.BlockSpec((1, TM, QW), tok),
            pl.BlockSpec((1, TM, UW), tok),
        ] + res_specs + [
            pl.BlockSpec((1, 1, 1, 6 * D), lambda bi, i: (bi, sel(i), 0, 0)),
            pl.BlockSpec((D, D), const),
            pl.BlockSpec((1, D), const),
            pl.BlockSpec((D, 128), const),
            pl.BlockSpec((1, 128), const),
            pl.BlockSpec((TM, TM), const),
            pl.BlockSpec((128, 128), const),
        ],
        out_specs=[
            pl.BlockSpec((1, TM, D), tok),
            pl.BlockSpec((1, TM, D), tok),
            pl.BlockSpec((1, TM, 128), tok),
            pl.BlockSpec((1, 1, 1, 128), lambda bi, i: (bi, i, 0, 0)),
        ],
        out_shape=[
            jax.ShapeDtypeStruct((b, s_out, D), F32),
            jax.ShapeDtypeStruct((b, s_out, D), BF16),
            jax.ShapeDtypeStruct((b, s_out, 128), F32),
            jax.ShapeDtypeStruct((b, nt, 1, 128), F32),
        ],
        compiler_params=_params(("arbitrary", "arbitrary")),
        name="out_projection_router",
    )(a, bmix, *residual, modsel, w_out, ln_g, rw, rb, tri, upper)


def _pack_halves(x, exact=False):
    half = x.shape[1] // 2
    a, b = x[:, 0:half], x[:, half:]
    if not exact:
        a, b = a.astype(BF16).astype(F32), b.astype(BF16).astype(F32)
    return lax.bitcast_convert_type(a, U32) | (lax.bitcast_convert_type(b, U32) >> 16)


def _unpack_halves(w):
    a = lax.bitcast_convert_type(w & jnp.uint32(0xFFFF0000), F32)
    b = lax.bitcast_convert_type(w << 16, F32)
    return jnp.concatenate([a, b], axis=1).astype(BF16)


def _group_copy(src_ref, src_row, dst_ref, dst_row, sem):
    return pltpu.make_async_copy(src_ref.at[pl.ds(pl.multiple_of(src_row, GRP), GRP), :],
                                 dst_ref.at[pl.ds(pl.multiple_of(dst_row, GRP), GRP), :], sem)


def _wait_groups(count, src_ref, dst_ref, sem):
    def wait(g, carry):
        _group_copy(src_ref, 0, dst_ref, 0, sem).wait()
        return carry

    lax.fori_loop(0, count, wait, 0)


def _dispatch_kernel(gdst_ref, ng_ref, zdst_ref, nz_ref, h_ref, r_ref, xs_ref, loc, zbuf, sems, zsem):
    i = pl.program_id(0)
    n = pl.num_programs(0)

    @pl.when(i == 0)
    def _():
        zbuf[...] = jnp.zeros(zbuf.shape, U32)

        def per_expert(e, carry):
            def zero(j, c):
                _group_copy(zbuf, 0, xs_ref, zdst_ref[e] + j * GRP, zsem).start()
                return c

            lax.fori_loop(0, nz_ref[e], zero, 0)
            return carry

        lax.fori_loop(0, N_EXP, per_expert, 0)
        _wait_groups(nz_ref[N_EXP], zbuf, xs_ref, zsem)

    def run(slot):
        buf = loc.at[slot]
        sem = sems.at[slot]

        @pl.when(i >= 2)
        def _():
            _wait_groups(ng_ref[i - 2], buf, xs_ref, sem)

        rt = r_ref[...].T
        rows = lax.broadcasted_iota(jnp.int32, (LR, TM), 0).astype(F32)
        perm = jnp.where((rows == rt[4:5, :]) | (rows == rt[5:6, :]), 1.0, 0.0).astype(BF16)
        buf[...] = _pack_halves(_dot(perm, h_ref[...]), exact=True)

        def issue(g, carry):
            _group_copy(buf, g * GRP, xs_ref, gdst_ref[i * NG + g], sem).start()
            return carry

        lax.fori_loop(0, ng_ref[i], issue, 0)

        @pl.when(i == n - 1)
        def _():
            _wait_groups(ng_ref[i], buf, xs_ref, sem)

            @pl.when(i >= 1)
            def _():
                _wait_groups(ng_ref[i - 1], loc.at[1 - slot], xs_ref, sems.at[1 - slot])

    pl.when(i % 2 == 0)(lambda: run(0))
    pl.when(i % 2 == 1)(lambda: run(1))


def _dispatch(gdst, ng, zdst, nz, h2, rinfo, cap):
    t = h2.shape[0]
    tok = lambda i, *_: (i, 0)
    return pl.pallas_call(
        _dispatch_kernel,
        grid_spec=pltpu.PrefetchScalarGridSpec(
            num_scalar_prefetch=4,
            grid=(t // TM,),
            in_specs=[
                pl.BlockSpec((TM, D), tok),
                pl.BlockSpec((TM, 128), tok),
            ],
            out_specs=pl.BlockSpec(memory_space=pl.ANY),
            scratch_shapes=[
                pltpu.VMEM((2, LR, D // 2), U32),
                pltpu.VMEM((GRP, D // 2), U32),
                pltpu.SemaphoreType.DMA((2,)),
                pltpu.SemaphoreType.DMA,
            ],
        ),
        out_shape=jax.ShapeDtypeStruct((cap, D // 2), U32),
        compiler_params=_params(("arbitrary",)),
        name="moe_dispatch",
    )(gdst, ng, zdst, nz, h2, rinfo)


def _expert_kernel(be_ref, nu_ref, x_ref, wg_ref, wu_ref, wd_ref, y_ref, wg_sc, wu_sc, wd_sc):
    i = pl.program_id(0)
    used = i < nu_ref[0]
    fresh = (i == 0) | (be_ref[i] != be_ref[jnp.maximum(i - 1, 0)])

    @pl.when(used & fresh)
    def _():
        wg_sc[...] = wg_ref[0, 0].astype(BF16)
        wu_sc[...] = wu_ref[0, 0].astype(BF16)
        wd_sc[...] = wd_ref[0, 0].astype(BF16)

    @pl.when(used)
    def _():
        xb = _unpack_halves(x_ref[...])
        g = _dot(xb, wg_sc[...])
        u = _dot(xb, wu_sc[...])
        hid = (g * _sigmoid(g)) * u
        y_ref[...] = _pack_halves(_dot(hid.astype(BF16), wd_sc[...]))

    @pl.when(jnp.logical_not(used))
    def _():
        y_ref[...] = jnp.zeros(y_ref.shape, U32)


def _experts(blk_e, n_used, xs, wg, wu, wd, layer):
    cap = xs.shape[0]
    row = lambda i, be, nu: (jnp.minimum(i, nu[0] - 1), 0)
    wsel = lambda i, be, nu: (layer, be[i], 0, 0)
    return pl.pallas_call(
        _expert_kernel,
        grid_spec=pltpu.PrefetchScalarGridSpec(
            num_scalar_prefetch=2,
            grid=(cap // BLK,),
            in_specs=[
                pl.BlockSpec((BLK, D // 2), row),
                pl.BlockSpec((1, 1, D, D_EXP), wsel),
                pl.BlockSpec((1, 1, D, D_EXP), wsel),
                pl.BlockSpec((1, 1, D_EXP, D), wsel),
            ],
            out_specs=pl.BlockSpec((BLK, D // 2), lambda i, be, nu: (i, 0)),
            scratch_shapes=[
                pltpu.VMEM((D, D_EXP), BF16),
                pltpu.VMEM((D, D_EXP), BF16),
                pltpu.VMEM((D_EXP, D), BF16),
            ],
        ),
        out_shape=jax.ShapeDtypeStruct((cap, D // 2), U32),
        compiler_params=_params(("arbitrary",)),
        name="moe_experts",
    )(blk_e, n_used, xs, wg, wu, wd)


def _combine_kernel(gdst_ref, ng_ref, y_ref, x1_ref, r_ref, mod_ref, o_ref, loc, sems):
    i = pl.program_id(0)
    n = pl.num_programs(0)

    def fetch(tile, slot):
        def issue(g, carry):
            _group_copy(y_ref, gdst_ref[tile * NG + g], loc.at[slot], g * GRP, sems.at[slot]).start()
            return carry

        lax.fori_loop(0, ng_ref[tile], issue, 0)

    @pl.when(i == 0)
    def _():
        loc[...] = jnp.zeros(loc.shape, U32)
        fetch(0, 0)

    def run(slot):
        @pl.when(i + 1 < n)
        def _():
            fetch(i + 1, 1 - slot)

        _wait_groups(ng_ref[i], y_ref, loc.at[slot], sems.at[slot])
        ysb = _unpack_halves(loc[slot])
        info = r_ref[...]
        col = lax.broadcasted_iota(jnp.int32, (TM, LR), 1).astype(F32)
        pick = jnp.concatenate([jnp.where(col == info[:, 4:5], 1.0, 0.0).astype(BF16),
                                jnp.where(col == info[:, 5:6], 1.0, 0.0).astype(BF16)], axis=0)
        y12 = _dot(pick, ysb)
        g2 = mod_ref[0, 0][:, 5 * D:6 * D]
        o_ref[...] = x1_ref[...] + g2 * (info[:, 2:3] * y12[0:TM] + info[:, 3:4] * y12[TM:2 * TM])

    pl.when(i % 2 == 0)(lambda: run(0))
    pl.when(i % 2 == 1)(lambda: run(1))


def _combine(gdst, ng, ys, x1, rinfo, modsel, *, tiles_per_batch, with_ctx):
    t = x1.shape[0]
    if with_ctx:
        msel = lambda i, *_: (i // tiles_per_batch, jnp.minimum(i % tiles_per_batch, 1), 0, 0)
    else:
        msel = lambda i, *_: (i // tiles_per_batch, 1, 0, 0)
    tok = lambda i, *_: (i, 0)
    return pl.pallas_call(
        _combine_kernel,
        grid_spec=pltpu.PrefetchScalarGridSpec(
            num_scalar_prefetch=2,
            grid=(t // TM,),
            in_specs=[
                pl.BlockSpec(memory_space=pl.ANY),
                pl.BlockSpec((TM, D), tok),
                pl.BlockSpec((TM, 128), tok),
                pl.BlockSpec((1, 1, 1, 6 * D), msel),
            ],
            out_specs=pl.BlockSpec((TM, D), tok),
            scratch_shapes=[pltpu.VMEM((2, LR, D // 2), U32), pltpu.SemaphoreType.DMA((2,))],
        ),
        out_shape=jax.ShapeDtypeStruct((t, D), F32),
        compiler_params=_params(("arbitrary",)),
        name="moe_combine",
    )(gdst, ng, ys, x1, rinfo, modsel)


def _moe(x1, h2, rinfo, g8, modsel, wg, wu, wd, layer, *, with_ctx):
    b, s, _ = x1.shape
    t = b * s
    n_tiles = t // TM
    i32 = jnp.int32
    grp = g8.reshape(n_tiles, 128)[:, 0:N_EXP].astype(i32)
    n_blk = -(-(2 * t + n_tiles * N_EXP * (GRP - 1) + N_EXP * (BLK - 1)) // BLK)
    cap = n_blk * BLK
    goff_end = jnp.cumsum(grp, axis=1)
    ng = goff_end[:, N_EXP - 1]
    rows_e = GRP * jnp.sum(grp, axis=0)
    padded = (rows_e + BLK - 1) // BLK * BLK
    pend = jnp.cumsum(padded)
    pstart = pend - padded
    run_dst = pstart[None, :] + GRP * (jnp.cumsum(grp, axis=0) - grp)
    gidx = jnp.arange(NG, dtype=i32)[None, :, None]
    first_g = (goff_end - grp)[:, None, :]
    in_run = (gidx >= first_g) & (gidx < goff_end[:, None, :])
    gdst = jnp.sum(jnp.where(in_run, run_dst[:, None, :] + GRP * (gidx - first_g), 0), axis=2)
    gdst = gdst.reshape(n_tiles * NG).astype(i32)
    nz = (padded - rows_e) // GRP
    nz = jnp.concatenate([nz, jnp.sum(nz, keepdims=True)]).astype(i32)
    zdst = (pstart + rows_e).astype(i32)
    blk_e = jnp.minimum(jnp.sum((pend[None, :] <= (jnp.arange(n_blk, dtype=i32) * BLK)[:, None]).astype(i32), axis=1),
                        N_EXP - 1).astype(i32)
    n_used = (pend[N_EXP - 1:N_EXP] // BLK).astype(i32)
    info = rinfo.reshape(t, 128)
    xs = _dispatch(gdst, ng.astype(i32), zdst, nz, h2.reshape(t, D), info, cap)
    ys = _experts(blk_e, n_used, xs, wg, wu, wd, layer)
    out = _combine(gdst, ng.astype(i32), ys, x1.reshape(t, D), info, modsel, tiles_per_batch=s // TM,
                   with_ctx=with_ctx)
    return out.reshape(b, s, D)


def _rope_tables(n_ctx, s_lat):
    rows = s_lat // GRID_W
    row = jnp.repeat(jnp.arange(rows, dtype=F32), GRID_W)
    col = jnp.tile(jnp.arange(GRID_W, dtype=F32), rows)
    n_freq = HD // 4
    inv = ROPE_THETA ** (-jnp.arange(n_freq, dtype=F32) / n_freq)
    ang = jnp.concatenate([row[:, None] * inv, col[:, None] * inv], axis=-1)
    cos = jnp.concatenate([jnp.ones((n_ctx, HD // 2), F32), jnp.cos(ang)], axis=0)
    sin = jnp.concatenate([jnp.zeros((n_ctx, HD // 2), F32), jnp.sin(ang)], axis=0)
    cos = jnp.concatenate([cos, cos, cos, cos], axis=1)
    sin = jnp.concatenate([-sin, sin, -sin, sin], axis=1)
    return cos, sin


def _block_diag_ones(n):
    i = jnp.arange(n) // HD
    return (i[:, None] == i[None, :]).astype(BF16)


def kernel(x, c, ctx, c_ctx, mod_w, mod_b, ln1_g, ln2_g, w_in_ab, w_out_ab, q_norm_a, k_norm_a, conv_w, conv_b,
           conv_ln_g, conv_ln_b, w_in_cd, w_out_cd, q_norm_c, k_norm_c, sink_c, pool_w, pool_scale,
           rt_grp_w, rt_grp_b, rt_exp_w, rt_exp_b, ex_gate, ex_up, ex_down):
    b, s_lat, _ = x.shape
    n_ctx = ctx.shape[1]
    assert n_ctx == TM and s_lat % TM == 0 and b <= 8

    cc = jnp.zeros((16, D), F32).at[0:b].set(c).at[8].set(c_ctx)
    mod = _modulation(cc, mod_w, mod_b)

    def modsel(l):
        return jnp.stack([jnp.broadcast_to(mod[l, 8], (b, 6 * D)), mod[l, 0:b]], axis=1).reshape(b, 2, 1, 6 * D)

    cos, sin = _rope_tables(n_ctx, s_lat)
    bdq = _block_diag_ones(QW)
    bdk = _block_diag_ones(KW)
    tri = (jnp.arange(TM)[:, None] > jnp.arange(TM)[None, :]).astype(BF16)
    upper = (jnp.arange(128)[:, None] < jnp.arange(128)[None, :]).astype(BF16)

    def router(l):
        rw = jnp.zeros((D, 128), F32).at[:, 0:N_EXP].set(rt_exp_w[l]).at[:, N_EXP:N_EXP + N_GROUPS].set(rt_grp_w[l])
        rb = jnp.zeros((1, 128), F32).at[0, 0:N_EXP].set(rt_exp_b[l]).at[0, N_EXP:N_EXP + N_GROUPS].set(rt_grp_b[l])
        return rw, rb

    def tile_gain(g, n):
        return jnp.tile(g, n).reshape(1, n * HD)

    s_tot = n_ctx + s_lat

    ms = modsel(0)
    q, k, v, glu = _project(ctx, x, 0, s_tot, ms, ln1_g[0:1], w_in_ab[0].astype(BF16), cos, sin,
                            tile_gain(q_norm_a[0], NH), tile_gain(k_norm_a[0], NKV), bdq, bdk, glu=True)
    att = _dense_attention(q, k, v, n_ctx)
    cv = _conv_module(glu, conv_w[0], conv_b[0:1], conv_ln_g[0:1], conv_ln_b[0:1], n_ctx)
    rw, rb = router(0)
    x1, h2, rinfo, g8 = _out_project(att, cv, (ctx, x), ms, w_out_ab[0].astype(BF16), ln2_g[0:1], rw, rb, tri, upper)
    xc = _moe(x1, h2, rinfo, g8, ms, ex_gate, ex_up, ex_down, 0, with_ctx=True)

    ms = modsel(1)
    q, k, v, u = _project(xc, xc, n_ctx // TM, s_tot, ms, ln1_g[1:2], w_in_cd[0].astype(BF16), cos, sin,
                          tile_gain(q_norm_c[0], NH), tile_gain(k_norm_c[0], NKV), bdq, bdk, glu=False)
    att = _window_attention(sink_c[0], q, k, v, n_ctx)
    pm = _pool_mixer(u, pool_w[0].astype(BF16), pool_scale[0:1], n_ctx)
    rw, rb = router(1)
    x1, h2, rinfo, g8 = _out_project(att, pm, (xc,), ms, w_out_cd[0].astype(BF16), ln2_g[1:2], rw, rb, tri, upper)
    return _moe(x1, h2, rinfo, g8, ms, ex_gate, ex_up, ex_down, 1, with_ctx=False)
```

```python
import functools

import jax
import jax.numpy as jnp
from jax import lax
from jax.experimental import pallas as pl
from jax.experimental.pallas import tpu as pltpu

F32 = jnp.float32
BF16 = jnp.bfloat16
U32 = jnp.uint32

D = 1024
HD = 64
NH = 8
NKV = 2
GQ = NH // NKV
QW = NH * HD
KW = NKV * HD
UW = 512
EPS = 1e-6
ROPE_THETA = 10000.0
GRID_W = 64
CONV_K = 31
WINDOW = 128
POOL_SIZES = (2, 4, 8, 16)
N_GROUPS = 4
PER_GROUP = 8
N_EXP = N_GROUPS * PER_GROUP
D_EXP = D // 2

TM = 256
TQ = 128
TQD = 256
TKL = 512
SUB = 64
HALO = 16
BLK = 512
GRP = 8
NG = 96
LR = NG * GRP
VMEM_LIMIT = 56 * 1024 * 1024


def _sigmoid(x):
    return 1.0 / (1.0 + jnp.exp(-x))


def _dot(a, b):
    return jnp.dot(a, b, preferred_element_type=F32)


def _dot_nt(a, b):
    return lax.dot_general(a, b, (((1,), (1,)), ((), ())), preferred_element_type=F32)


def _split(a):
    hi = a.astype(BF16)
    lo = (a - hi.astype(F32)).astype(BF16)
    return hi, lo


def _dot3(a, w):
    a_hi, a_lo = _split(a)
    w_hi, w_lo = _split(w)
    return _dot(a_hi, w_hi) + _dot(a_lo, w_hi) + _dot(a_hi, w_lo)


def _rmsnorm(x, g):
    return x * lax.rsqrt(jnp.mean(x * x, axis=-1, keepdims=True) + EPS) * g


def _params(sem, vmem=VMEM_LIMIT):
    return pltpu.CompilerParams(dimension_semantics=sem, vmem_limit_bytes=vmem)


def _mod_kernel(c_ref, w_ref, b_ref, o_ref):
    a = c_ref[...]
    a = a * _sigmoid(a)
    o_ref[0] = _dot3(a, w_ref[0]) + b_ref[0]


def _modulation(cc, mod_w, mod_b):
    depth = mod_w.shape[0]
    tn = 1024
    return pl.pallas_call(
        _mod_kernel,
        grid=(depth, 6 * D // tn),
        in_specs=[
            pl.BlockSpec((16, D), lambda l, j: (0, 0)),
            pl.BlockSpec((1, D, tn), lambda l, j: (l, 0, j)),
            pl.BlockSpec((1, 1, tn), lambda l, j: (l, 0, j)),
        ],
        out_specs=pl.BlockSpec((1, 16, tn), lambda l, j: (l, 0, j)),
        out_shape=jax.ShapeDtypeStruct((depth, 16, 6 * D), F32),
        compiler_params=_params(("arbitrary", "arbitrary")),
        name="modulation",
    )(cc, mod_w, mod_b.reshape(depth, 1, 6 * D))


def _swap_halves(t):
    w = t.shape[1]
    lane = lax.broadcasted_iota(jnp.int32, t.shape, 1)
    first = (lane & (HD - 1)) < (HD // 2)
    return jnp.where(first, pltpu.roll(t, w - HD // 2, 1), pltpu.roll(t, HD // 2, 1))


def _head_norm_rope(t, bd, g, cos, sin, scale):
    hi, lo = _split(t * t)
    ssq = _dot(hi, bd) + _dot(lo, bd)
    tn = t * lax.rsqrt(ssq * (1.0 / HD) + EPS) * g
    n = t.shape[1] // 128
    if n > 1:
        cos = jnp.concatenate([cos] * n, axis=1)
        sin = jnp.concatenate([sin] * n, axis=1)
    out = tn * cos + _swap_halves(tn) * sin
    return out * scale if scale != 1.0 else out


def _tile_rows(c_ref, x_ref):
    return jnp.where(pl.program_id(1) == 0, c_ref[0], x_ref[0])


def _proj_kernel(c_ref, x_ref, mod_ref, lng_ref, w_ref, cos_ref, sin_ref, qg_ref, kg_ref, bdq_ref, bdk_ref,
                 q_ref, k_ref, v_ref, u_ref, *, glu):
    mod = mod_ref[0, 0]
    h = _rmsnorm(_tile_rows(c_ref, x_ref), lng_ref[...]) * (1.0 + mod[:, D:2 * D]) + mod[:, 0:D]
    px = _dot(h.astype(BF16), w_ref[...])
    cos = cos_ref[...]
    sin = sin_ref[...]
    q = _head_norm_rope(px[:, 0:QW], bdq_ref[...], qg_ref[...], cos, sin, HD ** -0.5)
    k = _head_norm_rope(px[:, QW:QW + KW], bdk_ref[...], kg_ref[...], cos, sin, 1.0)
    v = px[:, QW + KW:QW + 2 * KW]
    for hh in range(NH):
        q_ref[0, hh] = q[:, hh * HD:(hh + 1) * HD].astype(BF16)
    lane = lax.broadcasted_iota(jnp.int32, v.shape, 1)
    ones_col = jnp.where(lane == HD, 1.0, 0.0)
    for j in range(NKV):
        k_ref[0, j] = k[:, j * HD:(j + 1) * HD].astype(BF16)
        vj = v if j == 0 else pltpu.roll(v, KW - j * HD, 1)
        v_ref[0, j] = jnp.where(lane < HD, vj, ones_col).astype(BF16)
    o_u = QW + 2 * KW
    if glu:
        u_ref[0] = px[:, o_u:o_u + UW] * _sigmoid(px[:, o_u + UW:o_u + 2 * UW])
    else:
        u_ref[0] = px[:, o_u:o_u + UW]


def _token_specs(lat_first):
    return [pl.BlockSpec((1, TM, D), lambda bi, i: (bi, 0, 0)),
            pl.BlockSpec((1, TM, D), lambda bi, i: (bi, jnp.maximum(i - 1, 0) + lat_first, 0))]


def _project(ctx_rows, lat_rows, lat_first, s_tot, modsel, ln_g, w_in, cos, sin, q_g, k_g, bdq, bdk, *, glu):
    b = ctx_rows.shape[0]
    n_in = w_in.shape[1]
    nt = s_tot // TM
    const = lambda bi, i: (0, 0)
    return pl.pallas_call(
        functools.partial(_proj_kernel, glu=glu),
        grid=(b, nt),
        in_specs=_token_specs(lat_first) + [
            pl.BlockSpec((1, 1, 1, 6 * D), lambda bi, i: (bi, jnp.minimum(i, 1), 0, 0)),
            pl.BlockSpec((1, D), const),
            pl.BlockSpec((D, n_in), const),
            pl.BlockSpec((TM, 128), lambda bi, i: (i, 0)),
            pl.BlockSpec((TM, 128), lambda bi, i: (i, 0)),
            pl.BlockSpec((1, QW), const),
            pl.BlockSpec((1, KW), const),
            pl.BlockSpec((QW, QW), const),
            pl.BlockSpec((KW, KW), const),
        ],
        out_specs=[
            pl.BlockSpec((1, NH, TM, HD), lambda bi, i: (bi, 0, i, 0)),
            pl.BlockSpec((1, NKV, TM, HD), lambda bi, i: (bi, 0, i, 0)),
            pl.BlockSpec((1, NKV, TM, 2 * HD), lambda bi, i: (bi, 0, i, 0)),
            pl.BlockSpec((1, TM, UW), lambda bi, i: (bi, i, 0)),
        ],
        out_shape=[
            jax.ShapeDtypeStruct((b, NH, s_tot, HD), BF16),
            jax.ShapeDtypeStruct((b, NKV, s_tot, HD), BF16),
            jax.ShapeDtypeStruct((b, NKV, s_tot, 2 * HD), BF16),
            jax.ShapeDtypeStruct((b, s_tot, UW), F32),
        ],
        compiler_params=_params(("arbitrary", "arbitrary")),
        name="in_projection",
    )(ctx_rows, lat_rows, modsel, ln_g, w_in, cos, sin, q_g, k_g, bdq, bdk)


def _merge_heads(o):
    tq = o.shape[0] // GQ
    return jnp.concatenate([o[g * tq:(g + 1) * tq] for g in range(GQ)], axis=1)


def _lane_max(s):
    return functools.reduce(jnp.maximum, [s[:, j * 128:(j + 1) * 128] for j in range(s.shape[1] // 128)])


def _dense_attn_kernel(q_ref, k_ref, v_ref, o_ref, sc_sc, sl_sc, m_sc, *, n_ctx, s_tot):
    qi = pl.program_id(2)
    q = q_ref[0].reshape(GQ * TQD, HD)
    n_lat = (s_tot - n_ctx) // TKL

    def chunks(n):
        spans = [(0, n_ctx)] + [(n_ctx + c * TKL, n_ctx + (c + 1) * TKL) for c in range(n)]
        return list(zip(spans, [sc_sc] + [sl_sc.at[c] for c in range(n)]))

    def scores(n):
        for i, ((lo, hi), slot) in enumerate(chunks(n)):
            s = _dot_nt(q, k_ref[0, 0, lo:hi, :])
            slot[...] = s
            m_sc[i] = _lane_max(s)
        m_lane = functools.reduce(jnp.maximum, [m_sc[i] for i in range(n + 1)])
        m_sc[0] = jnp.broadcast_to(jnp.max(m_lane, axis=1, keepdims=True), m_lane.shape)

    def weighted(n):
        acc = None
        for (lo, hi), slot in chunks(n):
            m = jnp.concatenate([m_sc[0]] * ((hi - lo) // 128), axis=1)
            part = _dot(jnp.exp(slot[...] - m).astype(BF16), v_ref[0, 0, lo:hi, :])
            acc = part if acc is None else acc + part
        o_ref[0] = _merge_heads(acc[:, 0:HD] / acc[:, HD:HD + 1]).astype(BF16)

    is_lat = qi * TQD >= n_ctx
    pl.when(is_lat)(lambda: scores(n_lat))
    pl.when(jnp.logical_not(is_lat))(lambda: scores(0))
    pl.when(is_lat)(lambda: weighted(n_lat))
    pl.when(jnp.logical_not(is_lat))(lambda: weighted(0))


def _dense_attention(q, k, v, n_ctx):
    b, _, s_tot, _ = q.shape
    rows = GQ * TQD
    return pl.pallas_call(
        functools.partial(_dense_attn_kernel, n_ctx=n_ctx, s_tot=s_tot),
        grid=(b, NKV, s_tot // TQD),
        in_specs=[
            pl.BlockSpec((1, GQ, TQD, HD), lambda bi, j, i: (bi, j, i, 0)),
            pl.BlockSpec((1, 1, s_tot, HD), lambda bi, j, i: (bi, j, 0, 0)),
            pl.BlockSpec((1, 1, s_tot, 2 * HD), lambda bi, j, i: (bi, j, 0, 0)),
        ],
        out_specs=pl.BlockSpec((1, TQD, GQ * HD), lambda bi, j, i: (bi, i, j)),
        out_shape=jax.ShapeDtypeStruct((b, s_tot, QW), BF16),
        scratch_shapes=[
            pltpu.VMEM((rows, n_ctx), F32),
            pltpu.VMEM(((s_tot - n_ctx) // TKL, rows, TKL), F32),
            pltpu.VMEM(((s_tot - n_ctx) // TKL + 1, rows, 128), F32),
        ],
        compiler_params=_params(("arbitrary", "arbitrary", "arbitrary")),
        name="dense_attention",
    )(q, k, v)


def _window_attn_kernel(sink_ref, q_ref, k_ref, v_ref, o_ref, sc_sc, sw_sc, m_sc, e_sc, *, n_ctx, s_tot):
    qi = pl.program_id(1)
    span = TQ + 2 * WINDOW
    q0 = n_ctx + qi * TQ
    start = pl.multiple_of(jnp.clip(q0 - WINDOW, n_ctx, s_tot - span), 128)
    row = lax.broadcasted_iota(jnp.int32, (TQ, span), 0)
    col = lax.broadcasted_iota(jnp.int32, (TQ, span), 1)
    band = jnp.where(jnp.abs((q0 - start) + row - col) <= WINDOW, 0.0, -jnp.inf)
    band = jnp.concatenate([band] * GQ, axis=0)
    for j in range(NKV):
        q = q_ref[0, j * GQ:(j + 1) * GQ].reshape(GQ * TQ, HD)
        s_c = _dot_nt(q, k_ref[0, j, 0:n_ctx, :])
        s_w = _dot_nt(q, k_ref[0, j, pl.ds(start, span), :]) + band
        sc_sc[j] = s_c
        sw_sc[j] = s_w
        sink = jnp.concatenate([jnp.full((TQ, 128), sink_ref[j * GQ + g], F32) for g in range(GQ)], axis=0)
        m_lane = jnp.maximum(_lane_max(s_c), _lane_max(s_w))
        m_sc[j] = jnp.maximum(jnp.broadcast_to(jnp.max(m_lane, axis=1, keepdims=True), m_lane.shape), sink)
        e_sc[j] = jnp.exp(sink - m_sc[j])
    for j in range(NKV):
        m = m_sc[j]
        p_c = jnp.exp(sc_sc[j] - jnp.concatenate([m] * (n_ctx // 128), axis=1)).astype(BF16)
        p_w = jnp.exp(sw_sc[j] - jnp.concatenate([m] * (span // 128), axis=1)).astype(BF16)
        acc = _dot(p_c, v_ref[0, j, 0:n_ctx, :]) + _dot(p_w, v_ref[0, j, pl.ds(start, span), :])
        l = acc[:, HD:HD + 1] + e_sc[j][:, 0:1]
        o_ref[0, :, j * GQ * HD:(j + 1) * GQ * HD] = _merge_heads(acc[:, 0:HD] / l).astype(BF16)


def _window_attention(sink, q, k, v, n_ctx):
    b, _, s_tot, _ = q.shape
    s_lat = s_tot - n_ctx
    off = n_ctx // TQ
    return pl.pallas_call(
        functools.partial(_window_attn_kernel, n_ctx=n_ctx, s_tot=s_tot),
        grid_spec=pltpu.PrefetchScalarGridSpec(
            num_scalar_prefetch=1,
            grid=(b, s_lat // TQ),
            in_specs=[
                pl.BlockSpec((1, NH, TQ, HD), lambda bi, i, sk: (bi, 0, i + off, 0)),
                pl.BlockSpec((1, NKV, s_tot, HD), lambda bi, i, sk: (bi, 0, 0, 0)),
                pl.BlockSpec((1, NKV, s_tot, 2 * HD), lambda bi, i, sk: (bi, 0, 0, 0)),
            ],
            out_specs=pl.BlockSpec((1, TQ, QW), lambda bi, i, sk: (bi, i, 0)),
            scratch_shapes=[
                pltpu.VMEM((NKV, GQ * TQ, n_ctx), F32),
                pltpu.VMEM((NKV, GQ * TQ, TQ + 2 * WINDOW), F32),
                pltpu.VMEM((NKV, GQ * TQ, 128), F32),
                pltpu.VMEM((NKV, GQ * TQ, 128), F32),
            ],
        ),
        out_shape=jax.ShapeDtypeStruct((b, s_lat, QW), BF16),
        compiler_params=_params(("arbitrary", "arbitrary")),
        name="window_attention",
    )(sink, q, k, v)


def _conv_kernel(g_ref, cw_ref, cb_ref, lg_ref, lb_ref, o_ref, pad_sc, win_sc, *, n_ctx, s_tot):
    zeros = jnp.zeros((HALO, UW), F32)
    pad_sc[0:HALO, :] = zeros
    pad_sc[HALO + n_ctx:2 * HALO + n_ctx, :] = zeros
    pad_sc[2 * HALO + s_tot:3 * HALO + s_tot, :] = zeros

    def fill(i, carry):
        src = pl.multiple_of(i * TM, TM)
        dst = pl.multiple_of(src + HALO + jnp.where(src >= n_ctx, HALO, 0), 8)
        pad_sc[pl.ds(dst, TM), :] = g_ref[0, pl.ds(src, TM), :]
        return carry

    lax.fori_loop(0, s_tot // TM, fill, 0)
    half = CONV_K // 2

    def tile(i, carry):
        src = pl.multiple_of(i * SUB, SUB)
        base = pl.multiple_of(src + jnp.where(src >= n_ctx, HALO, 0), 8)
        win = pad_sc[pl.ds(base, SUB + 2 * HALO), :]
        keep = SUB + 2 * HALO - 8
        for r in range(8):
            win_sc[r, 0:keep, :] = win[r:r + keep, :]
        acc = jnp.zeros((SUB, UW), F32) + cb_ref[...]
        for t in range(CONV_K):
            a, r = divmod(HALO - half + t, 8)
            acc = acc + win_sc[r, 8 * a:8 * a + SUB, :] * cw_ref[t:t + 1, :]
        mu = jnp.mean(acc, axis=-1, keepdims=True)
        xc = acc - mu
        var = jnp.mean(xc * xc, axis=-1, keepdims=True)
        yn = xc * lax.rsqrt(var + EPS) * lg_ref[...] + lb_ref[...]
        o_ref[0, pl.ds(src, SUB), :] = (yn * _sigmoid(yn)).astype(BF16)
        return carry

    lax.fori_loop(0, s_tot // SUB, tile, 0)


def _conv_module(glu, conv_w, conv_b, ln_g, ln_b, n_ctx):
    b, s_tot, _ = glu.shape
    const = lambda bi: (0, 0)
    return pl.pallas_call(
        functools.partial(_conv_kernel, n_ctx=n_ctx, s_tot=s_tot),
        grid=(b,),
        in_specs=[
            pl.BlockSpec((1, s_tot, UW), lambda bi: (bi, 0, 0)),
            pl.BlockSpec((CONV_K, UW), const),
            pl.BlockSpec((1, UW), const),
            pl.BlockSpec((1, UW), const),
            pl.BlockSpec((1, UW), const),
        ],
        out_specs=pl.BlockSpec((1, s_tot, UW), lambda bi: (bi, 0, 0)),
        out_shape=jax.ShapeDtypeStruct((b, s_tot, UW), BF16),
        scratch_shapes=[pltpu.VMEM((s_tot + 3 * HALO, UW), F32), pltpu.VMEM((8, SUB + 2 * HALO, UW), F32)],
        compiler_params=_params(("arbitrary",)),
        name="conv_module",
    )(glu, conv_w, conv_b, ln_g, ln_b)


def _pool_kernel(u_ref, pw_ref, ps_ref, o_ref, pad_sc, *, n_ctx, s_lat):
    zeros = jnp.zeros((HALO, UW), F32)
    pad_sc[0:HALO, :] = zeros
    pad_sc[HALO + s_lat:2 * HALO + s_lat, :] = zeros

    def fill(i, carry):
        src = pl.multiple_of(i * TM, TM)
        pad_sc[pl.ds(pl.multiple_of(src + HALO, 8), TM), :] = u_ref[0, pl.ds(pl.multiple_of(src + n_ctx, 8), TM), :]
        return carry

    lax.fori_loop(0, s_lat // TM, fill, 0)
    gw = UW // len(POOL_SIZES)

    def tile(i, carry):
        src = pl.multiple_of(i * SUB, SUB)
        win = pad_sc[pl.ds(src, SUB + 2 * HALO), :]
        t = src + lax.broadcasted_iota(jnp.int32, (SUB, 1), 0)
        outs = []
        for gi, w in enumerate(POOL_SIZES):
            lanes = slice(gi * gw, (gi + 1) * gw)
            tot = jnp.zeros((SUB, gw), F32)
            for d in range(-(w // 2), w - w // 2):
                tot = tot + win[HALO + d:HALO + d + SUB, lanes]
            lo = jnp.clip(t - w // 2, 0, s_lat)
            hi = jnp.clip(t - w // 2 + w, 0, s_lat)
            p = tot / (hi - lo).astype(F32) - win[HALO:HALO + SUB, lanes]
            outs.append(_dot(p.astype(BF16), pw_ref[gi]))
        y = jnp.concatenate(outs, axis=1) * ps_ref[...]
        o_ref[0, pl.ds(src, SUB), :] = y.astype(BF16)
        return carry

    lax.fori_loop(0, s_lat // SUB, tile, 0)


def _pool_mixer(u, pool_w, pool_scale, n_ctx):
    b, s_tot, _ = u.shape
    s_lat = s_tot - n_ctx
    gw = UW // len(POOL_SIZES)
    return pl.pallas_call(
        functools.partial(_pool_kernel, n_ctx=n_ctx, s_lat=s_lat),
        grid=(b,),
        in_specs=[
            pl.BlockSpec((1, s_tot, UW), lambda bi: (bi, 0, 0)),
            pl.BlockSpec((len(POOL_SIZES), gw, gw), lambda bi: (0, 0, 0)),
            pl.BlockSpec((1, UW), lambda bi: (0, 0)),
        ],
        out_specs=pl.BlockSpec((1, s_lat, UW), lambda bi: (bi, 0, 0)),
        out_shape=jax.ShapeDtypeStruct((b, s_lat, UW), BF16),
        scratch_shapes=[pltpu.VMEM((s_lat + 2 * HALO, UW), F32)],
        compiler_params=_params(("arbitrary",)),
        name="pool_mixer",
    )(u, pool_w, pool_scale)


def _route(logits):
    shape = logits.shape
    lane = lax.broadcasted_iota(jnp.int32, shape, 1)
    lanef = lane.astype(F32)
    is_g = (lane >= N_EXP) & (lane < N_EXP + N_GROUPS)
    big = 1e9
    lg = jnp.where(is_g, logits, -jnp.inf)
    gmax = jnp.max(lg, axis=1, keepdims=True)
    gsel = jnp.min(jnp.where(is_g & (lg == gmax), lanef, big), axis=1, keepdims=True) - float(N_EXP)
    gsum = jnp.sum(jnp.exp(lg - gmax), axis=1, keepdims=True)
    g_w = 1.0 / gsum
    in_grp = (lane < N_EXP) & ((lane >> 3).astype(F32) == gsel)
    el = jnp.where(in_grp, logits, -jnp.inf)
    emax = jnp.max(el, axis=1, keepdims=True)
    ee = jnp.exp(el - emax)
    p = ee / jnp.sum(ee, axis=1, keepdims=True)
    p1 = jnp.max(jnp.where(in_grp, p, -1.0), axis=1, keepdims=True)
    i1 = jnp.min(jnp.where(in_grp & (p == p1), lanef, big), axis=1, keepdims=True)
    rest = in_grp & (lanef != i1)
    p2 = jnp.max(jnp.where(rest, p, -1.0), axis=1, keepdims=True)
    i2 = jnp.min(jnp.where(rest & (p == p2), lanef, big), axis=1, keepdims=True)
    w1 = g_w * p1 / (p1 + p2)
    w2 = g_w * p2 / (p1 + p2)
    onehot = jnp.where((lanef == i1) | (lanef == i2), 1.0, 0.0)
    return lanef, i1, i2, w1, w2, onehot


def _outproj_kernel(a_ref, b_ref, *refs, with_ctx):
    if with_ctx:
        c_ref, x_ref = refs[0:2]
        x_in = _tile_rows(c_ref, x_ref)
    else:
        x_in = refs[0][0]
    mod_ref, w_ref, lng_ref, rw_ref, rb_ref, tri_ref, upper_ref, x1_ref, h2_ref, r_ref, g8_ref = refs[-11:]
    mod = mod_ref[0, 0]
    y = _dot(a_ref[0], w_ref[0:QW, :]) + _dot(b_ref[0], w_ref[QW:QW + UW, :])
    x1 = x_in + mod[:, 2 * D:3 * D] * y
    x1_ref[0] = x1
    h2 = _rmsnorm(x1, lng_ref[...]) * (1.0 + mod[:, 4 * D:5 * D]) + mod[:, 3 * D:4 * D]
    h2_ref[0] = h2.astype(BF16)
    logits = _dot3(h2, rw_ref[...]) + rb_ref[...]
    lanef, i1, i2, w1, w2, onehot = _route(logits)
    before = _dot(tri_ref[...], onehot.astype(BF16))
    cnt = jnp.sum(onehot, axis=0, keepdims=True)
    g8 = jnp.floor((cnt + float(GRP - 1)) * (1.0 / GRP))
    goff = _dot(jnp.broadcast_to(g8, (8, 128)).astype(BF16), upper_ref[...])[0:1, :]
    local = before + float(GRP) * goff
    l1 = jnp.sum(jnp.where(lanef == i1, local, 0.0), axis=1, keepdims=True)
    l2 = jnp.sum(jnp.where(lanef == i2, local, 0.0), axis=1, keepdims=True)
    info = jnp.where(lanef == 0.0, i1, jnp.where(lanef == 1.0, i2, jnp.where(lanef == 2.0, w1, jnp.where(
        lanef == 3.0, w2, jnp.where(lanef == 4.0, l1, jnp.where(lanef == 5.0, l2, 0.0))))))
    r_ref[0] = info
    g8_ref[0, 0] = g8


def _out_project(a, bmix, residual, modsel, w_out, ln_g, rw, rb, tri, upper):
    b, s_out, _ = a.shape
    nt = s_out // TM
    const = lambda bi, i: (0, 0)
    with_ctx = len(residual) == 2
    sel = (lambda i: jnp.minimum(i, 1)) if with_ctx else (lambda i: 1)
    tok = lambda bi, i: (bi, i, 0)
    first_lat = residual[0].shape[1] // TM - nt
    res_specs = _token_specs(0) if with_ctx else [pl.BlockSpec((1, TM, D), lambda bi, i: (bi, i + first_lat, 0))]
    return pl.pallas_call(
        functools.partial(_outproj_kernel, with_ctx=with_ctx),
        grid=(b, nt),
        in_specs=[
            pl.BlockSpec((1, TM, QW), tok),
            pl.BlockSpec((1, TM, UW), tok),
        ] + res_specs + [
            pl.BlockSpec((1, 1, 1, 6 * D), lambda bi, i: (bi, sel(i), 0, 0)),
            pl.BlockSpec((D, D), const),
            pl.BlockSpec((1, D), const),
            pl.BlockSpec((D, 128), const),
            pl.BlockSpec((1, 128), const),
            pl.BlockSpec((TM, TM), const),
            pl.BlockSpec((128, 128), const),
        ],
        out_specs=[
            pl.BlockSpec((1, TM, D), tok),
            pl.BlockSpec((1, TM, D), tok),
            pl.BlockSpec((1, TM, 128), tok),
            pl.BlockSpec((1, 1, 1, 128), lambda bi, i: (bi, i, 0, 0)),
        ],
        out_shape=[
            jax.ShapeDtypeStruct((b, s_out, D), F32),
            jax.ShapeDtypeStruct((b, s_out, D), BF16),
            jax.ShapeDtypeStruct((b, s_out, 128), F32),
            jax.ShapeDtypeStruct((b, nt, 1, 128), F32),
        ],
        compiler_params=_params(("arbitrary", "arbitrary")),
        name="out_projection_router",
    )(a, bmix, *residual, modsel, w_out, ln_g, rw, rb, tri, upper)


def _pack_halves(x, exact=False):
    half = x.shape[1] // 2
    a, b = x[:, 0:half], x[:, half:]
    if not exact:
        a, b = a.astype(BF16).astype(F32), b.astype(BF16).astype(F32)
    return lax.bitcast_convert_type(a, U32) | (lax.bitcast_convert_type(b, U32) >> 16)


def _unpack_halves(w):
    a = lax.bitcast_convert_type(w & jnp.uint32(0xFFFF0000), F32)
    b = lax.bitcast_convert_type(w << 16, F32)
    return jnp.concatenate([a, b], axis=1).astype(BF16)


def _group_copy(src_ref, src_row, dst_ref, dst_row, sem):
    return pltpu.make_async_copy(src_ref.at[pl.ds(pl.multiple_of(src_row, GRP), GRP), :],
                                 dst_ref.at[pl.ds(pl.multiple_of(dst_row, GRP), GRP), :], sem)


def _wait_groups(count, src_ref, dst_ref, sem):
    def wait(g, carry):
        _group_copy(src_ref, 0, dst_ref, 0, sem).wait()
        return carry

    lax.fori_loop(0, count, wait, 0)


def _wait_rows(count, src_ref, dst_ref, sem):
    @pl.when(count > 0)
    def _():
        rows = pl.multiple_of(count * GRP, GRP)
        pltpu.make_async_copy(src_ref.at[pl.ds(0, rows), :], dst_ref.at[pl.ds(0, rows), :], sem).wait()


def _start_runs(runs, tile, sem, copy):
    dst_ref, off_ref, len_ref = runs
    for e in range(N_EXP):
        idx = tile * N_EXP + e
        rows = len_ref[idx]

        @pl.when(rows > 0)
        def _():
            copy(pl.multiple_of(dst_ref[idx], GRP), pl.multiple_of(off_ref[idx], GRP),
                 pl.multiple_of(rows, GRP)).start(priority=e % 2)


def _dispatch_kernel(dst_ref, off_ref, len_ref, ng_ref, zdst_ref, nz_ref, h_ref, r_ref, xs_ref, loc, zbuf, sems,
                     zsem):
    runs = (dst_ref, off_ref, len_ref)
    i = pl.program_id(0)
    n = pl.num_programs(0)

    @pl.when(i == 0)
    def _():
        zbuf[...] = jnp.zeros(zbuf.shape, U32)

        def per_expert(e, carry):
            def zero(j, c):
                _group_copy(zbuf, 0, xs_ref, zdst_ref[e] + j * GRP, zsem).start()
                return c

            lax.fori_loop(0, nz_ref[e], zero, 0)
            return carry

        lax.fori_loop(0, N_EXP, per_expert, 0)
        _wait_groups(nz_ref[N_EXP], zbuf, xs_ref, zsem)

    def run(slot):
        buf = loc.at[slot]
        sem = sems.at[slot]

        @pl.when(i >= 2)
        def _():
            _wait_rows(ng_ref[i - 2], buf, xs_ref, sem)

        rt = r_ref[...].T
        rows = lax.broadcasted_iota(jnp.int32, (LR, TM), 0).astype(F32)
        perm = jnp.where((rows == rt[4:5, :]) | (rows == rt[5:6, :]), 1.0, 0.0).astype(BF16)
        buf[...] = _pack_halves(_dot(perm, h_ref[...]), exact=True)

        _start_runs(runs, i, sem, lambda dst, off, rows: pltpu.make_async_copy(
            buf.at[pl.ds(off, rows), :], xs_ref.at[pl.ds(dst, rows), :], sem))

        @pl.when(i == n - 1)
        def _():
            _wait_rows(ng_ref[i], buf, xs_ref, sem)

            @pl.when(i >= 1)
            def _():
                _wait_rows(ng_ref[i - 1], loc.at[1 - slot], xs_ref, sems.at[1 - slot])

    pl.when(i % 2 == 0)(lambda: run(0))
    pl.when(i % 2 == 1)(lambda: run(1))


def _dispatch(runs, ng, zdst, nz, h2, rinfo, cap):
    t = h2.shape[0]
    tok = lambda i, *_: (i, 0)
    return pl.pallas_call(
        _dispatch_kernel,
        grid_spec=pltpu.PrefetchScalarGridSpec(
            num_scalar_prefetch=6,
            grid=(t // TM,),
            in_specs=[
                pl.BlockSpec((TM, D), tok),
                pl.BlockSpec((TM, 128), tok),
            ],
            out_specs=pl.BlockSpec(memory_space=pl.ANY),
            scratch_shapes=[
                pltpu.VMEM((2, LR, D // 2), U32),
                pltpu.VMEM((GRP, D // 2), U32),
                pltpu.SemaphoreType.DMA((2,)),
                pltpu.SemaphoreType.DMA,
            ],
        ),
        out_shape=jax.ShapeDtypeStruct((cap, D // 2), U32),
        compiler_params=_params(("arbitrary",)),
        name="moe_dispatch",
    )(*runs, ng, zdst, nz, h2, rinfo)


def _expert_kernel(be_ref, nu_ref, x_ref, wg_ref, wu_ref, wd_ref, y_ref, wg_sc, wu_sc, wd_sc):
    i = pl.program_id(0)
    used = i < nu_ref[0]
    fresh = (i == 0) | (be_ref[i] != be_ref[jnp.maximum(i - 1, 0)])

    @pl.when(used & fresh)
    def _():
        wg_sc[...] = wg_ref[0, 0].astype(BF16)
        wu_sc[...] = wu_ref[0, 0].astype(BF16)
        wd_sc[...] = wd_ref[0, 0].astype(BF16)

    @pl.when(used)
    def _():
        xb = _unpack_halves(x_ref[...])
        g = _dot(xb, wg_sc[...])
        u = _dot(xb, wu_sc[...])
        hid = (g * _sigmoid(g)) * u
        y_ref[...] = _pack_halves(_dot(hid.astype(BF16), wd_sc[...]))

    @pl.when(jnp.logical_not(used))
    def _():
        y_ref[...] = jnp.zeros(y_ref.shape, U32)


def _experts(blk_e, n_used, xs, wg, wu, wd, layer):
    cap = xs.shape[0]
    row = lambda i, be, nu: (jnp.minimum(i, nu[0] - 1), 0)
    wsel = lambda i, be, nu: (layer, be[i], 0, 0)
    return pl.pallas_call(
        _expert_kernel,
        grid_spec=pltpu.PrefetchScalarGridSpec(
            num_scalar_prefetch=2,
            grid=(cap // BLK,),
            in_specs=[
                pl.BlockSpec((BLK, D // 2), row),
                pl.BlockSpec((1, 1, D, D_EXP), wsel),
                pl.BlockSpec((1, 1, D, D_EXP), wsel),
                pl.BlockSpec((1, 1, D_EXP, D), wsel),
            ],
            out_specs=pl.BlockSpec((BLK, D // 2), lambda i, be, nu: (i, 0)),
            scratch_shapes=[
                pltpu.VMEM((D, D_EXP), BF16),
                pltpu.VMEM((D, D_EXP), BF16),
                pltpu.VMEM((D_EXP, D), BF16),
            ],
        ),
        out_shape=jax.ShapeDtypeStruct((cap, D // 2), U32),
        compiler_params=_params(("arbitrary",)),
        name="moe_experts",
    )(blk_e, n_used, xs, wg, wu, wd)


def _combine_kernel(dst_ref, off_ref, len_ref, ng_ref, y_ref, x1_ref, r_ref, mod_ref, o_ref, loc, sems):
    runs = (dst_ref, off_ref, len_ref)
    i = pl.program_id(0)
    n = pl.num_programs(0)

    def fetch(tile, slot):
        _start_runs(runs, tile, sems.at[slot], lambda dst, off, rows: pltpu.make_async_copy(
            y_ref.at[pl.ds(dst, rows), :], loc.at[slot, pl.ds(off, rows), :], sems.at[slot]))

    @pl.when(i == 0)
    def _():
        loc[...] = jnp.zeros(loc.shape, U32)
        fetch(0, 0)

    def run(slot):
        @pl.when(i + 1 < n)
        def _():
            fetch(i + 1, 1 - slot)

        _wait_rows(ng_ref[i], y_ref, loc.at[slot], sems.at[slot])
        ysb = _unpack_halves(loc[slot])
        info = r_ref[...]
        col = lax.broadcasted_iota(jnp.int32, (TM, LR), 1).astype(F32)
        pick = jnp.concatenate([jnp.where(col == info[:, 4:5], 1.0, 0.0).astype(BF16),
                                jnp.where(col == info[:, 5:6], 1.0, 0.0).astype(BF16)], axis=0)
        y12 = _dot(pick, ysb)
        g2 = mod_ref[0, 0][:, 5 * D:6 * D]
        o_ref[...] = x1_ref[...] + g2 * (info[:, 2:3] * y12[0:TM] + info[:, 3:4] * y12[TM:2 * TM])

    pl.when(i % 2 == 0)(lambda: run(0))
    pl.when(i % 2 == 1)(lambda: run(1))


def _combine(runs, ng, ys, x1, rinfo, modsel, *, tiles_per_batch, with_ctx):
    t = x1.shape[0]
    if with_ctx:
        msel = lambda i, *_: (i // tiles_per_batch, jnp.minimum(i % tiles_per_batch, 1), 0, 0)
    else:
        msel = lambda i, *_: (i // tiles_per_batch, 1, 0, 0)
    tok = lambda i, *_: (i, 0)
    return pl.pallas_call(
        _combine_kernel,
        grid_spec=pltpu.PrefetchScalarGridSpec(
            num_scalar_prefetch=4,
            grid=(t // TM,),
            in_specs=[
                pl.BlockSpec(memory_space=pl.ANY),
                pl.BlockSpec((TM, D), tok),
                pl.BlockSpec((TM, 128), tok),
                pl.BlockSpec((1, 1, 1, 6 * D), msel),
            ],
            out_specs=pl.BlockSpec((TM, D), tok),
            scratch_shapes=[pltpu.VMEM((2, LR, D // 2), U32), pltpu.SemaphoreType.DMA((2,))],
        ),
        out_shape=jax.ShapeDtypeStruct((t, D), F32),
        compiler_params=_params(("arbitrary",)),
        name="moe_combine",
    )(*runs, ng, ys, x1, rinfo, modsel)


def _moe(x1, h2, rinfo, g8, modsel, wg, wu, wd, layer, *, with_ctx):
    b, s, _ = x1.shape
    t = b * s
    n_tiles = t // TM
    i32 = jnp.int32
    grp = g8.reshape(n_tiles, 128)[:, 0:N_EXP].astype(i32)
    n_blk = -(-(2 * t + n_tiles * N_EXP * (GRP - 1) + N_EXP * (BLK - 1)) // BLK)
    cap = n_blk * BLK
    goff_end = jnp.cumsum(grp, axis=1)
    ng = goff_end[:, N_EXP - 1]
    rows_e = GRP * jnp.sum(grp, axis=0)
    padded = (rows_e + BLK - 1) // BLK * BLK
    pend = jnp.cumsum(padded)
    pstart = pend - padded
    run_dst = (pstart[None, :] + GRP * (jnp.cumsum(grp, axis=0) - grp)).reshape(-1).astype(i32)
    run_off = (GRP * (goff_end - grp)).reshape(-1).astype(i32)
    run_len = (GRP * grp).reshape(-1).astype(i32)
    runs = (run_dst, run_off, run_len)
    nz = (padded - rows_e) // GRP
    nz = jnp.concatenate([nz, jnp.sum(nz, keepdims=True)]).astype(i32)
    zdst = (pstart + rows_e).astype(i32)
    blk_e = jnp.minimum(jnp.sum((pend[None, :] <= (jnp.arange(n_blk, dtype=i32) * BLK)[:, None]).astype(i32), axis=1),
                        N_EXP - 1).astype(i32)
    n_used = (pend[N_EXP - 1:N_EXP] // BLK).astype(i32)
    info = rinfo.reshape(t, 128)
    xs = _dispatch(runs, ng.astype(i32), zdst, nz, h2.reshape(t, D), info, cap)
    ys = _experts(blk_e, n_used, xs, wg, wu, wd, layer)
    out = _combine(runs, ng.astype(i32), ys, x1.reshape(t, D), info, modsel, tiles_per_batch=s // TM,
                   with_ctx=with_ctx)
    return out.reshape(b, s, D)


def _rope_tables(n_ctx, s_lat):
    rows = s_lat // GRID_W
    row = jnp.repeat(jnp.arange(rows, dtype=F32), GRID_W)
    col = jnp.tile(jnp.arange(GRID_W, dtype=F32), rows)
    n_freq = HD // 4
    inv = ROPE_THETA ** (-jnp.arange(n_freq, dtype=F32) / n_freq)
    ang = jnp.concatenate([row[:, None] * inv, col[:, None] * inv], axis=-1)
    cos = jnp.concatenate([jnp.ones((n_ctx, HD // 2), F32), jnp.cos(ang)], axis=0)
    sin = jnp.concatenate([jnp.zeros((n_ctx, HD // 2), F32), jnp.sin(ang)], axis=0)
    cos = jnp.concatenate([cos, cos, cos, cos], axis=1)
    sin = jnp.concatenate([-sin, sin, -sin, sin], axis=1)
    return cos, sin


def _block_diag_ones(n):
    i = jnp.arange(n) // HD
    return (i[:, None] == i[None, :]).astype(BF16)


def kernel(x, c, ctx, c_ctx, mod_w, mod_b, ln1_g, ln2_g, w_in_ab, w_out_ab, q_norm_a, k_norm_a, conv_w, conv_b,
           conv_ln_g, conv_ln_b, w_in_cd, w_out_cd, q_norm_c, k_norm_c, sink_c, pool_w, pool_scale,
           rt_grp_w, rt_grp_b, rt_exp_w, rt_exp_b, ex_gate, ex_up, ex_down):
    b, s_lat, _ = x.shape
    n_ctx = ctx.shape[1]
    assert n_ctx == TM and s_lat % TM == 0 and b <= 8

    cc = jnp.zeros((16, D), F32).at[0:b].set(c).at[8].set(c_ctx)
    mod = _modulation(cc, mod_w, mod_b)

    def modsel(l):
        return jnp.stack([jnp.broadcast_to(mod[l, 8], (b, 6 * D)), mod[l, 0:b]], axis=1).reshape(b, 2, 1, 6 * D)

    cos, sin = _rope_tables(n_ctx, s_lat)
    bdq = _block_diag_ones(QW)
    bdk = _block_diag_ones(KW)
    tri = (jnp.arange(TM)[:, None] > jnp.arange(TM)[None, :]).astype(BF16)
    upper = (jnp.arange(128)[:, None] < jnp.arange(128)[None, :]).astype(BF16)

    def router(l):
        rw = jnp.zeros((D, 128), F32).at[:, 0:N_EXP].set(rt_exp_w[l]).at[:, N_EXP:N_EXP + N_GROUPS].set(rt_grp_w[l])
        rb = jnp.zeros((1, 128), F32).at[0, 0:N_EXP].set(rt_exp_b[l]).at[0, N_EXP:N_EXP + N_GROUPS].set(rt_grp_b[l])
        return rw, rb

    def tile_gain(g, n):
        return jnp.tile(g, n).reshape(1, n * HD)

    s_tot = n_ctx + s_lat

    ms = modsel(0)
    q, k, v, glu = _project(ctx, x, 0, s_tot, ms, ln1_g[0:1], w_in_ab[0].astype(BF16), cos, sin,
                            tile_gain(q_norm_a[0], NH), tile_gain(k_norm_a[0], NKV), bdq, bdk, glu=True)
    att = _dense_attention(q, k, v, n_ctx)
    cv = _conv_module(glu, conv_w[0], conv_b[0:1], conv_ln_g[0:1], conv_ln_b[0:1], n_ctx)
    rw, rb = router(0)
    x1, h2, rinfo, g8 = _out_project(att, cv, (ctx, x), ms, w_out_ab[0].astype(BF16), ln2_g[0:1], rw, rb, tri, upper)
    xc = _moe(x1, h2, rinfo, g8, ms, ex_gate, ex_up, ex_down, 0, with_ctx=True)

    ms = modsel(1)
    q, k, v, u = _project(xc, xc, n_ctx // TM, s_tot, ms, ln1_g[1:2], w_in_cd[0].astype(BF16), cos, sin,
                          tile_gain(q_norm_c[0], NH), tile_gain(k_norm_c[0], NKV), bdq, bdk, glu=False)
    att = _window_attention(sink_c[0], q, k, v, n_ctx)
    pm = _pool_mixer(u, pool_w[0].astype(BF16), pool_scale[0:1], n_ctx)
    rw, rb = router(1)
    x1, h2, rinfo, g8 = _out_project(att, pm, (xc,), ms, w_out_cd[0].astype(BF16), ln2_g[1:2], rw, rb, tri, upper)
    return _moe(x1, h2, rinfo, g8, ms, ex_gate, ex_up, ex_down, 1, with_ctx=False)
```

```python
import functools

import jax
import jax.numpy as jnp
from jax import lax
from jax.experimental import pallas as pl
from jax.experimental.pallas import tpu as pltpu

F32 = jnp.float32
BF16 = jnp.bfloat16
U32 = jnp.uint32

D = 1024
HD = 64
NH = 8
NKV = 2
GQ = NH // NKV
QW = NH * HD
KW = NKV * HD
UW = 512
EPS = 1e-6
ROPE_THETA = 10000.0
GRID_W = 64
CONV_K = 31
WINDOW = 128
POOL_SIZES = (2, 4, 8, 16)
N_GROUPS = 4
PER_GROUP = 8
N_EXP = N_GROUPS * PER_GROUP
D_EXP = D // 2

TM = 256
PB = 2
TQ = 128
TQD = 256
TKL = 512
SUB = 64
HALO = 16
BLK = 512
GRP = 8
NG = 96
LR = NG * GRP
VMEM_LIMIT = 56 * 1024 * 1024


def _sigmoid(x):
    return 1.0 / (1.0 + jnp.exp(-x))


def _dot(a, b):
    return jnp.dot(a, b, preferred_element_type=F32)


def _dot_nt(a, b):
    return lax.dot_general(a, b, (((1,), (1,)), ((), ())), preferred_element_type=F32)


def _split(a):
    hi = a.astype(BF16)
    lo = (a - hi.astype(F32)).astype(BF16)
    return hi, lo


def _dot3(a, w):
    a_hi, a_lo = _split(a)
    w_hi, w_lo = _split(w)
    return _dot(a_hi, w_hi) + _dot(a_lo, w_hi) + _dot(a_hi, w_lo)


def _rmsnorm(x, g):
    return x * lax.rsqrt(jnp.mean(x * x, axis=-1, keepdims=True) + EPS) * g


def _params(sem, vmem=VMEM_LIMIT):
    return pltpu.CompilerParams(dimension_semantics=sem, vmem_limit_bytes=vmem)


def _mod_kernel(c_ref, w_ref, b_ref, o_ref):
    a = c_ref[...]
    a = a * _sigmoid(a)
    o_ref[0] = _dot3(a, w_ref[0]) + b_ref[0]


def _modulation(cc, mod_w, mod_b):
    depth = mod_w.shape[0]
    tn = 1024
    return pl.pallas_call(
        _mod_kernel,
        grid=(depth, 6 * D // tn),
        in_specs=[
            pl.BlockSpec((16, D), lambda l, j: (0, 0)),
            pl.BlockSpec((1, D, tn), lambda l, j: (l, 0, j)),
            pl.BlockSpec((1, 1, tn), lambda l, j: (l, 0, j)),
        ],
        out_specs=pl.BlockSpec((1, 16, tn), lambda l, j: (l, 0, j)),
        out_shape=jax.ShapeDtypeStruct((depth, 16, 6 * D), F32),
        compiler_params=_params(("arbitrary", "arbitrary")),
        name="modulation",
    )(cc, mod_w, mod_b.reshape(depth, 1, 6 * D))


def _swap_halves(t):
    w = t.shape[1]
    lane = lax.broadcasted_iota(jnp.int32, t.shape, 1)
    first = (lane & (HD - 1)) < (HD // 2)
    return jnp.where(first, pltpu.roll(t, w - HD // 2, 1), pltpu.roll(t, HD // 2, 1))


def _head_norm_rope(t, bd, g, cos, sin, scale):
    hi, lo = _split(t * t)
    ssq = _dot(hi, bd) + _dot(lo, bd)
    tn = t * lax.rsqrt(ssq * (1.0 / HD) + EPS) * g
    n = t.shape[1] // 128
    if n > 1:
        cos = jnp.concatenate([cos] * n, axis=1)
        sin = jnp.concatenate([sin] * n, axis=1)
    out = tn * cos + _swap_halves(tn) * sin
    return out * scale if scale != 1.0 else out


def _tile_rows(c_ref, x_ref, bb):
    return jnp.where(pl.program_id(1) == 0, c_ref[bb], x_ref[bb])


def _modulated(x, mod, lng, shift_at, scale_at):
    return _rmsnorm(x, lng) * (1.0 + mod[:, scale_at * D:(scale_at + 1) * D]) + mod[:, shift_at * D:(shift_at + 1) * D]


def _proj_kernel(c_ref, x_ref, mod_ref, lng_ref, w_ref, cos_ref, sin_ref, qg_ref, kg_ref, bdq_ref, bdk_ref,
                 q_ref, k_ref, v_ref, u_ref, *, glu):
    h = jnp.concatenate([_modulated(_tile_rows(c_ref, x_ref, bb), mod_ref[bb, 0], lng_ref[...], 0, 1)
                         for bb in range(PB)], axis=0)
    px = _dot(h.astype(BF16), w_ref[...])
    cos = jnp.concatenate([cos_ref[...]] * PB, axis=0)
    sin = jnp.concatenate([sin_ref[...]] * PB, axis=0)
    q = _head_norm_rope(px[:, 0:QW], bdq_ref[...], qg_ref[...], cos, sin, HD ** -0.5)
    k = _head_norm_rope(px[:, QW:QW + KW], bdk_ref[...], kg_ref[...], cos, sin, 1.0)
    v = px[:, QW + KW:QW + 2 * KW]
    lane = lax.broadcasted_iota(jnp.int32, (TM, KW), 1)
    ones_col = jnp.where(lane == HD, 1.0, 0.0)
    o_u = QW + 2 * KW
    if glu:
        u = px[:, o_u:o_u + UW] * _sigmoid(px[:, o_u + UW:o_u + 2 * UW])
    else:
        u = px[:, o_u:o_u + UW]
    for bb in range(PB):
        rows = slice(bb * TM, (bb + 1) * TM)
        for hh in range(NH):
            q_ref[bb, hh] = q[rows, hh * HD:(hh + 1) * HD].astype(BF16)
        for j in range(NKV):
            k_ref[bb, j] = k[rows, j * HD:(j + 1) * HD].astype(BF16)
            vj = v[rows] if j == 0 else pltpu.roll(v[rows], KW - j * HD, 1)
            v_ref[bb, j] = jnp.where(lane < HD, vj, ones_col).astype(BF16)
        u_ref[bb] = u[rows]


def _token_specs(lat_first):
    return [pl.BlockSpec((PB, TM, D), lambda bi, i: (bi, 0, 0)),
            pl.BlockSpec((PB, TM, D), lambda bi, i: (bi, jnp.maximum(i - 1, 0) + lat_first, 0))]


def _project(ctx_rows, lat_rows, lat_first, s_tot, modsel, ln_g, w_in, cos, sin, q_g, k_g, bdq, bdk, *, glu):
    b = ctx_rows.shape[0]
    n_in = w_in.shape[1]
    nt = s_tot // TM
    const = lambda bi, i: (0, 0)
    return pl.pallas_call(
        functools.partial(_proj_kernel, glu=glu),
        grid=(b // PB, nt),
        in_specs=_token_specs(lat_first) + [
            pl.BlockSpec((PB, 1, 1, 6 * D), lambda bi, i: (bi, jnp.minimum(i, 1), 0, 0)),
            pl.BlockSpec((1, D), const),
            pl.BlockSpec((D, n_in), const),
            pl.BlockSpec((TM, 128), lambda bi, i: (i, 0)),
            pl.BlockSpec((TM, 128), lambda bi, i: (i, 0)),
            pl.BlockSpec((1, QW), const),
            pl.BlockSpec((1, KW), const),
            pl.BlockSpec((QW, QW), const),
            pl.BlockSpec((KW, KW), const),
        ],
        out_specs=[
            pl.BlockSpec((PB, NH, TM, HD), lambda bi, i: (bi, 0, i, 0)),
            pl.BlockSpec((PB, NKV, TM, HD), lambda bi, i: (bi, 0, i, 0)),
            pl.BlockSpec((PB, NKV, TM, 2 * HD), lambda bi, i: (bi, 0, i, 0)),
            pl.BlockSpec((PB, TM, UW), lambda bi, i: (bi, i, 0)),
        ],
        out_shape=[
            jax.ShapeDtypeStruct((b, NH, s_tot, HD), BF16),
            jax.ShapeDtypeStruct((b, NKV, s_tot, HD), BF16),
            jax.ShapeDtypeStruct((b, NKV, s_tot, 2 * HD), BF16),
            jax.ShapeDtypeStruct((b, s_tot, UW), F32),
        ],
        compiler_params=_params(("arbitrary", "arbitrary")),
        name="in_projection",
    )(ctx_rows, lat_rows, modsel, ln_g, w_in, cos, sin, q_g, k_g, bdq, bdk)


def _merge_heads(o):
    tq = o.shape[0] // GQ
    return jnp.concatenate([o[g * tq:(g + 1) * tq] for g in range(GQ)], axis=1)


def _lane_max(s):
    return functools.reduce(jnp.maximum, [s[:, j * 128:(j + 1) * 128] for j in range(s.shape[1] // 128)])


def _dense_attn_kernel(q_ref, k_ref, v_ref, o_ref, sc_sc, sl_sc, m_sc, *, n_ctx, s_tot):
    qi = pl.program_id(2)
    q = q_ref[0].reshape(GQ * TQD, HD)
    n_lat = (s_tot - n_ctx) // TKL

    def chunks(n):
        spans = [(0, n_ctx)] + [(n_ctx + c * TKL, n_ctx + (c + 1) * TKL) for c in range(n)]
        return list(zip(spans, [sc_sc] + [sl_sc.at[c] for c in range(n)]))

    def scores(n):
        for i, ((lo, hi), slot) in enumerate(chunks(n)):
            s = _dot_nt(q, k_ref[0, 0, lo:hi, :])
            slot[...] = s
            m_sc[i] = _lane_max(s)
        m_lane = functools.reduce(jnp.maximum, [m_sc[i] for i in range(n + 1)])
        m_sc[0] = jnp.broadcast_to(jnp.max(m_lane, axis=1, keepdims=True), m_lane.shape)

    def weighted(n):
        acc = None
        for (lo, hi), slot in chunks(n):
            m = jnp.concatenate([m_sc[0]] * ((hi - lo) // 128), axis=1)
            part = _dot(jnp.exp(slot[...] - m).astype(BF16), v_ref[0, 0, lo:hi, :])
            acc = part if acc is None else acc + part
        o_ref[0] = _merge_heads(acc[:, 0:HD] / acc[:, HD:HD + 1]).astype(BF16)

    is_lat = qi * TQD >= n_ctx
    pl.when(is_lat)(lambda: scores(n_lat))
    pl.when(jnp.logical_not(is_lat))(lambda: scores(0))
    pl.when(is_lat)(lambda: weighted(n_lat))
    pl.when(jnp.logical_not(is_lat))(lambda: weighted(0))


def _dense_attention(q, k, v, n_ctx):
    b, _, s_tot, _ = q.shape
    rows = GQ * TQD
    return pl.pallas_call(
        functools.partial(_dense_attn_kernel, n_ctx=n_ctx, s_tot=s_tot),
        grid=(b, NKV, s_tot // TQD),
        in_specs=[
            pl.BlockSpec((1, GQ, TQD, HD), lambda bi, j, i: (bi, j, i, 0)),
            pl.BlockSpec((1, 1, s_tot, HD), lambda bi, j, i: (bi, j, 0, 0)),
            pl.BlockSpec((1, 1, s_tot, 2 * HD), lambda bi, j, i: (bi, j, 0, 0)),
        ],
        out_specs=pl.BlockSpec((1, TQD, GQ * HD), lambda bi, j, i: (bi, i, j)),
        out_shape=jax.ShapeDtypeStruct((b, s_tot, QW), BF16),
        scratch_shapes=[
            pltpu.VMEM((rows, n_ctx), F32),
            pltpu.VMEM(((s_tot - n_ctx) // TKL, rows, TKL), F32),
            pltpu.VMEM(((s_tot - n_ctx) // TKL + 1, rows, 128), F32),
        ],
        compiler_params=_params(("arbitrary", "arbitrary", "arbitrary")),
        name="dense_attention",
    )(q, k, v)


def _window_attn_kernel(sink_ref, q_ref, k_ref, v_ref, o_ref, sc_sc, sw_sc, m_sc, e_sc, *, n_ctx, s_tot):
    qi = pl.program_id(1)
    span = TQ + 2 * WINDOW
    q0 = n_ctx + qi * TQ
    start = pl.multiple_of(jnp.clip(q0 - WINDOW, n_ctx, s_tot - span), 128)
    row = lax.broadcasted_iota(jnp.int32, (TQ, span), 0)
    col = lax.broadcasted_iota(jnp.int32, (TQ, span), 1)
    band = jnp.where(jnp.abs((q0 - start) + row - col) <= WINDOW, 0.0, -jnp.inf)
    band = jnp.concatenate([band] * GQ, axis=0)
    for j in range(NKV):
        q = q_ref[0, j * GQ:(j + 1) * GQ].reshape(GQ * TQ, HD)
        s_c = _dot_nt(q, k_ref[0, j, 0:n_ctx, :])
        s_w = _dot_nt(q, k_ref[0, j, pl.ds(start, span), :]) + band
        sc_sc[j] = s_c
        sw_sc[j] = s_w
        sink = jnp.concatenate([jnp.full((TQ, 128), sink_ref[j * GQ + g], F32) for g in range(GQ)], axis=0)
        m_lane = jnp.maximum(_lane_max(s_c), _lane_max(s_w))
        m_sc[j] = jnp.maximum(jnp.broadcast_to(jnp.max(m_lane, axis=1, keepdims=True), m_lane.shape), sink)
        e_sc[j] = jnp.exp(sink - m_sc[j])
    for j in range(NKV):
        m = m_sc[j]
        p_c = jnp.exp(sc_sc[j] - jnp.concatenate([m] * (n_ctx // 128), axis=1)).astype(BF16)
        p_w = jnp.exp(sw_sc[j] - jnp.concatenate([m] * (span // 128), axis=1)).astype(BF16)
        acc = _dot(p_c, v_ref[0, j, 0:n_ctx, :]) + _dot(p_w, v_ref[0, j, pl.ds(start, span), :])
        l = acc[:, HD:HD + 1] + e_sc[j][:, 0:1]
        o_ref[0, :, j * GQ * HD:(j + 1) * GQ * HD] = _merge_heads(acc[:, 0:HD] / l).astype(BF16)


def _window_attention(sink, q, k, v, n_ctx):
    b, _, s_tot, _ = q.shape
    s_lat = s_tot - n_ctx
    off = n_ctx // TQ
    return pl.pallas_call(
        functools.partial(_window_attn_kernel, n_ctx=n_ctx, s_tot=s_tot),
        grid_spec=pltpu.PrefetchScalarGridSpec(
            num_scalar_prefetch=1,
            grid=(b, s_lat // TQ),
            in_specs=[
                pl.BlockSpec((1, NH, TQ, HD), lambda bi, i, sk: (bi, 0, i + off, 0)),
                pl.BlockSpec((1, NKV, s_tot, HD), lambda bi, i, sk: (bi, 0, 0, 0)),
                pl.BlockSpec((1, NKV, s_tot, 2 * HD), lambda bi, i, sk: (bi, 0, 0, 0)),
            ],
            out_specs=pl.BlockSpec((1, TQ, QW), lambda bi, i, sk: (bi, i, 0)),
            scratch_shapes=[
                pltpu.VMEM((NKV, GQ * TQ, n_ctx), F32),
                pltpu.VMEM((NKV, GQ * TQ, TQ + 2 * WINDOW), F32),
                pltpu.VMEM((NKV, GQ * TQ, 128), F32),
                pltpu.VMEM((NKV, GQ * TQ, 128), F32),
            ],
        ),
        out_shape=jax.ShapeDtypeStruct((b, s_lat, QW), BF16),
        compiler_params=_params(("arbitrary", "arbitrary")),
        name="window_attention",
    )(sink, q, k, v)


def _conv_kernel(g_ref, cw_ref, cb_ref, lg_ref, lb_ref, o_ref, pad_sc, win_sc, *, n_ctx, s_tot):
    zeros = jnp.zeros((HALO, UW), F32)
    pad_sc[0:HALO, :] = zeros
    pad_sc[HALO + n_ctx:2 * HALO + n_ctx, :] = zeros
    pad_sc[2 * HALO + s_tot:3 * HALO + s_tot, :] = zeros

    def fill(i, carry):
        src = pl.multiple_of(i * TM, TM)
        dst = pl.multiple_of(src + HALO + jnp.where(src >= n_ctx, HALO, 0), 8)
        pad_sc[pl.ds(dst, TM), :] = g_ref[0, pl.ds(src, TM), :]
        return carry

    lax.fori_loop(0, s_tot // TM, fill, 0)
    half = CONV_K // 2

    def tile(i, carry):
        src = pl.multiple_of(i * SUB, SUB)
        base = pl.multiple_of(src + jnp.where(src >= n_ctx, HALO, 0), 8)
        win = pad_sc[pl.ds(base, SUB + 2 * HALO), :]
        keep = SUB + 2 * HALO - 8
        for r in range(8):
            win_sc[r, 0:keep, :] = win[r:r + keep, :]
        acc = jnp.zeros((SUB, UW), F32) + cb_ref[...]
        for t in range(CONV_K):
            a, r = divmod(HALO - half + t, 8)
            acc = acc + win_sc[r, 8 * a:8 * a + SUB, :] * cw_ref[t:t + 1, :]
        mu = jnp.mean(acc, axis=-1, keepdims=True)
        xc = acc - mu
        var = jnp.mean(xc * xc, axis=-1, keepdims=True)
        yn = xc * lax.rsqrt(var + EPS) * lg_ref[...] + lb_ref[...]
        o_ref[0, pl.ds(src, SUB), :] = (yn * _sigmoid(yn)).astype(BF16)
        return carry

    lax.fori_loop(0, s_tot // SUB, tile, 0)


def _conv_module(glu, conv_w, conv_b, ln_g, ln_b, n_ctx):
    b, s_tot, _ = glu.shape
    const = lambda bi: (0, 0)
    return pl.pallas_call(
        functools.partial(_conv_kernel, n_ctx=n_ctx, s_tot=s_tot),
        grid=(b,),
        in_specs=[
            pl.BlockSpec((1, s_tot, UW), lambda bi: (bi, 0, 0)),
            pl.BlockSpec((CONV_K, UW), const),
            pl.BlockSpec((1, UW), const),
            pl.BlockSpec((1, UW), const),
            pl.BlockSpec((1, UW), const),
        ],
        out_specs=pl.BlockSpec((1, s_tot, UW), lambda bi: (bi, 0, 0)),
        out_shape=jax.ShapeDtypeStruct((b, s_tot, UW), BF16),
        scratch_shapes=[pltpu.VMEM((s_tot + 3 * HALO, UW), F32), pltpu.VMEM((8, SUB + 2 * HALO, UW), F32)],
        compiler_params=_params(("arbitrary",)),
        name="conv_module",
    )(glu, conv_w, conv_b, ln_g, ln_b)


def _pool_kernel(u_ref, pw_ref, ps_ref, o_ref, pad_sc, *, n_ctx, s_lat):
    zeros = jnp.zeros((HALO, UW), F32)
    pad_sc[0:HALO, :] = zeros
    pad_sc[HALO + s_lat:2 * HALO + s_lat, :] = zeros

    def fill(i, carry):
        src = pl.multiple_of(i * TM, TM)
        pad_sc[pl.ds(pl.multiple_of(src + HALO, 8), TM), :] = u_ref[0, pl.ds(pl.multiple_of(src + n_ctx, 8), TM), :]
        return carry

    lax.fori_loop(0, s_lat // TM, fill, 0)
    gw = UW // len(POOL_SIZES)

    def tile(i, carry):
        src = pl.multiple_of(i * SUB, SUB)
        win = pad_sc[pl.ds(src, SUB + 2 * HALO), :]
        t = src + lax.broadcasted_iota(jnp.int32, (SUB, 1), 0)
        outs = []
        for gi, w in enumerate(POOL_SIZES):
            lanes = slice(gi * gw, (gi + 1) * gw)
            tot = jnp.zeros((SUB, gw), F32)
            for d in range(-(w // 2), w - w // 2):
                tot = tot + win[HALO + d:HALO + d + SUB, lanes]
            lo = jnp.clip(t - w // 2, 0, s_lat)
            hi = jnp.clip(t - w // 2 + w, 0, s_lat)
            p = tot / (hi - lo).astype(F32) - win[HALO:HALO + SUB, lanes]
            outs.append(_dot(p.astype(BF16), pw_ref[gi]))
        y = jnp.concatenate(outs, axis=1) * ps_ref[...]
        o_ref[0, pl.ds(src, SUB), :] = y.astype(BF16)
        return carry

    lax.fori_loop(0, s_lat // SUB, tile, 0)


def _pool_mixer(u, pool_w, pool_scale, n_ctx):
    b, s_tot, _ = u.shape
    s_lat = s_tot - n_ctx
    gw = UW // len(POOL_SIZES)
    return pl.pallas_call(
        functools.partial(_pool_kernel, n_ctx=n_ctx, s_lat=s_lat),
        grid=(b,),
        in_specs=[
            pl.BlockSpec((1, s_tot, UW), lambda bi: (bi, 0, 0)),
            pl.BlockSpec((len(POOL_SIZES), gw, gw), lambda bi: (0, 0, 0)),
            pl.BlockSpec((1, UW), lambda bi: (0, 0)),
        ],
        out_specs=pl.BlockSpec((1, s_lat, UW), lambda bi: (bi, 0, 0)),
        out_shape=jax.ShapeDtypeStruct((b, s_lat, UW), BF16),
        scratch_shapes=[pltpu.VMEM((s_lat + 2 * HALO, UW), F32)],
        compiler_params=_params(("arbitrary",)),
        name="pool_mixer",
    )(u, pool_w, pool_scale)


def _route(logits):
    shape = logits.shape
    lane = lax.broadcasted_iota(jnp.int32, shape, 1)
    lanef = lane.astype(F32)
    is_g = (lane >= N_EXP) & (lane < N_EXP + N_GROUPS)
    big = 1e9
    lg = jnp.where(is_g, logits, -jnp.inf)
    gmax = jnp.max(lg, axis=1, keepdims=True)
    gsel = jnp.min(jnp.where(is_g & (lg == gmax), lanef, big), axis=1, keepdims=True) - float(N_EXP)
    gsum = jnp.sum(jnp.exp(lg - gmax), axis=1, keepdims=True)
    g_w = 1.0 / gsum
    in_grp = (lane < N_EXP) & ((lane >> 3).astype(F32) == gsel)
    el = jnp.where(in_grp, logits, -jnp.inf)
    emax = jnp.max(el, axis=1, keepdims=True)
    ee = jnp.exp(el - emax)
    p = ee / jnp.sum(ee, axis=1, keepdims=True)
    p1 = jnp.max(jnp.where(in_grp, p, -1.0), axis=1, keepdims=True)
    i1 = jnp.min(jnp.where(in_grp & (p == p1), lanef, big), axis=1, keepdims=True)
    rest = in_grp & (lanef != i1)
    p2 = jnp.max(jnp.where(rest, p, -1.0), axis=1, keepdims=True)
    i2 = jnp.min(jnp.where(rest & (p == p2), lanef, big), axis=1, keepdims=True)
    w1 = g_w * p1 / (p1 + p2)
    w2 = g_w * p2 / (p1 + p2)
    onehot = jnp.where((lanef == i1) | (lanef == i2), 1.0, 0.0)
    return lanef, i1, i2, w1, w2, onehot


def _outproj_kernel(a_ref, b_ref, *refs, with_ctx):
    mod_ref, w_ref, lng_ref, rw_ref, rb_ref, tri_ref, upper_ref, x1_ref, h2_ref, r_ref, g8_ref = refs[-11:]
    a = jnp.concatenate([a_ref[bb] for bb in range(PB)], axis=0)
    bmix = jnp.concatenate([b_ref[bb] for bb in range(PB)], axis=0)
    y = _dot(a, w_ref[0:QW, :]) + _dot(bmix, w_ref[QW:QW + UW, :])
    h2_parts = []
    for bb in range(PB):
        x_in = _tile_rows(refs[0], refs[1], bb) if with_ctx else refs[0][bb]
        mod = mod_ref[bb, 0]
        x1 = x_in + mod[:, 2 * D:3 * D] * y[bb * TM:(bb + 1) * TM]
        x1_ref[bb] = x1
        h2_parts.append(_modulated(x1, mod, lng_ref[...], 3, 4))
        h2_ref[bb] = h2_parts[-1].astype(BF16)
    logits = _dot3(jnp.concatenate(h2_parts, axis=0), rw_ref[...]) + rb_ref[...]
    lanef, i1, i2, w1, w2, onehot = _route(logits)
    for bb in range(PB):
        rows = slice(bb * TM, (bb + 1) * TM)
        hot = onehot[rows]
        before = _dot(tri_ref[...], hot.astype(BF16))
        cnt = jnp.sum(hot, axis=0, keepdims=True)
        g8 = jnp.floor((cnt + float(GRP - 1)) * (1.0 / GRP))
        goff = _dot(jnp.broadcast_to(g8, (8, 128)).astype(BF16), upper_ref[...])[0:1, :]
        local = before + float(GRP) * goff
        lane = lax.broadcasted_iota(jnp.int32, (TM, 128), 1).astype(F32)
        e1, e2 = i1[rows], i2[rows]
        l1 = jnp.sum(jnp.where(lane == e1, local, 0.0), axis=1, keepdims=True)
        l2 = jnp.sum(jnp.where(lane == e2, local, 0.0), axis=1, keepdims=True)
        r_ref[bb] = jnp.where(lane == 0.0, e1, jnp.where(lane == 1.0, e2, jnp.where(lane == 2.0, w1[rows], jnp.where(
            lane == 3.0, w2[rows], jnp.where(lane == 4.0, l1, jnp.where(lane == 5.0, l2, 0.0))))))
        g8_ref[bb, 0] = g8


def _out_project(a, bmix, residual, modsel, w_out, ln_g, rw, rb, tri, upper):
    b, s_out, _ = a.shape
    nt = s_out // TM
    const = lambda bi, i: (0, 0)
    with_ctx = len(residual) == 2
    sel = (lambda i: jnp.minimum(i, 1)) if with_ctx else (lambda i: 1)
    tok = lambda bi, i: (bi, i, 0)
    first_lat = residual[0].shape[1] // TM - nt
    res_specs = _token_specs(0) if with_ctx else [pl.BlockSpec((PB, TM, D), lambda bi, i: (bi, i + first_lat, 0))]
    return pl.pallas_call(
        functools.partial(_outproj_kernel, with_ctx=with_ctx),
        grid=(b // PB, nt),
        in_specs=[
            pl.BlockSpec((PB, TM, QW), tok),
            pl.BlockSpec((PB, TM, UW), tok),
        ] + res_specs + [
            pl.BlockSpec((PB, 1, 1, 6 * D), lambda bi, i: (bi, sel(i), 0, 0)),
            pl.BlockSpec((D, D), const),
            pl.BlockSpec((1, D), const),
            pl.BlockSpec((D, 128), const),
            pl.BlockSpec((1, 128), const),
            pl.BlockSpec((TM, TM), const),
            pl.BlockSpec((128, 128), const),
        ],
        out_specs=[
            pl.BlockSpec((PB, TM, D), tok),
            pl.BlockSpec((PB, TM, D), tok),
            pl.BlockSpec((PB, TM, 128), tok),
            pl.BlockSpec((PB, 1, 1, 128), lambda bi, i: (bi, i, 0, 0)),
        ],
        out_shape=[
            jax.ShapeDtypeStruct((b, s_out, D), F32),
            jax.ShapeDtypeStruct((b, s_out, D), BF16),
            jax.ShapeDtypeStruct((b, s_out, 128), F32),
            jax.ShapeDtypeStruct((b, nt, 1, 128), F32),
        ],
        compiler_params=_params(("arbitrary", "arbitrary")),
        name="out_projection_router",
    )(a, bmix, *residual, modsel, w_out, ln_g, rw, rb, tri, upper)


def _pack_halves(x, exact=False):
    half = x.shape[1] // 2
    a, b = x[:, 0:half], x[:, half:]
    if not exact:
        a, b = a.astype(BF16).astype(F32), b.astype(BF16).astype(F32)
    return lax.bitcast_convert_type(a, U32) | (lax.bitcast_convert_type(b, U32) >> 16)


def _unpack_halves(w):
    a = lax.bitcast_convert_type(w & jnp.uint32(0xFFFF0000), F32)
    b = lax.bitcast_convert_type(w << 16, F32)
    return jnp.concatenate([a, b], axis=1).astype(BF16)


def _rows_copy(src_ref, src_row, dst_ref, dst_row, rows, sem):
    rows = pl.multiple_of(rows, GRP)
    return pltpu.make_async_copy(src_ref.at[pl.ds(pl.multiple_of(src_row, GRP), rows), :],
                                 dst_ref.at[pl.ds(pl.multiple_of(dst_row, GRP), rows), :], sem)


def _wait_rows(count, src_ref, dst_ref, sem):
    @pl.when(count > 0)
    def _():
        _rows_copy(src_ref, 0, dst_ref, 0, count * GRP, sem).wait()


def _start_runs(runs, tile, copy):
    dst_ref, off_ref, len_ref = runs
    for e in range(N_EXP):
        idx = tile * N_EXP + e
        rows = len_ref[idx]

        @pl.when(rows > 0)
        def _():
            copy(dst_ref[idx], off_ref[idx], rows).start(priority=e % 2)


def _dispatch_kernel(dst_ref, off_ref, len_ref, ng_ref, zdst_ref, zlen_ref, h_ref, r_ref, xs_ref, loc, zbuf, sems,
                     zsem):
    runs = (dst_ref, off_ref, len_ref)
    i = pl.program_id(0)
    n = pl.num_programs(0)

    @pl.when(i == 0)
    def _():
        zbuf[...] = jnp.zeros(zbuf.shape, U32)
        for e in range(N_EXP):
            pl.when(zlen_ref[e] > 0)(lambda e=e: _rows_copy(zbuf, 0, xs_ref, zdst_ref[e], zlen_ref[e], zsem).start())
        for e in range(N_EXP):
            pl.when(zlen_ref[e] > 0)(lambda e=e: _rows_copy(zbuf, 0, xs_ref, zdst_ref[e], zlen_ref[e], zsem).wait())

    def run(slot):
        buf = loc.at[slot]
        sem = sems.at[slot]

        @pl.when(i >= 2)
        def _():
            _wait_rows(ng_ref[i - 2], buf, xs_ref, sem)

        rt = r_ref[...].T
        rows = lax.broadcasted_iota(jnp.int32, (LR, TM), 0).astype(F32)
        perm = jnp.where((rows == rt[4:5, :]) | (rows == rt[5:6, :]), 1.0, 0.0).astype(BF16)
        buf[...] = _pack_halves(_dot(perm, h_ref[...]), exact=True)

        _start_runs(runs, i, lambda dst, off, rows: _rows_copy(buf, off, xs_ref, dst, rows, sem))

        @pl.when(i == n - 1)
        def _():
            _wait_rows(ng_ref[i], buf, xs_ref, sem)

            @pl.when(i >= 1)
            def _():
                _wait_rows(ng_ref[i - 1], loc.at[1 - slot], xs_ref, sems.at[1 - slot])

    pl.when(i % 2 == 0)(lambda: run(0))
    pl.when(i % 2 == 1)(lambda: run(1))


def _dispatch(runs, ng, zdst, zlen, h2, rinfo, cap):
    t = h2.shape[0]
    tok = lambda i, *_: (i, 0)
    return pl.pallas_call(
        _dispatch_kernel,
        grid_spec=pltpu.PrefetchScalarGridSpec(
            num_scalar_prefetch=6,
            grid=(t // TM,),
            in_specs=[
                pl.BlockSpec((TM, D), tok),
                pl.BlockSpec((TM, 128), tok),
            ],
            out_specs=pl.BlockSpec(memory_space=pl.ANY),
            scratch_shapes=[
                pltpu.VMEM((2, LR, D // 2), U32),
                pltpu.VMEM((BLK, D // 2), U32),
                pltpu.SemaphoreType.DMA((2,)),
                pltpu.SemaphoreType.DMA,
            ],
        ),
        out_shape=jax.ShapeDtypeStruct((cap, D // 2), U32),
        compiler_params=_params(("arbitrary",)),
        name="moe_dispatch",
    )(*runs, ng, zdst, zlen, h2, rinfo)


def _expert_kernel(be_ref, nu_ref, x_ref, wg_ref, wu_ref, wd_ref, y_ref, wg_sc, wu_sc, wd_sc):
    i = pl.program_id(0)
    used = i < nu_ref[0]
    fresh = (i == 0) | (be_ref[i] != be_ref[jnp.maximum(i - 1, 0)])

    @pl.when(used & fresh)
    def _():
        wg_sc[...] = wg_ref[0, 0].astype(BF16)
        wu_sc[...] = wu_ref[0, 0].astype(BF16)
        wd_sc[...] = wd_ref[0, 0].astype(BF16)

    @pl.when(used)
    def _():
        xb = _unpack_halves(x_ref[...])
        g = _dot(xb, wg_sc[...])
        u = _dot(xb, wu_sc[...])
        hid = (g * _sigmoid(g)) * u
        y_ref[...] = _pack_halves(_dot(hid.astype(BF16), wd_sc[...]))

    @pl.when(jnp.logical_not(used))
    def _():
        y_ref[...] = jnp.zeros(y_ref.shape, U32)


def _experts(blk_e, n_used, xs, wg, wu, wd, layer):
    cap = xs.shape[0]
    row = lambda i, be, nu: (jnp.minimum(i, nu[0] - 1), 0)
    wsel = lambda i, be, nu: (layer, be[i], 0, 0)
    return pl.pallas_call(
        _expert_kernel,
        grid_spec=pltpu.PrefetchScalarGridSpec(
            num_scalar_prefetch=2,
            grid=(cap // BLK,),
            in_specs=[
                pl.BlockSpec((BLK, D // 2), row),
                pl.BlockSpec((1, 1, D, D_EXP), wsel),
                pl.BlockSpec((1, 1, D, D_EXP), wsel),
                pl.BlockSpec((1, 1, D_EXP, D), wsel),
            ],
            out_specs=pl.BlockSpec((BLK, D // 2), lambda i, be, nu: (i, 0)),
            scratch_shapes=[
                pltpu.VMEM((D, D_EXP), BF16),
                pltpu.VMEM((D, D_EXP), BF16),
                pltpu.VMEM((D_EXP, D), BF16),
            ],
        ),
        out_shape=jax.ShapeDtypeStruct((cap, D // 2), U32),
        compiler_params=_params(("arbitrary",)),
        name="moe_experts",
    )(blk_e, n_used, xs, wg, wu, wd)


def _combine_kernel(dst_ref, off_ref, len_ref, ng_ref, y_ref, x1_ref, r_ref, mod_ref, o_ref, loc, sems):
    runs = (dst_ref, off_ref, len_ref)
    i = pl.program_id(0)
    n = pl.num_programs(0)

    def fetch(tile, slot):
        _start_runs(runs, tile, lambda dst, off, rows: _rows_copy(y_ref, dst, loc.at[slot], off, rows, sems.at[slot]))

    @pl.when(i == 0)
    def _():
        loc[...] = jnp.zeros(loc.shape, U32)
        fetch(0, 0)

    def run(slot):
        @pl.when(i + 1 < n)
        def _():
            fetch(i + 1, 1 - slot)

        _wait_rows(ng_ref[i], y_ref, loc.at[slot], sems.at[slot])
        ysb = _unpack_halves(loc[slot])
        info = r_ref[...]
        col = lax.broadcasted_iota(jnp.int32, (TM, LR), 1).astype(F32)
        pick = jnp.concatenate([jnp.where(col == info[:, 4:5], 1.0, 0.0).astype(BF16),
                                jnp.where(col == info[:, 5:6], 1.0, 0.0).astype(BF16)], axis=0)
        y12 = _dot(pick, ysb)
        g2 = mod_ref[0, 0][:, 5 * D:6 * D]
        o_ref[...] = x1_ref[...] + g2 * (info[:, 2:3] * y12[0:TM] + info[:, 3:4] * y12[TM:2 * TM])

    pl.when(i % 2 == 0)(lambda: run(0))
    pl.when(i % 2 == 1)(lambda: run(1))


def _combine(runs, ng, ys, x1, rinfo, modsel, *, tiles_per_batch, with_ctx):
    t = x1.shape[0]
    if with_ctx:
        msel = lambda i, *_: (i // tiles_per_batch, jnp.minimum(i % tiles_per_batch, 1), 0, 0)
    else:
        msel = lambda i, *_: (i // tiles_per_batch, 1, 0, 0)
    tok = lambda i, *_: (i, 0)
    return pl.pallas_call(
        _combine_kernel,
        grid_spec=pltpu.PrefetchScalarGridSpec(
            num_scalar_prefetch=4,
            grid=(t // TM,),
            in_specs=[
                pl.BlockSpec(memory_space=pl.ANY),
                pl.BlockSpec((TM, D), tok),
                pl.BlockSpec((TM, 128), tok),
                pl.BlockSpec((1, 1, 1, 6 * D), msel),
            ],
            out_specs=pl.BlockSpec((TM, D), tok),
            scratch_shapes=[pltpu.VMEM((2, LR, D // 2), U32), pltpu.SemaphoreType.DMA((2,))],
        ),
        out_shape=jax.ShapeDtypeStruct((t, D), F32),
        compiler_params=_params(("arbitrary",)),
        name="moe_combine",
    )(*runs, ng, ys, x1, rinfo, modsel)


def _moe(x1, h2, rinfo, g8, modsel, wg, wu, wd, layer, *, with_ctx):
    b, s, _ = x1.shape
    t = b * s
    n_tiles = t // TM
    i32 = jnp.int32
    grp = g8.reshape(n_tiles, 128)[:, 0:N_EXP].astype(i32)
    n_blk = -(-(2 * t + n_tiles * N_EXP * (GRP - 1) + N_EXP * (BLK - 1)) // BLK)
    cap = n_blk * BLK
    goff_end = jnp.cumsum(grp, axis=1)
    ng = goff_end[:, N_EXP - 1].astype(i32)
    rows_e = GRP * jnp.sum(grp, axis=0)
    padded = (rows_e + BLK - 1) // BLK * BLK
    pend = jnp.cumsum(padded)
    pstart = pend - padded
    run_dst = (pstart[None, :] + GRP * (jnp.cumsum(grp, axis=0) - grp)).reshape(-1).astype(i32)
    run_off = (GRP * (goff_end - grp)).reshape(-1).astype(i32)
    run_len = (GRP * grp).reshape(-1).astype(i32)
    runs = (run_dst, run_off, run_len)
    zdst = (pstart + rows_e).astype(i32)
    zlen = (padded - rows_e).astype(i32)
    blk_e = jnp.minimum(jnp.sum((pend[None, :] <= (jnp.arange(n_blk, dtype=i32) * BLK)[:, None]).astype(i32), axis=1),
                        N_EXP - 1).astype(i32)
    n_used = (pend[N_EXP - 1:N_EXP] // BLK).astype(i32)
    info = rinfo.reshape(t, 128)
    xs = _dispatch(runs, ng, zdst, zlen, h2.reshape(t, D), info, cap)
    ys = _experts(blk_e, n_used, xs, wg, wu, wd, layer)
    out = _combine(runs, ng, ys, x1.reshape(t, D), info, modsel, tiles_per_batch=s // TM, with_ctx=with_ctx)
    return out.reshape(b, s, D)


def _rope_tables(n_ctx, s_lat):
    rows = s_lat // GRID_W
    row = jnp.repeat(jnp.arange(rows, dtype=F32), GRID_W)
    col = jnp.tile(jnp.arange(GRID_W, dtype=F32), rows)
    n_freq = HD // 4
    inv = ROPE_THETA ** (-jnp.arange(n_freq, dtype=F32) / n_freq)
    ang = jnp.concatenate([row[:, None] * inv, col[:, None] * inv], axis=-1)
    cos = jnp.concatenate([jnp.ones((n_ctx, HD // 2), F32), jnp.cos(ang)], axis=0)
    sin = jnp.concatenate([jnp.zeros((n_ctx, HD // 2), F32), jnp.sin(ang)], axis=0)
    cos = jnp.concatenate([cos, cos, cos, cos], axis=1)
    sin = jnp.concatenate([-sin, sin, -sin, sin], axis=1)
    return cos, sin


def _block_diag_ones(n):
    i = jnp.arange(n) // HD
    return (i[:, None] == i[None, :]).astype(BF16)


def kernel(x, c, ctx, c_ctx, mod_w, mod_b, ln1_g, ln2_g, w_in_ab, w_out_ab, q_norm_a, k_norm_a, conv_w, conv_b,
           conv_ln_g, conv_ln_b, w_in_cd, w_out_cd, q_norm_c, k_norm_c, sink_c, pool_w, pool_scale,
           rt_grp_w, rt_grp_b, rt_exp_w, rt_exp_b, ex_gate, ex_up, ex_down):
    b, s_lat, _ = x.shape
    n_ctx = ctx.shape[1]
    assert n_ctx == TM and s_lat % TM == 0 and b <= 8 and b % PB == 0

    cc = jnp.zeros((16, D), F32).at[0:b].set(c).at[8].set(c_ctx)
    mod = _modulation(cc, mod_w, mod_b)

    def modsel(l):
        return jnp.stack([jnp.broadcast_to(mod[l, 8], (b, 6 * D)), mod[l, 0:b]], axis=1).reshape(b, 2, 1, 6 * D)

    cos, sin = _rope_tables(n_ctx, s_lat)
    bdq = _block_diag_ones(QW)
    bdk = _block_diag_ones(KW)
    tri = (jnp.arange(TM)[:, None] > jnp.arange(TM)[None, :]).astype(BF16)
    upper = (jnp.arange(128)[:, None] < jnp.arange(128)[None, :]).astype(BF16)

    def router(l):
        rw = jnp.zeros((D, 128), F32).at[:, 0:N_EXP].set(rt_exp_w[l]).at[:, N_EXP:N_EXP + N_GROUPS].set(rt_grp_w[l])
        rb = jnp.zeros((1, 128), F32).at[0, 0:N_EXP].set(rt_exp_b[l]).at[0, N_EXP:N_EXP + N_GROUPS].set(rt_grp_b[l])
        return rw, rb

    def tile_gain(g, n):
        return jnp.tile(g, n).reshape(1, n * HD)

    s_tot = n_ctx + s_lat

    ms = modsel(0)
    q, k, v, glu = _project(ctx, x, 0, s_tot, ms, ln1_g[0:1], w_in_ab[0].astype(BF16), cos, sin,
                            tile_gain(q_norm_a[0], NH), tile_gain(k_norm_a[0], NKV), bdq, bdk, glu=True)
    att = _dense_attention(q, k, v, n_ctx)
    cv = _conv_module(glu, conv_w[0], conv_b[0:1], conv_ln_g[0:1], conv_ln_b[0:1], n_ctx)
    rw, rb = router(0)
    x1, h2, rinfo, g8 = _out_project(att, cv, (ctx, x), ms, w_out_ab[0].astype(BF16), ln2_g[0:1], rw, rb, tri, upper)
    xc = _moe(x1, h2, rinfo, g8, ms, ex_gate, ex_up, ex_down, 0, with_ctx=True)

    ms = modsel(1)
    q, k, v, u = _project(xc, xc, n_ctx // TM, s_tot, ms, ln1_g[1:2], w_in_cd[0].astype(BF16), cos, sin,
                          tile_gain(q_norm_c[0], NH), tile_gain(k_norm_c[0], NKV), bdq, bdk, glu=False)
    att = _window_attention(sink_c[0], q, k, v, n_ctx)
    pm = _pool_mixer(u, pool_w[0].astype(BF16), pool_scale[0:1], n_ctx)
    rw, rb = router(1)
    x1, h2, rinfo, g8 = _out_project(att, pm, (xc,), ms, w_out_cd[0].astype(BF16), ln2_g[1:2], rw, rb, tri, upper)
    return _moe(x1, h2, rinfo, g8, ms, ex_gate, ex_up, ex_down, 1, with_ctx=False)
```

```python
import functools

import jax
import jax.numpy as jnp
from jax import lax
from jax.experimental import pallas as pl
from jax.experimental.pallas import tpu as pltpu

F32 = jnp.float32
BF16 = jnp.bfloat16
U32 = jnp.uint32

D = 1024
HD = 64
NH = 8
NKV = 2
GQ = NH // NKV
QW = NH * HD
KW = NKV * HD
UW = 512
EPS = 1e-6
ROPE_THETA = 10000.0
GRID_W = 64
CONV_K = 31
WINDOW = 128
POOL_SIZES = (2, 4, 8, 16)
N_GROUPS = 4
PER_GROUP = 8
N_EXP = N_GROUPS * PER_GROUP
D_EXP = D // 2

TM = 256
PB = 4
TQ = 256
TQD = 256
TKL = 512
SUB = 64
HALO = 16
BLK = 1024
GRP = 8
NG = 96
LR = NG * GRP
VMEM_LIMIT = 56 * 1024 * 1024


def _sigmoid(x):
    return 1.0 / (1.0 + jnp.exp(-x))


def _dot(a, b):
    return jnp.dot(a, b, preferred_element_type=F32)


def _dot_nt(a, b):
    return lax.dot_general(a, b, (((1,), (1,)), ((), ())), preferred_element_type=F32)


def _split(a):
    hi = a.astype(BF16)
    lo = (a - hi.astype(F32)).astype(BF16)
    return hi, lo


def _dot3(a, w):
    a_hi, a_lo = _split(a)
    w_hi, w_lo = _split(w)
    return _dot(a_hi, w_hi) + _dot(a_lo, w_hi) + _dot(a_hi, w_lo)


def _rmsnorm(x, g):
    return x * lax.rsqrt(jnp.mean(x * x, axis=-1, keepdims=True) + EPS) * g


def _params(sem, vmem=VMEM_LIMIT):
    return pltpu.CompilerParams(dimension_semantics=sem, vmem_limit_bytes=vmem)


def _mod_kernel(c_ref, w_ref, b_ref, o_ref):
    a = c_ref[...]
    a = a * _sigmoid(a)
    o_ref[0] = _dot3(a, w_ref[0]) + b_ref[0]


def _modulation(cc, mod_w, mod_b):
    depth = mod_w.shape[0]
    tn = 1024
    return pl.pallas_call(
        _mod_kernel,
        grid=(depth, 6 * D // tn),
        in_specs=[
            pl.BlockSpec((16, D), lambda l, j: (0, 0)),
            pl.BlockSpec((1, D, tn), lambda l, j: (l, 0, j)),
            pl.BlockSpec((1, 1, tn), lambda l, j: (l, 0, j)),
        ],
        out_specs=pl.BlockSpec((1, 16, tn), lambda l, j: (l, 0, j)),
        out_shape=jax.ShapeDtypeStruct((depth, 16, 6 * D), F32),
        compiler_params=_params(("arbitrary", "arbitrary")),
        name="modulation",
    )(cc, mod_w, mod_b.reshape(depth, 1, 6 * D))


def _swap_halves(t):
    w = t.shape[1]
    lane = lax.broadcasted_iota(jnp.int32, t.shape, 1)
    first = (lane & (HD - 1)) < (HD // 2)
    return jnp.where(first, pltpu.roll(t, w - HD // 2, 1), pltpu.roll(t, HD // 2, 1))


def _head_norm_rope(t, bd, g, cos, sin, scale):
    hi, lo = _split(t * t)
    ssq = _dot(hi, bd) + _dot(lo, bd)
    tn = t * lax.rsqrt(ssq * (1.0 / HD) + EPS) * g
    n = t.shape[1] // 128
    if n > 1:
        cos = jnp.concatenate([cos] * n, axis=1)
        sin = jnp.concatenate([sin] * n, axis=1)
    out = tn * cos + _swap_halves(tn) * sin
    return out * scale if scale != 1.0 else out


def _tile_rows(c_ref, x_ref, bb):
    return jnp.where(pl.program_id(1) == 0, c_ref[bb], x_ref[bb])


def _modulated(x, mod, lng, shift_at, scale_at):
    return _rmsnorm(x, lng) * (1.0 + mod[:, scale_at * D:(scale_at + 1) * D]) + mod[:, shift_at * D:(shift_at + 1) * D]


def _proj_kernel(c_ref, x_ref, mod_ref, lng_ref, w_ref, cos_ref, sin_ref, qg_ref, kg_ref, bdq_ref, bdk_ref,
                 q_ref, k_ref, v_ref, u_ref, *, glu):
    h = jnp.concatenate([_modulated(_tile_rows(c_ref, x_ref, bb), mod_ref[bb, 0], lng_ref[...], 0, 1)
                         for bb in range(PB)], axis=0)
    px = _dot(h.astype(BF16), w_ref[...])
    cos = jnp.concatenate([cos_ref[...]] * PB, axis=0)
    sin = jnp.concatenate([sin_ref[...]] * PB, axis=0)
    q = _head_norm_rope(px[:, 0:QW], bdq_ref[...], qg_ref[...], cos, sin, HD ** -0.5)
    k = _head_norm_rope(px[:, QW:QW + KW], bdk_ref[...], kg_ref[...], cos, sin, 1.0)
    v = px[:, QW + KW:QW + 2 * KW]
    lane = lax.broadcasted_iota(jnp.int32, (TM, KW), 1)
    ones_col = jnp.where(lane == HD, 1.0, 0.0)
    o_u = QW + 2 * KW
    if glu:
        u = px[:, o_u:o_u + UW] * _sigmoid(px[:, o_u + UW:o_u + 2 * UW])
    else:
        u = px[:, o_u:o_u + UW]
    for bb in range(PB):
        rows = slice(bb * TM, (bb + 1) * TM)
        for hh in range(NH):
            q_ref[bb, hh] = q[rows, hh * HD:(hh + 1) * HD].astype(BF16)
        for j in range(NKV):
            k_ref[bb, j] = k[rows, j * HD:(j + 1) * HD].astype(BF16)
            vj = v[rows] if j == 0 else pltpu.roll(v[rows], KW - j * HD, 1)
            v_ref[bb, j] = jnp.where(lane < HD, vj, ones_col).astype(BF16)
        u_ref[bb] = u[rows]


def _token_specs(lat_first):
    return [pl.BlockSpec((PB, TM, D), lambda bi, i: (bi, 0, 0)),
            pl.BlockSpec((PB, TM, D), lambda bi, i: (bi, jnp.maximum(i - 1, 0) + lat_first, 0))]


def _project(ctx_rows, lat_rows, lat_first, s_tot, modsel, ln_g, w_in, cos, sin, q_g, k_g, bdq, bdk, *, glu):
    b = ctx_rows.shape[0]
    n_in = w_in.shape[1]
    nt = s_tot // TM
    const = lambda bi, i: (0, 0)
    return pl.pallas_call(
        functools.partial(_proj_kernel, glu=glu),
        grid=(b // PB, nt),
        in_specs=_token_specs(lat_first) + [
            pl.BlockSpec((PB, 1, 1, 6 * D), lambda bi, i: (bi, jnp.minimum(i, 1), 0, 0)),
            pl.BlockSpec((1, D), const),
            pl.BlockSpec((D, n_in), const),
            pl.BlockSpec((TM, 128), lambda bi, i: (i, 0)),
            pl.BlockSpec((TM, 128), lambda bi, i: (i, 0)),
            pl.BlockSpec((1, QW), const),
            pl.BlockSpec((1, KW), const),
            pl.BlockSpec((QW, QW), const),
            pl.BlockSpec((KW, KW), const),
        ],
        out_specs=[
            pl.BlockSpec((PB, NH, TM, HD), lambda bi, i: (bi, 0, i, 0)),
            pl.BlockSpec((PB, NKV, TM, HD), lambda bi, i: (bi, 0, i, 0)),
            pl.BlockSpec((PB, NKV, TM, 2 * HD), lambda bi, i: (bi, 0, i, 0)),
            pl.BlockSpec((PB, TM, UW), lambda bi, i: (bi, i, 0)),
        ],
        out_shape=[
            jax.ShapeDtypeStruct((b, NH, s_tot, HD), BF16),
            jax.ShapeDtypeStruct((b, NKV, s_tot, HD), BF16),
            jax.ShapeDtypeStruct((b, NKV, s_tot, 2 * HD), BF16),
            jax.ShapeDtypeStruct((b, s_tot, UW), F32),
        ],
        compiler_params=_params(("arbitrary", "arbitrary")),
        name="in_projection",
    )(ctx_rows, lat_rows, modsel, ln_g, w_in, cos, sin, q_g, k_g, bdq, bdk)


def _merge_heads(o):
    tq = o.shape[0] // GQ
    return jnp.concatenate([o[g * tq:(g + 1) * tq] for g in range(GQ)], axis=1)


def _lane_max(s):
    return functools.reduce(jnp.maximum, [s[:, j * 128:(j + 1) * 128] for j in range(s.shape[1] // 128)])


def _dense_attn_kernel(q_ref, k_ref, v_ref, o_ref, sc_sc, sl_sc, m_sc, *, n_ctx, s_tot):
    qi = pl.program_id(2)
    q = q_ref[0].reshape(GQ * TQD, HD)
    n_lat = (s_tot - n_ctx) // TKL

    def chunks(n):
        spans = [(0, n_ctx)] + [(n_ctx + c * TKL, n_ctx + (c + 1) * TKL) for c in range(n)]
        return list(zip(spans, [sc_sc] + [sl_sc.at[c] for c in range(n)]))

    def scores(n):
        for i, ((lo, hi), slot) in enumerate(chunks(n)):
            s = _dot_nt(q, k_ref[0, 0, lo:hi, :])
            slot[...] = s
            m_sc[i] = _lane_max(s)
        m_lane = functools.reduce(jnp.maximum, [m_sc[i] for i in range(n + 1)])
        m_sc[0] = jnp.broadcast_to(jnp.max(m_lane, axis=1, keepdims=True), m_lane.shape)

    def weighted(n):
        acc = None
        for (lo, hi), slot in chunks(n):
            m = jnp.concatenate([m_sc[0]] * ((hi - lo) // 128), axis=1)
            part = _dot(jnp.exp(slot[...] - m).astype(BF16), v_ref[0, 0, lo:hi, :])
            acc = part if acc is None else acc + part
        o_ref[0] = _merge_heads(acc[:, 0:HD] / acc[:, HD:HD + 1]).astype(BF16)

    is_lat = qi * TQD >= n_ctx
    pl.when(is_lat)(lambda: scores(n_lat))
    pl.when(jnp.logical_not(is_lat))(lambda: scores(0))
    pl.when(is_lat)(lambda: weighted(n_lat))
    pl.when(jnp.logical_not(is_lat))(lambda: weighted(0))


def _dense_attention(q, k, v, n_ctx):
    b, _, s_tot, _ = q.shape
    rows = GQ * TQD
    return pl.pallas_call(
        functools.partial(_dense_attn_kernel, n_ctx=n_ctx, s_tot=s_tot),
        grid=(b, NKV, s_tot // TQD),
        in_specs=[
            pl.BlockSpec((1, GQ, TQD, HD), lambda bi, j, i: (bi, j, i, 0)),
            pl.BlockSpec((1, 1, s_tot, HD), lambda bi, j, i: (bi, j, 0, 0)),
            pl.BlockSpec((1, 1, s_tot, 2 * HD), lambda bi, j, i: (bi, j, 0, 0)),
        ],
        out_specs=pl.BlockSpec((1, TQD, GQ * HD), lambda bi, j, i: (bi, i, j)),
        out_shape=jax.ShapeDtypeStruct((b, s_tot, QW), BF16),
        scratch_shapes=[
            pltpu.VMEM((rows, n_ctx), F32),
            pltpu.VMEM(((s_tot - n_ctx) // TKL, rows, TKL), F32),
            pltpu.VMEM(((s_tot - n_ctx) // TKL + 1, rows, 128), F32),
        ],
        compiler_params=_params(("arbitrary", "arbitrary", "arbitrary")),
        name="dense_attention",
    )(q, k, v)


def _window_attn_kernel(sink_ref, q_ref, k_ref, v_ref, o_ref, sc_sc, sw_sc, m_sc, e_sc, *, n_ctx, s_tot):
    qi = pl.program_id(1)
    span = TQ + 2 * WINDOW
    q0 = n_ctx + qi * TQ
    start = pl.multiple_of(jnp.clip(q0 - WINDOW, n_ctx, s_tot - span), 128)
    row = lax.broadcasted_iota(jnp.int32, (TQ, span), 0)
    col = lax.broadcasted_iota(jnp.int32, (TQ, span), 1)
    band = jnp.where(jnp.abs((q0 - start) + row - col) <= WINDOW, 0.0, -jnp.inf)
    band = jnp.concatenate([band] * GQ, axis=0)
    for j in range(NKV):
        q = q_ref[0, j * GQ:(j + 1) * GQ].reshape(GQ * TQ, HD)
        s_c = _dot_nt(q, k_ref[0, j, 0:n_ctx, :])
        s_w = _dot_nt(q, k_ref[0, j, pl.ds(start, span), :]) + band
        sc_sc[j] = s_c
        sw_sc[j] = s_w
        sink = jnp.concatenate([jnp.full((TQ, 128), sink_ref[j * GQ + g], F32) for g in range(GQ)], axis=0)
        m_lane = jnp.maximum(_lane_max(s_c), _lane_max(s_w))
        m_sc[j] = jnp.maximum(jnp.broadcast_to(jnp.max(m_lane, axis=1, keepdims=True), m_lane.shape), sink)
        e_sc[j] = jnp.exp(sink - m_sc[j])
    for j in range(NKV):
        m = m_sc[j]
        p_c = jnp.exp(sc_sc[j] - jnp.concatenate([m] * (n_ctx // 128), axis=1)).astype(BF16)
        p_w = jnp.exp(sw_sc[j] - jnp.concatenate([m] * (span // 128), axis=1)).astype(BF16)
        acc = _dot(p_c, v_ref[0, j, 0:n_ctx, :]) + _dot(p_w, v_ref[0, j, pl.ds(start, span), :])
        l = acc[:, HD:HD + 1] + e_sc[j][:, 0:1]
        o_ref[0, :, j * GQ * HD:(j + 1) * GQ * HD] = _merge_heads(acc[:, 0:HD] / l).astype(BF16)


def _window_attention(sink, q, k, v, n_ctx):
    b, _, s_tot, _ = q.shape
    s_lat = s_tot - n_ctx
    off = n_ctx // TQ
    return pl.pallas_call(
        functools.partial(_window_attn_kernel, n_ctx=n_ctx, s_tot=s_tot),
        grid_spec=pltpu.PrefetchScalarGridSpec(
            num_scalar_prefetch=1,
            grid=(b, s_lat // TQ),
            in_specs=[
                pl.BlockSpec((1, NH, TQ, HD), lambda bi, i, sk: (bi, 0, i + off, 0)),
                pl.BlockSpec((1, NKV, s_tot, HD), lambda bi, i, sk: (bi, 0, 0, 0)),
                pl.BlockSpec((1, NKV, s_tot, 2 * HD), lambda bi, i, sk: (bi, 0, 0, 0)),
            ],
            out_specs=pl.BlockSpec((1, TQ, QW), lambda bi, i, sk: (bi, i, 0)),
            scratch_shapes=[
                pltpu.VMEM((NKV, GQ * TQ, n_ctx), F32),
                pltpu.VMEM((NKV, GQ * TQ, TQ + 2 * WINDOW), F32),
                pltpu.VMEM((NKV, GQ * TQ, 128), F32),
                pltpu.VMEM((NKV, GQ * TQ, 128), F32),
            ],
        ),
        out_shape=jax.ShapeDtypeStruct((b, s_lat, QW), BF16),
        compiler_params=_params(("arbitrary", "arbitrary")),
        name="window_attention",
    )(sink, q, k, v)


def _conv_kernel(g_ref, cw_ref, cb_ref, lg_ref, lb_ref, o_ref, pad_sc, win_sc, *, n_ctx, s_tot):
    zeros = jnp.zeros((HALO, UW), F32)
    pad_sc[0:HALO, :] = zeros
    pad_sc[HALO + n_ctx:2 * HALO + n_ctx, :] = zeros
    pad_sc[2 * HALO + s_tot:3 * HALO + s_tot, :] = zeros

    def fill(i, carry):
        src = pl.multiple_of(i * TM, TM)
        dst = pl.multiple_of(src + HALO + jnp.where(src >= n_ctx, HALO, 0), 8)
        pad_sc[pl.ds(dst, TM), :] = g_ref[0, pl.ds(src, TM), :]
        return carry

    lax.fori_loop(0, s_tot // TM, fill, 0)
    half = CONV_K // 2

    def tile(i, carry):
        src = pl.multiple_of(i * SUB, SUB)
        base = pl.multiple_of(src + jnp.where(src >= n_ctx, HALO, 0), 8)
        win = pad_sc[pl.ds(base, SUB + 2 * HALO), :]
        keep = SUB + 2 * HALO - 8
        for r in range(8):
            win_sc[r, 0:keep, :] = win[r:r + keep, :]
        acc = jnp.zeros((SUB, UW), F32) + cb_ref[...]
        for t in range(CONV_K):
            a, r = divmod(HALO - half + t, 8)
            acc = acc + win_sc[r, 8 * a:8 * a + SUB, :] * cw_ref[t:t + 1, :]
        mu = jnp.mean(acc, axis=-1, keepdims=True)
        xc = acc - mu
        var = jnp.mean(xc * xc, axis=-1, keepdims=True)
        yn = xc * lax.rsqrt(var + EPS) * lg_ref[...] + lb_ref[...]
        o_ref[0, pl.ds(src, SUB), :] = (yn * _sigmoid(yn)).astype(BF16)
        return carry

    lax.fori_loop(0, s_tot // SUB, tile, 0)


def _conv_module(glu, conv_w, conv_b, ln_g, ln_b, n_ctx):
    b, s_tot, _ = glu.shape
    const = lambda bi: (0, 0)
    return pl.pallas_call(
        functools.partial(_conv_kernel, n_ctx=n_ctx, s_tot=s_tot),
        grid=(b,),
        in_specs=[
            pl.BlockSpec((1, s_tot, UW), lambda bi: (bi, 0, 0)),
            pl.BlockSpec((CONV_K, UW), const),
            pl.BlockSpec((1, UW), const),
            pl.BlockSpec((1, UW), const),
            pl.BlockSpec((1, UW), const),
        ],
        out_specs=pl.BlockSpec((1, s_tot, UW), lambda bi: (bi, 0, 0)),
        out_shape=jax.ShapeDtypeStruct((b, s_tot, UW), BF16),
        scratch_shapes=[pltpu.VMEM((s_tot + 3 * HALO, UW), F32), pltpu.VMEM((8, SUB + 2 * HALO, UW), F32)],
        compiler_params=_params(("arbitrary",)),
        name="conv_module",
    )(glu, conv_w, conv_b, ln_g, ln_b)


def _pool_kernel(u_ref, pw_ref, ps_ref, o_ref, pad_sc, *, n_ctx, s_lat):
    zeros = jnp.zeros((HALO, UW), F32)
    pad_sc[0:HALO, :] = zeros
    pad_sc[HALO + s_lat:2 * HALO + s_lat, :] = zeros

    def fill(i, carry):
        src = pl.multiple_of(i * TM, TM)
        pad_sc[pl.ds(pl.multiple_of(src + HALO, 8), TM), :] = u_ref[0, pl.ds(pl.multiple_of(src + n_ctx, 8), TM), :]
        return carry

    lax.fori_loop(0, s_lat // TM, fill, 0)
    gw = UW // len(POOL_SIZES)

    def tile(i, carry):
        src = pl.multiple_of(i * SUB, SUB)
        win = pad_sc[pl.ds(src, SUB + 2 * HALO), :]
        t = src + lax.broadcasted_iota(jnp.int32, (SUB, 1), 0)
        outs = []
        for gi, w in enumerate(POOL_SIZES):
            lanes = slice(gi * gw, (gi + 1) * gw)
            tot = jnp.zeros((SUB, gw), F32)
            for d in range(-(w // 2), w - w // 2):
                tot = tot + win[HALO + d:HALO + d + SUB, lanes]
            lo = jnp.clip(t - w // 2, 0, s_lat)
            hi = jnp.clip(t - w // 2 + w, 0, s_lat)
            p = tot / (hi - lo).astype(F32) - win[HALO:HALO + SUB, lanes]
            outs.append(_dot(p.astype(BF16), pw_ref[gi]))
        y = jnp.concatenate(outs, axis=1) * ps_ref[...]
        o_ref[0, pl.ds(src, SUB), :] = y.astype(BF16)
        return carry

    lax.fori_loop(0, s_lat // SUB, tile, 0)


def _pool_mixer(u, pool_w, pool_scale, n_ctx):
    b, s_tot, _ = u.shape
    s_lat = s_tot - n_ctx
    gw = UW // len(POOL_SIZES)
    return pl.pallas_call(
        functools.partial(_pool_kernel, n_ctx=n_ctx, s_lat=s_lat),
        grid=(b,),
        in_specs=[
            pl.BlockSpec((1, s_tot, UW), lambda bi: (bi, 0, 0)),
            pl.BlockSpec((len(POOL_SIZES), gw, gw), lambda bi: (0, 0, 0)),
            pl.BlockSpec((1, UW), lambda bi: (0, 0)),
        ],
        out_specs=pl.BlockSpec((1, s_lat, UW), lambda bi: (bi, 0, 0)),
        out_shape=jax.ShapeDtypeStruct((b, s_lat, UW), BF16),
        scratch_shapes=[pltpu.VMEM((s_lat + 2 * HALO, UW), F32)],
        compiler_params=_params(("arbitrary",)),
        name="pool_mixer",
    )(u, pool_w, pool_scale)


def _route(logits):
    shape = logits.shape
    lane = lax.broadcasted_iota(jnp.int32, shape, 1)
    lanef = lane.astype(F32)
    is_g = (lane >= N_EXP) & (lane < N_EXP + N_GROUPS)
    big = 1e9
    lg = jnp.where(is_g, logits, -jnp.inf)
    gmax = jnp.max(lg, axis=1, keepdims=True)
    gsel = jnp.min(jnp.where(is_g & (lg == gmax), lanef, big), axis=1, keepdims=True) - float(N_EXP)
    gsum = jnp.sum(jnp.exp(lg - gmax), axis=1, keepdims=True)
    g_w = 1.0 / gsum
    in_grp = (lane < N_EXP) & ((lane >> 3).astype(F32) == gsel)
    el = jnp.where(in_grp, logits, -jnp.inf)
    emax = jnp.max(el, axis=1, keepdims=True)
    ee = jnp.exp(el - emax)
    p = ee / jnp.sum(ee, axis=1, keepdims=True)
    p1 = jnp.max(jnp.where(in_grp, p, -1.0), axis=1, keepdims=True)
    i1 = jnp.min(jnp.where(in_grp & (p == p1), lanef, big), axis=1, keepdims=True)
    rest = in_grp & (lanef != i1)
    p2 = jnp.max(jnp.where(rest, p, -1.0), axis=1, keepdims=True)
    i2 = jnp.min(jnp.where(rest & (p == p2), lanef, big), axis=1, keepdims=True)
    w1 = g_w * p1 / (p1 + p2)
    w2 = g_w * p2 / (p1 + p2)
    onehot = jnp.where((lanef == i1) | (lanef == i2), 1.0, 0.0)
    return lanef, i1, i2, w1, w2, onehot


def _outproj_kernel(a_ref, b_ref, *refs, with_ctx):
    mod_ref, w_ref, lng_ref, rw_ref, rb_ref, tri_ref, upper_ref, x1_ref, h2_ref, r_ref, g8_ref = refs[-11:]
    a = jnp.concatenate([a_ref[bb] for bb in range(PB)], axis=0)
    bmix = jnp.concatenate([b_ref[bb] for bb in range(PB)], axis=0)
    y = _dot(a, w_ref[0:QW, :]) + _dot(bmix, w_ref[QW:QW + UW, :])
    h2_parts = []
    for bb in range(PB):
        x_in = _tile_rows(refs[0], refs[1], bb) if with_ctx else refs[0][bb]
        mod = mod_ref[bb, 0]
        x1 = x_in + mod[:, 2 * D:3 * D] * y[bb * TM:(bb + 1) * TM]
        x1_ref[bb] = x1
        h2_parts.append(_modulated(x1, mod, lng_ref[...], 3, 4))
        h2_ref[bb] = h2_parts[-1].astype(BF16)
    logits = _dot3(jnp.concatenate(h2_parts, axis=0), rw_ref[...]) + rb_ref[...]
    lanef, i1, i2, w1, w2, onehot = _route(logits)
    for bb in range(PB):
        rows = slice(bb * TM, (bb + 1) * TM)
        hot = onehot[rows]
        before = _dot(tri_ref[...], hot.astype(BF16))
        cnt = jnp.sum(hot, axis=0, keepdims=True)
        g8 = jnp.floor((cnt + float(GRP - 1)) * (1.0 / GRP))
        goff = _dot(jnp.broadcast_to(g8, (8, 128)).astype(BF16), upper_ref[...])[0:1, :]
        local = before + float(GRP) * goff
        lane = lax.broadcasted_iota(jnp.int32, (TM, 128), 1).astype(F32)
        e1, e2 = i1[rows], i2[rows]
        l1 = jnp.sum(jnp.where(lane == e1, local, 0.0), axis=1, keepdims=True)
        l2 = jnp.sum(jnp.where(lane == e2, local, 0.0), axis=1, keepdims=True)
        r_ref[bb] = jnp.where(lane == 0.0, e1, jnp.where(lane == 1.0, e2, jnp.where(lane == 2.0, w1[rows], jnp.where(
            lane == 3.0, w2[rows], jnp.where(lane == 4.0, l1, jnp.where(lane == 5.0, l2, 0.0))))))
        g8_ref[bb, 0] = g8


def _out_project(a, bmix, residual, modsel, w_out, ln_g, rw, rb, tri, upper):
    b, s_out, _ = a.shape
    nt = s_out // TM
    const = lambda bi, i: (0, 0)
    with_ctx = len(residual) == 2
    sel = (lambda i: jnp.minimum(i, 1)) if with_ctx else (lambda i: 1)
    tok = lambda bi, i: (bi, i, 0)
    first_lat = residual[0].shape[1] // TM - nt
    res_specs = _token_specs(0) if with_ctx else [pl.BlockSpec((PB, TM, D), lambda bi, i: (bi, i + first_lat, 0))]
    return pl.pallas_call(
        functools.partial(_outproj_kernel, with_ctx=with_ctx),
        grid=(b // PB, nt),
        in_specs=[
            pl.BlockSpec((PB, TM, QW), tok),
            pl.BlockSpec((PB, TM, UW), tok),
        ] + res_specs + [
            pl.BlockSpec((PB, 1, 1, 6 * D), lambda bi, i: (bi, sel(i), 0, 0)),
            pl.BlockSpec((D, D), const),
            pl.BlockSpec((1, D), const),
            pl.BlockSpec((D, 128), const),
            pl.BlockSpec((1, 128), const),
            pl.BlockSpec((TM, TM), const),
            pl.BlockSpec((128, 128), const),
        ],
        out_specs=[
            pl.BlockSpec((PB, TM, D), tok),
            pl.BlockSpec((PB, TM, D), tok),
            pl.BlockSpec((PB, TM, 128), tok),
            pl.BlockSpec((PB, 1, 1, 128), lambda bi, i: (bi, i, 0, 0)),
        ],
        out_shape=[
            jax.ShapeDtypeStruct((b, s_out, D), F32),
            jax.ShapeDtypeStruct((b, s_out, D), BF16),
            jax.ShapeDtypeStruct((b, s_out, 128), F32),
            jax.ShapeDtypeStruct((b, nt, 1, 128), F32),
        ],
        compiler_params=_params(("arbitrary", "arbitrary")),
        name="out_projection_router",
    )(a, bmix, *residual, modsel, w_out, ln_g, rw, rb, tri, upper)


def _pack_halves(x, exact=False):
    half = x.shape[1] // 2
    a, b = x[:, 0:half], x[:, half:]
    if not exact:
        a, b = a.astype(BF16).astype(F32), b.astype(BF16).astype(F32)
    return lax.bitcast_convert_type(a, U32) | (lax.bitcast_convert_type(b, U32) >> 16)


def _unpack_halves(w):
    a = lax.bitcast_convert_type(w & jnp.uint32(0xFFFF0000), F32)
    b = lax.bitcast_convert_type(w << 16, F32)
    return jnp.concatenate([a, b], axis=1).astype(BF16)


def _rows_copy(src_ref, src_row, dst_ref, dst_row, rows, sem):
    rows = pl.multiple_of(rows, GRP)
    return pltpu.make_async_copy(src_ref.at[pl.ds(pl.multiple_of(src_row, GRP), rows), :],
                                 dst_ref.at[pl.ds(pl.multiple_of(dst_row, GRP), rows), :], sem)


def _wait_rows(count, src_ref, dst_ref, sem):
    @pl.when(count > 0)
    def _():
        _rows_copy(src_ref, 0, dst_ref, 0, count * GRP, sem).wait()


def _start_runs(runs, tile, copy):
    dst_ref, off_ref, len_ref = runs
    for e in range(N_EXP):
        idx = tile * N_EXP + e
        rows = len_ref[idx]

        @pl.when(rows > 0)
        def _():
            copy(dst_ref[idx], off_ref[idx], rows).start(priority=e % 2)


def _dispatch_kernel(dst_ref, off_ref, len_ref, ng_ref, zdst_ref, zlen_ref, h_ref, r_ref, xs_ref, loc, zbuf, sems,
                     zsem):
    runs = (dst_ref, off_ref, len_ref)
    i = pl.program_id(0)
    n = pl.num_programs(0)

    @pl.when(i == 0)
    def _():
        zbuf[...] = jnp.zeros(zbuf.shape, U32)
        for e in range(N_EXP):
            pl.when(zlen_ref[e] > 0)(lambda e=e: _rows_copy(zbuf, 0, xs_ref, zdst_ref[e], zlen_ref[e], zsem).start())
        for e in range(N_EXP):
            pl.when(zlen_ref[e] > 0)(lambda e=e: _rows_copy(zbuf, 0, xs_ref, zdst_ref[e], zlen_ref[e], zsem).wait())

    def run(slot):
        buf = loc.at[slot]
        sem = sems.at[slot]

        @pl.when(i >= 2)
        def _():
            _wait_rows(ng_ref[i - 2], buf, xs_ref, sem)

        rt = r_ref[...].T
        rows = lax.broadcasted_iota(jnp.int32, (LR, TM), 0).astype(F32)
        perm = jnp.where((rows == rt[4:5, :]) | (rows == rt[5:6, :]), 1.0, 0.0).astype(BF16)
        buf[...] = _pack_halves(_dot(perm, h_ref[...]), exact=True)

        _start_runs(runs, i, lambda dst, off, rows: _rows_copy(buf, off, xs_ref, dst, rows, sem))

        @pl.when(i == n - 1)
        def _():
            _wait_rows(ng_ref[i], buf, xs_ref, sem)

            @pl.when(i >= 1)
            def _():
                _wait_rows(ng_ref[i - 1], loc.at[1 - slot], xs_ref, sems.at[1 - slot])

    pl.when(i % 2 == 0)(lambda: run(0))
    pl.when(i % 2 == 1)(lambda: run(1))


def _dispatch(runs, ng, zdst, zlen, h2, rinfo, cap):
    t = h2.shape[0]
    tok = lambda i, *_: (i, 0)
    return pl.pallas_call(
        _dispatch_kernel,
        grid_spec=pltpu.PrefetchScalarGridSpec(
            num_scalar_prefetch=6,
            grid=(t // TM,),
            in_specs=[
                pl.BlockSpec((TM, D), tok),
                pl.BlockSpec((TM, 128), tok),
            ],
            out_specs=pl.BlockSpec(memory_space=pl.ANY),
            scratch_shapes=[
                pltpu.VMEM((2, LR, D // 2), U32),
                pltpu.VMEM((BLK, D // 2), U32),
                pltpu.SemaphoreType.DMA((2,)),
                pltpu.SemaphoreType.DMA,
            ],
        ),
        out_shape=jax.ShapeDtypeStruct((cap, D // 2), U32),
        compiler_params=_params(("arbitrary",)),
        name="moe_dispatch",
    )(*runs, ng, zdst, zlen, h2, rinfo)


def _expert_kernel(be_ref, nu_ref, x_ref, wg_ref, wu_ref, wd_ref, y_ref, wg_sc, wu_sc, wd_sc):
    i = pl.program_id(0)
    used = i < nu_ref[0]
    fresh = (i == 0) | (be_ref[i] != be_ref[jnp.maximum(i - 1, 0)])

    @pl.when(used & fresh)
    def _():
        wg_sc[...] = wg_ref[0, 0].astype(BF16)
        wu_sc[...] = wu_ref[0, 0].astype(BF16)
        wd_sc[...] = wd_ref[0, 0].astype(BF16)

    @pl.when(used)
    def _():
        xb = _unpack_halves(x_ref[...])
        g = _dot(xb, wg_sc[...])
        u = _dot(xb, wu_sc[...])
        hid = (g * _sigmoid(g)) * u
        y_ref[...] = _pack_halves(_dot(hid.astype(BF16), wd_sc[...]))

    @pl.when(jnp.logical_not(used))
    def _():
        y_ref[...] = jnp.zeros(y_ref.shape, U32)


def _experts(blk_e, n_used, xs, wg, wu, wd, layer):
    cap = xs.shape[0]
    row = lambda i, be, nu: (jnp.minimum(i, nu[0] - 1), 0)
    wsel = lambda i, be, nu: (layer, be[i], 0, 0)
    return pl.pallas_call(
        _expert_kernel,
        grid_spec=pltpu.PrefetchScalarGridSpec(
            num_scalar_prefetch=2,
            grid=(cap // BLK,),
            in_specs=[
                pl.BlockSpec((BLK, D // 2), row),
                pl.BlockSpec((1, 1, D, D_EXP), wsel),
                pl.BlockSpec((1, 1, D, D_EXP), wsel),
                pl.BlockSpec((1, 1, D_EXP, D), wsel),
            ],
            out_specs=pl.BlockSpec((BLK, D // 2), lambda i, be, nu: (i, 0)),
            scratch_shapes=[
                pltpu.VMEM((D, D_EXP), BF16),
                pltpu.VMEM((D, D_EXP), BF16),
                pltpu.VMEM((D_EXP, D), BF16),
            ],
        ),
        out_shape=jax.ShapeDtypeStruct((cap, D // 2), U32),
        compiler_params=_params(("arbitrary",)),
        name="moe_experts",
    )(blk_e, n_used, xs, wg, wu, wd)


def _combine_kernel(dst_ref, off_ref, len_ref, ng_ref, y_ref, x1_ref, r_ref, mod_ref, o_ref, loc, sems):
    runs = (dst_ref, off_ref, len_ref)
    i = pl.program_id(0)
    n = pl.num_programs(0)

    def fetch(tile, slot):
        _start_runs(runs, tile, lambda dst, off, rows: _rows_copy(y_ref, dst, loc.at[slot], off, rows, sems.at[slot]))

    @pl.when(i == 0)
    def _():
        loc[...] = jnp.zeros(loc.shape, U32)
        fetch(0, 0)

    def run(slot):
        @pl.when(i + 1 < n)
        def _():
            fetch(i + 1, 1 - slot)

        _wait_rows(ng_ref[i], y_ref, loc.at[slot], sems.at[slot])
        ysb = _unpack_halves(loc[slot])
        info = r_ref[...]
        col = lax.broadcasted_iota(jnp.int32, (TM, LR), 1).astype(F32)
        pick = jnp.concatenate([jnp.where(col == info[:, 4:5], 1.0, 0.0).astype(BF16),
                                jnp.where(col == info[:, 5:6], 1.0, 0.0).astype(BF16)], axis=0)
        y12 = _dot(pick, ysb)
        g2 = mod_ref[0, 0][:, 5 * D:6 * D]
        o_ref[...] = x1_ref[...] + g2 * (info[:, 2:3] * y12[0:TM] + info[:, 3:4] * y12[TM:2 * TM])

    pl.when(i % 2 == 0)(lambda: run(0))
    pl.when(i % 2 == 1)(lambda: run(1))


def _combine(runs, ng, ys, x1, rinfo, modsel, *, tiles_per_batch, with_ctx):
    t = x1.shape[0]
    if with_ctx:
        msel = lambda i, *_: (i // tiles_per_batch, jnp.minimum(i % tiles_per_batch, 1), 0, 0)
    else:
        msel = lambda i, *_: (i // tiles_per_batch, 1, 0, 0)
    tok = lambda i, *_: (i, 0)
    return pl.pallas_call(
        _combine_kernel,
        grid_spec=pltpu.PrefetchScalarGridSpec(
            num_scalar_prefetch=4,
            grid=(t // TM,),
            in_specs=[
                pl.BlockSpec(memory_space=pl.ANY),
                pl.BlockSpec((TM, D), tok),
                pl.BlockSpec((TM, 128), tok),
                pl.BlockSpec((1, 1, 1, 6 * D), msel),
            ],
            out_specs=pl.BlockSpec((TM, D), tok),
            scratch_shapes=[pltpu.VMEM((2, LR, D // 2), U32), pltpu.SemaphoreType.DMA((2,))],
        ),
        out_shape=jax.ShapeDtypeStruct((t, D), F32),
        compiler_params=_params(("arbitrary",)),
        name="moe_combine",
    )(*runs, ng, ys, x1, rinfo, modsel)


def _moe(x1, h2, rinfo, g8, modsel, wg, wu, wd, layer, *, with_ctx):
    b, s, _ = x1.shape
    t = b * s
    n_tiles = t // TM
    i32 = jnp.int32
    grp = g8.reshape(n_tiles, 128)[:, 0:N_EXP].astype(i32)
    n_blk = -(-(2 * t + n_tiles * N_EXP * (GRP - 1) + N_EXP * (BLK - 1)) // BLK)
    cap = n_blk * BLK
    goff_end = jnp.cumsum(grp, axis=1)
    ng = goff_end[:, N_EXP - 1].astype(i32)
    rows_e = GRP * jnp.sum(grp, axis=0)
    padded = (rows_e + BLK - 1) // BLK * BLK
    pend = jnp.cumsum(padded)
    pstart = pend - padded
    run_dst = (pstart[None, :] + GRP * (jnp.cumsum(grp, axis=0) - grp)).reshape(-1).astype(i32)
    run_off = (GRP * (goff_end - grp)).reshape(-1).astype(i32)
    run_len = (GRP * grp).reshape(-1).astype(i32)
    runs = (run_dst, run_off, run_len)
    zdst = (pstart + rows_e).astype(i32)
    zlen = (padded - rows_e).astype(i32)
    blk_e = jnp.minimum(jnp.sum((pend[None, :] <= (jnp.arange(n_blk, dtype=i32) * BLK)[:, None]).astype(i32), axis=1),
                        N_EXP - 1).astype(i32)
    n_used = (pend[N_EXP - 1:N_EXP] // BLK).astype(i32)
    info = rinfo.reshape(t, 128)
    xs = _dispatch(runs, ng, zdst, zlen, h2.reshape(t, D), info, cap)
    ys = _experts(blk_e, n_used, xs, wg, wu, wd, layer)
    out = _combine(runs, ng, ys, x1.reshape(t, D), info, modsel, tiles_per_batch=s // TM, with_ctx=with_ctx)
    return out.reshape(b, s, D)


def _rope_tables(n_ctx, s_lat):
    rows = s_lat // GRID_W
    row = jnp.repeat(jnp.arange(rows, dtype=F32), GRID_W)
    col = jnp.tile(jnp.arange(GRID_W, dtype=F32), rows)
    n_freq = HD // 4
    inv = ROPE_THETA ** (-jnp.arange(n_freq, dtype=F32) / n_freq)
    ang = jnp.concatenate([row[:, None] * inv, col[:, None] * inv], axis=-1)
    cos = jnp.concatenate([jnp.ones((n_ctx, HD // 2), F32), jnp.cos(ang)], axis=0)
    sin = jnp.concatenate([jnp.zeros((n_ctx, HD // 2), F32), jnp.sin(ang)], axis=0)
    cos = jnp.concatenate([cos, cos, cos, cos], axis=1)
    sin = jnp.concatenate([-sin, sin, -sin, sin], axis=1)
    return cos, sin


def _block_diag_ones(n):
    i = jnp.arange(n) // HD
    return (i[:, None] == i[None, :]).astype(BF16)


def kernel(x, c, ctx, c_ctx, mod_w, mod_b, ln1_g, ln2_g, w_in_ab, w_out_ab, q_norm_a, k_norm_a, conv_w, conv_b,
           conv_ln_g, conv_ln_b, w_in_cd, w_out_cd, q_norm_c, k_norm_c, sink_c, pool_w, pool_scale,
           rt_grp_w, rt_grp_b, rt_exp_w, rt_exp_b, ex_gate, ex_up, ex_down):
    b, s_lat, _ = x.shape
    n_ctx = ctx.shape[1]
    assert n_ctx == TM and s_lat % TM == 0 and b <= 8 and b % PB == 0

    cc = jnp.zeros((16, D), F32).at[0:b].set(c).at[8].set(c_ctx)
    mod = _modulation(cc, mod_w, mod_b)

    def modsel(l):
        return jnp.stack([jnp.broadcast_to(mod[l, 8], (b, 6 * D)), mod[l, 0:b]], axis=1).reshape(b, 2, 1, 6 * D)

    cos, sin = _rope_tables(n_ctx, s_lat)
    bdq = _block_diag_ones(QW)
    bdk = _block_diag_ones(KW)
    tri = (jnp.arange(TM)[:, None] > jnp.arange(TM)[None, :]).astype(BF16)
    upper = (jnp.arange(128)[:, None] < jnp.arange(128)[None, :]).astype(BF16)

    def router(l):
        rw = jnp.zeros((D, 128), F32).at[:, 0:N_EXP].set(rt_exp_w[l]).at[:, N_EXP:N_EXP + N_GROUPS].set(rt_grp_w[l])
        rb = jnp.zeros((1, 128), F32).at[0, 0:N_EXP].set(rt_exp_b[l]).at[0, N_EXP:N_EXP + N_GROUPS].set(rt_grp_b[l])
        return rw, rb

    def tile_gain(g, n):
        return jnp.tile(g, n).reshape(1, n * HD)

    s_tot = n_ctx + s_lat

    ms = modsel(0)
    q, k, v, glu = _project(ctx, x, 0, s_tot, ms, ln1_g[0:1], w_in_ab[0].astype(BF16), cos, sin,
                            tile_gain(q_norm_a[0], NH), tile_gain(k_norm_a[0], NKV), bdq, bdk, glu=True)
    att = _dense_attention(q, k, v, n_ctx)
    cv = _conv_module(glu, conv_w[0], conv_b[0:1], conv_ln_g[0:1], conv_ln_b[0:1], n_ctx)
    rw, rb = router(0)
    x1, h2, rinfo, g8 = _out_project(att, cv, (ctx, x), ms, w_out_ab[0].astype(BF16), ln2_g[0:1], rw, rb, tri, upper)
    xc = _moe(x1, h2, rinfo, g8, ms, ex_gate, ex_up, ex_down, 0, with_ctx=True)

    ms = modsel(1)
    q, k, v, u = _project(xc, xc, n_ctx // TM, s_tot, ms, ln1_g[1:2], w_in_cd[0].astype(BF16), cos, sin,
                          tile_gain(q_norm_c[0], NH), tile_gain(k_norm_c[0], NKV), bdq, bdk, glu=False)
    att = _window_attention(sink_c[0], q, k, v, n_ctx)
    pm = _pool_mixer(u, pool_w[0].astype(BF16), pool_scale[0:1], n_ctx)
    rw, rb = router(1)
    x1, h2, rinfo, g8 = _out_project(att, pm, (xc,), ms, w_out_cd[0].astype(BF16), ln2_g[1:2], rw, rb, tri, upper)
    return _moe(x1, h2, rinfo, g8, ms, ex_gate, ex_up, ex_down, 1, with_ctx=False)
```

```python
import functools

import jax
import jax.numpy as jnp
from jax import lax
from jax.experimental import pallas as pl
from jax.experimental.pallas import tpu as pltpu

F32 = jnp.float32
BF16 = jnp.bfloat16
U32 = jnp.uint32

D = 1024
HD = 64
NH = 8
NKV = 2
GQ = NH // NKV
QW = NH * HD
KW = NKV * HD
UW = 512
EPS = 1e-6
ROPE_THETA = 10000.0
GRID_W = 64
CONV_K = 31
WINDOW = 128
POOL_SIZES = (2, 4, 8, 16)
N_GROUPS = 4
PER_GROUP = 8
N_EXP = N_GROUPS * PER_GROUP
D_EXP = D // 2

TM = 256
PB = 4
TQ = 256
TQD = 256
TKL = 512
SUB = 64
HALO = 16
BLK = 1024
GRP = 8
NG = 96
LR = NG * GRP
VMEM_LIMIT = 56 * 1024 * 1024


def _sigmoid(x):
    return 1.0 / (1.0 + jnp.exp(-x))


def _dot(a, b):
    return jnp.dot(a, b, preferred_element_type=F32)


def _dot_nt(a, b):
    return lax.dot_general(a, b, (((1,), (1,)), ((), ())), preferred_element_type=F32)


def _split(a):
    hi = a.astype(BF16)
    lo = (a - hi.astype(F32)).astype(BF16)
    return hi, lo


def _dot3(a, w):
    a_hi, a_lo = _split(a)
    w_hi, w_lo = _split(w)
    return _dot(a_hi, w_hi) + _dot(a_lo, w_hi) + _dot(a_hi, w_lo)


def _rmsnorm(x, g):
    return x * lax.rsqrt(jnp.mean(x * x, axis=-1, keepdims=True) + EPS) * g


def _params(sem, vmem=VMEM_LIMIT):
    return pltpu.CompilerParams(dimension_semantics=sem, vmem_limit_bytes=vmem)


def _mod_kernel(c_ref, w_ref, b_ref, o_ref):
    a = c_ref[...]
    a = a * _sigmoid(a)
    o_ref[0] = _dot3(a, w_ref[0]) + b_ref[0]


def _modulation(cc, mod_w, mod_b):
    depth = mod_w.shape[0]
    tn = 1024
    return pl.pallas_call(
        _mod_kernel,
        grid=(depth, 6 * D // tn),
        in_specs=[
            pl.BlockSpec((16, D), lambda l, j: (0, 0)),
            pl.BlockSpec((1, D, tn), lambda l, j: (l, 0, j)),
            pl.BlockSpec((1, 1, tn), lambda l, j: (l, 0, j)),
        ],
        out_specs=pl.BlockSpec((1, 16, tn), lambda l, j: (l, 0, j)),
        out_shape=jax.ShapeDtypeStruct((depth, 16, 6 * D), F32),
        compiler_params=_params(("arbitrary", "arbitrary")),
        name="modulation",
    )(cc, mod_w, mod_b.reshape(depth, 1, 6 * D))


def _swap_halves(t):
    w = t.shape[1]
    lane = lax.broadcasted_iota(jnp.int32, t.shape, 1)
    first = (lane & (HD - 1)) < (HD // 2)
    return jnp.where(first, pltpu.roll(t, w - HD // 2, 1), pltpu.roll(t, HD // 2, 1))


def _head_norm_rope(t, bd, g, cos, sin, scale):
    hi, lo = _split(t * t)
    ssq = _dot(hi, bd) + _dot(lo, bd)
    tn = t * lax.rsqrt(ssq * (1.0 / HD) + EPS) * g
    n = t.shape[1] // 128
    if n > 1:
        cos = jnp.concatenate([cos] * n, axis=1)
        sin = jnp.concatenate([sin] * n, axis=1)
    out = tn * cos + _swap_halves(tn) * sin
    return out * scale if scale != 1.0 else out


def _tile_rows(c_ref, x_ref, bb):
    return jnp.where(pl.program_id(1) == 0, c_ref[bb], x_ref[bb])


def _modulated(x, mod, lng, shift_at, scale_at):
    return _rmsnorm(x, lng) * (1.0 + mod[:, scale_at * D:(scale_at + 1) * D]) + mod[:, shift_at * D:(shift_at + 1) * D]


def _proj_kernel(c_ref, x_ref, mod_ref, lng_ref, w_ref, cos_ref, sin_ref, qg_ref, kg_ref, bdq_ref, bdk_ref,
                 q_ref, k_ref, v_ref, u_ref, *, glu):
    h = jnp.concatenate([_modulated(_tile_rows(c_ref, x_ref, bb), mod_ref[bb, 0], lng_ref[...], 0, 1)
                         for bb in range(PB)], axis=0)
    px = _dot(h.astype(BF16), w_ref[...])
    cos = jnp.concatenate([cos_ref[...]] * PB, axis=0)
    sin = jnp.concatenate([sin_ref[...]] * PB, axis=0)
    q = _head_norm_rope(px[:, 0:QW], bdq_ref[...], qg_ref[...], cos, sin, HD ** -0.5)
    k = _head_norm_rope(px[:, QW:QW + KW], bdk_ref[...], kg_ref[...], cos, sin, 1.0)
    v = px[:, QW + KW:QW + 2 * KW]
    lane = lax.broadcasted_iota(jnp.int32, (TM, KW), 1)
    ones_col = jnp.where(lane == HD, 1.0, 0.0)
    o_u = QW + 2 * KW
    if glu:
        u = px[:, o_u:o_u + UW] * _sigmoid(px[:, o_u + UW:o_u + 2 * UW])
    else:
        u = px[:, o_u:o_u + UW]
    for bb in range(PB):
        rows = slice(bb * TM, (bb + 1) * TM)
        for hh in range(NH):
            q_ref[bb, hh] = q[rows, hh * HD:(hh + 1) * HD].astype(BF16)
        for j in range(NKV):
            k_ref[bb, j] = k[rows, j * HD:(j + 1) * HD].astype(BF16)
            vj = v[rows] if j == 0 else pltpu.roll(v[rows], KW - j * HD, 1)
            v_ref[bb, j] = jnp.where(lane < HD, vj, ones_col).astype(BF16)
        u_ref[bb] = u[rows]


def _token_specs(lat_first):
    return [pl.BlockSpec((PB, TM, D), lambda bi, i: (bi, 0, 0)),
            pl.BlockSpec((PB, TM, D), lambda bi, i: (bi, jnp.maximum(i - 1, 0) + lat_first, 0))]


def _project(ctx_rows, lat_rows, lat_first, s_tot, modsel, ln_g, w_in, cos, sin, q_g, k_g, bdq, bdk, *, glu):
    b = ctx_rows.shape[0]
    n_in = w_in.shape[1]
    nt = s_tot // TM
    const = lambda bi, i: (0, 0)
    return pl.pallas_call(
        functools.partial(_proj_kernel, glu=glu),
        grid=(b // PB, nt),
        in_specs=_token_specs(lat_first) + [
            pl.BlockSpec((PB, 1, 1, 6 * D), lambda bi, i: (bi, jnp.minimum(i, 1), 0, 0)),
            pl.BlockSpec((1, D), const),
            pl.BlockSpec((D, n_in), const),
            pl.BlockSpec((TM, 128), lambda bi, i: (i, 0)),
            pl.BlockSpec((TM, 128), lambda bi, i: (i, 0)),
            pl.BlockSpec((1, QW), const),
            pl.BlockSpec((1, KW), const),
            pl.BlockSpec((QW, QW), const),
            pl.BlockSpec((KW, KW), const),
        ],
        out_specs=[
            pl.BlockSpec((PB, NH, TM, HD), lambda bi, i: (bi, 0, i, 0)),
            pl.BlockSpec((PB, NKV, TM, HD), lambda bi, i: (bi, 0, i, 0)),
            pl.BlockSpec((PB, NKV, TM, 2 * HD), lambda bi, i: (bi, 0, i, 0)),
            pl.BlockSpec((PB, TM, UW), lambda bi, i: (bi, i, 0)),
        ],
        out_shape=[
            jax.ShapeDtypeStruct((b, NH, s_tot, HD), BF16),
            jax.ShapeDtypeStruct((b, NKV, s_tot, HD), BF16),
            jax.ShapeDtypeStruct((b, NKV, s_tot, 2 * HD), BF16),
            jax.ShapeDtypeStruct((b, s_tot, UW), F32),
        ],
        compiler_params=_params(("arbitrary", "arbitrary")),
        name="in_projection",
    )(ctx_rows, lat_rows, modsel, ln_g, w_in, cos, sin, q_g, k_g, bdq, bdk)


def _merge_heads(o):
    tq = o.shape[0] // GQ
    return jnp.concatenate([o[g * tq:(g + 1) * tq] for g in range(GQ)], axis=1)


def _lane_max(s):
    return functools.reduce(jnp.maximum, [s[:, j * 128:(j + 1) * 128] for j in range(s.shape[1] // 128)])


def _dense_attn_kernel(q_ref, k_ref, v_ref, o_ref, sc_sc, sl_sc, m_sc, *, n_ctx, s_tot):
    qi = pl.program_id(2)
    q = q_ref[0].reshape(GQ * TQD, HD)
    n_lat = (s_tot - n_ctx) // TKL

    def chunks(n):
        spans = [(0, n_ctx)] + [(n_ctx + c * TKL, n_ctx + (c + 1) * TKL) for c in range(n)]
        return list(zip(spans, [sc_sc] + [sl_sc.at[c] for c in range(n)]))

    def scores(n):
        for i, ((lo, hi), slot) in enumerate(chunks(n)):
            s = _dot_nt(q, k_ref[0, 0, lo:hi, :])
            slot[...] = s
            m_sc[i] = _lane_max(s)
        m_lane = functools.reduce(jnp.maximum, [m_sc[i] for i in range(n + 1)])
        m_sc[0] = jnp.broadcast_to(jnp.max(m_lane, axis=1, keepdims=True), m_lane.shape)

    def weighted(n):
        acc = None
        for (lo, hi), slot in chunks(n):
            m = jnp.concatenate([m_sc[0]] * ((hi - lo) // 128), axis=1)
            part = _dot(jnp.exp(slot[...] - m).astype(BF16), v_ref[0, 0, lo:hi, :])
            acc = part if acc is None else acc + part
        o_ref[0] = _merge_heads(acc[:, 0:HD] / acc[:, HD:HD + 1]).astype(BF16)

    is_lat = qi * TQD >= n_ctx
    pl.when(is_lat)(lambda: scores(n_lat))
    pl.when(jnp.logical_not(is_lat))(lambda: scores(0))
    pl.when(is_lat)(lambda: weighted(n_lat))
    pl.when(jnp.logical_not(is_lat))(lambda: weighted(0))


def _dense_attention(q, k, v, n_ctx):
    b, _, s_tot, _ = q.shape
    rows = GQ * TQD
    return pl.pallas_call(
        functools.partial(_dense_attn_kernel, n_ctx=n_ctx, s_tot=s_tot),
        grid=(b, NKV, s_tot // TQD),
        in_specs=[
            pl.BlockSpec((1, GQ, TQD, HD), lambda bi, j, i: (bi, j, i, 0)),
            pl.BlockSpec((1, 1, s_tot, HD), lambda bi, j, i: (bi, j, 0, 0)),
            pl.BlockSpec((1, 1, s_tot, 2 * HD), lambda bi, j, i: (bi, j, 0, 0)),
        ],
        out_specs=pl.BlockSpec((1, TQD, GQ * HD), lambda bi, j, i: (bi, i, j)),
        out_shape=jax.ShapeDtypeStruct((b, s_tot, QW), BF16),
        scratch_shapes=[
            pltpu.VMEM((rows, n_ctx), F32),
            pltpu.VMEM(((s_tot - n_ctx) // TKL, rows, TKL), F32),
            pltpu.VMEM(((s_tot - n_ctx) // TKL + 1, rows, 128), F32),
        ],
        compiler_params=_params(("arbitrary", "arbitrary", "arbitrary")),
        name="dense_attention",
    )(q, k, v)


def _window_attn_kernel(sink_ref, q_ref, k_ref, v_ref, o_ref, sc_sc, sw_sc, m_sc, e_sc, band_sc, *, n_ctx, s_tot):
    qi = pl.program_id(1)
    span = TQ + 2 * WINDOW
    q0 = n_ctx + qi * TQ
    start = pl.multiple_of(jnp.clip(q0 - WINDOW, n_ctx, s_tot - span), 128)
    row = lax.broadcasted_iota(jnp.int32, (TQ, span), 0)
    col = lax.broadcasted_iota(jnp.int32, (TQ, span), 1)
    band_sc[...] = jnp.where(jnp.abs((q0 - start) + row - col) <= WINDOW, 0.0, -jnp.inf)

    def scores(j):
        q = q_ref[0, j * GQ:(j + 1) * GQ].reshape(GQ * TQ, HD)
        s_c = _dot_nt(q, k_ref[0, j, 0:n_ctx, :])
        s_w = _dot_nt(q, k_ref[0, j, pl.ds(start, span), :]) + jnp.concatenate([band_sc[...]] * GQ, axis=0)
        sc_sc[j] = s_c
        sw_sc[j] = s_w
        sink = jnp.concatenate([jnp.full((TQ, 128), sink_ref[j * GQ + g], F32) for g in range(GQ)], axis=0)
        m_lane = jnp.maximum(_lane_max(s_c), _lane_max(s_w))
        m_sc[j] = jnp.maximum(jnp.broadcast_to(jnp.max(m_lane, axis=1, keepdims=True), m_lane.shape), sink)
        e_sc[j] = jnp.exp(sink - m_sc[j])

    def weighted(j):
        m = m_sc[j]
        p_c = jnp.exp(sc_sc[j] - jnp.concatenate([m] * (n_ctx // 128), axis=1)).astype(BF16)
        p_w = jnp.exp(sw_sc[j] - jnp.concatenate([m] * (span // 128), axis=1)).astype(BF16)
        acc = _dot(p_c, v_ref[0, j, 0:n_ctx, :]) + _dot(p_w, v_ref[0, j, pl.ds(start, span), :])
        l = acc[:, HD:HD + 1] + e_sc[j][:, 0:1]
        o_ref[0, :, j * GQ * HD:(j + 1) * GQ * HD] = _merge_heads(acc[:, 0:HD] / l).astype(BF16)

    @pl.when(qi >= 0)
    def _():
        for j in range(NKV):
            scores(j)

    @pl.when(qi >= -1)
    def _():
        for j in range(NKV):
            weighted(j)


def _window_attention(sink, q, k, v, n_ctx):
    b, _, s_tot, _ = q.shape
    s_lat = s_tot - n_ctx
    off = n_ctx // TQ
    return pl.pallas_call(
        functools.partial(_window_attn_kernel, n_ctx=n_ctx, s_tot=s_tot),
        grid_spec=pltpu.PrefetchScalarGridSpec(
            num_scalar_prefetch=1,
            grid=(b, s_lat // TQ),
            in_specs=[
                pl.BlockSpec((1, NH, TQ, HD), lambda bi, i, sk: (bi, 0, i + off, 0)),
                pl.BlockSpec((1, NKV, s_tot, HD), lambda bi, i, sk: (bi, 0, 0, 0)),
                pl.BlockSpec((1, NKV, s_tot, 2 * HD), lambda bi, i, sk: (bi, 0, 0, 0)),
            ],
            out_specs=pl.BlockSpec((1, TQ, QW), lambda bi, i, sk: (bi, i, 0)),
            scratch_shapes=[
                pltpu.VMEM((NKV, GQ * TQ, n_ctx), F32),
                pltpu.VMEM((NKV, GQ * TQ, TQ + 2 * WINDOW), F32),
                pltpu.VMEM((NKV, GQ * TQ, 128), F32),
                pltpu.VMEM((NKV, GQ * TQ, 128), F32),
                pltpu.VMEM((TQ, TQ + 2 * WINDOW), F32),
            ],
        ),
        out_shape=jax.ShapeDtypeStruct((b, s_lat, QW), BF16),
        compiler_params=_params(("arbitrary", "arbitrary")),
        name="window_attention",
    )(sink, q, k, v)


def _conv_kernel(g_ref, cw_ref, cb_ref, lg_ref, lb_ref, o_ref, pad_sc, win_sc, *, n_ctx, s_tot):
    zeros = jnp.zeros((HALO, UW), F32)
    pad_sc[0:HALO, :] = zeros
    pad_sc[HALO + n_ctx:2 * HALO + n_ctx, :] = zeros
    pad_sc[2 * HALO + s_tot:3 * HALO + s_tot, :] = zeros

    def fill(i, carry):
        src = pl.multiple_of(i * TM, TM)
        dst = pl.multiple_of(src + HALO + jnp.where(src >= n_ctx, HALO, 0), 8)
        pad_sc[pl.ds(dst, TM), :] = g_ref[0, pl.ds(src, TM), :]
        return carry

    lax.fori_loop(0, s_tot // TM, fill, 0)
    half = CONV_K // 2

    def tile(i, carry):
        src = pl.multiple_of(i * SUB, SUB)
        base = pl.multiple_of(src + jnp.where(src >= n_ctx, HALO, 0), 8)
        win = pad_sc[pl.ds(base, SUB + 2 * HALO), :]
        keep = SUB + 2 * HALO - 8
        for r in range(8):
            win_sc[r, 0:keep, :] = win[r:r + keep, :]
        acc = jnp.zeros((SUB, UW), F32) + cb_ref[...]
        for t in range(CONV_K):
            a, r = divmod(HALO - half + t, 8)
            acc = acc + win_sc[r, 8 * a:8 * a + SUB, :] * cw_ref[t:t + 1, :]
        mu = jnp.mean(acc, axis=-1, keepdims=True)
        xc = acc - mu
        var = jnp.mean(xc * xc, axis=-1, keepdims=True)
        yn = xc * lax.rsqrt(var + EPS) * lg_ref[...] + lb_ref[...]
        o_ref[0, pl.ds(src, SUB), :] = (yn * _sigmoid(yn)).astype(BF16)
        return carry

    lax.fori_loop(0, s_tot // SUB, tile, 0)


def _conv_module(glu, conv_w, conv_b, ln_g, ln_b, n_ctx):
    b, s_tot, _ = glu.shape
    const = lambda bi: (0, 0)
    return pl.pallas_call(
        functools.partial(_conv_kernel, n_ctx=n_ctx, s_tot=s_tot),
        grid=(b,),
        in_specs=[
            pl.BlockSpec((1, s_tot, UW), lambda bi: (bi, 0, 0)),
            pl.BlockSpec((CONV_K, UW), const),
            pl.BlockSpec((1, UW), const),
            pl.BlockSpec((1, UW), const),
            pl.BlockSpec((1, UW), const),
        ],
        out_specs=pl.BlockSpec((1, s_tot, UW), lambda bi: (bi, 0, 0)),
        out_shape=jax.ShapeDtypeStruct((b, s_tot, UW), BF16),
        scratch_shapes=[pltpu.VMEM((s_tot + 3 * HALO, UW), F32), pltpu.VMEM((8, SUB + 2 * HALO, UW), F32)],
        compiler_params=_params(("arbitrary",)),
        name="conv_module",
    )(glu, conv_w, conv_b, ln_g, ln_b)


def _pool_kernel(u_ref, pw_ref, ps_ref, o_ref, pad_sc, *, n_ctx, s_lat):
    zeros = jnp.zeros((HALO, UW), F32)
    pad_sc[0:HALO, :] = zeros
    pad_sc[HALO + s_lat:2 * HALO + s_lat, :] = zeros

    def fill(i, carry):
        src = pl.multiple_of(i * TM, TM)
        pad_sc[pl.ds(pl.multiple_of(src + HALO, 8), TM), :] = u_ref[0, pl.ds(pl.multiple_of(src + n_ctx, 8), TM), :]
        return carry

    lax.fori_loop(0, s_lat // TM, fill, 0)
    gw = UW // len(POOL_SIZES)

    def tile(i, carry):
        src = pl.multiple_of(i * SUB, SUB)
        win = pad_sc[pl.ds(src, SUB + 2 * HALO), :]
        t = src + lax.broadcasted_iota(jnp.int32, (SUB, 1), 0)
        outs = []
        for gi, w in enumerate(POOL_SIZES):
            lanes = slice(gi * gw, (gi + 1) * gw)
            tot = jnp.zeros((SUB, gw), F32)
            for d in range(-(w // 2), w - w // 2):
                tot = tot + win[HALO + d:HALO + d + SUB, lanes]
            lo = jnp.clip(t - w // 2, 0, s_lat)
            hi = jnp.clip(t - w // 2 + w, 0, s_lat)
            p = tot / (hi - lo).astype(F32) - win[HALO:HALO + SUB, lanes]
            outs.append(_dot(p.astype(BF16), pw_ref[gi]))
        y = jnp.concatenate(outs, axis=1) * ps_ref[...]
        o_ref[0, pl.ds(src, SUB), :] = y.astype(BF16)
        return carry

    lax.fori_loop(0, s_lat // SUB, tile, 0)


def _pool_mixer(u, pool_w, pool_scale, n_ctx):
    b, s_tot, _ = u.shape
    s_lat = s_tot - n_ctx
    gw = UW // len(POOL_SIZES)
    return pl.pallas_call(
        functools.partial(_pool_kernel, n_ctx=n_ctx, s_lat=s_lat),
        grid=(b,),
        in_specs=[
            pl.BlockSpec((1, s_tot, UW), lambda bi: (bi, 0, 0)),
            pl.BlockSpec((len(POOL_SIZES), gw, gw), lambda bi: (0, 0, 0)),
            pl.BlockSpec((1, UW), lambda bi: (0, 0)),
        ],
        out_specs=pl.BlockSpec((1, s_lat, UW), lambda bi: (bi, 0, 0)),
        out_shape=jax.ShapeDtypeStruct((b, s_lat, UW), BF16),
        scratch_shapes=[pltpu.VMEM((s_lat + 2 * HALO, UW), F32)],
        compiler_params=_params(("arbitrary",)),
        name="pool_mixer",
    )(u, pool_w, pool_scale)


def _route(logits):
    shape = logits.shape
    lane = lax.broadcasted_iota(jnp.int32, shape, 1)
    lanef = lane.astype(F32)
    is_g = (lane >= N_EXP) & (lane < N_EXP + N_GROUPS)
    big = 1e9
    lg = jnp.where(is_g, logits, -jnp.inf)
    gmax = jnp.max(lg, axis=1, keepdims=True)
    gsel = jnp.min(jnp.where(is_g & (lg == gmax), lanef, big), axis=1, keepdims=True) - float(N_EXP)
    gsum = jnp.sum(jnp.exp(lg - gmax), axis=1, keepdims=True)
    g_w = 1.0 / gsum
    in_grp = (lane < N_EXP) & ((lane >> 3).astype(F32) == gsel)
    el = jnp.where(in_grp, logits, -jnp.inf)
    emax = jnp.max(el, axis=1, keepdims=True)
    ee = jnp.exp(el - emax)
    p = ee / jnp.sum(ee, axis=1, keepdims=True)
    p1 = jnp.max(jnp.where(in_grp, p, -1.0), axis=1, keepdims=True)
    i1 = jnp.min(jnp.where(in_grp & (p == p1), lanef, big), axis=1, keepdims=True)
    rest = in_grp & (lanef != i1)
    p2 = jnp.max(jnp.where(rest, p, -1.0), axis=1, keepdims=True)
    i2 = jnp.min(jnp.where(rest & (p == p2), lanef, big), axis=1, keepdims=True)
    w1 = g_w * p1 / (p1 + p2)
    w2 = g_w * p2 / (p1 + p2)
    onehot = jnp.where((lanef == i1) | (lanef == i2), 1.0, 0.0)
    return lanef, i1, i2, w1, w2, onehot


def _outproj_kernel(a_ref, b_ref, *refs, with_ctx):
    mod_ref, w_ref, lng_ref, rw_ref, rb_ref, tri_ref, upper_ref, x1_ref, h2_ref, r_ref, g8_ref = refs[-11:]
    a = jnp.concatenate([a_ref[bb] for bb in range(PB)], axis=0)
    bmix = jnp.concatenate([b_ref[bb] for bb in range(PB)], axis=0)
    y = _dot(a, w_ref[0:QW, :]) + _dot(bmix, w_ref[QW:QW + UW, :])
    h2_parts = []
    for bb in range(PB):
        x_in = _tile_rows(refs[0], refs[1], bb) if with_ctx else refs[0][bb]
        mod = mod_ref[bb, 0]
        x1 = x_in + mod[:, 2 * D:3 * D] * y[bb * TM:(bb + 1) * TM]
        x1_ref[bb] = x1
        h2_parts.append(_modulated(x1, mod, lng_ref[...], 3, 4))
        h2_ref[bb] = h2_parts[-1].astype(BF16)
    logits = _dot3(jnp.concatenate(h2_parts, axis=0), rw_ref[...]) + rb_ref[...]
    lanef, i1, i2, w1, w2, onehot = _route(logits)
    for bb in range(PB):
        rows = slice(bb * TM, (bb + 1) * TM)
        hot = onehot[rows]
        before = _dot(tri_ref[...], hot.astype(BF16))
        cnt = jnp.sum(hot, axis=0, keepdims=True)
        g8 = jnp.floor((cnt + float(GRP - 1)) * (1.0 / GRP))
        goff = _dot(jnp.broadcast_to(g8, (8, 128)).astype(BF16), upper_ref[...])[0:1, :]
        local = before + float(GRP) * goff
        lane = lax.broadcasted_iota(jnp.int32, (TM, 128), 1).astype(F32)
        e1, e2 = i1[rows], i2[rows]
        l1 = jnp.sum(jnp.where(lane == e1, local, 0.0), axis=1, keepdims=True)
        l2 = jnp.sum(jnp.where(lane == e2, local, 0.0), axis=1, keepdims=True)
        r_ref[bb] = jnp.where(lane == 0.0, e1, jnp.where(lane == 1.0, e2, jnp.where(lane == 2.0, w1[rows], jnp.where(
            lane == 3.0, w2[rows], jnp.where(lane == 4.0, l1, jnp.where(lane == 5.0, l2, 0.0))))))
        g8_ref[bb, 0] = g8


def _out_project(a, bmix, residual, modsel, w_out, ln_g, rw, rb, tri, upper):
    b, s_out, _ = a.shape
    nt = s_out // TM
    const = lambda bi, i: (0, 0)
    with_ctx = len(residual) == 2
    sel = (lambda i: jnp.minimum(i, 1)) if with_ctx else (lambda i: 1)
    tok = lambda bi, i: (bi, i, 0)
    first_lat = residual[0].shape[1] // TM - nt
    res_specs = _token_specs(0) if with_ctx else [pl.BlockSpec((PB, TM, D), lambda bi, i: (bi, i + first_lat, 0))]
    return pl.pallas_call(
        functools.partial(_outproj_kernel, with_ctx=with_ctx),
        grid=(b // PB, nt),
        in_specs=[
            pl.BlockSpec((PB, TM, QW), tok),
            pl.BlockSpec((PB, TM, UW), tok),
        ] + res_specs + [
            pl.BlockSpec((PB, 1, 1, 6 * D), lambda bi, i: (bi, sel(i), 0, 0)),
            pl.BlockSpec((D, D), const),
            pl.BlockSpec((1, D), const),
            pl.BlockSpec((D, 128), const),
            pl.BlockSpec((1, 128), const),
            pl.BlockSpec((TM, TM), const),
            pl.BlockSpec((128, 128), const),
        ],
        out_specs=[
            pl.BlockSpec((PB, TM, D), tok),
            pl.BlockSpec((PB, TM, D), tok),
            pl.BlockSpec((PB, TM, 128), tok),
            pl.BlockSpec((PB, 1, 1, 128), lambda bi, i: (bi, i, 0, 0)),
        ],
        out_shape=[
            jax.ShapeDtypeStruct((b, s_out, D), F32),
            jax.ShapeDtypeStruct((b, s_out, D), BF16),
            jax.ShapeDtypeStruct((b, s_out, 128), F32),
            jax.ShapeDtypeStruct((b, nt, 1, 128), F32),
        ],
        compiler_params=_params(("arbitrary", "arbitrary")),
        name="out_projection_router",
    )(a, bmix, *residual, modsel, w_out, ln_g, rw, rb, tri, upper)


def _pack_halves(x, exact=False):
    half = x.shape[1] // 2
    a, b = x[:, 0:half], x[:, half:]
    if not exact:
        a, b = a.astype(BF16).astype(F32), b.astype(BF16).astype(F32)
    return lax.bitcast_convert_type(a, U32) | (lax.bitcast_convert_type(b, U32) >> 16)


def _unpack_halves(w):
    a = lax.bitcast_convert_type(w & jnp.uint32(0xFFFF0000), F32)
    b = lax.bitcast_convert_type(w << 16, F32)
    return jnp.concatenate([a, b], axis=1).astype(BF16)


def _rows_copy(src_ref, src_row, dst_ref, dst_row, rows, sem):
    rows = pl.multiple_of(rows, GRP)
    return pltpu.make_async_copy(src_ref.at[pl.ds(pl.multiple_of(src_row, GRP), rows), :],
                                 dst_ref.at[pl.ds(pl.multiple_of(dst_row, GRP), rows), :], sem)


def _wait_rows(count, src_ref, dst_ref, sem):
    @pl.when(count > 0)
    def _():
        _rows_copy(src_ref, 0, dst_ref, 0, count * GRP, sem).wait()


def _start_runs(runs, tile, copy):
    dst_ref, off_ref, len_ref = runs
    for e in range(N_EXP):
        idx = tile * N_EXP + e
        rows = len_ref[idx]

        @pl.when(rows > 0)
        def _():
            copy(dst_ref[idx], off_ref[idx], rows).start(priority=e % 2)


def _dispatch_kernel(dst_ref, off_ref, len_ref, ng_ref, zdst_ref, zlen_ref, h_ref, r_ref, xs_ref, loc, zbuf, sems,
                     zsem):
    runs = (dst_ref, off_ref, len_ref)
    i = pl.program_id(0)
    n = pl.num_programs(0)

    @pl.when(i == 0)
    def _():
        zbuf[...] = jnp.zeros(zbuf.shape, U32)
        for e in range(N_EXP):
            pl.when(zlen_ref[e] > 0)(lambda e=e: _rows_copy(zbuf, 0, xs_ref, zdst_ref[e], zlen_ref[e], zsem).start())
        for e in range(N_EXP):
            pl.when(zlen_ref[e] > 0)(lambda e=e: _rows_copy(zbuf, 0, xs_ref, zdst_ref[e], zlen_ref[e], zsem).wait())

    def run(slot):
        buf = loc.at[slot]
        sem = sems.at[slot]

        @pl.when(i >= 2)
        def _():
            _wait_rows(ng_ref[i - 2], buf, xs_ref, sem)

        rt = r_ref[...].T
        rows = lax.broadcasted_iota(jnp.int32, (LR, TM), 0).astype(F32)
        perm = jnp.where((rows == rt[4:5, :]) | (rows == rt[5:6, :]), 1.0, 0.0).astype(BF16)
        buf[...] = _pack_halves(_dot(perm, h_ref[...]), exact=True)

        _start_runs(runs, i, lambda dst, off, rows: _rows_copy(buf, off, xs_ref, dst, rows, sem))

        @pl.when(i == n - 1)
        def _():
            _wait_rows(ng_ref[i], buf, xs_ref, sem)

            @pl.when(i >= 1)
            def _():
                _wait_rows(ng_ref[i - 1], loc.at[1 - slot], xs_ref, sems.at[1 - slot])

    pl.when(i % 2 == 0)(lambda: run(0))
    pl.when(i % 2 == 1)(lambda: run(1))


def _dispatch(runs, ng, zdst, zlen, h2, rinfo, cap):
    t = h2.shape[0]
    tok = lambda i, *_: (i, 0)
    return pl.pallas_call(
        _dispatch_kernel,
        grid_spec=pltpu.PrefetchScalarGridSpec(
            num_scalar_prefetch=6,
            grid=(t // TM,),
            in_specs=[
                pl.BlockSpec((TM, D), tok),
                pl.BlockSpec((TM, 128), tok),
            ],
            out_specs=pl.BlockSpec(memory_space=pl.ANY),
            scratch_shapes=[
                pltpu.VMEM((2, LR, D // 2), U32),
                pltpu.VMEM((BLK, D // 2), U32),
                pltpu.SemaphoreType.DMA((2,)),
                pltpu.SemaphoreType.DMA,
            ],
        ),
        out_shape=jax.ShapeDtypeStruct((cap, D // 2), U32),
        compiler_params=_params(("arbitrary",)),
        name="moe_dispatch",
    )(*runs, ng, zdst, zlen, h2, rinfo)


def _expert_kernel(be_ref, nu_ref, x_ref, wg_ref, wu_ref, wd_ref, y_ref, wg_sc, wu_sc, wd_sc):
    i = pl.program_id(0)
    used = i < nu_ref[0]
    fresh = (i == 0) | (be_ref[i] != be_ref[jnp.maximum(i - 1, 0)])

    @pl.when(used & fresh)
    def _():
        wg_sc[...] = wg_ref[0, 0].astype(BF16)
        wu_sc[...] = wu_ref[0, 0].astype(BF16)
        wd_sc[...] = wd_ref[0, 0].astype(BF16)

    @pl.when(used)
    def _():
        xb = _unpack_halves(x_ref[...])
        g = _dot(xb, wg_sc[...])
        u = _dot(xb, wu_sc[...])
        hid = (g * _sigmoid(g)) * u
        y_ref[...] = _pack_halves(_dot(hid.astype(BF16), wd_sc[...]))

    @pl.when(jnp.logical_not(used))
    def _():
        y_ref[...] = jnp.zeros(y_ref.shape, U32)


def _experts(blk_e, n_used, xs, wg, wu, wd, layer):
    cap = xs.shape[0]
    row = lambda i, be, nu: (jnp.minimum(i, nu[0] - 1), 0)
    wsel = lambda i, be, nu: (layer, be[i], 0, 0)
    return pl.pallas_call(
        _expert_kernel,
        grid_spec=pltpu.PrefetchScalarGridSpec(
            num_scalar_prefetch=2,
            grid=(cap // BLK,),
            in_specs=[
                pl.BlockSpec((BLK, D // 2), row),
                pl.BlockSpec((1, 1, D, D_EXP), wsel),
                pl.BlockSpec((1, 1, D, D_EXP), wsel),
                pl.BlockSpec((1, 1, D_EXP, D), wsel),
            ],
            out_specs=pl.BlockSpec((BLK, D // 2), lambda i, be, nu: (i, 0)),
            scratch_shapes=[
                pltpu.VMEM((D, D_EXP), BF16),
                pltpu.VMEM((D, D_EXP), BF16),
                pltpu.VMEM((D_EXP, D), BF16),
            ],
        ),
        out_shape=jax.ShapeDtypeStruct((cap, D // 2), U32),
        compiler_params=_params(("arbitrary",)),
        name="moe_experts",
    )(blk_e, n_used, xs, wg, wu, wd)


def _combine_kernel(dst_ref, off_ref, len_ref, ng_ref, y_ref, x1_ref, r_ref, mod_ref, o_ref, loc, sems):
    runs = (dst_ref, off_ref, len_ref)
    i = pl.program_id(0)
    n = pl.num_programs(0)

    def fetch(tile, slot):
        _start_runs(runs, tile, lambda dst, off, rows: _rows_copy(y_ref, dst, loc.at[slot], off, rows, sems.at[slot]))

    @pl.when(i == 0)
    def _():
        loc[...] = jnp.zeros(loc.shape, U32)
        fetch(0, 0)

    def run(slot):
        @pl.when(i + 1 < n)
        def _():
            fetch(i + 1, 1 - slot)

        _wait_rows(ng_ref[i], y_ref, loc.at[slot], sems.at[slot])
        ysb = _unpack_halves(loc[slot])
        info = r_ref[...]
        col = lax.broadcasted_iota(jnp.int32, (TM, LR), 1).astype(F32)
        pick = jnp.concatenate([jnp.where(col == info[:, 4:5], 1.0, 0.0).astype(BF16),
                                jnp.where(col == info[:, 5:6], 1.0, 0.0).astype(BF16)], axis=0)
        y12 = _dot(pick, ysb)
        g2 = mod_ref[0, 0][:, 5 * D:6 * D]
        o_ref[...] = x1_ref[...] + g2 * (info[:, 2:3] * y12[0:TM] + info[:, 3:4] * y12[TM:2 * TM])

    pl.when(i % 2 == 0)(lambda: run(0))
    pl.when(i % 2 == 1)(lambda: run(1))


def _combine(runs, ng, ys, x1, rinfo, modsel, *, tiles_per_batch, with_ctx):
    t = x1.shape[0]
    if with_ctx:
        msel = lambda i, *_: (i // tiles_per_batch, jnp.minimum(i % tiles_per_batch, 1), 0, 0)
    else:
        msel = lambda i, *_: (i // tiles_per_batch, 1, 0, 0)
    tok = lambda i, *_: (i, 0)
    return pl.pallas_call(
        _combine_kernel,
        grid_spec=pltpu.PrefetchScalarGridSpec(
            num_scalar_prefetch=4,
            grid=(t // TM,),
            in_specs=[
                pl.BlockSpec(memory_space=pl.ANY),
                pl.BlockSpec((TM, D), tok),
                pl.BlockSpec((TM, 128), tok),
                pl.BlockSpec((1, 1, 1, 6 * D), msel),
            ],
            out_specs=pl.BlockSpec((TM, D), tok),
            scratch_shapes=[pltpu.VMEM((2, LR, D // 2), U32), pltpu.SemaphoreType.DMA((2,))],
        ),
        out_shape=jax.ShapeDtypeStruct((t, D), F32),
        compiler_params=_params(("arbitrary",)),
        name="moe_combine",
    )(*runs, ng, ys, x1, rinfo, modsel)


def _moe(x1, h2, rinfo, g8, modsel, wg, wu, wd, layer, *, with_ctx):
    b, s, _ = x1.shape
    t = b * s
    n_tiles = t // TM
    i32 = jnp.int32
    grp = g8.reshape(n_tiles, 128)[:, 0:N_EXP].astype(i32)
    n_blk = -(-(2 * t + n_tiles * N_EXP * (GRP - 1) + N_EXP * (BLK - 1)) // BLK)
    cap = n_blk * BLK
    goff_end = jnp.cumsum(grp, axis=1)
    ng = goff_end[:, N_EXP - 1].astype(i32)
    rows_e = GRP * jnp.sum(grp, axis=0)
    padded = (rows_e + BLK - 1) // BLK * BLK
    pend = jnp.cumsum(padded)
    pstart = pend - padded
    run_dst = (pstart[None, :] + GRP * (jnp.cumsum(grp, axis=0) - grp)).reshape(-1).astype(i32)
    run_off = (GRP * (goff_end - grp)).reshape(-1).astype(i32)
    run_len = (GRP * grp).reshape(-1).astype(i32)
    runs = (run_dst, run_off, run_len)
    zdst = (pstart + rows_e).astype(i32)
    zlen = (padded - rows_e).astype(i32)
    blk_e = jnp.minimum(jnp.sum((pend[None, :] <= (jnp.arange(n_blk, dtype=i32) * BLK)[:, None]).astype(i32), axis=1),
                        N_EXP - 1).astype(i32)
    n_used = (pend[N_EXP - 1:N_EXP] // BLK).astype(i32)
    info = rinfo.reshape(t, 128)
    xs = _dispatch(runs, ng, zdst, zlen, h2.reshape(t, D), info, cap)
    ys = _experts(blk_e, n_used, xs, wg, wu, wd, layer)
    out = _combine(runs, ng, ys, x1.reshape(t, D), info, modsel, tiles_per_batch=s // TM, with_ctx=with_ctx)
    return out.reshape(b, s, D)


def _rope_tables(n_ctx, s_lat):
    rows = s_lat // GRID_W
    row = jnp.repeat(jnp.arange(rows, dtype=F32), GRID_W)
    col = jnp.tile(jnp.arange(GRID_W, dtype=F32), rows)
    n_freq = HD // 4
    inv = ROPE_THETA ** (-jnp.arange(n_freq, dtype=F32) / n_freq)
    ang = jnp.concatenate([row[:, None] * inv, col[:, None] * inv], axis=-1)
    cos = jnp.concatenate([jnp.ones((n_ctx, HD // 2), F32), jnp.cos(ang)], axis=0)
    sin = jnp.concatenate([jnp.zeros((n_ctx, HD // 2), F32), jnp.sin(ang)], axis=0)
    cos = jnp.concatenate([cos, cos, cos, cos], axis=1)
    sin = jnp.concatenate([-sin, sin, -sin, sin], axis=1)
    return cos, sin


def _block_diag_ones(n):
    i = jnp.arange(n) // HD
    return (i[:, None] == i[None, :]).astype(BF16)


def kernel(x, c, ctx, c_ctx, mod_w, mod_b, ln1_g, ln2_g, w_in_ab, w_out_ab, q_norm_a, k_norm_a, conv_w, conv_b,
           conv_ln_g, conv_ln_b, w_in_cd, w_out_cd, q_norm_c, k_norm_c, sink_c, pool_w, pool_scale,
           rt_grp_w, rt_grp_b, rt_exp_w, rt_exp_b, ex_gate, ex_up, ex_down):
    b, s_lat, _ = x.shape
    n_ctx = ctx.shape[1]
    assert n_ctx == TM and s_lat % TM == 0 and b <= 8 and b % PB == 0

    cc = jnp.zeros((16, D), F32).at[0:b].set(c).at[8].set(c_ctx)
    mod = _modulation(cc, mod_w, mod_b)

    def modsel(l):
        return jnp.stack([jnp.broadcast_to(mod[l, 8], (b, 6 * D)), mod[l, 0:b]], axis=1).reshape(b, 2, 1, 6 * D)

    cos, sin = _rope_tables(n_ctx, s_lat)
    bdq = _block_diag_ones(QW)
    bdk = _block_diag_ones(KW)
    tri = (jnp.arange(TM)[:, None] > jnp.arange(TM)[None, :]).astype(BF16)
    upper = (jnp.arange(128)[:, None] < jnp.arange(128)[None, :]).astype(BF16)

    def router(l):
        rw = jnp.zeros((D, 128), F32).at[:, 0:N_EXP].set(rt_exp_w[l]).at[:, N_EXP:N_EXP + N_GROUPS].set(rt_grp_w[l])
        rb = jnp.zeros((1, 128), F32).at[0, 0:N_EXP].set(rt_exp_b[l]).at[0, N_EXP:N_EXP + N_GROUPS].set(rt_grp_b[l])
        return rw, rb

    def tile_gain(g, n):
        return jnp.tile(g, n).reshape(1, n * HD)

    s_tot = n_ctx + s_lat

    ms = modsel(0)
    q, k, v, glu = _project(ctx, x, 0, s_tot, ms, ln1_g[0:1], w_in_ab[0].astype(BF16), cos, sin,
                            tile_gain(q_norm_a[0], NH), tile_gain(k_norm_a[0], NKV), bdq, bdk, glu=True)
    att = _dense_attention(q, k, v, n_ctx)
    cv = _conv_module(glu, conv_w[0], conv_b[0:1], conv_ln_g[0:1], conv_ln_b[0:1], n_ctx)
    rw, rb = router(0)
    x1, h2, rinfo, g8 = _out_project(att, cv, (ctx, x), ms, w_out_ab[0].astype(BF16), ln2_g[0:1], rw, rb, tri, upper)
    xc = _moe(x1, h2, rinfo, g8, ms, ex_gate, ex_up, ex_down, 0, with_ctx=True)

    ms = modsel(1)
    q, k, v, u = _project(xc, xc, n_ctx // TM, s_tot, ms, ln1_g[1:2], w_in_cd[0].astype(BF16), cos, sin,
                          tile_gain(q_norm_c[0], NH), tile_gain(k_norm_c[0], NKV), bdq, bdk, glu=False)
    att = _window_attention(sink_c[0], q, k, v, n_ctx)
    pm = _pool_mixer(u, pool_w[0].astype(BF16), pool_scale[0:1], n_ctx)
    rw, rb = router(1)
    x1, h2, rinfo, g8 = _out_project(att, pm, (xc,), ms, w_out_cd[0].astype(BF16), ln2_g[1:2], rw, rb, tri, upper)
    return _moe(x1, h2, rinfo, g8, ms, ex_gate, ex_up, ex_down, 1, with_ctx=False)
```

```python
import functools

import jax
import jax.numpy as jnp
from jax import lax
from jax.experimental import pallas as pl
from jax.experimental.pallas import tpu as pltpu

F32 = jnp.float32
BF16 = jnp.bfloat16
U32 = jnp.uint32

D = 1024
HD = 64
NH = 8
NKV = 2
GQ = NH // NKV
QW = NH * HD
KW = NKV * HD
UW = 512
EPS = 1e-6
ROPE_THETA = 10000.0
GRID_W = 64
CONV_K = 31
WINDOW = 128
POOL_SIZES = (2, 4, 8, 16)
N_GROUPS = 4
PER_GROUP = 8
N_EXP = N_GROUPS * PER_GROUP
D_EXP = D // 2

TM = 256
PB = 4
TQ = 256
TQD = 256
TKL = 512
SUB = 64
HALO = 16
BLK = 1024
GRP = 8
NG = 96
LR = NG * GRP
VMEM_LIMIT = 56 * 1024 * 1024


def _sigmoid(x):
    return 1.0 / (1.0 + jnp.exp(-x))


def _dot(a, b):
    return jnp.dot(a, b, preferred_element_type=F32)


def _dot_nt(a, b):
    return lax.dot_general(a, b, (((1,), (1,)), ((), ())), preferred_element_type=F32)


def _split(a):
    hi = a.astype(BF16)
    lo = (a - hi.astype(F32)).astype(BF16)
    return hi, lo


def _dot3(a, w):
    a_hi, a_lo = _split(a)
    w_hi, w_lo = _split(w)
    return _dot(a_hi, w_hi) + _dot(a_lo, w_hi) + _dot(a_hi, w_lo)


def _rmsnorm(x, g):
    return x * lax.rsqrt(jnp.mean(x * x, axis=-1, keepdims=True) + EPS) * g


def _params(sem, vmem=VMEM_LIMIT):
    return pltpu.CompilerParams(dimension_semantics=sem, vmem_limit_bytes=vmem)


def _mod_kernel(c_ref, w_ref, b_ref, o_ref):
    a = c_ref[...]
    a = a * _sigmoid(a)
    o_ref[0] = _dot3(a, w_ref[0]) + b_ref[0]


def _modulation(cc, mod_w, mod_b):
    depth = mod_w.shape[0]
    tn = 1024
    return pl.pallas_call(
        _mod_kernel,
        grid=(depth, 6 * D // tn),
        in_specs=[
            pl.BlockSpec((16, D), lambda l, j: (0, 0)),
            pl.BlockSpec((1, D, tn), lambda l, j: (l, 0, j)),
            pl.BlockSpec((1, 1, tn), lambda l, j: (l, 0, j)),
        ],
        out_specs=pl.BlockSpec((1, 16, tn), lambda l, j: (l, 0, j)),
        out_shape=jax.ShapeDtypeStruct((depth, 16, 6 * D), F32),
        compiler_params=_params(("arbitrary", "arbitrary")),
        name="modulation",
    )(cc, mod_w, mod_b.reshape(depth, 1, 6 * D))


def _swap_halves(t):
    w = t.shape[1]
    lane = lax.broadcasted_iota(jnp.int32, t.shape, 1)
    first = (lane & (HD - 1)) < (HD // 2)
    return jnp.where(first, pltpu.roll(t, w - HD // 2, 1), pltpu.roll(t, HD // 2, 1))


def _head_norm_rope(t, bd, g, cos, sin, scale):
    hi, lo = _split(t * t)
    ssq = _dot(hi, bd) + _dot(lo, bd)
    tn = t * lax.rsqrt(ssq * (1.0 / HD) + EPS) * g
    n = t.shape[1] // 128
    if n > 1:
        cos = jnp.concatenate([cos] * n, axis=1)
        sin = jnp.concatenate([sin] * n, axis=1)
    out = tn * cos + _swap_halves(tn) * sin
    return out * scale if scale != 1.0 else out


def _tile_rows(c_ref, x_ref, bb):
    return jnp.where(pl.program_id(1) == 0, c_ref[bb], x_ref[bb])


def _modulated(x, mod, lng, shift_at, scale_at):
    return _rmsnorm(x, lng) * (1.0 + mod[:, scale_at * D:(scale_at + 1) * D]) + mod[:, shift_at * D:(shift_at + 1) * D]


def _proj_kernel(c_ref, x_ref, mod_ref, lng_ref, w_ref, cos_ref, sin_ref, qg_ref, kg_ref, bdq_ref, bdk_ref,
                 q_ref, k_ref, v_ref, u_ref, *, glu):
    h = jnp.concatenate([_modulated(_tile_rows(c_ref, x_ref, bb), mod_ref[bb, 0], lng_ref[...], 0, 1)
                         for bb in range(PB)], axis=0)
    px = _dot(h.astype(BF16), w_ref[...])
    _proj_epilogue(px, cos_ref, sin_ref, qg_ref, kg_ref, bdq_ref, bdk_ref, q_ref, k_ref, v_ref, u_ref, glu)


def _proj_epilogue(px, cos_ref, sin_ref, qg_ref, kg_ref, bdq_ref, bdk_ref, q_ref, k_ref, v_ref, u_ref, glu):
    cos = jnp.concatenate([cos_ref[...]] * PB, axis=0)
    sin = jnp.concatenate([sin_ref[...]] * PB, axis=0)
    q = _head_norm_rope(px[:, 0:QW], bdq_ref[...], qg_ref[...], cos, sin, HD ** -0.5)
    k = _head_norm_rope(px[:, QW:QW + KW], bdk_ref[...], kg_ref[...], cos, sin, 1.0)
    v = px[:, QW + KW:QW + 2 * KW]
    lane = lax.broadcasted_iota(jnp.int32, (TM, KW), 1)
    ones_col = jnp.where(lane == HD, 1.0, 0.0)
    o_u = QW + 2 * KW
    if glu:
        u = px[:, o_u:o_u + UW] * _sigmoid(px[:, o_u + UW:o_u + 2 * UW])
    else:
        u = px[:, o_u:o_u + UW]
    for bb in range(PB):
        rows = slice(bb * TM, (bb + 1) * TM)
        for hh in range(NH):
            q_ref[bb, hh] = q[rows, hh * HD:(hh + 1) * HD].astype(BF16)
        for j in range(NKV):
            k_ref[bb, j] = k[rows, j * HD:(j + 1) * HD].astype(BF16)
            vj = v[rows] if j == 0 else pltpu.roll(v[rows], KW - j * HD, 1)
            v_ref[bb, j] = jnp.where(lane < HD, vj, ones_col).astype(BF16)
        u_ref[bb] = u[rows]


def _token_specs(lat_first):
    return [pl.BlockSpec((PB, TM, D), lambda bi, i: (bi, 0, 0)),
            pl.BlockSpec((PB, TM, D), lambda bi, i: (bi, jnp.maximum(i - 1, 0) + lat_first, 0))]


def _project(ctx_rows, lat_rows, lat_first, s_tot, modsel, ln_g, w_in, cos, sin, q_g, k_g, bdq, bdk, *, glu):
    b = ctx_rows.shape[0]
    n_in = w_in.shape[1]
    nt = s_tot // TM
    const = lambda bi, i: (0, 0)
    return pl.pallas_call(
        functools.partial(_proj_kernel, glu=glu),
        grid=(b // PB, nt),
        in_specs=_token_specs(lat_first) + [
            pl.BlockSpec((PB, 1, 1, 6 * D), lambda bi, i: (bi, jnp.minimum(i, 1), 0, 0)),
            pl.BlockSpec((1, D), const),
            pl.BlockSpec((D, n_in), const),
            pl.BlockSpec((TM, 128), lambda bi, i: (i, 0)),
            pl.BlockSpec((TM, 128), lambda bi, i: (i, 0)),
            pl.BlockSpec((1, QW), const),
            pl.BlockSpec((1, KW), const),
            pl.BlockSpec((QW, QW), const),
            pl.BlockSpec((KW, KW), const),
        ],
        out_specs=[
            pl.BlockSpec((PB, NH, TM, HD), lambda bi, i: (bi, 0, i, 0)),
            pl.BlockSpec((PB, NKV, TM, HD), lambda bi, i: (bi, 0, i, 0)),
            pl.BlockSpec((PB, NKV, TM, 2 * HD), lambda bi, i: (bi, 0, i, 0)),
            pl.BlockSpec((PB, TM, UW), lambda bi, i: (bi, i, 0)),
        ],
        out_shape=[
            jax.ShapeDtypeStruct((b, NH, s_tot, HD), BF16),
            jax.ShapeDtypeStruct((b, NKV, s_tot, HD), BF16),
            jax.ShapeDtypeStruct((b, NKV, s_tot, 2 * HD), BF16),
            jax.ShapeDtypeStruct((b, s_tot, UW), F32),
        ],
        compiler_params=_params(("arbitrary", "arbitrary")),
        name="in_projection",
    )(ctx_rows, lat_rows, modsel, ln_g, w_in, cos, sin, q_g, k_g, bdq, bdk)


def _merge_heads(o):
    tq = o.shape[0] // GQ
    return jnp.concatenate([o[g * tq:(g + 1) * tq] for g in range(GQ)], axis=1)


def _lane_max(s):
    return functools.reduce(jnp.maximum, [s[:, j * 128:(j + 1) * 128] for j in range(s.shape[1] // 128)])


def _dense_attn_kernel(q_ref, k_ref, v_ref, o_ref, sc_sc, sl_sc, m_sc, *, n_ctx, s_tot):
    qi = pl.program_id(2)
    q = q_ref[0].reshape(GQ * TQD, HD)
    n_lat = (s_tot - n_ctx) // TKL

    def chunks(n):
        spans = [(0, n_ctx)] + [(n_ctx + c * TKL, n_ctx + (c + 1) * TKL) for c in range(n)]
        return list(zip(spans, [sc_sc] + [sl_sc.at[c] for c in range(n)]))

    def scores(n):
        for i, ((lo, hi), slot) in enumerate(chunks(n)):
            s = _dot_nt(q, k_ref[0, 0, lo:hi, :])
            slot[...] = s
            m_sc[i] = _lane_max(s)
        m_lane = functools.reduce(jnp.maximum, [m_sc[i] for i in range(n + 1)])
        m_sc[0] = jnp.broadcast_to(jnp.max(m_lane, axis=1, keepdims=True), m_lane.shape)

    def weighted(n):
        acc = None
        for (lo, hi), slot in chunks(n):
            m = jnp.concatenate([m_sc[0]] * ((hi - lo) // 128), axis=1)
            part = _dot(jnp.exp(slot[...] - m).astype(BF16), v_ref[0, 0, lo:hi, :])
            acc = part if acc is None else acc + part
        o_ref[0] = _merge_heads(acc[:, 0:HD] / acc[:, HD:HD + 1]).astype(BF16)

    is_lat = qi * TQD >= n_ctx
    pl.when(is_lat)(lambda: scores(n_lat))
    pl.when(jnp.logical_not(is_lat))(lambda: scores(0))
    pl.when(is_lat)(lambda: weighted(n_lat))
    pl.when(jnp.logical_not(is_lat))(lambda: weighted(0))


def _dense_attention(q, k, v, n_ctx):
    b, _, s_tot, _ = q.shape
    rows = GQ * TQD
    return pl.pallas_call(
        functools.partial(_dense_attn_kernel, n_ctx=n_ctx, s_tot=s_tot),
        grid=(b, NKV, s_tot // TQD),
        in_specs=[
            pl.BlockSpec((1, GQ, TQD, HD), lambda bi, j, i: (bi, j, i, 0)),
            pl.BlockSpec((1, 1, s_tot, HD), lambda bi, j, i: (bi, j, 0, 0)),
            pl.BlockSpec((1, 1, s_tot, 2 * HD), lambda bi, j, i: (bi, j, 0, 0)),
        ],
        out_specs=pl.BlockSpec((1, TQD, GQ * HD), lambda bi, j, i: (bi, i, j)),
        out_shape=jax.ShapeDtypeStruct((b, s_tot, QW), BF16),
        scratch_shapes=[
            pltpu.VMEM((rows, n_ctx), F32),
            pltpu.VMEM(((s_tot - n_ctx) // TKL, rows, TKL), F32),
            pltpu.VMEM(((s_tot - n_ctx) // TKL + 1, rows, 128), F32),
        ],
        compiler_params=_params(("arbitrary", "arbitrary", "arbitrary")),
        name="dense_attention",
    )(q, k, v)


def _window_attn_kernel(sink_ref, q_ref, k_ref, v_ref, o_ref, sc_sc, sw_sc, m_sc, e_sc, *, n_ctx, s_tot):
    qi = pl.program_id(1)
    span = TQ + 2 * WINDOW
    q0 = n_ctx + qi * TQ
    start = pl.multiple_of(jnp.clip(q0 - WINDOW, n_ctx, s_tot - span), 128)
    row = lax.broadcasted_iota(jnp.int32, (TQ, span), 0)
    col = lax.broadcasted_iota(jnp.int32, (TQ, span), 1)
    band = jnp.where(jnp.abs((q0 - start) + row - col) <= WINDOW, 0.0, -jnp.inf)
    band = jnp.concatenate([band] * GQ, axis=0)
    for j in range(NKV):
        q = q_ref[0, j * GQ:(j + 1) * GQ].reshape(GQ * TQ, HD)
        s_c = _dot_nt(q, k_ref[0, j, 0:n_ctx, :])
        s_w = _dot_nt(q, k_ref[0, j, pl.ds(start, span), :]) + band
        sc_sc[j] = s_c
        sw_sc[j] = s_w
        sink = jnp.concatenate([jnp.full((TQ, 128), sink_ref[j * GQ + g], F32) for g in range(GQ)], axis=0)
        m_lane = jnp.maximum(_lane_max(s_c), _lane_max(s_w))
        m_sc[j] = jnp.maximum(jnp.broadcast_to(jnp.max(m_lane, axis=1, keepdims=True), m_lane.shape), sink)
        e_sc[j] = jnp.exp(sink - m_sc[j])
    for j in range(NKV):
        m = m_sc[j]
        p_c = jnp.exp(sc_sc[j] - jnp.concatenate([m] * (n_ctx // 128), axis=1)).astype(BF16)
        p_w = jnp.exp(sw_sc[j] - jnp.concatenate([m] * (span // 128), axis=1)).astype(BF16)
        acc = _dot(p_c, v_ref[0, j, 0:n_ctx, :]) + _dot(p_w, v_ref[0, j, pl.ds(start, span), :])
        l = acc[:, HD:HD + 1] + e_sc[j][:, 0:1]
        o_ref[0, :, j * GQ * HD:(j + 1) * GQ * HD] = _merge_heads(acc[:, 0:HD] / l).astype(BF16)


def _window_attention(sink, q, k, v, n_ctx):
    b, _, s_tot, _ = q.shape
    s_lat = s_tot - n_ctx
    off = n_ctx // TQ
    return pl.pallas_call(
        functools.partial(_window_attn_kernel, n_ctx=n_ctx, s_tot=s_tot),
        grid_spec=pltpu.PrefetchScalarGridSpec(
            num_scalar_prefetch=1,
            grid=(b, s_lat // TQ),
            in_specs=[
                pl.BlockSpec((1, NH, TQ, HD), lambda bi, i, sk: (bi, 0, i + off, 0)),
                pl.BlockSpec((1, NKV, s_tot, HD), lambda bi, i, sk: (bi, 0, 0, 0)),
                pl.BlockSpec((1, NKV, s_tot, 2 * HD), lambda bi, i, sk: (bi, 0, 0, 0)),
            ],
            out_specs=pl.BlockSpec((1, TQ, QW), lambda bi, i, sk: (bi, i, 0)),
            scratch_shapes=[
                pltpu.VMEM((NKV, GQ * TQ, n_ctx), F32),
                pltpu.VMEM((NKV, GQ * TQ, TQ + 2 * WINDOW), F32),
                pltpu.VMEM((NKV, GQ * TQ, 128), F32),
                pltpu.VMEM((NKV, GQ * TQ, 128), F32),
            ],
        ),
        out_shape=jax.ShapeDtypeStruct((b, s_lat, QW), BF16),
        compiler_params=_params(("arbitrary", "arbitrary")),
        name="window_attention",
    )(sink, q, k, v)


def _conv_kernel(g_ref, cw_ref, cb_ref, lg_ref, lb_ref, o_ref, pad_sc, win_sc, *, n_ctx, s_tot):
    zeros = jnp.zeros((HALO, UW), F32)
    pad_sc[0:HALO, :] = zeros
    pad_sc[HALO + n_ctx:2 * HALO + n_ctx, :] = zeros
    pad_sc[2 * HALO + s_tot:3 * HALO + s_tot, :] = zeros

    def fill(i, carry):
        src = pl.multiple_of(i * TM, TM)
        dst = pl.multiple_of(src + HALO + jnp.where(src >= n_ctx, HALO, 0), 8)
        pad_sc[pl.ds(dst, TM), :] = g_ref[0, pl.ds(src, TM), :]
        return carry

    lax.fori_loop(0, s_tot // TM, fill, 0)
    half = CONV_K // 2

    def tile(i, carry):
        src = pl.multiple_of(i * SUB, SUB)
        base = pl.multiple_of(src + jnp.where(src >= n_ctx, HALO, 0), 8)
        win = pad_sc[pl.ds(base, SUB + 2 * HALO), :]
        keep = SUB + 2 * HALO - 8
        for r in range(8):
            win_sc[r, 0:keep, :] = win[r:r + keep, :]
        acc = jnp.zeros((SUB, UW), F32) + cb_ref[...]
        for t in range(CONV_K):
            a, r = divmod(HALO - half + t, 8)
            acc = acc + win_sc[r, 8 * a:8 * a + SUB, :] * cw_ref[t:t + 1, :]
        mu = jnp.mean(acc, axis=-1, keepdims=True)
        xc = acc - mu
        var = jnp.mean(xc * xc, axis=-1, keepdims=True)
        yn = xc * lax.rsqrt(var + EPS) * lg_ref[...] + lb_ref[...]
        o_ref[0, pl.ds(src, SUB), :] = (yn * _sigmoid(yn)).astype(BF16)
        return carry

    lax.fori_loop(0, s_tot // SUB, tile, 0)


def _conv_module(glu, conv_w, conv_b, ln_g, ln_b, n_ctx):
    b, s_tot, _ = glu.shape
    const = lambda bi: (0, 0)
    return pl.pallas_call(
        functools.partial(_conv_kernel, n_ctx=n_ctx, s_tot=s_tot),
        grid=(b,),
        in_specs=[
            pl.BlockSpec((1, s_tot, UW), lambda bi: (bi, 0, 0)),
            pl.BlockSpec((CONV_K, UW), const),
            pl.BlockSpec((1, UW), const),
            pl.BlockSpec((1, UW), const),
            pl.BlockSpec((1, UW), const),
        ],
        out_specs=pl.BlockSpec((1, s_tot, UW), lambda bi: (bi, 0, 0)),
        out_shape=jax.ShapeDtypeStruct((b, s_tot, UW), BF16),
        scratch_shapes=[pltpu.VMEM((s_tot + 3 * HALO, UW), F32), pltpu.VMEM((8, SUB + 2 * HALO, UW), F32)],
        compiler_params=_params(("arbitrary",)),
        name="conv_module",
    )(glu, conv_w, conv_b, ln_g, ln_b)


def _pool_kernel(u_ref, pw_ref, ps_ref, o_ref, pad_sc, *, n_ctx, s_lat):
    zeros = jnp.zeros((HALO, UW), F32)
    pad_sc[0:HALO, :] = zeros
    pad_sc[HALO + s_lat:2 * HALO + s_lat, :] = zeros

    def fill(i, carry):
        src = pl.multiple_of(i * TM, TM)
        pad_sc[pl.ds(pl.multiple_of(src + HALO, 8), TM), :] = u_ref[0, pl.ds(pl.multiple_of(src + n_ctx, 8), TM), :]
        return carry

    lax.fori_loop(0, s_lat // TM, fill, 0)
    gw = UW // len(POOL_SIZES)

    def tile(i, carry):
        src = pl.multiple_of(i * SUB, SUB)
        win = pad_sc[pl.ds(src, SUB + 2 * HALO), :]
        t = src + lax.broadcasted_iota(jnp.int32, (SUB, 1), 0)
        outs = []
        for gi, w in enumerate(POOL_SIZES):
            lanes = slice(gi * gw, (gi + 1) * gw)
            tot = jnp.zeros((SUB, gw), F32)
            for d in range(-(w // 2), w - w // 2):
                tot = tot + win[HALO + d:HALO + d + SUB, lanes]
            lo = jnp.clip(t - w // 2, 0, s_lat)
            hi = jnp.clip(t - w // 2 + w, 0, s_lat)
            p = tot / (hi - lo).astype(F32) - win[HALO:HALO + SUB, lanes]
            outs.append(_dot(p.astype(BF16), pw_ref[gi]))
        y = jnp.concatenate(outs, axis=1) * ps_ref[...]
        o_ref[0, pl.ds(src, SUB), :] = y.astype(BF16)
        return carry

    lax.fori_loop(0, s_lat // SUB, tile, 0)


def _pool_mixer(u, pool_w, pool_scale, n_ctx):
    b, s_tot, _ = u.shape
    s_lat = s_tot - n_ctx
    gw = UW // len(POOL_SIZES)
    return pl.pallas_call(
        functools.partial(_pool_kernel, n_ctx=n_ctx, s_lat=s_lat),
        grid=(b,),
        in_specs=[
            pl.BlockSpec((1, s_tot, UW), lambda bi: (bi, 0, 0)),
            pl.BlockSpec((len(POOL_SIZES), gw, gw), lambda bi: (0, 0, 0)),
            pl.BlockSpec((1, UW), lambda bi: (0, 0)),
        ],
        out_specs=pl.BlockSpec((1, s_lat, UW), lambda bi: (bi, 0, 0)),
        out_shape=jax.ShapeDtypeStruct((b, s_lat, UW), BF16),
        scratch_shapes=[pltpu.VMEM((s_lat + 2 * HALO, UW), F32)],
        compiler_params=_params(("arbitrary",)),
        name="pool_mixer",
    )(u, pool_w, pool_scale)


def _route(logits):
    shape = logits.shape
    lane = lax.broadcasted_iota(jnp.int32, shape, 1)
    lanef = lane.astype(F32)
    is_g = (lane >= N_EXP) & (lane < N_EXP + N_GROUPS)
    big = 1e9
    lg = jnp.where(is_g, logits, -jnp.inf)
    gmax = jnp.max(lg, axis=1, keepdims=True)
    gsel = jnp.min(jnp.where(is_g & (lg == gmax), lanef, big), axis=1, keepdims=True) - float(N_EXP)
    gsum = jnp.sum(jnp.exp(lg - gmax), axis=1, keepdims=True)
    g_w = 1.0 / gsum
    in_grp = (lane < N_EXP) & ((lane >> 3).astype(F32) == gsel)
    el = jnp.where(in_grp, logits, -jnp.inf)
    emax = jnp.max(el, axis=1, keepdims=True)
    ee = jnp.exp(el - emax)
    p = ee / jnp.sum(ee, axis=1, keepdims=True)
    p1 = jnp.max(jnp.where(in_grp, p, -1.0), axis=1, keepdims=True)
    i1 = jnp.min(jnp.where(in_grp & (p == p1), lanef, big), axis=1, keepdims=True)
    rest = in_grp & (lanef != i1)
    p2 = jnp.max(jnp.where(rest, p, -1.0), axis=1, keepdims=True)
    i2 = jnp.min(jnp.where(rest & (p == p2), lanef, big), axis=1, keepdims=True)
    w1 = g_w * p1 / (p1 + p2)
    w2 = g_w * p2 / (p1 + p2)
    onehot = jnp.where((lanef == i1) | (lanef == i2), 1.0, 0.0)
    return lanef, i1, i2, w1, w2, onehot


def _outproj_kernel(a_ref, b_ref, *refs, with_ctx):
    mod_ref, w_ref, lng_ref, rw_ref, rb_ref, tri_ref, upper_ref, x1_ref, h2_ref, r_ref, g8_ref = refs[-11:]
    a = jnp.concatenate([a_ref[bb] for bb in range(PB)], axis=0)
    bmix = jnp.concatenate([b_ref[bb] for bb in range(PB)], axis=0)
    y = _dot(a, w_ref[0:QW, :]) + _dot(bmix, w_ref[QW:QW + UW, :])
    h2_parts = []
    for bb in range(PB):
        x_in = _tile_rows(refs[0], refs[1], bb) if with_ctx else refs[0][bb]
        mod = mod_ref[bb, 0]
        x1 = x_in + mod[:, 2 * D:3 * D] * y[bb * TM:(bb + 1) * TM]
        x1_ref[bb] = x1
        h2_parts.append(_modulated(x1, mod, lng_ref[...], 3, 4))
        h2_ref[bb] = h2_parts[-1].astype(BF16)
    logits = _dot3(jnp.concatenate(h2_parts, axis=0), rw_ref[...]) + rb_ref[...]
    lanef, i1, i2, w1, w2, onehot = _route(logits)
    for bb in range(PB):
        rows = slice(bb * TM, (bb + 1) * TM)
        hot = onehot[rows]
        before = _dot(tri_ref[...], hot.astype(BF16))
        cnt = jnp.sum(hot, axis=0, keepdims=True)
        g8 = jnp.floor((cnt + float(GRP - 1)) * (1.0 / GRP))
        goff = _dot(jnp.broadcast_to(g8, (8, 128)).astype(BF16), upper_ref[...])[0:1, :]
        local = before + float(GRP) * goff
        lane = lax.broadcasted_iota(jnp.int32, (TM, 128), 1).astype(F32)
        e1, e2 = i1[rows], i2[rows]
        l1 = jnp.sum(jnp.where(lane == e1, local, 0.0), axis=1, keepdims=True)
        l2 = jnp.sum(jnp.where(lane == e2, local, 0.0), axis=1, keepdims=True)
        r_ref[bb] = jnp.where(lane == 0.0, e1, jnp.where(lane == 1.0, e2, jnp.where(lane == 2.0, w1[rows], jnp.where(
            lane == 3.0, w2[rows], jnp.where(lane == 4.0, l1, jnp.where(lane == 5.0, l2, 0.0))))))
        g8_ref[bb, 0] = g8


def _out_project(a, bmix, residual, modsel, w_out, ln_g, rw, rb, tri, upper):
    b, s_out, _ = a.shape
    nt = s_out // TM
    const = lambda bi, i: (0, 0)
    with_ctx = len(residual) == 2
    sel = (lambda i: jnp.minimum(i, 1)) if with_ctx else (lambda i: 1)
    tok = lambda bi, i: (bi, i, 0)
    first_lat = residual[0].shape[1] // TM - nt
    res_specs = _token_specs(0) if with_ctx else [pl.BlockSpec((PB, TM, D), lambda bi, i: (bi, i + first_lat, 0))]
    return pl.pallas_call(
        functools.partial(_outproj_kernel, with_ctx=with_ctx),
        grid=(b // PB, nt),
        in_specs=[
            pl.BlockSpec((PB, TM, QW), tok),
            pl.BlockSpec((PB, TM, UW), tok),
        ] + res_specs + [
            pl.BlockSpec((PB, 1, 1, 6 * D), lambda bi, i: (bi, sel(i), 0, 0)),
            pl.BlockSpec((D, D), const),
            pl.BlockSpec((1, D), const),
            pl.BlockSpec((D, 128), const),
            pl.BlockSpec((1, 128), const),
            pl.BlockSpec((TM, TM), const),
            pl.BlockSpec((128, 128), const),
        ],
        out_specs=[
            pl.BlockSpec((PB, TM, D), tok),
            pl.BlockSpec((PB, TM, D), tok),
            pl.BlockSpec((PB, TM, 128), tok),
            pl.BlockSpec((PB, 1, 1, 128), lambda bi, i: (bi, i, 0, 0)),
        ],
        out_shape=[
            jax.ShapeDtypeStruct((b, s_out, D), F32),
            jax.ShapeDtypeStruct((b, s_out, D), BF16),
            jax.ShapeDtypeStruct((b, s_out, 128), F32),
            jax.ShapeDtypeStruct((b, nt, 1, 128), F32),
        ],
        compiler_params=_params(("arbitrary", "arbitrary")),
        name="out_projection_router",
    )(a, bmix, *residual, modsel, w_out, ln_g, rw, rb, tri, upper)


def _pack_halves(x, exact=False):
    half = x.shape[1] // 2
    a, b = x[:, 0:half], x[:, half:]
    if not exact:
        a, b = a.astype(BF16).astype(F32), b.astype(BF16).astype(F32)
    return lax.bitcast_convert_type(a, U32) | (lax.bitcast_convert_type(b, U32) >> 16)


def _unpack_halves(w):
    a = lax.bitcast_convert_type(w & jnp.uint32(0xFFFF0000), F32)
    b = lax.bitcast_convert_type(w << 16, F32)
    return jnp.concatenate([a, b], axis=1).astype(BF16)


def _rows_copy(src_ref, src_row, dst_ref, dst_row, rows, sem):
    rows = pl.multiple_of(rows, GRP)
    return pltpu.make_async_copy(src_ref.at[pl.ds(pl.multiple_of(src_row, GRP), rows), :],
                                 dst_ref.at[pl.ds(pl.multiple_of(dst_row, GRP), rows), :], sem)


def _wait_rows(count, src_ref, dst_ref, sem):
    @pl.when(count > 0)
    def _():
        _rows_copy(src_ref, 0, dst_ref, 0, count * GRP, sem).wait()


def _start_runs(runs, tile, copy):
    dst_ref, off_ref, len_ref = runs
    for e in range(N_EXP):
        idx = tile * N_EXP + e
        rows = len_ref[idx]

        @pl.when(rows > 0)
        def _():
            copy(dst_ref[idx], off_ref[idx], rows).start(priority=e % 2)


def _dispatch_kernel(dst_ref, off_ref, len_ref, ng_ref, zdst_ref, zlen_ref, h_ref, r_ref, xs_ref, loc, zbuf, sems,
                     zsem):
    runs = (dst_ref, off_ref, len_ref)
    i = pl.program_id(0)
    n = pl.num_programs(0)

    @pl.when(i == 0)
    def _():
        zbuf[...] = jnp.zeros(zbuf.shape, U32)
        for e in range(N_EXP):
            pl.when(zlen_ref[e] > 0)(lambda e=e: _rows_copy(zbuf, 0, xs_ref, zdst_ref[e], zlen_ref[e], zsem).start())
        for e in range(N_EXP):
            pl.when(zlen_ref[e] > 0)(lambda e=e: _rows_copy(zbuf, 0, xs_ref, zdst_ref[e], zlen_ref[e], zsem).wait())

    def run(slot):
        buf = loc.at[slot]
        sem = sems.at[slot]

        @pl.when(i >= 2)
        def _():
            _wait_rows(ng_ref[i - 2], buf, xs_ref, sem)

        rt = r_ref[...].T
        rows = lax.broadcasted_iota(jnp.int32, (LR, TM), 0).astype(F32)
        perm = jnp.where((rows == rt[4:5, :]) | (rows == rt[5:6, :]), 1.0, 0.0).astype(BF16)
        buf[...] = _pack_halves(_dot(perm, h_ref[...]), exact=True)

        _start_runs(runs, i, lambda dst, off, rows: _rows_copy(buf, off, xs_ref, dst, rows, sem))

        @pl.when(i == n - 1)
        def _():
            _wait_rows(ng_ref[i], buf, xs_ref, sem)

            @pl.when(i >= 1)
            def _():
                _wait_rows(ng_ref[i - 1], loc.at[1 - slot], xs_ref, sems.at[1 - slot])

    pl.when(i % 2 == 0)(lambda: run(0))
    pl.when(i % 2 == 1)(lambda: run(1))


def _dispatch(runs, ng, zdst, zlen, h2, rinfo, cap):
    t = h2.shape[0]
    tok = lambda i, *_: (i, 0)
    return pl.pallas_call(
        _dispatch_kernel,
        grid_spec=pltpu.PrefetchScalarGridSpec(
            num_scalar_prefetch=6,
            grid=(t // TM,),
            in_specs=[
                pl.BlockSpec((TM, D), tok),
                pl.BlockSpec((TM, 128), tok),
            ],
            out_specs=pl.BlockSpec(memory_space=pl.ANY),
            scratch_shapes=[
                pltpu.VMEM((2, LR, D // 2), U32),
                pltpu.VMEM((BLK, D // 2), U32),
                pltpu.SemaphoreType.DMA((2,)),
                pltpu.SemaphoreType.DMA,
            ],
        ),
        out_shape=jax.ShapeDtypeStruct((cap, D // 2), U32),
        compiler_params=_params(("arbitrary",)),
        name="moe_dispatch",
    )(*runs, ng, zdst, zlen, h2, rinfo)


def _expert_kernel(be_ref, nu_ref, x_ref, wg_ref, wu_ref, wd_ref, y_ref, wg_sc, wu_sc, wd_sc):
    i = pl.program_id(0)
    used = i < nu_ref[0]
    fresh = (i == 0) | (be_ref[i] != be_ref[jnp.maximum(i - 1, 0)])

    @pl.when(used & fresh)
    def _():
        wg_sc[...] = wg_ref[0, 0].astype(BF16)
        wu_sc[...] = wu_ref[0, 0].astype(BF16)
        wd_sc[...] = wd_ref[0, 0].astype(BF16)

    @pl.when(used)
    def _():
        xb = _unpack_halves(x_ref[...])
        g = _dot(xb, wg_sc[...])
        u = _dot(xb, wu_sc[...])
        hid = (g * _sigmoid(g)) * u
        y_ref[...] = _pack_halves(_dot(hid.astype(BF16), wd_sc[...]))

    @pl.when(jnp.logical_not(used))
    def _():
        y_ref[...] = jnp.zeros(y_ref.shape, U32)


def _experts(blk_e, n_used, xs, wg, wu, wd, layer):
    cap = xs.shape[0]
    row = lambda i, be, nu: (jnp.minimum(i, nu[0] - 1), 0)
    wsel = lambda i, be, nu: (layer, be[i], 0, 0)
    return pl.pallas_call(
        _expert_kernel,
        grid_spec=pltpu.PrefetchScalarGridSpec(
            num_scalar_prefetch=2,
            grid=(cap // BLK,),
            in_specs=[
                pl.BlockSpec((BLK, D // 2), row),
                pl.BlockSpec((1, 1, D, D_EXP), wsel),
                pl.BlockSpec((1, 1, D, D_EXP), wsel),
                pl.BlockSpec((1, 1, D_EXP, D), wsel),
            ],
            out_specs=pl.BlockSpec((BLK, D // 2), lambda i, be, nu: (i, 0)),
            scratch_shapes=[
                pltpu.VMEM((D, D_EXP), BF16),
                pltpu.VMEM((D, D_EXP), BF16),
                pltpu.VMEM((D_EXP, D), BF16),
            ],
        ),
        out_shape=jax.ShapeDtypeStruct((cap, D // 2), U32),
        compiler_params=_params(("arbitrary",)),
        name="moe_experts",
    )(blk_e, n_used, xs, wg, wu, wd)


def _combine_kernel(dst_ref, off_ref, len_ref, ng_ref, y_ref, x1_ref, r_ref, mod_ref, o_ref, loc, sems):
    runs = (dst_ref, off_ref, len_ref)
    i = pl.program_id(0)
    n = pl.num_programs(0)

    def fetch(tile, slot):
        _start_runs(runs, tile, lambda dst, off, rows: _rows_copy(y_ref, dst, loc.at[slot], off, rows, sems.at[slot]))

    @pl.when(i == 0)
    def _():
        loc[...] = jnp.zeros(loc.shape, U32)
        fetch(0, 0)

    def run(slot):
        @pl.when(i + 1 < n)
        def _():
            fetch(i + 1, 1 - slot)

        _wait_rows(ng_ref[i], y_ref, loc.at[slot], sems.at[slot])
        o_ref[...] = _unsort_add(loc[slot], r_ref[...], x1_ref[...], mod_ref[0, 0][:, 5 * D:6 * D])

    pl.when(i % 2 == 0)(lambda: run(0))
    pl.when(i % 2 == 1)(lambda: run(1))


def _combine(runs, ng, ys, x1, rinfo, modsel, *, tiles_per_batch, with_ctx):
    t = x1.shape[0]
    if with_ctx:
        msel = lambda i, *_: (i // tiles_per_batch, jnp.minimum(i % tiles_per_batch, 1), 0, 0)
    else:
        msel = lambda i, *_: (i // tiles_per_batch, 1, 0, 0)
    tok = lambda i, *_: (i, 0)
    return pl.pallas_call(
        _combine_kernel,
        grid_spec=pltpu.PrefetchScalarGridSpec(
            num_scalar_prefetch=4,
            grid=(t // TM,),
            in_specs=[
                pl.BlockSpec(memory_space=pl.ANY),
                pl.BlockSpec((TM, D), tok),
                pl.BlockSpec((TM, 128), tok),
                pl.BlockSpec((1, 1, 1, 6 * D), msel),
            ],
            out_specs=pl.BlockSpec((TM, D), tok),
            scratch_shapes=[pltpu.VMEM((2, LR, D // 2), U32), pltpu.SemaphoreType.DMA((2,))],
        ),
        out_shape=jax.ShapeDtypeStruct((t, D), F32),
        compiler_params=_params(("arbitrary",)),
        name="moe_combine",
    )(*runs, ng, ys, x1, rinfo, modsel)


def _unsort_add(ysw, info, x1, g2):
    col = lax.broadcasted_iota(jnp.int32, (TM, LR), 1).astype(F32)
    pick = jnp.concatenate([jnp.where(col == info[:, 4:5], 1.0, 0.0).astype(BF16),
                            jnp.where(col == info[:, 5:6], 1.0, 0.0).astype(BF16)], axis=0)
    y12 = _dot(pick, _unpack_halves(ysw))
    return x1 + g2 * (info[:, 2:3] * y12[0:TM] + info[:, 3:4] * y12[TM:2 * TM])


def _combine_proj_kernel(dst_ref, off_ref, len_ref, ng_ref, y_ref, x1_ref, r_ref, mod0_ref, mod_ref, lng_ref, w_ref,
                         cos_ref, sin_ref, qg_ref, kg_ref, bdq_ref, bdk_ref,
                         xc_ref, q_ref, k_ref, v_ref, u_ref, loc, sems, *, nt):
    runs = (dst_ref, off_ref, len_ref)
    bp = pl.program_id(0)
    i = pl.program_id(1)
    step = bp * nt + i
    slot = step % 2

    def fetch(bp_, i_, slot_):
        for bb in range(PB):
            tile = (bp_ * PB + bb) * nt + i_
            _start_runs(runs, tile, lambda dst, off, rows, bb=bb: _rows_copy(
                y_ref, dst, loc.at[slot_, bb], off, rows, sems.at[slot_, bb]))

    @pl.when(step == 0)
    def _():
        loc[...] = jnp.zeros(loc.shape, U32)
        fetch(0, 0, 0)

    @pl.when(step + 1 < pl.num_programs(0) * nt)
    def _():
        wrap = i + 1 == nt
        fetch(jnp.where(wrap, bp + 1, bp), jnp.where(wrap, 0, i + 1), 1 - slot)

    @pl.when(step >= 0)
    def _():
        for bb in range(PB):
            tile = (bp * PB + bb) * nt + i
            _wait_rows(ng_ref[tile], y_ref, loc.at[slot, bb], sems.at[slot, bb])
            xc_ref[bb] = _unsort_add(loc[slot, bb], r_ref[bb], x1_ref[bb], mod0_ref[bb, 0][:, 5 * D:6 * D])

    @pl.when(step >= -1)
    def _():
        h = jnp.concatenate([_modulated(xc_ref[bb], mod_ref[bb, 0], lng_ref[...], 0, 1) for bb in range(PB)], axis=0)
        px = _dot(h.astype(BF16), w_ref[...])
        _proj_epilogue(px, cos_ref, sin_ref, qg_ref, kg_ref, bdq_ref, bdk_ref, q_ref, k_ref, v_ref, u_ref, False)


def _combine_project(runs, ng, ys, x1, rinfo, modsel0, modsel, ln_g, w_in, cos, sin, q_g, k_g, bdq, bdk):
    b, s_tot, _ = x1.shape
    nt = s_tot // TM
    n_in = w_in.shape[1]
    const = lambda bi, i, *_: (0, 0)
    tok = lambda bi, i, *_: (bi, i, 0)
    msel = lambda bi, i, *_: (bi, jnp.minimum(i, 1), 0, 0)
    head = lambda bi, i, *_: (bi, 0, i, 0)
    return pl.pallas_call(
        functools.partial(_combine_proj_kernel, nt=nt),
        grid_spec=pltpu.PrefetchScalarGridSpec(
            num_scalar_prefetch=4,
            grid=(b // PB, nt),
            in_specs=[
                pl.BlockSpec(memory_space=pl.ANY),
                pl.BlockSpec((PB, TM, D), tok),
                pl.BlockSpec((PB, TM, 128), tok),
                pl.BlockSpec((PB, 1, 1, 6 * D), msel),
                pl.BlockSpec((PB, 1, 1, 6 * D), msel),
                pl.BlockSpec((1, D), const),
                pl.BlockSpec((D, n_in), const),
                pl.BlockSpec((TM, 128), lambda bi, i, *_: (i, 0)),
                pl.BlockSpec((TM, 128), lambda bi, i, *_: (i, 0)),
                pl.BlockSpec((1, QW), const),
                pl.BlockSpec((1, KW), const),
                pl.BlockSpec((QW, QW), const),
                pl.BlockSpec((KW, KW), const),
            ],
            out_specs=[
                pl.BlockSpec((PB, TM, D), tok),
                pl.BlockSpec((PB, NH, TM, HD), head),
                pl.BlockSpec((PB, NKV, TM, HD), head),
                pl.BlockSpec((PB, NKV, TM, 2 * HD), head),
                pl.BlockSpec((PB, TM, UW), tok),
            ],
            scratch_shapes=[pltpu.VMEM((2, PB, LR, D // 2), U32), pltpu.SemaphoreType.DMA((2, PB))],
        ),
        out_shape=[
            jax.ShapeDtypeStruct((b, s_tot, D), F32),
            jax.ShapeDtypeStruct((b, NH, s_tot, HD), BF16),
            jax.ShapeDtypeStruct((b, NKV, s_tot, HD), BF16),
            jax.ShapeDtypeStruct((b, NKV, s_tot, 2 * HD), BF16),
            jax.ShapeDtypeStruct((b, s_tot, UW), F32),
        ],
        compiler_params=_params(("arbitrary", "arbitrary")),
        name="moe_combine_in_projection",
    )(*runs, ng, ys, x1, rinfo, modsel0, modsel, ln_g, w_in, cos, sin, q_g, k_g, bdq, bdk)


def _moe(x1, h2, rinfo, g8, modsel, wg, wu, wd, layer, *, with_ctx, combine=True):
    b, s, _ = x1.shape
    t = b * s
    n_tiles = t // TM
    i32 = jnp.int32
    grp = g8.reshape(n_tiles, 128)[:, 0:N_EXP].astype(i32)
    n_blk = -(-(2 * t + n_tiles * N_EXP * (GRP - 1) + N_EXP * (BLK - 1)) // BLK)
    cap = n_blk * BLK
    goff_end = jnp.cumsum(grp, axis=1)
    ng = goff_end[:, N_EXP - 1].astype(i32)
    rows_e = GRP * jnp.sum(grp, axis=0)
    padded = (rows_e + BLK - 1) // BLK * BLK
    pend = jnp.cumsum(padded)
    pstart = pend - padded
    run_dst = (pstart[None, :] + GRP * (jnp.cumsum(grp, axis=0) - grp)).reshape(-1).astype(i32)
    run_off = (GRP * (goff_end - grp)).reshape(-1).astype(i32)
    run_len = (GRP * grp).reshape(-1).astype(i32)
    runs = (run_dst, run_off, run_len)
    zdst = (pstart + rows_e).astype(i32)
    zlen = (padded - rows_e).astype(i32)
    blk_e = jnp.minimum(jnp.sum((pend[None, :] <= (jnp.arange(n_blk, dtype=i32) * BLK)[:, None]).astype(i32), axis=1),
                        N_EXP - 1).astype(i32)
    n_used = (pend[N_EXP - 1:N_EXP] // BLK).astype(i32)
    info = rinfo.reshape(t, 128)
    xs = _dispatch(runs, ng, zdst, zlen, h2.reshape(t, D), info, cap)
    ys = _experts(blk_e, n_used, xs, wg, wu, wd, layer)
    if not combine:
        return runs, ng, ys
    out = _combine(runs, ng, ys, x1.reshape(t, D), info, modsel, tiles_per_batch=s // TM, with_ctx=with_ctx)
    return out.reshape(b, s, D)


def _rope_tables(n_ctx, s_lat):
    rows = s_lat // GRID_W
    row = jnp.repeat(jnp.arange(rows, dtype=F32), GRID_W)
    col = jnp.tile(jnp.arange(GRID_W, dtype=F32), rows)
    n_freq = HD // 4
    inv = ROPE_THETA ** (-jnp.arange(n_freq, dtype=F32) / n_freq)
    ang = jnp.concatenate([row[:, None] * inv, col[:, None] * inv], axis=-1)
    cos = jnp.concatenate([jnp.ones((n_ctx, HD // 2), F32), jnp.cos(ang)], axis=0)
    sin = jnp.concatenate([jnp.zeros((n_ctx, HD // 2), F32), jnp.sin(ang)], axis=0)
    cos = jnp.concatenate([cos, cos, cos, cos], axis=1)
    sin = jnp.concatenate([-sin, sin, -sin, sin], axis=1)
    return cos, sin


def _block_diag_ones(n):
    i = jnp.arange(n) // HD
    return (i[:, None] == i[None, :]).astype(BF16)


def kernel(x, c, ctx, c_ctx, mod_w, mod_b, ln1_g, ln2_g, w_in_ab, w_out_ab, q_norm_a, k_norm_a, conv_w, conv_b,
           conv_ln_g, conv_ln_b, w_in_cd, w_out_cd, q_norm_c, k_norm_c, sink_c, pool_w, pool_scale,
           rt_grp_w, rt_grp_b, rt_exp_w, rt_exp_b, ex_gate, ex_up, ex_down):
    b, s_lat, _ = x.shape
    n_ctx = ctx.shape[1]
    assert n_ctx == TM and s_lat % TM == 0 and b <= 8 and b % PB == 0

    cc = jnp.zeros((16, D), F32).at[0:b].set(c).at[8].set(c_ctx)
    mod = _modulation(cc, mod_w, mod_b)

    def modsel(l):
        return jnp.stack([jnp.broadcast_to(mod[l, 8], (b, 6 * D)), mod[l, 0:b]], axis=1).reshape(b, 2, 1, 6 * D)

    cos, sin = _rope_tables(n_ctx, s_lat)
    bdq = _block_diag_ones(QW)
    bdk = _block_diag_ones(KW)
    tri = (jnp.arange(TM)[:, None] > jnp.arange(TM)[None, :]).astype(BF16)
    upper = (jnp.arange(128)[:, None] < jnp.arange(128)[None, :]).astype(BF16)

    def router(l):
        rw = jnp.zeros((D, 128), F32).at[:, 0:N_EXP].set(rt_exp_w[l]).at[:, N_EXP:N_EXP + N_GROUPS].set(rt_grp_w[l])
        rb = jnp.zeros((1, 128), F32).at[0, 0:N_EXP].set(rt_exp_b[l]).at[0, N_EXP:N_EXP + N_GROUPS].set(rt_grp_b[l])
        return rw, rb

    def tile_gain(g, n):
        return jnp.tile(g, n).reshape(1, n * HD)

    s_tot = n_ctx + s_lat

    ms = modsel(0)
    q, k, v, glu = _project(ctx, x, 0, s_tot, ms, ln1_g[0:1], w_in_ab[0].astype(BF16), cos, sin,
                            tile_gain(q_norm_a[0], NH), tile_gain(k_norm_a[0], NKV), bdq, bdk, glu=True)
    att = _dense_attention(q, k, v, n_ctx)
    cv = _conv_module(glu, conv_w[0], conv_b[0:1], conv_ln_g[0:1], conv_ln_b[0:1], n_ctx)
    rw, rb = router(0)
    x1, h2, rinfo, g8 = _out_project(att, cv, (ctx, x), ms, w_out_ab[0].astype(BF16), ln2_g[0:1], rw, rb, tri, upper)
    runs, ng, ys = _moe(x1, h2, rinfo, g8, ms, ex_gate, ex_up, ex_down, 0, with_ctx=True, combine=False)

    ms0, ms = ms, modsel(1)
    xc, q, k, v, u = _combine_project(runs, ng, ys, x1, rinfo, ms0, ms, ln1_g[1:2], w_in_cd[0].astype(BF16), cos, sin,
                                      tile_gain(q_norm_c[0], NH), tile_gain(k_norm_c[0], NKV), bdq, bdk)
    att = _window_attention(sink_c[0], q, k, v, n_ctx)
    pm = _pool_mixer(u, pool_w[0].astype(BF16), pool_scale[0:1], n_ctx)
    rw, rb = router(1)
    x1, h2, rinfo, g8 = _out_project(att, pm, (xc,), ms, w_out_cd[0].astype(BF16), ln2_g[1:2], rw, rb, tri, upper)
    return _moe(x1, h2, rinfo, g8, ms, ex_gate, ex_up, ex_down, 1, with_ctx=False)
```

```python
import functools

import jax
import jax.numpy as jnp
from jax import lax
from jax.experimental import pallas as pl
from jax.experimental.pallas import tpu as pltpu

F32 = jnp.float32
BF16 = jnp.bfloat16
U32 = jnp.uint32

D = 1024
HD = 64
NH = 8
NKV = 2
GQ = NH // NKV
QW = NH * HD
KW = NKV * HD
UW = 512
EPS = 1e-6
ROPE_THETA = 10000.0
GRID_W = 64
CONV_K = 31
WINDOW = 128
POOL_SIZES = (2, 4, 8, 16)
N_GROUPS = 4
PER_GROUP = 8
N_EXP = N_GROUPS * PER_GROUP
D_EXP = D // 2

TM = 256
PB = 4
TQ = 256
TQD = 256
TKL = 512
SUB = 64
HALO = 16
BLK = 1024
GRP = 8
NG = 96
LR = NG * GRP
VMEM_LIMIT = 56 * 1024 * 1024


def _sigmoid(x):
    return 1.0 / (1.0 + jnp.exp(-x))


def _dot(a, b):
    return jnp.dot(a, b, preferred_element_type=F32)


def _dot_nt(a, b):
    return lax.dot_general(a, b, (((1,), (1,)), ((), ())), preferred_element_type=F32)


def _split(a):
    hi = a.astype(BF16)
    lo = (a - hi.astype(F32)).astype(BF16)
    return hi, lo


def _dot3(a, w):
    a_hi, a_lo = _split(a)
    w_hi, w_lo = _split(w)
    return _dot(a_hi, w_hi) + _dot(a_lo, w_hi) + _dot(a_hi, w_lo)


def _rmsnorm(x, g):
    return x * lax.rsqrt(jnp.mean(x * x, axis=-1, keepdims=True) + EPS) * g


def _params(sem, vmem=VMEM_LIMIT):
    return pltpu.CompilerParams(dimension_semantics=sem, vmem_limit_bytes=vmem)


def _mod_kernel(c_ref, w_ref, b_ref, o_ref):
    a = c_ref[...]
    a = a * _sigmoid(a)
    o_ref[0] = _dot3(a, w_ref[0]) + b_ref[0]


def _modulation(cc, mod_w, mod_b):
    depth = mod_w.shape[0]
    tn = 1024
    return pl.pallas_call(
        _mod_kernel,
        grid=(depth, 6 * D // tn),
        in_specs=[
            pl.BlockSpec((16, D), lambda l, j: (0, 0)),
            pl.BlockSpec((1, D, tn), lambda l, j: (l, 0, j)),
            pl.BlockSpec((1, 1, tn), lambda l, j: (l, 0, j)),
        ],
        out_specs=pl.BlockSpec((1, 16, tn), lambda l, j: (l, 0, j)),
        out_shape=jax.ShapeDtypeStruct((depth, 16, 6 * D), F32),
        compiler_params=_params(("arbitrary", "arbitrary")),
        name="modulation",
    )(cc, mod_w, mod_b.reshape(depth, 1, 6 * D))


def _swap_halves(t):
    w = t.shape[1]
    lane = lax.broadcasted_iota(jnp.int32, t.shape, 1)
    first = (lane & (HD - 1)) < (HD // 2)
    return jnp.where(first, pltpu.roll(t, w - HD // 2, 1), pltpu.roll(t, HD // 2, 1))


def _head_norm_rope(t, bd, g, cos, sin, scale):
    hi, lo = _split(t * t)
    ssq = _dot(hi, bd) + _dot(lo, bd)
    tn = t * lax.rsqrt(ssq * (1.0 / HD) + EPS) * g
    n = t.shape[1] // 128
    if n > 1:
        cos = jnp.concatenate([cos] * n, axis=1)
        sin = jnp.concatenate([sin] * n, axis=1)
    out = tn * cos + _swap_halves(tn) * sin
    return out * scale if scale != 1.0 else out


def _tile_rows(c_ref, x_ref, bb):
    return jnp.where(pl.program_id(1) == 0, c_ref[bb], x_ref[bb])


def _modulated(x, mod, lng, shift_at, scale_at):
    return _rmsnorm(x, lng) * (1.0 + mod[:, scale_at * D:(scale_at + 1) * D]) + mod[:, shift_at * D:(shift_at + 1) * D]


def _proj_kernel(c_ref, x_ref, mod_ref, lng_ref, w_ref, cos_ref, sin_ref, qg_ref, kg_ref, bdq_ref, bdk_ref,
                 q_ref, k_ref, v_ref, u_ref, *, glu):
    h = jnp.concatenate([_modulated(_tile_rows(c_ref, x_ref, bb), mod_ref[bb, 0], lng_ref[...], 0, 1)
                         for bb in range(PB)], axis=0)
    px = _dot(h.astype(BF16), w_ref[...])
    _proj_epilogue(px, cos_ref, sin_ref, qg_ref, kg_ref, bdq_ref, bdk_ref, q_ref, k_ref, v_ref, u_ref, glu)


def _proj_epilogue(px, cos_ref, sin_ref, qg_ref, kg_ref, bdq_ref, bdk_ref, q_ref, k_ref, v_ref, u_ref, glu):
    cos = jnp.concatenate([cos_ref[...]] * PB, axis=0)
    sin = jnp.concatenate([sin_ref[...]] * PB, axis=0)
    q = _head_norm_rope(px[:, 0:QW], bdq_ref[...], qg_ref[...], cos, sin, HD ** -0.5)
    k = _head_norm_rope(px[:, QW:QW + KW], bdk_ref[...], kg_ref[...], cos, sin, 1.0)
    v = px[:, QW + KW:QW + 2 * KW]
    lane = lax.broadcasted_iota(jnp.int32, (TM, KW), 1)
    ones_col = jnp.where(lane == HD, 1.0, 0.0)
    o_u = QW + 2 * KW
    if glu:
        u = px[:, o_u:o_u + UW] * _sigmoid(px[:, o_u + UW:o_u + 2 * UW])
    else:
        u = px[:, o_u:o_u + UW]
    for bb in range(PB):
        rows = slice(bb * TM, (bb + 1) * TM)
        for hh in range(NH):
            q_ref[bb, hh] = q[rows, hh * HD:(hh + 1) * HD].astype(BF16)
        for j in range(NKV):
            k_ref[bb, j] = k[rows, j * HD:(j + 1) * HD].astype(BF16)
            vj = v[rows] if j == 0 else pltpu.roll(v[rows], KW - j * HD, 1)
            v_ref[bb, j] = jnp.where(lane < HD, vj, ones_col).astype(BF16)
        u_ref[bb] = u[rows]


def _token_specs(lat_first):
    return [pl.BlockSpec((PB, TM, D), lambda bi, i: (bi, 0, 0)),
            pl.BlockSpec((PB, TM, D), lambda bi, i: (bi, jnp.maximum(i - 1, 0) + lat_first, 0))]


def _project(ctx_rows, lat_rows, lat_first, s_tot, modsel, ln_g, w_in, cos, sin, q_g, k_g, bdq, bdk, *, glu):
    b = ctx_rows.shape[0]
    n_in = w_in.shape[1]
    nt = s_tot // TM
    const = lambda bi, i: (0, 0)
    return pl.pallas_call(
        functools.partial(_proj_kernel, glu=glu),
        grid=(b // PB, nt),
        in_specs=_token_specs(lat_first) + [
            pl.BlockSpec((PB, 1, 1, 6 * D), lambda bi, i: (bi, jnp.minimum(i, 1), 0, 0)),
            pl.BlockSpec((1, D), const),
            pl.BlockSpec((D, n_in), const),
            pl.BlockSpec((TM, 128), lambda bi, i: (i, 0)),
            pl.BlockSpec((TM, 128), lambda bi, i: (i, 0)),
            pl.BlockSpec((1, QW), const),
            pl.BlockSpec((1, KW), const),
            pl.BlockSpec((QW, QW), const),
            pl.BlockSpec((KW, KW), const),
        ],
        out_specs=[
            pl.BlockSpec((PB, NH, TM, HD), lambda bi, i: (bi, 0, i, 0)),
            pl.BlockSpec((PB, NKV, TM, HD), lambda bi, i: (bi, 0, i, 0)),
            pl.BlockSpec((PB, NKV, TM, 2 * HD), lambda bi, i: (bi, 0, i, 0)),
            pl.BlockSpec((PB, TM, UW), lambda bi, i: (bi, i, 0)),
        ],
        out_shape=[
            jax.ShapeDtypeStruct((b, NH, s_tot, HD), BF16),
            jax.ShapeDtypeStruct((b, NKV, s_tot, HD), BF16),
            jax.ShapeDtypeStruct((b, NKV, s_tot, 2 * HD), BF16),
            jax.ShapeDtypeStruct((b, s_tot, UW), F32),
        ],
        compiler_params=_params(("arbitrary", "arbitrary")),
        name="in_projection",
    )(ctx_rows, lat_rows, modsel, ln_g, w_in, cos, sin, q_g, k_g, bdq, bdk)


def _merge_heads(o):
    tq = o.shape[0] // GQ
    return jnp.concatenate([o[g * tq:(g + 1) * tq] for g in range(GQ)], axis=1)


def _lane_max(s):
    return functools.reduce(jnp.maximum, [s[:, j * 128:(j + 1) * 128] for j in range(s.shape[1] // 128)])


def _dense_attn_kernel(q_ref, k_ref, v_ref, o_ref, sc_sc, sl_sc, m_sc, *, n_ctx, s_tot):
    qi = pl.program_id(2)
    q = q_ref[0].reshape(GQ * TQD, HD)
    n_lat = (s_tot - n_ctx) // TKL

    def chunks(n):
        spans = [(0, n_ctx)] + [(n_ctx + c * TKL, n_ctx + (c + 1) * TKL) for c in range(n)]
        return list(zip(spans, [sc_sc] + [sl_sc.at[c] for c in range(n)]))

    def scores(n):
        for i, ((lo, hi), slot) in enumerate(chunks(n)):
            s = _dot_nt(q, k_ref[0, 0, lo:hi, :])
            slot[...] = s
            m_sc[i] = _lane_max(s)
        m_lane = functools.reduce(jnp.maximum, [m_sc[i] for i in range(n + 1)])
        m_sc[0] = jnp.broadcast_to(jnp.max(m_lane, axis=1, keepdims=True), m_lane.shape)

    def weighted(n):
        acc = None
        for (lo, hi), slot in chunks(n):
            m = jnp.concatenate([m_sc[0]] * ((hi - lo) // 128), axis=1)
            part = _dot(jnp.exp(slot[...] - m).astype(BF16), v_ref[0, 0, lo:hi, :])
            acc = part if acc is None else acc + part
        o_ref[0] = _merge_heads(acc[:, 0:HD] / acc[:, HD:HD + 1]).astype(BF16)

    is_lat = qi * TQD >= n_ctx
    pl.when(is_lat)(lambda: scores(n_lat))
    pl.when(jnp.logical_not(is_lat))(lambda: scores(0))
    pl.when(is_lat)(lambda: weighted(n_lat))
    pl.when(jnp.logical_not(is_lat))(lambda: weighted(0))


def _dense_attention(q, k, v, n_ctx):
    b, _, s_tot, _ = q.shape
    rows = GQ * TQD
    return pl.pallas_call(
        functools.partial(_dense_attn_kernel, n_ctx=n_ctx, s_tot=s_tot),
        grid=(b, NKV, s_tot // TQD),
        in_specs=[
            pl.BlockSpec((1, GQ, TQD, HD), lambda bi, j, i: (bi, j, i, 0)),
            pl.BlockSpec((1, 1, s_tot, HD), lambda bi, j, i: (bi, j, 0, 0)),
            pl.BlockSpec((1, 1, s_tot, 2 * HD), lambda bi, j, i: (bi, j, 0, 0)),
        ],
        out_specs=pl.BlockSpec((1, TQD, GQ * HD), lambda bi, j, i: (bi, i, j)),
        out_shape=jax.ShapeDtypeStruct((b, s_tot, QW), BF16),
        scratch_shapes=[
            pltpu.VMEM((rows, n_ctx), F32),
            pltpu.VMEM(((s_tot - n_ctx) // TKL, rows, TKL), F32),
            pltpu.VMEM(((s_tot - n_ctx) // TKL + 1, rows, 128), F32),
        ],
        compiler_params=_params(("arbitrary", "arbitrary", "arbitrary")),
        name="dense_attention",
    )(q, k, v)


def _window_attn_kernel(sink_ref, q_ref, k_ref, v_ref, o_ref, sc_sc, sw_sc, m_sc, e_sc, *, n_ctx, s_tot):
    qi = pl.program_id(1)
    span = TQ + 2 * WINDOW
    q0 = n_ctx + qi * TQ
    start = pl.multiple_of(jnp.clip(q0 - WINDOW, n_ctx, s_tot - span), 128)
    row = lax.broadcasted_iota(jnp.int32, (TQ, span), 0)
    col = lax.broadcasted_iota(jnp.int32, (TQ, span), 1)
    band = jnp.where(jnp.abs((q0 - start) + row - col) <= WINDOW, 0.0, -jnp.inf)
    band = jnp.concatenate([band] * GQ, axis=0)
    for j in range(NKV):
        q = q_ref[0, j * GQ:(j + 1) * GQ].reshape(GQ * TQ, HD)
        s_c = _dot_nt(q, k_ref[0, j, 0:n_ctx, :])
        s_w = _dot_nt(q, k_ref[0, j, pl.ds(start, span), :]) + band
        sc_sc[j] = s_c
        sw_sc[j] = s_w
        sink = jnp.concatenate([jnp.full((TQ, 128), sink_ref[j * GQ + g], F32) for g in range(GQ)], axis=0)
        m_lane = jnp.maximum(_lane_max(s_c), _lane_max(s_w))
        m_sc[j] = jnp.maximum(jnp.broadcast_to(jnp.max(m_lane, axis=1, keepdims=True), m_lane.shape), sink)
        e_sc[j] = jnp.exp(sink - m_sc[j])
    for j in range(NKV):
        m = m_sc[j]
        p_c = jnp.exp(sc_sc[j] - jnp.concatenate([m] * (n_ctx // 128), axis=1)).astype(BF16)
        p_w = jnp.exp(sw_sc[j] - jnp.concatenate([m] * (span // 128), axis=1)).astype(BF16)
        acc = _dot(p_c, v_ref[0, j, 0:n_ctx, :]) + _dot(p_w, v_ref[0, j, pl.ds(start, span), :])
        l = acc[:, HD:HD + 1] + e_sc[j][:, 0:1]
        o_ref[0, :, j * GQ * HD:(j + 1) * GQ * HD] = _merge_heads(acc[:, 0:HD] / l).astype(BF16)


def _window_attention(sink, q, k, v, n_ctx):
    b, _, s_tot, _ = q.shape
    s_lat = s_tot - n_ctx
    off = n_ctx // TQ
    return pl.pallas_call(
        functools.partial(_window_attn_kernel, n_ctx=n_ctx, s_tot=s_tot),
        grid_spec=pltpu.PrefetchScalarGridSpec(
            num_scalar_prefetch=1,
            grid=(b, s_lat // TQ),
            in_specs=[
                pl.BlockSpec((1, NH, TQ, HD), lambda bi, i, sk: (bi, 0, i + off, 0)),
                pl.BlockSpec((1, NKV, s_tot, HD), lambda bi, i, sk: (bi, 0, 0, 0)),
                pl.BlockSpec((1, NKV, s_tot, 2 * HD), lambda bi, i, sk: (bi, 0, 0, 0)),
            ],
            out_specs=pl.BlockSpec((1, TQ, QW), lambda bi, i, sk: (bi, i, 0)),
            scratch_shapes=[
                pltpu.VMEM((NKV, GQ * TQ, n_ctx), F32),
                pltpu.VMEM((NKV, GQ * TQ, TQ + 2 * WINDOW), F32),
                pltpu.VMEM((NKV, GQ * TQ, 128), F32),
                pltpu.VMEM((NKV, GQ * TQ, 128), F32),
            ],
        ),
        out_shape=jax.ShapeDtypeStruct((b, s_lat, QW), BF16),
        compiler_params=_params(("arbitrary", "arbitrary")),
        name="window_attention",
    )(sink, q, k, v)


def _conv_kernel(g_ref, cw_ref, cb_ref, lg_ref, lb_ref, o_ref, pad_sc, win_sc, *, n_ctx, s_tot):
    zeros = jnp.zeros((HALO, UW), F32)
    pad_sc[0:HALO, :] = zeros
    pad_sc[HALO + n_ctx:2 * HALO + n_ctx, :] = zeros
    pad_sc[2 * HALO + s_tot:3 * HALO + s_tot, :] = zeros

    def fill(i, carry):
        src = pl.multiple_of(i * TM, TM)
        dst = pl.multiple_of(src + HALO + jnp.where(src >= n_ctx, HALO, 0), 8)
        pad_sc[pl.ds(dst, TM), :] = g_ref[0, pl.ds(src, TM), :]
        return carry

    lax.fori_loop(0, s_tot // TM, fill, 0)
    half = CONV_K // 2

    def tile(i, carry):
        src = pl.multiple_of(i * SUB, SUB)
        base = pl.multiple_of(src + jnp.where(src >= n_ctx, HALO, 0), 8)
        win = pad_sc[pl.ds(base, SUB + 2 * HALO), :]
        keep = SUB + 2 * HALO - 8
        for r in range(8):
            win_sc[r, 0:keep, :] = win[r:r + keep, :]
        acc = jnp.zeros((SUB, UW), F32) + cb_ref[...]
        for t in range(CONV_K):
            a, r = divmod(HALO - half + t, 8)
            acc = acc + win_sc[r, 8 * a:8 * a + SUB, :] * cw_ref[t:t + 1, :]
        mu = jnp.mean(acc, axis=-1, keepdims=True)
        xc = acc - mu
        var = jnp.mean(xc * xc, axis=-1, keepdims=True)
        yn = xc * lax.rsqrt(var + EPS) * lg_ref[...] + lb_ref[...]
        o_ref[0, pl.ds(src, SUB), :] = (yn * _sigmoid(yn)).astype(BF16)
        return carry

    lax.fori_loop(0, s_tot // SUB, tile, 0)


def _conv_module(glu, conv_w, conv_b, ln_g, ln_b, n_ctx):
    b, s_tot, _ = glu.shape
    const = lambda bi: (0, 0)
    return pl.pallas_call(
        functools.partial(_conv_kernel, n_ctx=n_ctx, s_tot=s_tot),
        grid=(b,),
        in_specs=[
            pl.BlockSpec((1, s_tot, UW), lambda bi: (bi, 0, 0)),
            pl.BlockSpec((CONV_K, UW), const),
            pl.BlockSpec((1, UW), const),
            pl.BlockSpec((1, UW), const),
            pl.BlockSpec((1, UW), const),
        ],
        out_specs=pl.BlockSpec((1, s_tot, UW), lambda bi: (bi, 0, 0)),
        out_shape=jax.ShapeDtypeStruct((b, s_tot, UW), BF16),
        scratch_shapes=[pltpu.VMEM((s_tot + 3 * HALO, UW), F32), pltpu.VMEM((8, SUB + 2 * HALO, UW), F32)],
        compiler_params=_params(("arbitrary",)),
        name="conv_module",
    )(glu, conv_w, conv_b, ln_g, ln_b)


def _pool_kernel(u_ref, pw_ref, ps_ref, o_ref, pad_sc, *, n_ctx, s_lat):
    zeros = jnp.zeros((HALO, UW), F32)
    pad_sc[0:HALO, :] = zeros
    pad_sc[HALO + s_lat:2 * HALO + s_lat, :] = zeros

    def fill(i, carry):
        src = pl.multiple_of(i * TM, TM)
        pad_sc[pl.ds(pl.multiple_of(src + HALO, 8), TM), :] = u_ref[0, pl.ds(pl.multiple_of(src + n_ctx, 8), TM), :]
        return carry

    lax.fori_loop(0, s_lat // TM, fill, 0)
    gw = UW // len(POOL_SIZES)

    def tile(i, carry):
        src = pl.multiple_of(i * SUB, SUB)
        win = pad_sc[pl.ds(src, SUB + 2 * HALO), :]
        t = src + lax.broadcasted_iota(jnp.int32, (SUB, 1), 0)
        outs = []
        for gi, w in enumerate(POOL_SIZES):
            lanes = slice(gi * gw, (gi + 1) * gw)
            tot = jnp.zeros((SUB, gw), F32)
            for d in range(-(w // 2), w - w // 2):
                tot = tot + win[HALO + d:HALO + d + SUB, lanes]
            lo = jnp.clip(t - w // 2, 0, s_lat)
            hi = jnp.clip(t - w // 2 + w, 0, s_lat)
            p = tot / (hi - lo).astype(F32) - win[HALO:HALO + SUB, lanes]
            outs.append(_dot(p.astype(BF16), pw_ref[gi]))
        y = jnp.concatenate(outs, axis=1) * ps_ref[...]
        o_ref[0, pl.ds(src, SUB), :] = y.astype(BF16)
        return carry

    lax.fori_loop(0, s_lat // SUB, tile, 0)


def _pool_mixer(u, pool_w, pool_scale, n_ctx):
    b, s_tot, _ = u.shape
    s_lat = s_tot - n_ctx
    gw = UW // len(POOL_SIZES)
    return pl.pallas_call(
        functools.partial(_pool_kernel, n_ctx=n_ctx, s_lat=s_lat),
        grid=(b,),
        in_specs=[
            pl.BlockSpec((1, s_tot, UW), lambda bi: (bi, 0, 0)),
            pl.BlockSpec((len(POOL_SIZES), gw, gw), lambda bi: (0, 0, 0)),
            pl.BlockSpec((1, UW), lambda bi: (0, 0)),
        ],
        out_specs=pl.BlockSpec((1, s_lat, UW), lambda bi: (bi, 0, 0)),
        out_shape=jax.ShapeDtypeStruct((b, s_lat, UW), BF16),
        scratch_shapes=[pltpu.VMEM((s_lat + 2 * HALO, UW), F32)],
        compiler_params=_params(("arbitrary",)),
        name="pool_mixer",
    )(u, pool_w, pool_scale)


def _route(logits):
    shape = logits.shape
    lane = lax.broadcasted_iota(jnp.int32, shape, 1)
    lanef = lane.astype(F32)
    is_g = (lane >= N_EXP) & (lane < N_EXP + N_GROUPS)
    big = 1e9
    lg = jnp.where(is_g, logits, -jnp.inf)
    gmax = jnp.max(lg, axis=1, keepdims=True)
    gsel = jnp.min(jnp.where(is_g & (lg == gmax), lanef, big), axis=1, keepdims=True) - float(N_EXP)
    gsum = jnp.sum(jnp.exp(lg - gmax), axis=1, keepdims=True)
    g_w = 1.0 / gsum
    in_grp = (lane < N_EXP) & ((lane >> 3).astype(F32) == gsel)
    el = jnp.where(in_grp, logits, -jnp.inf)
    emax = jnp.max(el, axis=1, keepdims=True)
    ee = jnp.exp(el - emax)
    p = ee / jnp.sum(ee, axis=1, keepdims=True)
    p1 = jnp.max(jnp.where(in_grp, p, -1.0), axis=1, keepdims=True)
    i1 = jnp.min(jnp.where(in_grp & (p == p1), lanef, big), axis=1, keepdims=True)
    rest = in_grp & (lanef != i1)
    p2 = jnp.max(jnp.where(rest, p, -1.0), axis=1, keepdims=True)
    i2 = jnp.min(jnp.where(rest & (p == p2), lanef, big), axis=1, keepdims=True)
    w1 = g_w * p1 / (p1 + p2)
    w2 = g_w * p2 / (p1 + p2)
    onehot = jnp.where((lanef == i1) | (lanef == i2), 1.0, 0.0)
    return lanef, i1, i2, w1, w2, onehot


def _outproj_kernel(a_ref, b_ref, *refs, with_ctx):
    mod_ref, w_ref, lng_ref, rw_ref, rb_ref, tri_ref, upper_ref, x1_ref, h2_ref, r_ref, g8_ref = refs[-11:]
    a = jnp.concatenate([a_ref[bb] for bb in range(PB)], axis=0)
    bmix = jnp.concatenate([b_ref[bb] for bb in range(PB)], axis=0)
    y = _dot(a, w_ref[0:QW, :]) + _dot(bmix, w_ref[QW:QW + UW, :])
    h2_parts = []
    for bb in range(PB):
        x_in = _tile_rows(refs[0], refs[1], bb) if with_ctx else refs[0][bb]
        mod = mod_ref[bb, 0]
        x1 = x_in + mod[:, 2 * D:3 * D] * y[bb * TM:(bb + 1) * TM]
        x1_ref[bb] = x1
        h2_parts.append(_modulated(x1, mod, lng_ref[...], 3, 4))
        h2_ref[bb] = h2_parts[-1].astype(BF16)
    logits = _dot3(jnp.concatenate(h2_parts, axis=0), rw_ref[...]) + rb_ref[...]
    lanef, i1, i2, w1, w2, onehot = _route(logits)
    for bb in range(PB):
        rows = slice(bb * TM, (bb + 1) * TM)
        hot = onehot[rows]
        before = _dot(tri_ref[...], hot.astype(BF16))
        cnt = jnp.sum(hot, axis=0, keepdims=True)
        g8 = jnp.floor((cnt + float(GRP - 1)) * (1.0 / GRP))
        goff = _dot(jnp.broadcast_to(g8, (8, 128)).astype(BF16), upper_ref[...])[0:1, :]
        local = before + float(GRP) * goff
        lane = lax.broadcasted_iota(jnp.int32, (TM, 128), 1).astype(F32)
        e1, e2 = i1[rows], i2[rows]
        l1 = jnp.sum(jnp.where(lane == e1, local, 0.0), axis=1, keepdims=True)
        l2 = jnp.sum(jnp.where(lane == e2, local, 0.0), axis=1, keepdims=True)
        r_ref[bb] = jnp.where(lane == 0.0, e1, jnp.where(lane == 1.0, e2, jnp.where(lane == 2.0, w1[rows], jnp.where(
            lane == 3.0, w2[rows], jnp.where(lane == 4.0, l1, jnp.where(lane == 5.0, l2, 0.0))))))
        g8_ref[bb, 0] = g8


def _out_project(a, bmix, residual, modsel, w_out, ln_g, rw, rb, tri, upper):
    b, s_out, _ = a.shape
    nt = s_out // TM
    const = lambda bi, i: (0, 0)
    with_ctx = len(residual) == 2
    sel = (lambda i: jnp.minimum(i, 1)) if with_ctx else (lambda i: 1)
    tok = lambda bi, i: (bi, i, 0)
    first_lat = residual[0].shape[1] // TM - nt
    res_specs = _token_specs(0) if with_ctx else [pl.BlockSpec((PB, TM, D), lambda bi, i: (bi, i + first_lat, 0))]
    return pl.pallas_call(
        functools.partial(_outproj_kernel, with_ctx=with_ctx),
        grid=(b // PB, nt),
        in_specs=[
            pl.BlockSpec((PB, TM, QW), tok),
            pl.BlockSpec((PB, TM, UW), tok),
        ] + res_specs + [
            pl.BlockSpec((PB, 1, 1, 6 * D), lambda bi, i: (bi, sel(i), 0, 0)),
            pl.BlockSpec((D, D), const),
            pl.BlockSpec((1, D), const),
            pl.BlockSpec((D, 128), const),
            pl.BlockSpec((1, 128), const),
            pl.BlockSpec((TM, TM), const),
            pl.BlockSpec((128, 128), const),
        ],
        out_specs=[
            pl.BlockSpec((PB, TM, D), tok),
            pl.BlockSpec((PB, TM, D), tok),
            pl.BlockSpec((PB, TM, 128), tok),
            pl.BlockSpec((PB, 1, 1, 128), lambda bi, i: (bi, i, 0, 0)),
        ],
        out_shape=[
            jax.ShapeDtypeStruct((b, s_out, D), F32),
            jax.ShapeDtypeStruct((b, s_out, D), BF16),
            jax.ShapeDtypeStruct((b, s_out, 128), F32),
            jax.ShapeDtypeStruct((b, nt, 1, 128), F32),
        ],
        compiler_params=_params(("arbitrary", "arbitrary")),
        name="out_projection_router",
    )(a, bmix, *residual, modsel, w_out, ln_g, rw, rb, tri, upper)


def _pack_halves(x, exact=False):
    half = x.shape[1] // 2
    a, b = x[:, 0:half], x[:, half:]
    if not exact:
        a, b = a.astype(BF16).astype(F32), b.astype(BF16).astype(F32)
    return lax.bitcast_convert_type(a, U32) | (lax.bitcast_convert_type(b, U32) >> 16)


def _unpack_halves(w):
    a = lax.bitcast_convert_type(w & jnp.uint32(0xFFFF0000), F32)
    b = lax.bitcast_convert_type(w << 16, F32)
    return jnp.concatenate([a, b], axis=1).astype(BF16)


def _rows_copy(src_ref, src_row, dst_ref, dst_row, rows, sem):
    rows = pl.multiple_of(rows, GRP)
    return pltpu.make_async_copy(src_ref.at[pl.ds(pl.multiple_of(src_row, GRP), rows), :],
                                 dst_ref.at[pl.ds(pl.multiple_of(dst_row, GRP), rows), :], sem)


def _wait_rows(count, src_ref, dst_ref, sem):
    @pl.when(count > 0)
    def _():
        _rows_copy(src_ref, 0, dst_ref, 0, count * GRP, sem).wait()


def _start_runs(runs, tile, copy):
    dst_ref, off_ref, len_ref = runs
    for e in range(N_EXP):
        idx = tile * N_EXP + e
        rows = len_ref[idx]

        @pl.when(rows > 0)
        def _():
            copy(dst_ref[idx], off_ref[idx], rows).start(priority=e % 2)


def _dispatch_kernel(dst_ref, off_ref, len_ref, ng_ref, zdst_ref, zlen_ref, h_ref, r_ref, xs_ref, loc, zbuf, sems,
                     zsem):
    runs = (dst_ref, off_ref, len_ref)
    i = pl.program_id(0)
    n = pl.num_programs(0)

    @pl.when(i == 0)
    def _():
        zbuf[...] = jnp.zeros(zbuf.shape, U32)
        for e in range(N_EXP):
            pl.when(zlen_ref[e] > 0)(lambda e=e: _rows_copy(zbuf, 0, xs_ref, zdst_ref[e], zlen_ref[e], zsem).start())
        for e in range(N_EXP):
            pl.when(zlen_ref[e] > 0)(lambda e=e: _rows_copy(zbuf, 0, xs_ref, zdst_ref[e], zlen_ref[e], zsem).wait())

    def run(slot):
        buf = loc.at[slot]
        sem = sems.at[slot]

        @pl.when(i >= 2)
        def _():
            _wait_rows(ng_ref[i - 2], buf, xs_ref, sem)

        rt = r_ref[...].T
        rows = lax.broadcasted_iota(jnp.int32, (LR, TM), 0).astype(F32)
        perm = jnp.where((rows == rt[4:5, :]) | (rows == rt[5:6, :]), 1.0, 0.0).astype(BF16)
        buf[...] = _pack_halves(_dot(perm, h_ref[...]), exact=True)

        _start_runs(runs, i, lambda dst, off, rows: _rows_copy(buf, off, xs_ref, dst, rows, sem))

        @pl.when(i == n - 1)
        def _():
            _wait_rows(ng_ref[i], buf, xs_ref, sem)

            @pl.when(i >= 1)
            def _():
                _wait_rows(ng_ref[i - 1], loc.at[1 - slot], xs_ref, sems.at[1 - slot])

    pl.when(i % 2 == 0)(lambda: run(0))
    pl.when(i % 2 == 1)(lambda: run(1))


def _dispatch(runs, ng, zdst, zlen, h2, rinfo, cap):
    t = h2.shape[0]
    tok = lambda i, *_: (i, 0)
    return pl.pallas_call(
        _dispatch_kernel,
        grid_spec=pltpu.PrefetchScalarGridSpec(
            num_scalar_prefetch=6,
            grid=(t // TM,),
            in_specs=[
                pl.BlockSpec((TM, D), tok),
                pl.BlockSpec((TM, 128), tok),
            ],
            out_specs=pl.BlockSpec(memory_space=pl.ANY),
            scratch_shapes=[
                pltpu.VMEM((2, LR, D // 2), U32),
                pltpu.VMEM((BLK, D // 2), U32),
                pltpu.SemaphoreType.DMA((2,)),
                pltpu.SemaphoreType.DMA,
            ],
        ),
        out_shape=jax.ShapeDtypeStruct((cap, D // 2), U32),
        compiler_params=_params(("arbitrary",)),
        name="moe_dispatch",
    )(*runs, ng, zdst, zlen, h2, rinfo)


def _expert_kernel(be_ref, nu_ref, nv_ref, x_ref, wg_ref, wu_ref, wd_ref, y_ref, wg_sc, wu_sc, wd_sc):
    del nu_ref
    i = pl.program_id(0)
    valid = nv_ref[i]
    fresh = (i == 0) | (be_ref[i] != be_ref[jnp.maximum(i - 1, 0)])

    @pl.when((valid > 0) & fresh)
    def _():
        wg_sc[...] = wg_ref[0, 0].astype(BF16)
        wu_sc[...] = wu_ref[0, 0].astype(BF16)
        wd_sc[...] = wd_ref[0, 0].astype(BF16)

    quarter = BLK // 4
    for m in range(quarter, BLK + 1, quarter):
        @pl.when((valid > m - quarter) & (valid <= m))
        def _(m=m):
            xb = _unpack_halves(x_ref[0:m, :])
            g = _dot(xb, wg_sc[...])
            u = _dot(xb, wu_sc[...])
            hid = (g * _sigmoid(g)) * u
            y_ref[0:m, :] = _pack_halves(_dot(hid.astype(BF16), wd_sc[...]))
            if m < BLK:
                y_ref[m:BLK, :] = jnp.zeros((BLK - m, D // 2), U32)

    @pl.when(valid <= 0)
    def _():
        y_ref[...] = jnp.zeros(y_ref.shape, U32)


def _experts(blk_e, n_used, n_valid, xs, wg, wu, wd, layer):
    cap = xs.shape[0]
    row = lambda i, be, nu, nv: (jnp.minimum(i, nu[0] - 1), 0)
    wsel = lambda i, be, nu, nv: (layer, be[i], 0, 0)
    return pl.pallas_call(
        _expert_kernel,
        grid_spec=pltpu.PrefetchScalarGridSpec(
            num_scalar_prefetch=3,
            grid=(cap // BLK,),
            in_specs=[
                pl.BlockSpec((BLK, D // 2), row),
                pl.BlockSpec((1, 1, D, D_EXP), wsel),
                pl.BlockSpec((1, 1, D, D_EXP), wsel),
                pl.BlockSpec((1, 1, D_EXP, D), wsel),
            ],
            out_specs=pl.BlockSpec((BLK, D // 2), lambda i, be, nu, nv: (i, 0)),
            scratch_shapes=[
                pltpu.VMEM((D, D_EXP), BF16),
                pltpu.VMEM((D, D_EXP), BF16),
                pltpu.VMEM((D_EXP, D), BF16),
            ],
        ),
        out_shape=jax.ShapeDtypeStruct((cap, D // 2), U32),
        compiler_params=_params(("arbitrary",)),
        name="moe_experts",
    )(blk_e, n_used, n_valid, xs, wg, wu, wd)


def _combine_kernel(dst_ref, off_ref, len_ref, ng_ref, y_ref, x1_ref, r_ref, mod_ref, o_ref, loc, sems):
    runs = (dst_ref, off_ref, len_ref)
    i = pl.program_id(0)
    n = pl.num_programs(0)

    def fetch(tile, slot):
        _start_runs(runs, tile, lambda dst, off, rows: _rows_copy(y_ref, dst, loc.at[slot], off, rows, sems.at[slot]))

    @pl.when(i == 0)
    def _():
        loc[...] = jnp.zeros(loc.shape, U32)
        fetch(0, 0)

    def run(slot):
        @pl.when(i + 1 < n)
        def _():
            fetch(i + 1, 1 - slot)

        _wait_rows(ng_ref[i], y_ref, loc.at[slot], sems.at[slot])
        o_ref[...] = _unsort_add(loc[slot], r_ref[...], x1_ref[...], mod_ref[0, 0][:, 5 * D:6 * D])

    pl.when(i % 2 == 0)(lambda: run(0))
    pl.when(i % 2 == 1)(lambda: run(1))


def _combine(runs, ng, ys, x1, rinfo, modsel, *, tiles_per_batch, with_ctx):
    t = x1.shape[0]
    if with_ctx:
        msel = lambda i, *_: (i // tiles_per_batch, jnp.minimum(i % tiles_per_batch, 1), 0, 0)
    else:
        msel = lambda i, *_: (i // tiles_per_batch, 1, 0, 0)
    tok = lambda i, *_: (i, 0)
    return pl.pallas_call(
        _combine_kernel,
        grid_spec=pltpu.PrefetchScalarGridSpec(
            num_scalar_prefetch=4,
            grid=(t // TM,),
            in_specs=[
                pl.BlockSpec(memory_space=pl.ANY),
                pl.BlockSpec((TM, D), tok),
                pl.BlockSpec((TM, 128), tok),
                pl.BlockSpec((1, 1, 1, 6 * D), msel),
            ],
            out_specs=pl.BlockSpec((TM, D), tok),
            scratch_shapes=[pltpu.VMEM((2, LR, D // 2), U32), pltpu.SemaphoreType.DMA((2,))],
        ),
        out_shape=jax.ShapeDtypeStruct((t, D), F32),
        compiler_params=_params(("arbitrary",)),
        name="moe_combine",
    )(*runs, ng, ys, x1, rinfo, modsel)


def _unsort_add(ysw, info, x1, g2):
    col = lax.broadcasted_iota(jnp.int32, (TM, LR), 1).astype(F32)
    pick = jnp.concatenate([jnp.where(col == info[:, 4:5], 1.0, 0.0).astype(BF16),
                            jnp.where(col == info[:, 5:6], 1.0, 0.0).astype(BF16)], axis=0)
    y12 = _dot(pick, _unpack_halves(ysw))
    return x1 + g2 * (info[:, 2:3] * y12[0:TM] + info[:, 3:4] * y12[TM:2 * TM])


def _combine_proj_kernel(dst_ref, off_ref, len_ref, ng_ref, y_ref, x1_ref, r_ref, mod0_ref, mod_ref, lng_ref, w_ref,
                         cos_ref, sin_ref, qg_ref, kg_ref, bdq_ref, bdk_ref,
                         xc_ref, q_ref, k_ref, v_ref, u_ref, loc, sems, *, nt):
    runs = (dst_ref, off_ref, len_ref)
    bp = pl.program_id(0)
    i = pl.program_id(1)
    step = bp * nt + i
    slot = step % 2

    def fetch(bp_, i_, slot_):
        for bb in range(PB):
            tile = (bp_ * PB + bb) * nt + i_
            _start_runs(runs, tile, lambda dst, off, rows, bb=bb: _rows_copy(
                y_ref, dst, loc.at[slot_, bb], off, rows, sems.at[slot_, bb]))

    @pl.when(step == 0)
    def _():
        loc[...] = jnp.zeros(loc.shape, U32)
        fetch(0, 0, 0)

    @pl.when(step + 1 < pl.num_programs(0) * nt)
    def _():
        wrap = i + 1 == nt
        fetch(jnp.where(wrap, bp + 1, bp), jnp.where(wrap, 0, i + 1), 1 - slot)

    @pl.when(step >= 0)
    def _():
        for bb in range(PB):
            tile = (bp * PB + bb) * nt + i
            _wait_rows(ng_ref[tile], y_ref, loc.at[slot, bb], sems.at[slot, bb])
            xc_ref[bb] = _unsort_add(loc[slot, bb], r_ref[bb], x1_ref[bb], mod0_ref[bb, 0][:, 5 * D:6 * D])

    @pl.when(step >= -1)
    def _():
        h = jnp.concatenate([_modulated(xc_ref[bb], mod_ref[bb, 0], lng_ref[...], 0, 1) for bb in range(PB)], axis=0)
        px = _dot(h.astype(BF16), w_ref[...])
        _proj_epilogue(px, cos_ref, sin_ref, qg_ref, kg_ref, bdq_ref, bdk_ref, q_ref, k_ref, v_ref, u_ref, False)


def _combine_project(runs, ng, ys, x1, rinfo, modsel0, modsel, ln_g, w_in, cos, sin, q_g, k_g, bdq, bdk):
    b, s_tot, _ = x1.shape
    nt = s_tot // TM
    n_in = w_in.shape[1]
    const = lambda bi, i, *_: (0, 0)
    tok = lambda bi, i, *_: (bi, i, 0)
    msel = lambda bi, i, *_: (bi, jnp.minimum(i, 1), 0, 0)
    head = lambda bi, i, *_: (bi, 0, i, 0)
    return pl.pallas_call(
        functools.partial(_combine_proj_kernel, nt=nt),
        grid_spec=pltpu.PrefetchScalarGridSpec(
            num_scalar_prefetch=4,
            grid=(b // PB, nt),
            in_specs=[
                pl.BlockSpec(memory_space=pl.ANY),
                pl.BlockSpec((PB, TM, D), tok),
                pl.BlockSpec((PB, TM, 128), tok),
                pl.BlockSpec((PB, 1, 1, 6 * D), msel),
                pl.BlockSpec((PB, 1, 1, 6 * D), msel),
                pl.BlockSpec((1, D), const),
                pl.BlockSpec((D, n_in), const),
                pl.BlockSpec((TM, 128), lambda bi, i, *_: (i, 0)),
                pl.BlockSpec((TM, 128), lambda bi, i, *_: (i, 0)),
                pl.BlockSpec((1, QW), const),
                pl.BlockSpec((1, KW), const),
                pl.BlockSpec((QW, QW), const),
                pl.BlockSpec((KW, KW), const),
            ],
            out_specs=[
                pl.BlockSpec((PB, TM, D), tok),
                pl.BlockSpec((PB, NH, TM, HD), head),
                pl.BlockSpec((PB, NKV, TM, HD), head),
                pl.BlockSpec((PB, NKV, TM, 2 * HD), head),
                pl.BlockSpec((PB, TM, UW), tok),
            ],
            scratch_shapes=[pltpu.VMEM((2, PB, LR, D // 2), U32), pltpu.SemaphoreType.DMA((2, PB))],
        ),
        out_shape=[
            jax.ShapeDtypeStruct((b, s_tot, D), F32),
            jax.ShapeDtypeStruct((b, NH, s_tot, HD), BF16),
            jax.ShapeDtypeStruct((b, NKV, s_tot, HD), BF16),
            jax.ShapeDtypeStruct((b, NKV, s_tot, 2 * HD), BF16),
            jax.ShapeDtypeStruct((b, s_tot, UW), F32),
        ],
        compiler_params=_params(("arbitrary", "arbitrary")),
        name="moe_combine_in_projection",
    )(*runs, ng, ys, x1, rinfo, modsel0, modsel, ln_g, w_in, cos, sin, q_g, k_g, bdq, bdk)


def _moe(x1, h2, rinfo, g8, modsel, wg, wu, wd, layer, *, with_ctx, combine=True):
    b, s, _ = x1.shape
    t = b * s
    n_tiles = t // TM
    i32 = jnp.int32
    grp = g8.reshape(n_tiles, 128)[:, 0:N_EXP].astype(i32)
    n_blk = -(-(2 * t + n_tiles * N_EXP * (GRP - 1) + N_EXP * (BLK - 1)) // BLK)
    cap = n_blk * BLK
    goff_end = jnp.cumsum(grp, axis=1)
    ng = goff_end[:, N_EXP - 1].astype(i32)
    rows_e = GRP * jnp.sum(grp, axis=0)
    padded = (rows_e + BLK - 1) // BLK * BLK
    pend = jnp.cumsum(padded)
    pstart = pend - padded
    run_dst = (pstart[None, :] + GRP * (jnp.cumsum(grp, axis=0) - grp)).reshape(-1).astype(i32)
    run_off = (GRP * (goff_end - grp)).reshape(-1).astype(i32)
    run_len = (GRP * grp).reshape(-1).astype(i32)
    runs = (run_dst, run_off, run_len)
    zdst = (pstart + rows_e).astype(i32)
    zlen = (padded - rows_e).astype(i32)
    blk_e = jnp.minimum(jnp.sum((pend[None, :] <= (jnp.arange(n_blk, dtype=i32) * BLK)[:, None]).astype(i32), axis=1),
                        N_EXP - 1).astype(i32)
    n_used = (pend[N_EXP - 1:N_EXP] // BLK).astype(i32)
    blk_row0 = jnp.arange(n_blk, dtype=i32) * BLK
    row_end = jnp.sum(jnp.where(blk_e[:, None] == jnp.arange(N_EXP, dtype=i32)[None, :], zdst[None, :], 0), axis=1)
    n_valid = jnp.clip(row_end - blk_row0, 0, BLK).astype(i32)
    info = rinfo.reshape(t, 128)
    xs = _dispatch(runs, ng, zdst, zlen, h2.reshape(t, D), info, cap)
    ys = _experts(blk_e, n_used, n_valid, xs, wg, wu, wd, layer)
    if not combine:
        return runs, ng, ys
    out = _combine(runs, ng, ys, x1.reshape(t, D), info, modsel, tiles_per_batch=s // TM, with_ctx=with_ctx)
    return out.reshape(b, s, D)


def _rope_tables(n_ctx, s_lat):
    rows = s_lat // GRID_W
    row = jnp.repeat(jnp.arange(rows, dtype=F32), GRID_W)
    col = jnp.tile(jnp.arange(GRID_W, dtype=F32), rows)
    n_freq = HD // 4
    inv = ROPE_THETA ** (-jnp.arange(n_freq, dtype=F32) / n_freq)
    ang = jnp.concatenate([row[:, None] * inv, col[:, None] * inv], axis=-1)
    cos = jnp.concatenate([jnp.ones((n_ctx, HD // 2), F32), jnp.cos(ang)], axis=0)
    sin = jnp.concatenate([jnp.zeros((n_ctx, HD // 2), F32), jnp.sin(ang)], axis=0)
    cos = jnp.concatenate([cos, cos, cos, cos], axis=1)
    sin = jnp.concatenate([-sin, sin, -sin, sin], axis=1)
    return cos, sin


def _block_diag_ones(n):
    i = jnp.arange(n) // HD
    return (i[:, None] == i[None, :]).astype(BF16)


def kernel(x, c, ctx, c_ctx, mod_w, mod_b, ln1_g, ln2_g, w_in_ab, w_out_ab, q_norm_a, k_norm_a, conv_w, conv_b,
           conv_ln_g, conv_ln_b, w_in_cd, w_out_cd, q_norm_c, k_norm_c, sink_c, pool_w, pool_scale,
           rt_grp_w, rt_grp_b, rt_exp_w, rt_exp_b, ex_gate, ex_up, ex_down):
    b, s_lat, _ = x.shape
    n_ctx = ctx.shape[1]
    assert n_ctx == TM and s_lat % TM == 0 and b <= 8 and b % PB == 0

    cc = jnp.zeros((16, D), F32).at[0:b].set(c).at[8].set(c_ctx)
    mod = _modulation(cc, mod_w, mod_b)

    def modsel(l):
        return jnp.stack([jnp.broadcast_to(mod[l, 8], (b, 6 * D)), mod[l, 0:b]], axis=1).reshape(b, 2, 1, 6 * D)

    cos, sin = _rope_tables(n_ctx, s_lat)
    bdq = _block_diag_ones(QW)
    bdk = _block_diag_ones(KW)
    tri = (jnp.arange(TM)[:, None] > jnp.arange(TM)[None, :]).astype(BF16)
    upper = (jnp.arange(128)[:, None] < jnp.arange(128)[None, :]).astype(BF16)

    def router(l):
        rw = jnp.zeros((D, 128), F32).at[:, 0:N_EXP].set(rt_exp_w[l]).at[:, N_EXP:N_EXP + N_GROUPS].set(rt_grp_w[l])
        rb = jnp.zeros((1, 128), F32).at[0, 0:N_EXP].set(rt_exp_b[l]).at[0, N_EXP:N_EXP + N_GROUPS].set(rt_grp_b[l])
        return rw, rb

    def tile_gain(g, n):
        return jnp.tile(g, n).reshape(1, n * HD)

    s_tot = n_ctx + s_lat

    ms = modsel(0)
    q, k, v, glu = _project(ctx, x, 0, s_tot, ms, ln1_g[0:1], w_in_ab[0].astype(BF16), cos, sin,
                            tile_gain(q_norm_a[0], NH), tile_gain(k_norm_a[0], NKV), bdq, bdk, glu=True)
    att = _dense_attention(q, k, v, n_ctx)
    cv = _conv_module(glu, conv_w[0], conv_b[0:1], conv_ln_g[0:1], conv_ln_b[0:1], n_ctx)
    rw, rb = router(0)
    x1, h2, rinfo, g8 = _out_project(att, cv, (ctx, x), ms, w_out_ab[0].astype(BF16), ln2_g[0:1], rw, rb, tri, upper)
    runs, ng, ys = _moe(x1, h2, rinfo, g8, ms, ex_gate, ex_up, ex_down, 0, with_ctx=True, combine=False)

    ms0, ms = ms, modsel(1)
    xc, q, k, v, u = _combine_project(runs, ng, ys, x1, rinfo, ms0, ms, ln1_g[1:2], w_in_cd[0].astype(BF16), cos, sin,
                                      tile_gain(q_norm_c[0], NH), tile_gain(k_norm_c[0], NKV), bdq, bdk)
    att = _window_attention(sink_c[0], q, k, v, n_ctx)
    pm = _pool_mixer(u, pool_w[0].astype(BF16), pool_scale[0:1], n_ctx)
    rw, rb = router(1)
    x1, h2, rinfo, g8 = _out_project(att, pm, (xc,), ms, w_out_cd[0].astype(BF16), ln2_g[1:2], rw, rb, tri, upper)
    return _moe(x1, h2, rinfo, g8, ms, ex_gate, ex_up, ex_down, 1, with_ctx=False)
```

```python
import functools

import jax
import jax.numpy as jnp
from jax import lax
from jax.experimental import pallas as pl
from jax.experimental.pallas import tpu as pltpu

F32 = jnp.float32
BF16 = jnp.bfloat16
U32 = jnp.uint32

D = 1024
HD = 64
NH = 8
NKV = 2
GQ = NH // NKV
QW = NH * HD
KW = NKV * HD
UW = 512
EPS = 1e-6
ROPE_THETA = 10000.0
GRID_W = 64
CONV_K = 31
WINDOW = 128
POOL_SIZES = (2, 4, 8, 16)
N_GROUPS = 4
PER_GROUP = 8
N_EXP = N_GROUPS * PER_GROUP
D_EXP = D // 2

TM = 256
PB = 4
TQ = 256
TQD = 256
TKL = 512
SUB = 64
HALO = 16
BLK = 1024
GRP = 8
NG = 96
LR = NG * GRP
VMEM_LIMIT = 56 * 1024 * 1024


def _sigmoid(x):
    return 1.0 / (1.0 + jnp.exp(-x))


def _dot(a, b):
    return jnp.dot(a, b, preferred_element_type=F32)


def _dot_nt(a, b):
    return lax.dot_general(a, b, (((1,), (1,)), ((), ())), preferred_element_type=F32)


def _split(a):
    hi = a.astype(BF16)
    lo = (a - hi.astype(F32)).astype(BF16)
    return hi, lo


def _dot3(a, w):
    a_hi, a_lo = _split(a)
    w_hi, w_lo = _split(w)
    return _dot(a_hi, w_hi) + _dot(a_lo, w_hi) + _dot(a_hi, w_lo)


def _rmsnorm(x, g):
    return x * lax.rsqrt(jnp.mean(x * x, axis=-1, keepdims=True) + EPS) * g


def _params(sem, vmem=VMEM_LIMIT):
    return pltpu.CompilerParams(dimension_semantics=sem, vmem_limit_bytes=vmem)


def _mod_kernel(c_ref, w_ref, b_ref, o_ref):
    a = c_ref[...]
    a = a * _sigmoid(a)
    o_ref[0] = _dot3(a, w_ref[0]) + b_ref[0]


def _modulation(cc, mod_w, mod_b):
    depth = mod_w.shape[0]
    tn = 1024
    return pl.pallas_call(
        _mod_kernel,
        grid=(depth, 6 * D // tn),
        in_specs=[
            pl.BlockSpec((16, D), lambda l, j: (0, 0)),
            pl.BlockSpec((1, D, tn), lambda l, j: (l, 0, j)),
            pl.BlockSpec((1, 1, tn), lambda l, j: (l, 0, j)),
        ],
        out_specs=pl.BlockSpec((1, 16, tn), lambda l, j: (l, 0, j)),
        out_shape=jax.ShapeDtypeStruct((depth, 16, 6 * D), F32),
        compiler_params=_params(("arbitrary", "arbitrary")),
        name="modulation",
    )(cc, mod_w, mod_b.reshape(depth, 1, 6 * D))


def _swap_halves(t):
    w = t.shape[1]
    lane = lax.broadcasted_iota(jnp.int32, t.shape, 1)
    first = (lane & (HD - 1)) < (HD // 2)
    return jnp.where(first, pltpu.roll(t, w - HD // 2, 1), pltpu.roll(t, HD // 2, 1))


def _head_norm_rope(t, bd, g, cos, sin, scale):
    hi, lo = _split(t * t)
    ssq = _dot(hi, bd) + _dot(lo, bd)
    tn = t * lax.rsqrt(ssq * (1.0 / HD) + EPS) * g
    n = t.shape[1] // 128
    if n > 1:
        cos = jnp.concatenate([cos] * n, axis=1)
        sin = jnp.concatenate([sin] * n, axis=1)
    out = tn * cos + _swap_halves(tn) * sin
    return out * scale if scale != 1.0 else out


def _tile_rows(c_ref, x_ref, bb):
    return jnp.where(pl.program_id(1) == 0, c_ref[bb], x_ref[bb])


def _modulated(x, mod, lng, shift_at, scale_at):
    return _rmsnorm(x, lng) * (1.0 + mod[:, scale_at * D:(scale_at + 1) * D]) + mod[:, shift_at * D:(shift_at + 1) * D]


def _proj_kernel(c_ref, x_ref, mod_ref, lng_ref, w_ref, cos_ref, sin_ref, qg_ref, kg_ref, bdq_ref, bdk_ref,
                 q_ref, k_ref, v_ref, u_ref, *, glu):
    h = jnp.concatenate([_modulated(_tile_rows(c_ref, x_ref, bb), mod_ref[bb, 0], lng_ref[...], 0, 1)
                         for bb in range(PB)], axis=0)
    px = _dot(h.astype(BF16), w_ref[...])
    _proj_epilogue(px, cos_ref, sin_ref, qg_ref, kg_ref, bdq_ref, bdk_ref, q_ref, k_ref, v_ref, u_ref, glu)


def _proj_epilogue(px, cos_ref, sin_ref, qg_ref, kg_ref, bdq_ref, bdk_ref, q_ref, k_ref, v_ref, u_ref, glu):
    cos = jnp.concatenate([cos_ref[...]] * PB, axis=0)
    sin = jnp.concatenate([sin_ref[...]] * PB, axis=0)
    q = _head_norm_rope(px[:, 0:QW], bdq_ref[...], qg_ref[...], cos, sin, HD ** -0.5)
    k = _head_norm_rope(px[:, QW:QW + KW], bdk_ref[...], kg_ref[...], cos, sin, 1.0)
    v = px[:, QW + KW:QW + 2 * KW]
    lane = lax.broadcasted_iota(jnp.int32, (TM, KW), 1)
    ones_col = jnp.where(lane == HD, 1.0, 0.0)
    o_u = QW + 2 * KW
    if glu:
        u = px[:, o_u:o_u + UW] * _sigmoid(px[:, o_u + UW:o_u + 2 * UW])
    else:
        u = px[:, o_u:o_u + UW]
    for bb in range(PB):
        rows = slice(bb * TM, (bb + 1) * TM)
        for hh in range(NH):
            q_ref[bb, hh] = q[rows, hh * HD:(hh + 1) * HD].astype(BF16)
        for j in range(NKV):
            k_ref[bb, j] = k[rows, j * HD:(j + 1) * HD].astype(BF16)
            vj = v[rows] if j == 0 else pltpu.roll(v[rows], KW - j * HD, 1)
            v_ref[bb, j] = jnp.where(lane < HD, vj, ones_col).astype(BF16)
        u_ref[bb] = u[rows]


def _token_specs(lat_first):
    return [pl.BlockSpec((PB, TM, D), lambda bi, i: (bi, 0, 0)),
            pl.BlockSpec((PB, TM, D), lambda bi, i: (bi, jnp.maximum(i - 1, 0) + lat_first, 0))]


def _project(ctx_rows, lat_rows, lat_first, s_tot, modsel, ln_g, w_in, cos, sin, q_g, k_g, bdq, bdk, *, glu):
    b = ctx_rows.shape[0]
    n_in = w_in.shape[1]
    nt = s_tot // TM
    const = lambda bi, i: (0, 0)
    return pl.pallas_call(
        functools.partial(_proj_kernel, glu=glu),
        grid=(b // PB, nt),
        in_specs=_token_specs(lat_first) + [
            pl.BlockSpec((PB, 1, 1, 6 * D), lambda bi, i: (bi, jnp.minimum(i, 1), 0, 0)),
            pl.BlockSpec((1, D), const),
            pl.BlockSpec((D, n_in), const),
            pl.BlockSpec((TM, 128), lambda bi, i: (i, 0)),
            pl.BlockSpec((TM, 128), lambda bi, i: (i, 0)),
            pl.BlockSpec((1, QW), const),
            pl.BlockSpec((1, KW), const),
            pl.BlockSpec((QW, QW), const),
            pl.BlockSpec((KW, KW), const),
        ],
        out_specs=[
            pl.BlockSpec((PB, NH, TM, HD), lambda bi, i: (bi, 0, i, 0)),
            pl.BlockSpec((PB, NKV, TM, HD), lambda bi, i: (bi, 0, i, 0)),
            pl.BlockSpec((PB, NKV, TM, 2 * HD), lambda bi, i: (bi, 0, i, 0)),
            pl.BlockSpec((PB, TM, UW), lambda bi, i: (bi, i, 0)),
        ],
        out_shape=[
            jax.ShapeDtypeStruct((b, NH, s_tot, HD), BF16),
            jax.ShapeDtypeStruct((b, NKV, s_tot, HD), BF16),
            jax.ShapeDtypeStruct((b, NKV, s_tot, 2 * HD), BF16),
            jax.ShapeDtypeStruct((b, s_tot, UW), F32),
        ],
        compiler_params=_params(("arbitrary", "arbitrary")),
        name="in_projection",
    )(ctx_rows, lat_rows, modsel, ln_g, w_in, cos, sin, q_g, k_g, bdq, bdk)


def _merge_heads(o):
    tq = o.shape[0] // GQ
    return jnp.concatenate([o[g * tq:(g + 1) * tq] for g in range(GQ)], axis=1)


def _lane_max(s):
    return functools.reduce(jnp.maximum, [s[:, j * 128:(j + 1) * 128] for j in range(s.shape[1] // 128)])


def _dense_attn_kernel(q_ref, k_ref, v_ref, o_ref, sc_sc, sl_sc, m_sc, *, n_ctx, s_tot):
    qi = pl.program_id(2)
    q = q_ref[0].reshape(GQ * TQD, HD)
    n_lat = (s_tot - n_ctx) // TKL

    def chunks(n):
        spans = [(0, n_ctx)] + [(n_ctx + c * TKL, n_ctx + (c + 1) * TKL) for c in range(n)]
        return list(zip(spans, [sc_sc] + [sl_sc.at[c] for c in range(n)]))

    def scores(n):
        for i, ((lo, hi), slot) in enumerate(chunks(n)):
            s = _dot_nt(q, k_ref[0, 0, lo:hi, :])
            slot[...] = s
            m_sc[i] = _lane_max(s)
        m_lane = functools.reduce(jnp.maximum, [m_sc[i] for i in range(n + 1)])
        m_sc[0] = jnp.broadcast_to(jnp.max(m_lane, axis=1, keepdims=True), m_lane.shape)

    def weighted(n):
        acc = None
        for (lo, hi), slot in chunks(n):
            m = jnp.concatenate([m_sc[0]] * ((hi - lo) // 128), axis=1)
            part = _dot(jnp.exp(slot[...] - m).astype(BF16), v_ref[0, 0, lo:hi, :])
            acc = part if acc is None else acc + part
        o_ref[0] = _merge_heads(acc[:, 0:HD] / acc[:, HD:HD + 1]).astype(BF16)

    is_lat = qi * TQD >= n_ctx
    pl.when(is_lat)(lambda: scores(n_lat))
    pl.when(jnp.logical_not(is_lat))(lambda: scores(0))
    pl.when(is_lat)(lambda: weighted(n_lat))
    pl.when(jnp.logical_not(is_lat))(lambda: weighted(0))


def _dense_attention(q, k, v, n_ctx):
    b, _, s_tot, _ = q.shape
    rows = GQ * TQD
    return pl.pallas_call(
        functools.partial(_dense_attn_kernel, n_ctx=n_ctx, s_tot=s_tot),
        grid=(b, NKV, s_tot // TQD),
        in_specs=[
            pl.BlockSpec((1, GQ, TQD, HD), lambda bi, j, i: (bi, j, i, 0)),
            pl.BlockSpec((1, 1, s_tot, HD), lambda bi, j, i: (bi, j, 0, 0)),
            pl.BlockSpec((1, 1, s_tot, 2 * HD), lambda bi, j, i: (bi, j, 0, 0)),
        ],
        out_specs=pl.BlockSpec((1, TQD, GQ * HD), lambda bi, j, i: (bi, i, j)),
        out_shape=jax.ShapeDtypeStruct((b, s_tot, QW), BF16),
        scratch_shapes=[
            pltpu.VMEM((rows, n_ctx), F32),
            pltpu.VMEM(((s_tot - n_ctx) // TKL, rows, TKL), F32),
            pltpu.VMEM(((s_tot - n_ctx) // TKL + 1, rows, 128), F32),
        ],
        compiler_params=_params(("arbitrary", "arbitrary", "arbitrary")),
        name="dense_attention",
    )(q, k, v)


def _window_attn_kernel(sink_ref, q_ref, k_ref, v_ref, o_ref, sc_sc, sw_sc, m_sc, e_sc, *, n_ctx, s_tot):
    qi = pl.program_id(1)
    span = TQ + 2 * WINDOW
    q0 = n_ctx + qi * TQ
    start = pl.multiple_of(jnp.clip(q0 - WINDOW, n_ctx, s_tot - span), 128)
    row = lax.broadcasted_iota(jnp.int32, (TQ, span), 0)
    col = lax.broadcasted_iota(jnp.int32, (TQ, span), 1)
    band = jnp.where(jnp.abs((q0 - start) + row - col) <= WINDOW, 0.0, -jnp.inf)
    band = jnp.concatenate([band] * GQ, axis=0)
    for j in range(NKV):
        q = q_ref[0, j * GQ:(j + 1) * GQ].reshape(GQ * TQ, HD)
        s_c = _dot_nt(q, k_ref[0, j, 0:n_ctx, :])
        s_w = _dot_nt(q, k_ref[0, j, pl.ds(start, span), :]) + band
        sc_sc[j] = s_c
        sw_sc[j] = s_w
        sink = jnp.concatenate([jnp.full((TQ, 128), sink_ref[j * GQ + g], F32) for g in range(GQ)], axis=0)
        m_lane = jnp.maximum(_lane_max(s_c), _lane_max(s_w))
        m_sc[j] = jnp.maximum(jnp.broadcast_to(jnp.max(m_lane, axis=1, keepdims=True), m_lane.shape), sink)
        e_sc[j] = jnp.exp(sink - m_sc[j])
    for j in range(NKV):
        m = m_sc[j]
        p_c = jnp.exp(sc_sc[j] - jnp.concatenate([m] * (n_ctx // 128), axis=1)).astype(BF16)
        p_w = jnp.exp(sw_sc[j] - jnp.concatenate([m] * (span // 128), axis=1)).astype(BF16)
        acc = _dot(p_c, v_ref[0, j, 0:n_ctx, :]) + _dot(p_w, v_ref[0, j, pl.ds(start, span), :])
        l = acc[:, HD:HD + 1] + e_sc[j][:, 0:1]
        o_ref[0, :, j * GQ * HD:(j + 1) * GQ * HD] = _merge_heads(acc[:, 0:HD] / l).astype(BF16)


def _window_attention(sink, q, k, v, n_ctx):
    b, _, s_tot, _ = q.shape
    s_lat = s_tot - n_ctx
    off = n_ctx // TQ
    return pl.pallas_call(
        functools.partial(_window_attn_kernel, n_ctx=n_ctx, s_tot=s_tot),
        grid_spec=pltpu.PrefetchScalarGridSpec(
            num_scalar_prefetch=1,
            grid=(b, s_lat // TQ),
            in_specs=[
                pl.BlockSpec((1, NH, TQ, HD), lambda bi, i, sk: (bi, 0, i + off, 0)),
                pl.BlockSpec((1, NKV, s_tot, HD), lambda bi, i, sk: (bi, 0, 0, 0)),
                pl.BlockSpec((1, NKV, s_tot, 2 * HD), lambda bi, i, sk: (bi, 0, 0, 0)),
            ],
            out_specs=pl.BlockSpec((1, TQ, QW), lambda bi, i, sk: (bi, i, 0)),
            scratch_shapes=[
                pltpu.VMEM((NKV, GQ * TQ, n_ctx), F32),
                pltpu.VMEM((NKV, GQ * TQ, TQ + 2 * WINDOW), F32),
                pltpu.VMEM((NKV, GQ * TQ, 128), F32),
                pltpu.VMEM((NKV, GQ * TQ, 128), F32),
            ],
        ),
        out_shape=jax.ShapeDtypeStruct((b, s_lat, QW), BF16),
        compiler_params=_params(("arbitrary", "arbitrary")),
        name="window_attention",
    )(sink, q, k, v)


def _conv_kernel(g_ref, cw_ref, cb_ref, lg_ref, lb_ref, o_ref, pad_sc, win_sc, *, n_ctx, s_tot):
    zeros = jnp.zeros((HALO, UW), F32)
    pad_sc[0:HALO, :] = zeros
    pad_sc[HALO + n_ctx:2 * HALO + n_ctx, :] = zeros
    pad_sc[2 * HALO + s_tot:3 * HALO + s_tot, :] = zeros

    def fill(i, carry):
        src = pl.multiple_of(i * TM, TM)
        dst = pl.multiple_of(src + HALO + jnp.where(src >= n_ctx, HALO, 0), 8)
        pad_sc[pl.ds(dst, TM), :] = g_ref[0, pl.ds(src, TM), :]
        return carry

    lax.fori_loop(0, s_tot // TM, fill, 0)
    half = CONV_K // 2

    def tile(i, carry):
        src = pl.multiple_of(i * SUB, SUB)
        base = pl.multiple_of(src + jnp.where(src >= n_ctx, HALO, 0), 8)
        win = pad_sc[pl.ds(base, SUB + 2 * HALO), :]
        keep = SUB + 2 * HALO - 8
        for r in range(8):
            win_sc[r, 0:keep, :] = win[r:r + keep, :]
        acc = jnp.zeros((SUB, UW), F32) + cb_ref[...]
        for t in range(CONV_K):
            a, r = divmod(HALO - half + t, 8)
            acc = acc + win_sc[r, 8 * a:8 * a + SUB, :] * cw_ref[t:t + 1, :]
        mu = jnp.mean(acc, axis=-1, keepdims=True)
        xc = acc - mu
        var = jnp.mean(xc * xc, axis=-1, keepdims=True)
        yn = xc * lax.rsqrt(var + EPS) * lg_ref[...] + lb_ref[...]
        o_ref[0, pl.ds(src, SUB), :] = (yn * _sigmoid(yn)).astype(BF16)
        return carry

    lax.fori_loop(0, s_tot // SUB, tile, 0)


def _conv_module(glu, conv_w, conv_b, ln_g, ln_b, n_ctx):
    b, s_tot, _ = glu.shape
    const = lambda bi: (0, 0)
    return pl.pallas_call(
        functools.partial(_conv_kernel, n_ctx=n_ctx, s_tot=s_tot),
        grid=(b,),
        in_specs=[
            pl.BlockSpec((1, s_tot, UW), lambda bi: (bi, 0, 0)),
            pl.BlockSpec((CONV_K, UW), const),
            pl.BlockSpec((1, UW), const),
            pl.BlockSpec((1, UW), const),
            pl.BlockSpec((1, UW), const),
        ],
        out_specs=pl.BlockSpec((1, s_tot, UW), lambda bi: (bi, 0, 0)),
        out_shape=jax.ShapeDtypeStruct((b, s_tot, UW), BF16),
        scratch_shapes=[pltpu.VMEM((s_tot + 3 * HALO, UW), F32), pltpu.VMEM((8, SUB + 2 * HALO, UW), F32)],
        compiler_params=_params(("arbitrary",)),
        name="conv_module",
    )(glu, conv_w, conv_b, ln_g, ln_b)


def _pool_kernel(u_ref, pw_ref, ps_ref, o_ref, pad_sc, *, n_ctx, s_lat):
    zeros = jnp.zeros((HALO, UW), F32)
    pad_sc[0:HALO, :] = zeros
    pad_sc[HALO + s_lat:2 * HALO + s_lat, :] = zeros

    def fill(i, carry):
        src = pl.multiple_of(i * TM, TM)
        pad_sc[pl.ds(pl.multiple_of(src + HALO, 8), TM), :] = u_ref[0, pl.ds(pl.multiple_of(src + n_ctx, 8), TM), :]
        return carry

    lax.fori_loop(0, s_lat // TM, fill, 0)
    gw = UW // len(POOL_SIZES)

    def tile(i, carry):
        src = pl.multiple_of(i * SUB, SUB)
        win = pad_sc[pl.ds(src, SUB + 2 * HALO), :]
        t = src + lax.broadcasted_iota(jnp.int32, (SUB, 1), 0)
        outs = []
        for gi, w in enumerate(POOL_SIZES):
            lanes = slice(gi * gw, (gi + 1) * gw)
            tot = jnp.zeros((SUB, gw), F32)
            for d in range(-(w // 2), w - w // 2):
                tot = tot + win[HALO + d:HALO + d + SUB, lanes]
            lo = jnp.clip(t - w // 2, 0, s_lat)
            hi = jnp.clip(t - w // 2 + w, 0, s_lat)
            p = tot / (hi - lo).astype(F32) - win[HALO:HALO + SUB, lanes]
            outs.append(_dot(p.astype(BF16), pw_ref[gi]))
        y = jnp.concatenate(outs, axis=1) * ps_ref[...]
        o_ref[0, pl.ds(src, SUB), :] = y.astype(BF16)
        return carry

    lax.fori_loop(0, s_lat // SUB, tile, 0)


def _pool_mixer(u, pool_w, pool_scale, n_ctx):
    b, s_tot, _ = u.shape
    s_lat = s_tot - n_ctx
    gw = UW // len(POOL_SIZES)
    return pl.pallas_call(
        functools.partial(_pool_kernel, n_ctx=n_ctx, s_lat=s_lat),
        grid=(b,),
        in_specs=[
            pl.BlockSpec((1, s_tot, UW), lambda bi: (bi, 0, 0)),
            pl.BlockSpec((len(POOL_SIZES), gw, gw), lambda bi: (0, 0, 0)),
            pl.BlockSpec((1, UW), lambda bi: (0, 0)),
        ],
        out_specs=pl.BlockSpec((1, s_lat, UW), lambda bi: (bi, 0, 0)),
        out_shape=jax.ShapeDtypeStruct((b, s_lat, UW), BF16),
        scratch_shapes=[pltpu.VMEM((s_lat + 2 * HALO, UW), F32)],
        compiler_params=_params(("arbitrary",)),
        name="pool_mixer",
    )(u, pool_w, pool_scale)


def _route(logits):
    shape = logits.shape
    lane = lax.broadcasted_iota(jnp.int32, shape, 1)
    lanef = lane.astype(F32)
    is_g = (lane >= N_EXP) & (lane < N_EXP + N_GROUPS)
    big = 1e9
    lg = jnp.where(is_g, logits, -jnp.inf)
    gmax = jnp.max(lg, axis=1, keepdims=True)
    gsel = jnp.min(jnp.where(is_g & (lg == gmax), lanef, big), axis=1, keepdims=True) - float(N_EXP)
    gsum = jnp.sum(jnp.exp(lg - gmax), axis=1, keepdims=True)
    g_w = 1.0 / gsum
    in_grp = (lane < N_EXP) & ((lane >> (PER_GROUP.bit_length() - 1)).astype(F32) == gsel)
    el = jnp.where(in_grp, logits, -jnp.inf)
    emax = jnp.max(el, axis=1, keepdims=True)
    ee = jnp.exp(el - emax)
    p = ee / jnp.sum(ee, axis=1, keepdims=True)
    p1 = jnp.max(jnp.where(in_grp, p, -1.0), axis=1, keepdims=True)
    i1 = jnp.min(jnp.where(in_grp & (p == p1), lanef, big), axis=1, keepdims=True)
    rest = in_grp & (lanef != i1)
    p2 = jnp.max(jnp.where(rest, p, -1.0), axis=1, keepdims=True)
    i2 = jnp.min(jnp.where(rest & (p == p2), lanef, big), axis=1, keepdims=True)
    w1 = g_w * p1 / (p1 + p2)
    w2 = g_w * p2 / (p1 + p2)
    onehot = jnp.where((lanef == i1) | (lanef == i2), 1.0, 0.0)
    return lanef, i1, i2, w1, w2, onehot


def _outproj_kernel(a_ref, b_ref, *refs, with_ctx):
    mod_ref, w_ref, lng_ref, rw_ref, rb_ref, tri_ref, upper_ref, x1_ref, h2_ref, r_ref, g8_ref = refs[-11:]
    a = jnp.concatenate([a_ref[bb] for bb in range(PB)], axis=0)
    bmix = jnp.concatenate([b_ref[bb] for bb in range(PB)], axis=0)
    y = _dot(a, w_ref[0:QW, :]) + _dot(bmix, w_ref[QW:QW + UW, :])
    h2_parts = []
    for bb in range(PB):
        x_in = _tile_rows(refs[0], refs[1], bb) if with_ctx else refs[0][bb]
        mod = mod_ref[bb, 0]
        x1 = x_in + mod[:, 2 * D:3 * D] * y[bb * TM:(bb + 1) * TM]
        x1_ref[bb] = x1
        h2_parts.append(_modulated(x1, mod, lng_ref[...], 3, 4))
        h2_ref[bb] = h2_parts[-1].astype(BF16)
    logits = _dot3(jnp.concatenate(h2_parts, axis=0), rw_ref[...]) + rb_ref[...]
    lanef, i1, i2, w1, w2, onehot = _route(logits)
    for bb in range(PB):
        rows = slice(bb * TM, (bb + 1) * TM)
        hot = onehot[rows]
        before = _dot(tri_ref[...], hot.astype(BF16))
        cnt = jnp.sum(hot, axis=0, keepdims=True)
        g8 = jnp.floor((cnt + float(GRP - 1)) * (1.0 / GRP))
        goff = _dot(jnp.broadcast_to(g8, (8, 128)).astype(BF16), upper_ref[...])[0:1, :]
        local = before + float(GRP) * goff
        lane = lax.broadcasted_iota(jnp.int32, (TM, 128), 1).astype(F32)
        e1, e2 = i1[rows], i2[rows]
        l1 = jnp.sum(jnp.where(lane == e1, local, 0.0), axis=1, keepdims=True)
        l2 = jnp.sum(jnp.where(lane == e2, local, 0.0), axis=1, keepdims=True)
        r_ref[bb] = jnp.where(lane == 0.0, e1, jnp.where(lane == 1.0, e2, jnp.where(lane == 2.0, w1[rows], jnp.where(
            lane == 3.0, w2[rows], jnp.where(lane == 4.0, l1, jnp.where(lane == 5.0, l2, 0.0))))))
        g8_ref[bb, 0] = g8


def _out_project(a, bmix, residual, modsel, w_out, ln_g, rw, rb, tri, upper):
    b, s_out, _ = a.shape
    nt = s_out // TM
    const = lambda bi, i: (0, 0)
    with_ctx = len(residual) == 2
    sel = (lambda i: jnp.minimum(i, 1)) if with_ctx else (lambda i: 1)
    tok = lambda bi, i: (bi, i, 0)
    first_lat = residual[0].shape[1] // TM - nt
    res_specs = _token_specs(0) if with_ctx else [pl.BlockSpec((PB, TM, D), lambda bi, i: (bi, i + first_lat, 0))]
    return pl.pallas_call(
        functools.partial(_outproj_kernel, with_ctx=with_ctx),
        grid=(b // PB, nt),
        in_specs=[
            pl.BlockSpec((PB, TM, QW), tok),
            pl.BlockSpec((PB, TM, UW), tok),
        ] + res_specs + [
            pl.BlockSpec((PB, 1, 1, 6 * D), lambda bi, i: (bi, sel(i), 0, 0)),
            pl.BlockSpec((D, D), const),
            pl.BlockSpec((1, D), const),
            pl.BlockSpec((D, 128), const),
            pl.BlockSpec((1, 128), const),
            pl.BlockSpec((TM, TM), const),
            pl.BlockSpec((128, 128), const),
        ],
        out_specs=[
            pl.BlockSpec((PB, TM, D), tok),
            pl.BlockSpec((PB, TM, D), tok),
            pl.BlockSpec((PB, TM, 128), tok),
            pl.BlockSpec((PB, 1, 1, 128), lambda bi, i: (bi, i, 0, 0)),
        ],
        out_shape=[
            jax.ShapeDtypeStruct((b, s_out, D), F32),
            jax.ShapeDtypeStruct((b, s_out, D), BF16),
            jax.ShapeDtypeStruct((b, s_out, 128), F32),
            jax.ShapeDtypeStruct((b, nt, 1, 128), F32),
        ],
        compiler_params=_params(("arbitrary", "arbitrary")),
        name="out_projection_router",
    )(a, bmix, *residual, modsel, w_out, ln_g, rw, rb, tri, upper)


def _pack_halves(x, exact=False):
    half = x.shape[1] // 2
    a, b = x[:, 0:half], x[:, half:]
    if not exact:
        a, b = a.astype(BF16).astype(F32), b.astype(BF16).astype(F32)
    return lax.bitcast_convert_type(a, U32) | (lax.bitcast_convert_type(b, U32) >> 16)


def _unpack_halves(w):
    a = lax.bitcast_convert_type(w & jnp.uint32(0xFFFF0000), F32)
    b = lax.bitcast_convert_type(w << 16, F32)
    return jnp.concatenate([a, b], axis=1).astype(BF16)


def _rows_copy(src_ref, src_row, dst_ref, dst_row, rows, sem):
    rows = pl.multiple_of(rows, GRP)
    return pltpu.make_async_copy(src_ref.at[pl.ds(pl.multiple_of(src_row, GRP), rows), :],
                                 dst_ref.at[pl.ds(pl.multiple_of(dst_row, GRP), rows), :], sem)


def _wait_rows(count, src_ref, dst_ref, sem):
    @pl.when(count > 0)
    def _():
        _rows_copy(src_ref, 0, dst_ref, 0, count * GRP, sem).wait()


def _start_runs(runs, tile, copy):
    dst_ref, off_ref, len_ref = runs
    for e in range(N_EXP):
        idx = tile * N_EXP + e
        rows = len_ref[idx]

        @pl.when(rows > 0)
        def _():
            copy(dst_ref[idx], off_ref[idx], rows).start(priority=1)


def _dispatch_kernel(dst_ref, off_ref, len_ref, ng_ref, zdst_ref, zlen_ref, h_ref, r_ref, xs_ref, loc, zbuf, sems,
                     zsem):
    runs = (dst_ref, off_ref, len_ref)
    i = pl.program_id(0)
    n = pl.num_programs(0)

    @pl.when(i == 0)
    def _():
        zbuf[...] = jnp.zeros(zbuf.shape, U32)
        for e in range(N_EXP):
            pl.when(zlen_ref[e] > 0)(lambda e=e: _rows_copy(zbuf, 0, xs_ref, zdst_ref[e], zlen_ref[e], zsem).start())
        for e in range(N_EXP):
            pl.when(zlen_ref[e] > 0)(lambda e=e: _rows_copy(zbuf, 0, xs_ref, zdst_ref[e], zlen_ref[e], zsem).wait())

    def run(slot):
        buf = loc.at[slot]
        sem = sems.at[slot]

        @pl.when(i >= 2)
        def _():
            _wait_rows(ng_ref[i - 2], buf, xs_ref, sem)

        rt = r_ref[...].T
        rows = lax.broadcasted_iota(jnp.int32, (LR, TM), 0).astype(F32)
        perm = jnp.where((rows == rt[4:5, :]) | (rows == rt[5:6, :]), 1.0, 0.0).astype(BF16)
        buf[...] = _pack_halves(_dot(perm, h_ref[...]), exact=True)

        _start_runs(runs, i, lambda dst, off, rows: _rows_copy(buf, off, xs_ref, dst, rows, sem))

        @pl.when(i == n - 1)
        def _():
            _wait_rows(ng_ref[i], buf, xs_ref, sem)

            @pl.when(i >= 1)
            def _():
                _wait_rows(ng_ref[i - 1], loc.at[1 - slot], xs_ref, sems.at[1 - slot])

    pl.when(i % 2 == 0)(lambda: run(0))
    pl.when(i % 2 == 1)(lambda: run(1))


def _dispatch(runs, ng, zdst, zlen, h2, rinfo, cap):
    t = h2.shape[0]
    tok = lambda i, *_: (i, 0)
    return pl.pallas_call(
        _dispatch_kernel,
        grid_spec=pltpu.PrefetchScalarGridSpec(
            num_scalar_prefetch=6,
            grid=(t // TM,),
            in_specs=[
                pl.BlockSpec((TM, D), tok),
                pl.BlockSpec((TM, 128), tok),
            ],
            out_specs=pl.BlockSpec(memory_space=pl.ANY),
            scratch_shapes=[
                pltpu.VMEM((2, LR, D // 2), U32),
                pltpu.VMEM((BLK, D // 2), U32),
                pltpu.SemaphoreType.DMA((2,)),
                pltpu.SemaphoreType.DMA,
            ],
        ),
        out_shape=jax.ShapeDtypeStruct((cap, D // 2), U32),
        compiler_params=_params(("arbitrary",)),
        name="moe_dispatch",
    )(*runs, ng, zdst, zlen, h2, rinfo)


def _expert_kernel(be_ref, nu_ref, x_ref, wg_ref, wu_ref, wd_ref, y_ref, wg_sc, wu_sc, wd_sc):
    i = pl.program_id(0)
    used = i < nu_ref[0]
    fresh = (i == 0) | (be_ref[i] != be_ref[jnp.maximum(i - 1, 0)])

    @pl.when(used & fresh)
    def _():
        wg_sc[...] = wg_ref[0, 0].astype(BF16)
        wu_sc[...] = wu_ref[0, 0].astype(BF16)
        wd_sc[...] = wd_ref[0, 0].astype(BF16)

    @pl.when(used)
    def _():
        xb = _unpack_halves(x_ref[...])
        g = _dot(xb, wg_sc[...])
        u = _dot(xb, wu_sc[...])
        hid = (g * _sigmoid(g)) * u
        y_ref[...] = _pack_halves(_dot(hid.astype(BF16), wd_sc[...]))

    @pl.when(jnp.logical_not(used))
    def _():
        y_ref[...] = jnp.zeros(y_ref.shape, U32)


def _experts(blk_e, n_used, xs, wg, wu, wd, layer):
    cap = xs.shape[0]
    row = lambda i, be, nu: (jnp.minimum(i, nu[0] - 1), 0)
    wsel = lambda i, be, nu: (layer, be[i], 0, 0)
    return pl.pallas_call(
        _expert_kernel,
        grid_spec=pltpu.PrefetchScalarGridSpec(
            num_scalar_prefetch=2,
            grid=(cap // BLK,),
            in_specs=[
                pl.BlockSpec((BLK, D // 2), row),
                pl.BlockSpec((1, 1, D, D_EXP), wsel),
                pl.BlockSpec((1, 1, D, D_EXP), wsel),
                pl.BlockSpec((1, 1, D_EXP, D), wsel),
            ],
            out_specs=pl.BlockSpec((BLK, D // 2), lambda i, be, nu: (i, 0)),
            scratch_shapes=[
                pltpu.VMEM((D, D_EXP), BF16),
                pltpu.VMEM((D, D_EXP), BF16),
                pltpu.VMEM((D_EXP, D), BF16),
            ],
        ),
        out_shape=jax.ShapeDtypeStruct((cap, D // 2), U32),
        compiler_params=_params(("arbitrary",)),
        name="moe_experts",
    )(blk_e, n_used, xs, wg, wu, wd)


def _combine_kernel(dst_ref, off_ref, len_ref, ng_ref, y_ref, x1_ref, r_ref, mod_ref, o_ref, loc, sems):
    runs = (dst_ref, off_ref, len_ref)
    i = pl.program_id(0)
    n = pl.num_programs(0)

    def fetch(tile, slot):
        _start_runs(runs, tile, lambda dst, off, rows: _rows_copy(y_ref, dst, loc.at[slot], off, rows, sems.at[slot]))

    @pl.when(i == 0)
    def _():
        loc[...] = jnp.zeros(loc.shape, U32)
        fetch(0, 0)

    def run(slot):
        @pl.when(i + 1 < n)
        def _():
            fetch(i + 1, 1 - slot)

        _wait_rows(ng_ref[i], y_ref, loc.at[slot], sems.at[slot])
        o_ref[...] = _unsort_add(loc[slot], r_ref[...], x1_ref[...], mod_ref[0, 0][:, 5 * D:6 * D])

    pl.when(i % 2 == 0)(lambda: run(0))
    pl.when(i % 2 == 1)(lambda: run(1))


def _combine(runs, ng, ys, x1, rinfo, modsel, *, tiles_per_batch, with_ctx):
    t = x1.shape[0]
    if with_ctx:
        msel = lambda i, *_: (i // tiles_per_batch, jnp.minimum(i % tiles_per_batch, 1), 0, 0)
    else:
        msel = lambda i, *_: (i // tiles_per_batch, 1, 0, 0)
    tok = lambda i, *_: (i, 0)
    return pl.pallas_call(
        _combine_kernel,
        grid_spec=pltpu.PrefetchScalarGridSpec(
            num_scalar_prefetch=4,
            grid=(t // TM,),
            in_specs=[
                pl.BlockSpec(memory_space=pl.ANY),
                pl.BlockSpec((TM, D), tok),
                pl.BlockSpec((TM, 128), tok),
                pl.BlockSpec((1, 1, 1, 6 * D), msel),
            ],
            out_specs=pl.BlockSpec((TM, D), tok),
            scratch_shapes=[pltpu.VMEM((2, LR, D // 2), U32), pltpu.SemaphoreType.DMA((2,))],
        ),
        out_shape=jax.ShapeDtypeStruct((t, D), F32),
        compiler_params=_params(("arbitrary",)),
        name="moe_combine",
    )(*runs, ng, ys, x1, rinfo, modsel)


def _unsort_add(ysw, info, x1, g2):
    col = lax.broadcasted_iota(jnp.int32, (TM, LR), 1).astype(F32)
    pick = jnp.concatenate([jnp.where(col == info[:, 4:5], 1.0, 0.0).astype(BF16),
                            jnp.where(col == info[:, 5:6], 1.0, 0.0).astype(BF16)], axis=0)
    y12 = _dot(pick, _unpack_halves(ysw))
    return x1 + g2 * (info[:, 2:3] * y12[0:TM] + info[:, 3:4] * y12[TM:2 * TM])


def _combine_proj_kernel(dst_ref, off_ref, len_ref, ng_ref, y_ref, x1_ref, r_ref, mod0_ref, mod_ref, lng_ref, w_ref,
                         cos_ref, sin_ref, qg_ref, kg_ref, bdq_ref, bdk_ref,
                         xc_ref, q_ref, k_ref, v_ref, u_ref, loc, sems, *, nt):
    runs = (dst_ref, off_ref, len_ref)
    bp = pl.program_id(0)
    i = pl.program_id(1)
    step = bp * nt + i
    slot = step % 2

    def fetch(bp_, i_, slot_):
        for bb in range(PB):
            tile = (bp_ * PB + bb) * nt + i_
            _start_runs(runs, tile, lambda dst, off, rows, bb=bb: _rows_copy(
                y_ref, dst, loc.at[slot_, bb], off, rows, sems.at[slot_, bb]))

    @pl.when(step == 0)
    def _():
        loc[...] = jnp.zeros(loc.shape, U32)
        fetch(0, 0, 0)

    @pl.when(step + 1 < pl.num_programs(0) * nt)
    def _():
        wrap = i + 1 == nt
        fetch(jnp.where(wrap, bp + 1, bp), jnp.where(wrap, 0, i + 1), 1 - slot)

    @pl.when(step >= 0)
    def _():
        for bb in range(PB):
            tile = (bp * PB + bb) * nt + i
            _wait_rows(ng_ref[tile], y_ref, loc.at[slot, bb], sems.at[slot, bb])
            xc_ref[bb] = _unsort_add(loc[slot, bb], r_ref[bb], x1_ref[bb], mod0_ref[bb, 0][:, 5 * D:6 * D])

    @pl.when(step >= -1)
    def _():
        h = jnp.concatenate([_modulated(xc_ref[bb], mod_ref[bb, 0], lng_ref[...], 0, 1) for bb in range(PB)], axis=0)
        px = _dot(h.astype(BF16), w_ref[...])
        _proj_epilogue(px, cos_ref, sin_ref, qg_ref, kg_ref, bdq_ref, bdk_ref, q_ref, k_ref, v_ref, u_ref, False)


def _combine_project(runs, ng, ys, x1, rinfo, modsel0, modsel, ln_g, w_in, cos, sin, q_g, k_g, bdq, bdk):
    b, s_tot, _ = x1.shape
    nt = s_tot // TM
    n_in = w_in.shape[1]
    const = lambda bi, i, *_: (0, 0)
    tok = lambda bi, i, *_: (bi, i, 0)
    msel = lambda bi, i, *_: (bi, jnp.minimum(i, 1), 0, 0)
    head = lambda bi, i, *_: (bi, 0, i, 0)
    return pl.pallas_call(
        functools.partial(_combine_proj_kernel, nt=nt),
        grid_spec=pltpu.PrefetchScalarGridSpec(
            num_scalar_prefetch=4,
            grid=(b // PB, nt),
            in_specs=[
                pl.BlockSpec(memory_space=pl.ANY),
                pl.BlockSpec((PB, TM, D), tok),
                pl.BlockSpec((PB, TM, 128), tok),
                pl.BlockSpec((PB, 1, 1, 6 * D), msel),
                pl.BlockSpec((PB, 1, 1, 6 * D), msel),
                pl.BlockSpec((1, D), const),
                pl.BlockSpec((D, n_in), const),
                pl.BlockSpec((TM, 128), lambda bi, i, *_: (i, 0)),
                pl.BlockSpec((TM, 128), lambda bi, i, *_: (i, 0)),
                pl.BlockSpec((1, QW), const),
                pl.BlockSpec((1, KW), const),
                pl.BlockSpec((QW, QW), const),
                pl.BlockSpec((KW, KW), const),
            ],
            out_specs=[
                pl.BlockSpec((PB, TM, D), tok),
                pl.BlockSpec((PB, NH, TM, HD), head),
                pl.BlockSpec((PB, NKV, TM, HD), head),
                pl.BlockSpec((PB, NKV, TM, 2 * HD), head),
                pl.BlockSpec((PB, TM, UW), tok),
            ],
            scratch_shapes=[pltpu.VMEM((2, PB, LR, D // 2), U32), pltpu.SemaphoreType.DMA((2, PB))],
        ),
        out_shape=[
            jax.ShapeDtypeStruct((b, s_tot, D), F32),
            jax.ShapeDtypeStruct((b, NH, s_tot, HD), BF16),
            jax.ShapeDtypeStruct((b, NKV, s_tot, HD), BF16),
            jax.ShapeDtypeStruct((b, NKV, s_tot, 2 * HD), BF16),
            jax.ShapeDtypeStruct((b, s_tot, UW), F32),
        ],
        compiler_params=_params(("arbitrary", "arbitrary")),
        name="moe_combine_in_projection",
    )(*runs, ng, ys, x1, rinfo, modsel0, modsel, ln_g, w_in, cos, sin, q_g, k_g, bdq, bdk)


def _moe(x1, h2, rinfo, g8, modsel, wg, wu, wd, layer, *, with_ctx, combine=True):
    b, s, _ = x1.shape
    t = b * s
    n_tiles = t // TM
    i32 = jnp.int32
    grp = g8.reshape(n_tiles, 128)[:, 0:N_EXP].astype(i32)
    n_blk = -(-(2 * t + n_tiles * N_EXP * (GRP - 1) + N_EXP * (BLK - 1)) // BLK)
    cap = n_blk * BLK
    goff_end = jnp.cumsum(grp, axis=1)
    ng = goff_end[:, N_EXP - 1].astype(i32)
    rows_e = GRP * jnp.sum(grp, axis=0)
    padded = (rows_e + BLK - 1) // BLK * BLK
    pend = jnp.cumsum(padded)
    pstart = pend - padded
    run_dst = (pstart[None, :] + GRP * (jnp.cumsum(grp, axis=0) - grp)).reshape(-1).astype(i32)
    run_off = (GRP * (goff_end - grp)).reshape(-1).astype(i32)
    run_len = (GRP * grp).reshape(-1).astype(i32)
    runs = (run_dst, run_off, run_len)
    zdst = (pstart + rows_e).astype(i32)
    zlen = (padded - rows_e).astype(i32)
    blk_e = jnp.minimum(jnp.sum((pend[None, :] <= (jnp.arange(n_blk, dtype=i32) * BLK)[:, None]).astype(i32), axis=1),
                        N_EXP - 1).astype(i32)
    n_used = (pend[N_EXP - 1:N_EXP] // BLK).astype(i32)
    info = rinfo.reshape(t, 128)
    xs = _dispatch(runs, ng, zdst, zlen, h2.reshape(t, D), info, cap)
    ys = _experts(blk_e, n_used, xs, wg, wu, wd, layer)
    if not combine:
        return runs, ng, ys
    out = _combine(runs, ng, ys, x1.reshape(t, D), info, modsel, tiles_per_batch=s // TM, with_ctx=with_ctx)
    return out.reshape(b, s, D)


def _rope_tables(n_ctx, s_lat):
    rows = s_lat // GRID_W
    row = jnp.repeat(jnp.arange(rows, dtype=F32), GRID_W)
    col = jnp.tile(jnp.arange(GRID_W, dtype=F32), rows)
    n_freq = HD // 4
    inv = ROPE_THETA ** (-jnp.arange(n_freq, dtype=F32) / n_freq)
    ang = jnp.concatenate([row[:, None] * inv, col[:, None] * inv], axis=-1)
    cos = jnp.concatenate([jnp.ones((n_ctx, HD // 2), F32), jnp.cos(ang)], axis=0)
    sin = jnp.concatenate([jnp.zeros((n_ctx, HD // 2), F32), jnp.sin(ang)], axis=0)
    cos = jnp.concatenate([cos, cos, cos, cos], axis=1)
    sin = jnp.concatenate([-sin, sin, -sin, sin], axis=1)
    return cos, sin


def _block_diag_ones(n):
    i = jnp.arange(n) // HD
    return (i[:, None] == i[None, :]).astype(BF16)


def kernel(x, c, ctx, c_ctx, mod_w, mod_b, ln1_g, ln2_g, w_in_ab, w_out_ab, q_norm_a, k_norm_a, conv_w, conv_b,
           conv_ln_g, conv_ln_b, w_in_cd, w_out_cd, q_norm_c, k_norm_c, sink_c, pool_w, pool_scale,
           rt_grp_w, rt_grp_b, rt_exp_w, rt_exp_b, ex_gate, ex_up, ex_down):
    b, s_lat, _ = x.shape
    n_ctx = ctx.shape[1]
    assert n_ctx == TM and s_lat % TM == 0 and b <= 8 and b % PB == 0

    cc = jnp.zeros((16, D), F32).at[0:b].set(c).at[8].set(c_ctx)
    mod = _modulation(cc, mod_w, mod_b)

    def modsel(l):
        return jnp.stack([jnp.broadcast_to(mod[l, 8], (b, 6 * D)), mod[l, 0:b]], axis=1).reshape(b, 2, 1, 6 * D)

    cos, sin = _rope_tables(n_ctx, s_lat)
    bdq = _block_diag_ones(QW)
    bdk = _block_diag_ones(KW)
    tri = (jnp.arange(TM)[:, None] > jnp.arange(TM)[None, :]).astype(BF16)
    upper = (jnp.arange(128)[:, None] < jnp.arange(128)[None, :]).astype(BF16)

    def router(l):
        rw = jnp.zeros((D, 128), F32).at[:, 0:N_EXP].set(rt_exp_w[l]).at[:, N_EXP:N_EXP + N_GROUPS].set(rt_grp_w[l])
        rb = jnp.zeros((1, 128), F32).at[0, 0:N_EXP].set(rt_exp_b[l]).at[0, N_EXP:N_EXP + N_GROUPS].set(rt_grp_b[l])
        return rw, rb

    def tile_gain(g, n):
        return jnp.tile(g, n).reshape(1, n * HD)

    s_tot = n_ctx + s_lat

    ms = modsel(0)
    q, k, v, glu = _project(ctx, x, 0, s_tot, ms, ln1_g[0:1], w_in_ab[0].astype(BF16), cos, sin,
                            tile_gain(q_norm_a[0], NH), tile_gain(k_norm_a[0], NKV), bdq, bdk, glu=True)
    att = _dense_attention(q, k, v, n_ctx)
    cv = _conv_module(glu, conv_w[0], conv_b[0:1], conv_ln_g[0:1], conv_ln_b[0:1], n_ctx)
    rw, rb = router(0)
    x1, h2, rinfo, g8 = _out_project(att, cv, (ctx, x), ms, w_out_ab[0].astype(BF16), ln2_g[0:1], rw, rb, tri, upper)
    runs, ng, ys = _moe(x1, h2, rinfo, g8, ms, ex_gate, ex_up, ex_down, 0, with_ctx=True, combine=False)

    ms0, ms = ms, modsel(1)
    xc, q, k, v, u = _combine_project(runs, ng, ys, x1, rinfo, ms0, ms, ln1_g[1:2], w_in_cd[0].astype(BF16), cos, sin,
                                      tile_gain(q_norm_c[0], NH), tile_gain(k_norm_c[0], NKV), bdq, bdk)
    att = _window_attention(sink_c[0], q, k, v, n_ctx)
    pm = _pool_mixer(u, pool_w[0].astype(BF16), pool_scale[0:1], n_ctx)
    rw, rb = router(1)
    x1, h2, rinfo, g8 = _out_project(att, pm, (xc,), ms, w_out_cd[0].astype(BF16), ln2_g[1:2], rw, rb, tri, upper)
    return _moe(x1, h2, rinfo, g8, ms, ex_gate, ex_up, ex_down, 1, with_ctx=False)
```

```python
import functools

import jax
import jax.numpy as jnp
from jax import lax
from jax.experimental import pallas as pl
from jax.experimental.pallas import tpu as pltpu

F32 = jnp.float32
BF16 = jnp.bfloat16
U32 = jnp.uint32

D = 1024
HD = 64
NH = 8
NKV = 2
GQ = NH // NKV
QW = NH * HD
KW = NKV * HD
UW = 512
EPS = 1e-6
ROPE_THETA = 10000.0
GRID_W = 64
CONV_K = 31
WINDOW = 128
POOL_SIZES = (2, 4, 8, 16)
N_GROUPS = 4
PER_GROUP = 8
N_EXP = N_GROUPS * PER_GROUP
D_EXP = D // 2

TM = 256
PB = 4
TQ = 256
TQD = 256
TKL = 512
SUB = 64
HALO = 16
BLK = 1024
GRP = 8
NG = 96
LR = NG * GRP
VMEM_LIMIT = 56 * 1024 * 1024


def _sigmoid(x):
    return 1.0 / (1.0 + jnp.exp(-x))


def _dot(a, b):
    return jnp.dot(a, b, preferred_element_type=F32)


def _dot_nt(a, b):
    return lax.dot_general(a, b, (((1,), (1,)), ((), ())), preferred_element_type=F32)


def _split(a):
    hi = a.astype(BF16)
    lo = (a - hi.astype(F32)).astype(BF16)
    return hi, lo


def _dot3(a, w):
    a_hi, a_lo = _split(a)
    w_hi, w_lo = _split(w)
    return _dot(a_hi, w_hi) + _dot(a_lo, w_hi) + _dot(a_hi, w_lo)


def _rmsnorm(x, g):
    return x * lax.rsqrt(jnp.mean(x * x, axis=-1, keepdims=True) + EPS) * g


def _params(sem, vmem=VMEM_LIMIT):
    return pltpu.CompilerParams(dimension_semantics=sem, vmem_limit_bytes=vmem)


def _mod_kernel(c_ref, w_ref, b_ref, o_ref):
    a = c_ref[...]
    a = a * _sigmoid(a)
    o_ref[0] = _dot3(a, w_ref[0]) + b_ref[0]


def _modulation(cc, mod_w, mod_b):
    depth = mod_w.shape[0]
    tn = 1024
    return pl.pallas_call(
        _mod_kernel,
        grid=(depth, 6 * D // tn),
        in_specs=[
            pl.BlockSpec((16, D), lambda l, j: (0, 0)),
            pl.BlockSpec((1, D, tn), lambda l, j: (l, 0, j)),
            pl.BlockSpec((1, 1, tn), lambda l, j: (l, 0, j)),
        ],
        out_specs=pl.BlockSpec((1, 16, tn), lambda l, j: (l, 0, j)),
        out_shape=jax.ShapeDtypeStruct((depth, 16, 6 * D), F32),
        compiler_params=_params(("arbitrary", "arbitrary")),
        name="modulation",
    )(cc, mod_w, mod_b.reshape(depth, 1, 6 * D))


def _swap_halves(t):
    w = t.shape[1]
    lane = lax.broadcasted_iota(jnp.int32, t.shape, 1)
    first = (lane & (HD - 1)) < (HD // 2)
    return jnp.where(first, pltpu.roll(t, w - HD // 2, 1), pltpu.roll(t, HD // 2, 1))


def _head_norm_rope(t, bd, g, cos, sin, scale):
    ssq = _dot((t * t).astype(BF16), bd)
    tn = t * lax.rsqrt(ssq * (1.0 / HD) + EPS) * g
    n = t.shape[1] // 128
    if n > 1:
        cos = jnp.concatenate([cos] * n, axis=1)
        sin = jnp.concatenate([sin] * n, axis=1)
    out = tn * cos + _swap_halves(tn) * sin
    return out * scale if scale != 1.0 else out


def _tile_rows(c_ref, x_ref, bb):
    return jnp.where(pl.program_id(1) == 0, c_ref[bb], x_ref[bb])


def _modulated(x, mod, lng, shift_at, scale_at):
    return _rmsnorm(x, lng) * (1.0 + mod[:, scale_at * D:(scale_at + 1) * D]) + mod[:, shift_at * D:(shift_at + 1) * D]


def _proj_kernel(c_ref, x_ref, mod_ref, lng_ref, w_ref, cos_ref, sin_ref, qg_ref, kg_ref, bdq_ref, bdk_ref,
                 q_ref, k_ref, v_ref, u_ref, *, glu):
    h = jnp.concatenate([_modulated(_tile_rows(c_ref, x_ref, bb), mod_ref[bb, 0], lng_ref[...], 0, 1)
                         for bb in range(PB)], axis=0)
    px = _dot(h.astype(BF16), w_ref[...])
    _proj_epilogue(px, cos_ref, sin_ref, qg_ref, kg_ref, bdq_ref, bdk_ref, q_ref, k_ref, v_ref, u_ref, glu)


def _proj_epilogue(px, cos_ref, sin_ref, qg_ref, kg_ref, bdq_ref, bdk_ref, q_ref, k_ref, v_ref, u_ref, glu):
    cos = jnp.concatenate([cos_ref[...]] * PB, axis=0)
    sin = jnp.concatenate([sin_ref[...]] * PB, axis=0)
    q = _head_norm_rope(px[:, 0:QW], bdq_ref[...], qg_ref[...], cos, sin, HD ** -0.5)
    k = _head_norm_rope(px[:, QW:QW + KW], bdk_ref[...], kg_ref[...], cos, sin, 1.0)
    v = px[:, QW + KW:QW + 2 * KW]
    lane = lax.broadcasted_iota(jnp.int32, (TM, KW), 1)
    ones_col = jnp.where(lane == HD, 1.0, 0.0)
    o_u = QW + 2 * KW
    if glu:
        u = px[:, o_u:o_u + UW] * _sigmoid(px[:, o_u + UW:o_u + 2 * UW])
    else:
        u = px[:, o_u:o_u + UW]
    for bb in range(PB):
        rows = slice(bb * TM, (bb + 1) * TM)
        for hh in range(NH):
            q_ref[bb, hh] = q[rows, hh * HD:(hh + 1) * HD].astype(BF16)
        for j in range(NKV):
            k_ref[bb, j] = k[rows, j * HD:(j + 1) * HD].astype(BF16)
            vj = v[rows] if j == 0 else pltpu.roll(v[rows], KW - j * HD, 1)
            v_ref[bb, j] = jnp.where(lane < HD, vj, ones_col).astype(BF16)
        u_ref[bb] = u[rows]


def _token_specs(lat_first):
    return [pl.BlockSpec((PB, TM, D), lambda bi, i: (bi, 0, 0)),
            pl.BlockSpec((PB, TM, D), lambda bi, i: (bi, jnp.maximum(i - 1, 0) + lat_first, 0))]


def _project(ctx_rows, lat_rows, lat_first, s_tot, modsel, ln_g, w_in, cos, sin, q_g, k_g, bdq, bdk, *, glu):
    b = ctx_rows.shape[0]
    n_in = w_in.shape[1]
    nt = s_tot // TM
    const = lambda bi, i: (0, 0)
    return pl.pallas_call(
        functools.partial(_proj_kernel, glu=glu),
        grid=(b // PB, nt),
        in_specs=_token_specs(lat_first) + [
            pl.BlockSpec((PB, 1, 1, 6 * D), lambda bi, i: (bi, jnp.minimum(i, 1), 0, 0)),
            pl.BlockSpec((1, D), const),
            pl.BlockSpec((D, n_in), const),
            pl.BlockSpec((TM, 128), lambda bi, i: (i, 0)),
            pl.BlockSpec((TM, 128), lambda bi, i: (i, 0)),
            pl.BlockSpec((1, QW), const),
            pl.BlockSpec((1, KW), const),
            pl.BlockSpec((QW, QW), const),
            pl.BlockSpec((KW, KW), const),
        ],
        out_specs=[
            pl.BlockSpec((PB, NH, TM, HD), lambda bi, i: (bi, 0, i, 0)),
            pl.BlockSpec((PB, NKV, TM, HD), lambda bi, i: (bi, 0, i, 0)),
            pl.BlockSpec((PB, NKV, TM, 2 * HD), lambda bi, i: (bi, 0, i, 0)),
            pl.BlockSpec((PB, TM, UW), lambda bi, i: (bi, i, 0)),
        ],
        out_shape=[
            jax.ShapeDtypeStruct((b, NH, s_tot, HD), BF16),
            jax.ShapeDtypeStruct((b, NKV, s_tot, HD), BF16),
            jax.ShapeDtypeStruct((b, NKV, s_tot, 2 * HD), BF16),
            jax.ShapeDtypeStruct((b, s_tot, UW), F32),
        ],
        compiler_params=_params(("arbitrary", "arbitrary")),
        name="in_projection",
    )(ctx_rows, lat_rows, modsel, ln_g, w_in, cos, sin, q_g, k_g, bdq, bdk)


def _merge_heads(o):
    tq = o.shape[0] // GQ
    return jnp.concatenate([o[g * tq:(g + 1) * tq] for g in range(GQ)], axis=1)


def _lane_max(s):
    return functools.reduce(jnp.maximum, [s[:, j * 128:(j + 1) * 128] for j in range(s.shape[1] // 128)])


def _dense_attn_kernel(q_ref, k_ref, v_ref, o_ref, sc_sc, sl_sc, m_sc, *, n_ctx, s_tot):
    qi = pl.program_id(2)
    q = q_ref[0].reshape(GQ * TQD, HD)
    n_lat = (s_tot - n_ctx) // TKL

    def chunks(n):
        spans = [(0, n_ctx)] + [(n_ctx + c * TKL, n_ctx + (c + 1) * TKL) for c in range(n)]
        return list(zip(spans, [sc_sc] + [sl_sc.at[c] for c in range(n)]))

    def scores(n):
        for i, ((lo, hi), slot) in enumerate(chunks(n)):
            s = _dot_nt(q, k_ref[0, 0, lo:hi, :])
            slot[...] = s
            m_sc[i] = _lane_max(s)
        m_lane = functools.reduce(jnp.maximum, [m_sc[i] for i in range(n + 1)])
        m_sc[0] = jnp.broadcast_to(jnp.max(m_lane, axis=1, keepdims=True), m_lane.shape)

    def weighted(n):
        acc = None
        for (lo, hi), slot in chunks(n):
            m = jnp.concatenate([m_sc[0]] * ((hi - lo) // 128), axis=1)
            part = _dot(jnp.exp(slot[...] - m).astype(BF16), v_ref[0, 0, lo:hi, :])
            acc = part if acc is None else acc + part
        o_ref[0] = _merge_heads(acc[:, 0:HD] / acc[:, HD:HD + 1]).astype(BF16)

    is_lat = qi * TQD >= n_ctx
    pl.when(is_lat)(lambda: scores(n_lat))
    pl.when(jnp.logical_not(is_lat))(lambda: scores(0))
    pl.when(is_lat)(lambda: weighted(n_lat))
    pl.when(jnp.logical_not(is_lat))(lambda: weighted(0))


def _dense_attention(q, k, v, n_ctx):
    b, _, s_tot, _ = q.shape
    rows = GQ * TQD
    return pl.pallas_call(
        functools.partial(_dense_attn_kernel, n_ctx=n_ctx, s_tot=s_tot),
        grid=(b, NKV, s_tot // TQD),
        in_specs=[
            pl.BlockSpec((1, GQ, TQD, HD), lambda bi, j, i: (bi, j, i, 0)),
            pl.BlockSpec((1, 1, s_tot, HD), lambda bi, j, i: (bi, j, 0, 0)),
            pl.BlockSpec((1, 1, s_tot, 2 * HD), lambda bi, j, i: (bi, j, 0, 0)),
        ],
        out_specs=pl.BlockSpec((1, TQD, GQ * HD), lambda bi, j, i: (bi, i, j)),
        out_shape=jax.ShapeDtypeStruct((b, s_tot, QW), BF16),
        scratch_shapes=[
            pltpu.VMEM((rows, n_ctx), F32),
            pltpu.VMEM(((s_tot - n_ctx) // TKL, rows, TKL), F32),
            pltpu.VMEM(((s_tot - n_ctx) // TKL + 1, rows, 128), F32),
        ],
        compiler_params=_params(("arbitrary", "arbitrary", "arbitrary")),
        name="dense_attention",
    )(q, k, v)


def _window_attn_kernel(sink_ref, q_ref, k_ref, v_ref, o_ref, sc_sc, sw_sc, m_sc, e_sc, *, n_ctx, s_tot):
    qi = pl.program_id(1)
    span = TQ + 2 * WINDOW
    q0 = n_ctx + qi * TQ
    start = pl.multiple_of(jnp.clip(q0 - WINDOW, n_ctx, s_tot - span), 128)
    row = lax.broadcasted_iota(jnp.int32, (TQ, span), 0)
    col = lax.broadcasted_iota(jnp.int32, (TQ, span), 1)
    band = jnp.where(jnp.abs((q0 - start) + row - col) <= WINDOW, 0.0, -jnp.inf)
    band = jnp.concatenate([band] * GQ, axis=0)
    for j in range(NKV):
        q = q_ref[0, j * GQ:(j + 1) * GQ].reshape(GQ * TQ, HD)
        s_c = _dot_nt(q, k_ref[0, j, 0:n_ctx, :])
        s_w = _dot_nt(q, k_ref[0, j, pl.ds(start, span), :]) + band
        sc_sc[j] = s_c
        sw_sc[j] = s_w
        sink = jnp.concatenate([jnp.full((TQ, 128), sink_ref[j * GQ + g], F32) for g in range(GQ)], axis=0)
        m_lane = jnp.maximum(_lane_max(s_c), _lane_max(s_w))
        m_sc[j] = jnp.maximum(jnp.broadcast_to(jnp.max(m_lane, axis=1, keepdims=True), m_lane.shape), sink)
        e_sc[j] = jnp.exp(sink - m_sc[j])
    for j in range(NKV):
        m = m_sc[j]
        p_c = jnp.exp(sc_sc[j] - jnp.concatenate([m] * (n_ctx // 128), axis=1)).astype(BF16)
        p_w = jnp.exp(sw_sc[j] - jnp.concatenate([m] * (span // 128), axis=1)).astype(BF16)
        acc = _dot(p_c, v_ref[0, j, 0:n_ctx, :]) + _dot(p_w, v_ref[0, j, pl.ds(start, span), :])
        l = acc[:, HD:HD + 1] + e_sc[j][:, 0:1]
        o_ref[0, :, j * GQ * HD:(j + 1) * GQ * HD] = _merge_heads(acc[:, 0:HD] / l).astype(BF16)


def _window_attention(sink, q, k, v, n_ctx):
    b, _, s_tot, _ = q.shape
    s_lat = s_tot - n_ctx
    off = n_ctx // TQ
    return pl.pallas_call(
        functools.partial(_window_attn_kernel, n_ctx=n_ctx, s_tot=s_tot),
        grid_spec=pltpu.PrefetchScalarGridSpec(
            num_scalar_prefetch=1,
            grid=(b, s_lat // TQ),
            in_specs=[
                pl.BlockSpec((1, NH, TQ, HD), lambda bi, i, sk: (bi, 0, i + off, 0)),
                pl.BlockSpec((1, NKV, s_tot, HD), lambda bi, i, sk: (bi, 0, 0, 0)),
                pl.BlockSpec((1, NKV, s_tot, 2 * HD), lambda bi, i, sk: (bi, 0, 0, 0)),
            ],
            out_specs=pl.BlockSpec((1, TQ, QW), lambda bi, i, sk: (bi, i, 0)),
            scratch_shapes=[
                pltpu.VMEM((NKV, GQ * TQ, n_ctx), F32),
                pltpu.VMEM((NKV, GQ * TQ, TQ + 2 * WINDOW), F32),
                pltpu.VMEM((NKV, GQ * TQ, 128), F32),
                pltpu.VMEM((NKV, GQ * TQ, 128), F32),
            ],
        ),
        out_shape=jax.ShapeDtypeStruct((b, s_lat, QW), BF16),
        compiler_params=_params(("arbitrary", "arbitrary")),
        name="window_attention",
    )(sink, q, k, v)


def _conv_kernel(g_ref, cw_ref, cb_ref, lg_ref, lb_ref, o_ref, pad_sc, win_sc, *, n_ctx, s_tot):
    zeros = jnp.zeros((HALO, UW), F32)
    pad_sc[0:HALO, :] = zeros
    pad_sc[HALO + n_ctx:2 * HALO + n_ctx, :] = zeros
    pad_sc[2 * HALO + s_tot:3 * HALO + s_tot, :] = zeros

    def fill(i, carry):
        src = pl.multiple_of(i * TM, TM)
        dst = pl.multiple_of(src + HALO + jnp.where(src >= n_ctx, HALO, 0), 8)
        pad_sc[pl.ds(dst, TM), :] = g_ref[0, pl.ds(src, TM), :]
        return carry

    lax.fori_loop(0, s_tot // TM, fill, 0)
    half = CONV_K // 2

    def tile(i, carry):
        src = pl.multiple_of(i * SUB, SUB)
        base = pl.multiple_of(src + jnp.where(src >= n_ctx, HALO, 0), 8)
        win = pad_sc[pl.ds(base, SUB + 2 * HALO), :]
        keep = SUB + 2 * HALO - 8
        for r in range(8):
            win_sc[r, 0:keep, :] = win[r:r + keep, :]
        acc = jnp.zeros((SUB, UW), F32) + cb_ref[...]
        for t in range(CONV_K):
            a, r = divmod(HALO - half + t, 8)
            acc = acc + win_sc[r, 8 * a:8 * a + SUB, :] * cw_ref[t:t + 1, :]
        mu = jnp.mean(acc, axis=-1, keepdims=True)
        xc = acc - mu
        var = jnp.mean(xc * xc, axis=-1, keepdims=True)
        yn = xc * lax.rsqrt(var + EPS) * lg_ref[...] + lb_ref[...]
        o_ref[0, pl.ds(src, SUB), :] = (yn * _sigmoid(yn)).astype(BF16)
        return carry

    lax.fori_loop(0, s_tot // SUB, tile, 0)


def _conv_module(glu, conv_w, conv_b, ln_g, ln_b, n_ctx):
    b, s_tot, _ = glu.shape
    const = lambda bi: (0, 0)
    return pl.pallas_call(
        functools.partial(_conv_kernel, n_ctx=n_ctx, s_tot=s_tot),
        grid=(b,),
        in_specs=[
            pl.BlockSpec((1, s_tot, UW), lambda bi: (bi, 0, 0)),
            pl.BlockSpec((CONV_K, UW), const),
            pl.BlockSpec((1, UW), const),
            pl.BlockSpec((1, UW), const),
            pl.BlockSpec((1, UW), const),
        ],
        out_specs=pl.BlockSpec((1, s_tot, UW), lambda bi: (bi, 0, 0)),
        out_shape=jax.ShapeDtypeStruct((b, s_tot, UW), BF16),
        scratch_shapes=[pltpu.VMEM((s_tot + 3 * HALO, UW), F32), pltpu.VMEM((8, SUB + 2 * HALO, UW), F32)],
        compiler_params=_params(("arbitrary",)),
        name="conv_module",
    )(glu, conv_w, conv_b, ln_g, ln_b)


def _pool_kernel(u_ref, pw_ref, ps_ref, o_ref, pad_sc, *, n_ctx, s_lat):
    zeros = jnp.zeros((HALO, UW), F32)
    pad_sc[0:HALO, :] = zeros
    pad_sc[HALO + s_lat:2 * HALO + s_lat, :] = zeros

    def fill(i, carry):
        src = pl.multiple_of(i * TM, TM)
        pad_sc[pl.ds(pl.multiple_of(src + HALO, 8), TM), :] = u_ref[0, pl.ds(pl.multiple_of(src + n_ctx, 8), TM), :]
        return carry

    lax.fori_loop(0, s_lat // TM, fill, 0)
    gw = UW // len(POOL_SIZES)

    def tile(i, carry):
        src = pl.multiple_of(i * SUB, SUB)
        win = pad_sc[pl.ds(src, SUB + 2 * HALO), :]
        t = src + lax.broadcasted_iota(jnp.int32, (SUB, 1), 0)
        outs = []
        for gi, w in enumerate(POOL_SIZES):
            lanes = slice(gi * gw, (gi + 1) * gw)
            tot = jnp.zeros((SUB, gw), F32)
            for d in range(-(w // 2), w - w // 2):
                tot = tot + win[HALO + d:HALO + d + SUB, lanes]
            lo = jnp.clip(t - w // 2, 0, s_lat)
            hi = jnp.clip(t - w // 2 + w, 0, s_lat)
            p = tot / (hi - lo).astype(F32) - win[HALO:HALO + SUB, lanes]
            outs.append(_dot(p.astype(BF16), pw_ref[gi]))
        y = jnp.concatenate(outs, axis=1) * ps_ref[...]
        o_ref[0, pl.ds(src, SUB), :] = y.astype(BF16)
        return carry

    lax.fori_loop(0, s_lat // SUB, tile, 0)


def _pool_mixer(u, pool_w, pool_scale, n_ctx):
    b, s_tot, _ = u.shape
    s_lat = s_tot - n_ctx
    gw = UW // len(POOL_SIZES)
    return pl.pallas_call(
        functools.partial(_pool_kernel, n_ctx=n_ctx, s_lat=s_lat),
        grid=(b,),
        in_specs=[
            pl.BlockSpec((1, s_tot, UW), lambda bi: (bi, 0, 0)),
            pl.BlockSpec((len(POOL_SIZES), gw, gw), lambda bi: (0, 0, 0)),
            pl.BlockSpec((1, UW), lambda bi: (0, 0)),
        ],
        out_specs=pl.BlockSpec((1, s_lat, UW), lambda bi: (bi, 0, 0)),
        out_shape=jax.ShapeDtypeStruct((b, s_lat, UW), BF16),
        scratch_shapes=[pltpu.VMEM((s_lat + 2 * HALO, UW), F32)],
        compiler_params=_params(("arbitrary",)),
        name="pool_mixer",
    )(u, pool_w, pool_scale)


def _route(logits):
    shape = logits.shape
    lane = lax.broadcasted_iota(jnp.int32, shape, 1)
    lanef = lane.astype(F32)
    is_g = (lane >= N_EXP) & (lane < N_EXP + N_GROUPS)
    big = 1e9
    lg = jnp.where(is_g, logits, -jnp.inf)
    gmax = jnp.max(lg, axis=1, keepdims=True)
    gsel = jnp.min(jnp.where(is_g & (lg == gmax), lanef, big), axis=1, keepdims=True) - float(N_EXP)
    gsum = jnp.sum(jnp.exp(lg - gmax), axis=1, keepdims=True)
    g_w = 1.0 / gsum
    in_grp = (lane < N_EXP) & ((lane >> (PER_GROUP.bit_length() - 1)).astype(F32) == gsel)
    el = jnp.where(in_grp, logits, -jnp.inf)
    emax = jnp.max(el, axis=1, keepdims=True)
    ee = jnp.exp(el - emax)
    p = ee / jnp.sum(ee, axis=1, keepdims=True)
    p1 = jnp.max(jnp.where(in_grp, p, -1.0), axis=1, keepdims=True)
    i1 = jnp.min(jnp.where(in_grp & (p == p1), lanef, big), axis=1, keepdims=True)
    rest = in_grp & (lanef != i1)
    p2 = jnp.max(jnp.where(rest, p, -1.0), axis=1, keepdims=True)
    i2 = jnp.min(jnp.where(rest & (p == p2), lanef, big), axis=1, keepdims=True)
    w1 = g_w * p1 / (p1 + p2)
    w2 = g_w * p2 / (p1 + p2)
    onehot = jnp.where((lanef == i1) | (lanef == i2), 1.0, 0.0)
    return lanef, i1, i2, w1, w2, onehot


def _outproj_kernel(a_ref, b_ref, *refs, with_ctx):
    mod_ref, w_ref, lng_ref, rw_ref, rb_ref, tri_ref, upper_ref, x1_ref, h2_ref, r_ref, g8_ref = refs[-11:]
    a = jnp.concatenate([a_ref[bb] for bb in range(PB)], axis=0)
    bmix = jnp.concatenate([b_ref[bb] for bb in range(PB)], axis=0)
    y = _dot(a, w_ref[0:QW, :]) + _dot(bmix, w_ref[QW:QW + UW, :])
    h2_parts = []
    for bb in range(PB):
        x_in = _tile_rows(refs[0], refs[1], bb) if with_ctx else refs[0][bb]
        mod = mod_ref[bb, 0]
        x1 = x_in + mod[:, 2 * D:3 * D] * y[bb * TM:(bb + 1) * TM]
        x1_ref[bb] = x1
        h2_parts.append(_modulated(x1, mod, lng_ref[...], 3, 4))
        h2_ref[bb] = h2_parts[-1].astype(BF16)
    logits = _dot3(jnp.concatenate(h2_parts, axis=0), rw_ref[...]) + rb_ref[...]
    lanef, i1, i2, w1, w2, onehot = _route(logits)
    for bb in range(PB):
        rows = slice(bb * TM, (bb + 1) * TM)
        hot = onehot[rows]
        before = _dot(tri_ref[...], hot.astype(BF16))
        cnt = jnp.sum(hot, axis=0, keepdims=True)
        g8 = jnp.floor((cnt + float(GRP - 1)) * (1.0 / GRP))
        goff = _dot(jnp.broadcast_to(g8, (8, 128)).astype(BF16), upper_ref[...])[0:1, :]
        local = before + float(GRP) * goff
        lane = lax.broadcasted_iota(jnp.int32, (TM, 128), 1).astype(F32)
        e1, e2 = i1[rows], i2[rows]
        l1 = jnp.sum(jnp.where(lane == e1, local, 0.0), axis=1, keepdims=True)
        l2 = jnp.sum(jnp.where(lane == e2, local, 0.0), axis=1, keepdims=True)
        r_ref[bb] = jnp.where(lane == 0.0, e1, jnp.where(lane == 1.0, e2, jnp.where(lane == 2.0, w1[rows], jnp.where(
            lane == 3.0, w2[rows], jnp.where(lane == 4.0, l1, jnp.where(lane == 5.0, l2, 0.0))))))
        g8_ref[bb, 0] = g8


def _out_project(a, bmix, residual, modsel, w_out, ln_g, rw, rb, tri, upper):
    b, s_out, _ = a.shape
    nt = s_out // TM
    const = lambda bi, i: (0, 0)
    with_ctx = len(residual) == 2
    sel = (lambda i: jnp.minimum(i, 1)) if with_ctx else (lambda i: 1)
    tok = lambda bi, i: (bi, i, 0)
    first_lat = residual[0].shape[1] // TM - nt
    res_specs = _token_specs(0) if with_ctx else [pl.BlockSpec((PB, TM, D), lambda bi, i: (bi, i + first_lat, 0))]
    return pl.pallas_call(
        functools.partial(_outproj_kernel, with_ctx=with_ctx),
        grid=(b // PB, nt),
        in_specs=[
            pl.BlockSpec((PB, TM, QW), tok),
            pl.BlockSpec((PB, TM, UW), tok),
        ] + res_specs + [
            pl.BlockSpec((PB, 1, 1, 6 * D), lambda bi, i: (bi, sel(i), 0, 0)),
            pl.BlockSpec((D, D), const),
            pl.BlockSpec((1, D), const),
            pl.BlockSpec((D, 128), const),
            pl.BlockSpec((1, 128), const),
            pl.BlockSpec((TM, TM), const),
            pl.BlockSpec((128, 128), const),
        ],
        out_specs=[
            pl.BlockSpec((PB, TM, D), tok),
            pl.BlockSpec((PB, TM, D), tok),
            pl.BlockSpec((PB, TM, 128), tok),
            pl.BlockSpec((PB, 1, 1, 128), lambda bi, i: (bi, i, 0, 0)),
        ],
        out_shape=[
            jax.ShapeDtypeStruct((b, s_out, D), F32),
            jax.ShapeDtypeStruct((b, s_out, D), BF16),
            jax.ShapeDtypeStruct((b, s_out, 128), F32),
            jax.ShapeDtypeStruct((b, nt, 1, 128), F32),
        ],
        compiler_params=_params(("arbitrary", "arbitrary")),
        name="out_projection_router",
    )(a, bmix, *residual, modsel, w_out, ln_g, rw, rb, tri, upper)


def _pack_halves(x, exact=False):
    half = x.shape[1] // 2
    a, b = x[:, 0:half], x[:, half:]
    if not exact:
        a, b = a.astype(BF16).astype(F32), b.astype(BF16).astype(F32)
    return lax.bitcast_convert_type(a, U32) | (lax.bitcast_convert_type(b, U32) >> 16)


def _unpack_halves(w):
    a = lax.bitcast_convert_type(w & jnp.uint32(0xFFFF0000), F32)
    b = lax.bitcast_convert_type(w << 16, F32)
    return jnp.concatenate([a, b], axis=1).astype(BF16)


def _rows_copy(src_ref, src_row, dst_ref, dst_row, rows, sem):
    rows = pl.multiple_of(rows, GRP)
    return pltpu.make_async_copy(src_ref.at[pl.ds(pl.multiple_of(src_row, GRP), rows), :],
                                 dst_ref.at[pl.ds(pl.multiple_of(dst_row, GRP), rows), :], sem)


def _wait_rows(count, src_ref, dst_ref, sem):
    @pl.when(count > 0)
    def _():
        _rows_copy(src_ref, 0, dst_ref, 0, count * GRP, sem).wait()


def _start_runs(runs, tile, copy):
    dst_ref, off_ref, len_ref = runs
    for e in range(N_EXP):
        idx = tile * N_EXP + e
        rows = len_ref[idx]

        @pl.when(rows > 0)
        def _():
            copy(dst_ref[idx], off_ref[idx], rows).start(priority=1)


def _dispatch_kernel(dst_ref, off_ref, len_ref, ng_ref, zdst_ref, zlen_ref, h_ref, r_ref, xs_ref, loc, zbuf, sems,
                     zsem):
    runs = (dst_ref, off_ref, len_ref)
    i = pl.program_id(0)
    n = pl.num_programs(0)

    @pl.when(i == 0)
    def _():
        zbuf[...] = jnp.zeros(zbuf.shape, U32)
        for e in range(N_EXP):
            pl.when(zlen_ref[e] > 0)(lambda e=e: _rows_copy(zbuf, 0, xs_ref, zdst_ref[e], zlen_ref[e], zsem).start())
        for e in range(N_EXP):
            pl.when(zlen_ref[e] > 0)(lambda e=e: _rows_copy(zbuf, 0, xs_ref, zdst_ref[e], zlen_ref[e], zsem).wait())

    def run(slot):
        buf = loc.at[slot]
        sem = sems.at[slot]

        @pl.when(i >= 2)
        def _():
            _wait_rows(ng_ref[i - 2], buf, xs_ref, sem)

        rt = r_ref[...].T
        rows = lax.broadcasted_iota(jnp.int32, (LR, TM), 0).astype(F32)
        perm = jnp.where((rows == rt[4:5, :]) | (rows == rt[5:6, :]), 1.0, 0.0).astype(BF16)
        buf[...] = _pack_halves(_dot(perm, h_ref[...]), exact=True)

        _start_runs(runs, i, lambda dst, off, rows: _rows_copy(buf, off, xs_ref, dst, rows, sem))

        @pl.when(i == n - 1)
        def _():
            _wait_rows(ng_ref[i], buf, xs_ref, sem)

            @pl.when(i >= 1)
            def _():
                _wait_rows(ng_ref[i - 1], loc.at[1 - slot], xs_ref, sems.at[1 - slot])

    pl.when(i % 2 == 0)(lambda: run(0))
    pl.when(i % 2 == 1)(lambda: run(1))


def _dispatch(runs, ng, zdst, zlen, h2, rinfo, cap):
    t = h2.shape[0]
    tok = lambda i, *_: (i, 0)
    return pl.pallas_call(
        _dispatch_kernel,
        grid_spec=pltpu.PrefetchScalarGridSpec(
            num_scalar_prefetch=6,
            grid=(t // TM,),
            in_specs=[
                pl.BlockSpec((TM, D), tok),
                pl.BlockSpec((TM, 128), tok),
            ],
            out_specs=pl.BlockSpec(memory_space=pl.ANY),
            scratch_shapes=[
                pltpu.VMEM((2, LR, D // 2), U32),
                pltpu.VMEM((BLK, D // 2), U32),
                pltpu.SemaphoreType.DMA((2,)),
                pltpu.SemaphoreType.DMA,
            ],
        ),
        out_shape=jax.ShapeDtypeStruct((cap, D // 2), U32),
        compiler_params=_params(("arbitrary",)),
        name="moe_dispatch",
    )(*runs, ng, zdst, zlen, h2, rinfo)


def _expert_kernel(be_ref, nu_ref, x_ref, wg_ref, wu_ref, wd_ref, y_ref, wg_sc, wu_sc, wd_sc):
    i = pl.program_id(0)
    used = i < nu_ref[0]
    fresh = (i == 0) | (be_ref[i] != be_ref[jnp.maximum(i - 1, 0)])

    @pl.when(used & fresh)
    def _():
        wg_sc[...] = wg_ref[0, 0].astype(BF16)
        wu_sc[...] = wu_ref[0, 0].astype(BF16)
        wd_sc[...] = wd_ref[0, 0].astype(BF16)

    @pl.when(used)
    def _():
        xb = _unpack_halves(x_ref[...])
        g = _dot(xb, wg_sc[...])
        u = _dot(xb, wu_sc[...])
        hid = (g * _sigmoid(g)) * u
        y_ref[...] = _pack_halves(_dot(hid.astype(BF16), wd_sc[...]))

    @pl.when(jnp.logical_not(used))
    def _():
        y_ref[...] = jnp.zeros(y_ref.shape, U32)


def _experts(blk_e, n_used, xs, wg, wu, wd, layer):
    cap = xs.shape[0]
    row = lambda i, be, nu: (jnp.minimum(i, nu[0] - 1), 0)
    wsel = lambda i, be, nu: (layer, be[i], 0, 0)
    return pl.pallas_call(
        _expert_kernel,
        grid_spec=pltpu.PrefetchScalarGridSpec(
            num_scalar_prefetch=2,
            grid=(cap // BLK,),
            in_specs=[
                pl.BlockSpec((BLK, D // 2), row),
                pl.BlockSpec((1, 1, D, D_EXP), wsel),
                pl.BlockSpec((1, 1, D, D_EXP), wsel),
                pl.BlockSpec((1, 1, D_EXP, D), wsel),
            ],
            out_specs=pl.BlockSpec((BLK, D // 2), lambda i, be, nu: (i, 0)),
            scratch_shapes=[
                pltpu.VMEM((D, D_EXP), BF16),
                pltpu.VMEM((D, D_EXP), BF16),
                pltpu.VMEM((D_EXP, D), BF16),
            ],
        ),
        out_shape=jax.ShapeDtypeStruct((cap, D // 2), U32),
        compiler_params=_params(("arbitrary",)),
        name="moe_experts",
    )(blk_e, n_used, xs, wg, wu, wd)


def _combine_kernel(dst_ref, off_ref, len_ref, ng_ref, y_ref, x1_ref, r_ref, mod_ref, o_ref, loc, sems):
    runs = (dst_ref, off_ref, len_ref)
    i = pl.program_id(0)
    n = pl.num_programs(0)

    def fetch(tile, slot):
        _start_runs(runs, tile, lambda dst, off, rows: _rows_copy(y_ref, dst, loc.at[slot], off, rows, sems.at[slot]))

    @pl.when(i == 0)
    def _():
        loc[...] = jnp.zeros(loc.shape, U32)
        fetch(0, 0)

    def run(slot):
        @pl.when(i + 1 < n)
        def _():
            fetch(i + 1, 1 - slot)

        _wait_rows(ng_ref[i], y_ref, loc.at[slot], sems.at[slot])
        o_ref[...] = _unsort_add(loc[slot], r_ref[...], x1_ref[...], mod_ref[0, 0][:, 5 * D:6 * D])

    pl.when(i % 2 == 0)(lambda: run(0))
    pl.when(i % 2 == 1)(lambda: run(1))


def _combine(runs, ng, ys, x1, rinfo, modsel, *, tiles_per_batch, with_ctx):
    t = x1.shape[0]
    if with_ctx:
        msel = lambda i, *_: (i // tiles_per_batch, jnp.minimum(i % tiles_per_batch, 1), 0, 0)
    else:
        msel = lambda i, *_: (i // tiles_per_batch, 1, 0, 0)
    tok = lambda i, *_: (i, 0)
    return pl.pallas_call(
        _combine_kernel,
        grid_spec=pltpu.PrefetchScalarGridSpec(
            num_scalar_prefetch=4,
            grid=(t // TM,),
            in_specs=[
                pl.BlockSpec(memory_space=pl.ANY),
                pl.BlockSpec((TM, D), tok),
                pl.BlockSpec((TM, 128), tok),
                pl.BlockSpec((1, 1, 1, 6 * D), msel),
            ],
            out_specs=pl.BlockSpec((TM, D), tok),
            scratch_shapes=[pltpu.VMEM((2, LR, D // 2), U32), pltpu.SemaphoreType.DMA((2,))],
        ),
        out_shape=jax.ShapeDtypeStruct((t, D), F32),
        compiler_params=_params(("arbitrary",)),
        name="moe_combine",
    )(*runs, ng, ys, x1, rinfo, modsel)


def _unsort_add(ysw, info, x1, g2):
    col = lax.broadcasted_iota(jnp.int32, (TM, LR), 1).astype(F32)
    pick = jnp.concatenate([jnp.where(col == info[:, 4:5], 1.0, 0.0).astype(BF16),
                            jnp.where(col == info[:, 5:6], 1.0, 0.0).astype(BF16)], axis=0)
    y12 = _dot(pick, _unpack_halves(ysw))
    return x1 + g2 * (info[:, 2:3] * y12[0:TM] + info[:, 3:4] * y12[TM:2 * TM])


def _combine_proj_kernel(dst_ref, off_ref, len_ref, ng_ref, y_ref, x1_ref, r_ref, mod0_ref, mod_ref, lng_ref, w_ref,
                         cos_ref, sin_ref, qg_ref, kg_ref, bdq_ref, bdk_ref,
                         xc_ref, q_ref, k_ref, v_ref, u_ref, loc, sems, *, nt):
    runs = (dst_ref, off_ref, len_ref)
    bp = pl.program_id(0)
    i = pl.program_id(1)
    step = bp * nt + i
    slot = step % 2

    def fetch(bp_, i_, slot_):
        for bb in range(PB):
            tile = (bp_ * PB + bb) * nt + i_
            _start_runs(runs, tile, lambda dst, off, rows, bb=bb: _rows_copy(
                y_ref, dst, loc.at[slot_, bb], off, rows, sems.at[slot_, bb]))

    @pl.when(step == 0)
    def _():
        loc[...] = jnp.zeros(loc.shape, U32)
        fetch(0, 0, 0)

    @pl.when(step + 1 < pl.num_programs(0) * nt)
    def _():
        wrap = i + 1 == nt
        fetch(jnp.where(wrap, bp + 1, bp), jnp.where(wrap, 0, i + 1), 1 - slot)

    @pl.when(step >= 0)
    def _():
        for bb in range(PB):
            tile = (bp * PB + bb) * nt + i
            _wait_rows(ng_ref[tile], y_ref, loc.at[slot, bb], sems.at[slot, bb])
            xc_ref[bb] = _unsort_add(loc[slot, bb], r_ref[bb], x1_ref[bb], mod0_ref[bb, 0][:, 5 * D:6 * D])

    @pl.when(step >= -1)
    def _():
        h = jnp.concatenate([_modulated(xc_ref[bb], mod_ref[bb, 0], lng_ref[...], 0, 1) for bb in range(PB)], axis=0)
        px = _dot(h.astype(BF16), w_ref[...])
        _proj_epilogue(px, cos_ref, sin_ref, qg_ref, kg_ref, bdq_ref, bdk_ref, q_ref, k_ref, v_ref, u_ref, False)


def _combine_project(runs, ng, ys, x1, rinfo, modsel0, modsel, ln_g, w_in, cos, sin, q_g, k_g, bdq, bdk):
    b, s_tot, _ = x1.shape
    nt = s_tot // TM
    n_in = w_in.shape[1]
    const = lambda bi, i, *_: (0, 0)
    tok = lambda bi, i, *_: (bi, i, 0)
    msel = lambda bi, i, *_: (bi, jnp.minimum(i, 1), 0, 0)
    head = lambda bi, i, *_: (bi, 0, i, 0)
    return pl.pallas_call(
        functools.partial(_combine_proj_kernel, nt=nt),
        grid_spec=pltpu.PrefetchScalarGridSpec(
            num_scalar_prefetch=4,
            grid=(b // PB, nt),
            in_specs=[
                pl.BlockSpec(memory_space=pl.ANY),
                pl.BlockSpec((PB, TM, D), tok),
                pl.BlockSpec((PB, TM, 128), tok),
                pl.BlockSpec((PB, 1, 1, 6 * D), msel),
                pl.BlockSpec((PB, 1, 1, 6 * D), msel),
                pl.BlockSpec((1, D), const),
                pl.BlockSpec((D, n_in), const),
                pl.BlockSpec((TM, 128), lambda bi, i, *_: (i, 0)),
                pl.BlockSpec((TM, 128), lambda bi, i, *_: (i, 0)),
                pl.BlockSpec((1, QW), const),
                pl.BlockSpec((1, KW), const),
                pl.BlockSpec((QW, QW), const),
                pl.BlockSpec((KW, KW), const),
            ],
            out_specs=[
                pl.BlockSpec((PB, TM, D), tok),
                pl.BlockSpec((PB, NH, TM, HD), head),
                pl.BlockSpec((PB, NKV, TM, HD), head),
                pl.BlockSpec((PB, NKV, TM, 2 * HD), head),
                pl.BlockSpec((PB, TM, UW), tok),
            ],
            scratch_shapes=[pltpu.VMEM((2, PB, LR, D // 2), U32), pltpu.SemaphoreType.DMA((2, PB))],
        ),
        out_shape=[
            jax.ShapeDtypeStruct((b, s_tot, D), F32),
            jax.ShapeDtypeStruct((b, NH, s_tot, HD), BF16),
            jax.ShapeDtypeStruct((b, NKV, s_tot, HD), BF16),
            jax.ShapeDtypeStruct((b, NKV, s_tot, 2 * HD), BF16),
            jax.ShapeDtypeStruct((b, s_tot, UW), F32),
        ],
        compiler_params=_params(("arbitrary", "arbitrary")),
        name="moe_combine_in_projection",
    )(*runs, ng, ys, x1, rinfo, modsel0, modsel, ln_g, w_in, cos, sin, q_g, k_g, bdq, bdk)


def _moe(x1, h2, rinfo, g8, modsel, wg, wu, wd, layer, *, with_ctx, combine=True):
    b, s, _ = x1.shape
    t = b * s
    n_tiles = t // TM
    i32 = jnp.int32
    grp = g8.reshape(n_tiles, 128)[:, 0:N_EXP].astype(i32)
    n_blk = -(-(2 * t + n_tiles * N_EXP * (GRP - 1) + N_EXP * (BLK - 1)) // BLK)
    cap = n_blk * BLK
    goff_end = jnp.cumsum(grp, axis=1)
    ng = goff_end[:, N_EXP - 1].astype(i32)
    rows_e = GRP * jnp.sum(grp, axis=0)
    padded = (rows_e + BLK - 1) // BLK * BLK
    pend = jnp.cumsum(padded)
    pstart = pend - padded
    run_dst = (pstart[None, :] + GRP * (jnp.cumsum(grp, axis=0) - grp)).reshape(-1).astype(i32)
    run_off = (GRP * (goff_end - grp)).reshape(-1).astype(i32)
    run_len = (GRP * grp).reshape(-1).astype(i32)
    runs = (run_dst, run_off, run_len)
    zdst = (pstart + rows_e).astype(i32)
    zlen = (padded - rows_e).astype(i32)
    blk_e = jnp.minimum(jnp.sum((pend[None, :] <= (jnp.arange(n_blk, dtype=i32) * BLK)[:, None]).astype(i32), axis=1),
                        N_EXP - 1).astype(i32)
    n_used = (pend[N_EXP - 1:N_EXP] // BLK).astype(i32)
    info = rinfo.reshape(t, 128)
    xs = _dispatch(runs, ng, zdst, zlen, h2.reshape(t, D), info, cap)
    ys = _experts(blk_e, n_used, xs, wg, wu, wd, layer)
    if not combine:
        return runs, ng, ys
    out = _combine(runs, ng, ys, x1.reshape(t, D), info, modsel, tiles_per_batch=s // TM, with_ctx=with_ctx)
    return out.reshape(b, s, D)


def _rope_tables(n_ctx, s_lat):
    rows = s_lat // GRID_W
    row = jnp.repeat(jnp.arange(rows, dtype=F32), GRID_W)
    col = jnp.tile(jnp.arange(GRID_W, dtype=F32), rows)
    n_freq = HD // 4
    inv = ROPE_THETA ** (-jnp.arange(n_freq, dtype=F32) / n_freq)
    ang = jnp.concatenate([row[:, None] * inv, col[:, None] * inv], axis=-1)
    cos = jnp.concatenate([jnp.ones((n_ctx, HD // 2), F32), jnp.cos(ang)], axis=0)
    sin = jnp.concatenate([jnp.zeros((n_ctx, HD // 2), F32), jnp.sin(ang)], axis=0)
    cos = jnp.concatenate([cos, cos, cos, cos], axis=1)
    sin = jnp.concatenate([-sin, sin, -sin, sin], axis=1)
    return cos, sin


def _block_diag_ones(n):
    i = jnp.arange(n) // HD
    return (i[:, None] == i[None, :]).astype(BF16)


def kernel(x, c, ctx, c_ctx, mod_w, mod_b, ln1_g, ln2_g, w_in_ab, w_out_ab, q_norm_a, k_norm_a, conv_w, conv_b,
           conv_ln_g, conv_ln_b, w_in_cd, w_out_cd, q_norm_c, k_norm_c, sink_c, pool_w, pool_scale,
           rt_grp_w, rt_grp_b, rt_exp_w, rt_exp_b, ex_gate, ex_up, ex_down):
    b, s_lat, _ = x.shape
    n_ctx = ctx.shape[1]
    assert n_ctx == TM and s_lat % TM == 0 and b <= 8 and b % PB == 0

    cc = jnp.zeros((16, D), F32).at[0:b].set(c).at[8].set(c_ctx)
    mod = _modulation(cc, mod_w, mod_b)

    def modsel(l):
        return jnp.stack([jnp.broadcast_to(mod[l, 8], (b, 6 * D)), mod[l, 0:b]], axis=1).reshape(b, 2, 1, 6 * D)

    cos, sin = _rope_tables(n_ctx, s_lat)
    bdq = _block_diag_ones(QW)
    bdk = _block_diag_ones(KW)
    tri = (jnp.arange(TM)[:, None] > jnp.arange(TM)[None, :]).astype(BF16)
    upper = (jnp.arange(128)[:, None] < jnp.arange(128)[None, :]).astype(BF16)

    def router(l):
        rw = jnp.zeros((D, 128), F32).at[:, 0:N_EXP].set(rt_exp_w[l]).at[:, N_EXP:N_EXP + N_GROUPS].set(rt_grp_w[l])
        rb = jnp.zeros((1, 128), F32).at[0, 0:N_EXP].set(rt_exp_b[l]).at[0, N_EXP:N_EXP + N_GROUPS].set(rt_grp_b[l])
        return rw, rb

    def tile_gain(g, n):
        return jnp.tile(g, n).reshape(1, n * HD)

    s_tot = n_ctx + s_lat

    ms = modsel(0)
    q, k, v, glu = _project(ctx, x, 0, s_tot, ms, ln1_g[0:1], w_in_ab[0].astype(BF16), cos, sin,
                            tile_gain(q_norm_a[0], NH), tile_gain(k_norm_a[0], NKV), bdq, bdk, glu=True)
    att = _dense_attention(q, k, v, n_ctx)
    cv = _conv_module(glu, conv_w[0], conv_b[0:1], conv_ln_g[0:1], conv_ln_b[0:1], n_ctx)
    rw, rb = router(0)
    x1, h2, rinfo, g8 = _out_project(att, cv, (ctx, x), ms, w_out_ab[0].astype(BF16), ln2_g[0:1], rw, rb, tri, upper)
    runs, ng, ys = _moe(x1, h2, rinfo, g8, ms, ex_gate, ex_up, ex_down, 0, with_ctx=True, combine=False)

    ms0, ms = ms, modsel(1)
    xc, q, k, v, u = _combine_project(runs, ng, ys, x1, rinfo, ms0, ms, ln1_g[1:2], w_in_cd[0].astype(BF16), cos, sin,
                                      tile_gain(q_norm_c[0], NH), tile_gain(k_norm_c[0], NKV), bdq, bdk)
    att = _window_attention(sink_c[0], q, k, v, n_ctx)
    pm = _pool_mixer(u, pool_w[0].astype(BF16), pool_scale[0:1], n_ctx)
    rw, rb = router(1)
    x1, h2, rinfo, g8 = _out_project(att, pm, (xc,), ms, w_out_cd[0].astype(BF16), ln2_g[1:2], rw, rb, tri, upper)
    return _moe(x1, h2, rinfo, g8, ms, ex_gate, ex_up, ex_down, 1, with_ctx=False)
```

```python
import functools

import jax
import jax.numpy as jnp
from jax import lax
from jax.experimental import pallas as pl
from jax.experimental.pallas import tpu as pltpu

F32 = jnp.float32
BF16 = jnp.bfloat16
U32 = jnp.uint32

D = 1024
HD = 64
NH = 8
NKV = 2
GQ = NH // NKV
QW = NH * HD
KW = NKV * HD
UW = 512
EPS = 1e-6
ROPE_THETA = 10000.0
GRID_W = 64
CONV_K = 31
WINDOW = 128
POOL_SIZES = (2, 4, 8, 16)
N_GROUPS = 4
PER_GROUP = 8
N_EXP = N_GROUPS * PER_GROUP
D_EXP = D // 2

TM = 256
PB = 4
TQ = 256
TQD = 256
TKL = 512
SUB = 64
HALO = 16
BLK = 1024
GRP = 8
NG = 96
LR = NG * GRP
VMEM_LIMIT = 56 * 1024 * 1024


def _sigmoid(x):
    return 1.0 / (1.0 + jnp.exp(-x))


def _dot(a, b):
    return jnp.dot(a, b, preferred_element_type=F32)


def _dot_nt(a, b):
    return lax.dot_general(a, b, (((1,), (1,)), ((), ())), preferred_element_type=F32)


def _split(a):
    hi = a.astype(BF16)
    lo = (a - hi.astype(F32)).astype(BF16)
    return hi, lo


def _dot3(a, w):
    a_hi, a_lo = _split(a)
    w_hi, w_lo = _split(w)
    return _dot(a_hi, w_hi) + _dot(a_lo, w_hi) + _dot(a_hi, w_lo)


def _rmsnorm(x, g):
    return x * lax.rsqrt(jnp.mean(x * x, axis=-1, keepdims=True) + EPS) * g


def _params(sem, vmem=VMEM_LIMIT):
    return pltpu.CompilerParams(dimension_semantics=sem, vmem_limit_bytes=vmem)


def _mod_kernel(c_ref, w_ref, b_ref, o_ref):
    a = c_ref[...]
    a = a * _sigmoid(a)
    o_ref[0] = _dot3(a, w_ref[0]) + b_ref[0]


def _modulation(cc, mod_w, mod_b):
    depth = mod_w.shape[0]
    tn = 1024
    return pl.pallas_call(
        _mod_kernel,
        grid=(depth, 6 * D // tn),
        in_specs=[
            pl.BlockSpec((16, D), lambda l, j: (0, 0)),
            pl.BlockSpec((1, D, tn), lambda l, j: (l, 0, j)),
            pl.BlockSpec((1, 1, tn), lambda l, j: (l, 0, j)),
        ],
        out_specs=pl.BlockSpec((1, 16, tn), lambda l, j: (l, 0, j)),
        out_shape=jax.ShapeDtypeStruct((depth, 16, 6 * D), F32),
        compiler_params=_params(("arbitrary", "arbitrary")),
        name="modulation",
    )(cc, mod_w, mod_b.reshape(depth, 1, 6 * D))


def _swap_halves(t):
    w = t.shape[1]
    lane = lax.broadcasted_iota(jnp.int32, t.shape, 1)
    first = (lane & (HD - 1)) < (HD // 2)
    return jnp.where(first, pltpu.roll(t, w - HD // 2, 1), pltpu.roll(t, HD // 2, 1))


def _head_norm_rope(t, bd, g, cos, sin, scale):
    ssq = _dot((t * t).astype(BF16), bd)
    tn = t * lax.rsqrt(ssq * (1.0 / HD) + EPS) * g
    n = t.shape[1] // 128
    if n > 1:
        cos = jnp.concatenate([cos] * n, axis=1)
        sin = jnp.concatenate([sin] * n, axis=1)
    out = tn * cos + _swap_halves(tn) * sin
    return out * scale if scale != 1.0 else out


def _tile_rows(c_ref, x_ref, bb):
    return jnp.where(pl.program_id(1) == 0, c_ref[bb], x_ref[bb])


def _modulated(x, mod, lng, shift_at, scale_at):
    return _rmsnorm(x, lng) * (1.0 + mod[:, scale_at * D:(scale_at + 1) * D]) + mod[:, shift_at * D:(shift_at + 1) * D]


def _proj_kernel(c_ref, x_ref, mod_ref, lng_ref, w_ref, cos_ref, sin_ref, qg_ref, kg_ref, bdq_ref, bdk_ref,
                 q_ref, k_ref, v_ref, u_ref, *, glu):
    h = jnp.concatenate([_modulated(_tile_rows(c_ref, x_ref, bb), mod_ref[bb, 0], lng_ref[...], 0, 1)
                         for bb in range(PB)], axis=0)
    px = _dot(h.astype(BF16), w_ref[...])
    _proj_epilogue(px, cos_ref, sin_ref, qg_ref, kg_ref, bdq_ref, bdk_ref, q_ref, k_ref, v_ref, u_ref, glu)


def _proj_epilogue(px, cos_ref, sin_ref, qg_ref, kg_ref, bdq_ref, bdk_ref, q_ref, k_ref, v_ref, u_ref, glu):
    cos = jnp.concatenate([cos_ref[...]] * PB, axis=0)
    sin = jnp.concatenate([sin_ref[...]] * PB, axis=0)
    q = _head_norm_rope(px[:, 0:QW], bdq_ref[...], qg_ref[...], cos, sin, HD ** -0.5)
    k = _head_norm_rope(px[:, QW:QW + KW], bdk_ref[...], kg_ref[...], cos, sin, 1.0)
    v = px[:, QW + KW:QW + 2 * KW]
    lane = lax.broadcasted_iota(jnp.int32, (TM, KW), 1)
    ones_col = jnp.where(lane == HD, 1.0, 0.0)
    o_u = QW + 2 * KW
    if glu:
        u = px[:, o_u:o_u + UW] * _sigmoid(px[:, o_u + UW:o_u + 2 * UW])
    else:
        u = px[:, o_u:o_u + UW]
    for bb in range(PB):
        rows = slice(bb * TM, (bb + 1) * TM)
        for hh in range(NH):
            q_ref[bb, hh] = q[rows, hh * HD:(hh + 1) * HD].astype(BF16)
        for j in range(NKV):
            k_ref[bb, j] = k[rows, j * HD:(j + 1) * HD].astype(BF16)
            vj = v[rows] if j == 0 else pltpu.roll(v[rows], KW - j * HD, 1)
            v_ref[bb, j] = jnp.where(lane < HD, vj, ones_col).astype(BF16)
        u_ref[bb] = u[rows]


def _token_specs(lat_first):
    return [pl.BlockSpec((PB, TM, D), lambda bi, i: (bi, 0, 0)),
            pl.BlockSpec((PB, TM, D), lambda bi, i: (bi, jnp.maximum(i - 1, 0) + lat_first, 0))]


def _project(ctx_rows, lat_rows, lat_first, s_tot, modsel, ln_g, w_in, cos, sin, q_g, k_g, bdq, bdk, *, glu):
    b = ctx_rows.shape[0]
    n_in = w_in.shape[1]
    nt = s_tot // TM
    const = lambda bi, i: (0, 0)
    return pl.pallas_call(
        functools.partial(_proj_kernel, glu=glu),
        grid=(b // PB, nt),
        in_specs=_token_specs(lat_first) + [
            pl.BlockSpec((PB, 1, 1, 6 * D), lambda bi, i: (bi, jnp.minimum(i, 1), 0, 0)),
            pl.BlockSpec((1, D), const),
            pl.BlockSpec((D, n_in), const),
            pl.BlockSpec((TM, 128), lambda bi, i: (i, 0)),
            pl.BlockSpec((TM, 128), lambda bi, i: (i, 0)),
            pl.BlockSpec((1, QW), const),
            pl.BlockSpec((1, KW), const),
            pl.BlockSpec((QW, QW), const),
            pl.BlockSpec((KW, KW), const),
        ],
        out_specs=[
            pl.BlockSpec((PB, NH, TM, HD), lambda bi, i: (bi, 0, i, 0)),
            pl.BlockSpec((PB, NKV, TM, HD), lambda bi, i: (bi, 0, i, 0)),
            pl.BlockSpec((PB, NKV, TM, 2 * HD), lambda bi, i: (bi, 0, i, 0)),
            pl.BlockSpec((PB, TM, UW), lambda bi, i: (bi, i, 0)),
        ],
        out_shape=[
            jax.ShapeDtypeStruct((b, NH, s_tot, HD), BF16),
            jax.ShapeDtypeStruct((b, NKV, s_tot, HD), BF16),
            jax.ShapeDtypeStruct((b, NKV, s_tot, 2 * HD), BF16),
            jax.ShapeDtypeStruct((b, s_tot, UW), F32),
        ],
        compiler_params=_params(("arbitrary", "arbitrary")),
        name="in_projection",
    )(ctx_rows, lat_rows, modsel, ln_g, w_in, cos, sin, q_g, k_g, bdq, bdk)


def _merge_heads(o):
    tq = o.shape[0] // GQ
    return jnp.concatenate([o[g * tq:(g + 1) * tq] for g in range(GQ)], axis=1)


def _lane_max(s):
    return functools.reduce(jnp.maximum, [s[:, j * 128:(j + 1) * 128] for j in range(s.shape[1] // 128)])


def _dense_attn_kernel(q_ref, k_ref, v_ref, o_ref, sc_sc, sl_sc, m_sc, *, n_ctx, s_tot):
    qi = pl.program_id(2)
    q = q_ref[0].reshape(GQ * TQD, HD)
    n_lat = (s_tot - n_ctx) // TKL

    def chunks(n):
        spans = [(0, n_ctx)] + [(n_ctx + c * TKL, n_ctx + (c + 1) * TKL) for c in range(n)]
        return list(zip(spans, [sc_sc] + [sl_sc.at[c] for c in range(n)]))

    def scores(n):
        for i, ((lo, hi), slot) in enumerate(chunks(n)):
            s = _dot_nt(q, k_ref[0, 0, lo:hi, :])
            slot[...] = s
            m_sc[i] = _lane_max(s)
        m_lane = functools.reduce(jnp.maximum, [m_sc[i] for i in range(n + 1)])
        m_sc[0] = jnp.broadcast_to(jnp.max(m_lane, axis=1, keepdims=True), m_lane.shape)

    def weighted(n):
        acc = None
        for (lo, hi), slot in chunks(n):
            m = jnp.concatenate([m_sc[0]] * ((hi - lo) // 128), axis=1)
            part = _dot(jnp.exp(slot[...] - m).astype(BF16), v_ref[0, 0, lo:hi, :])
            acc = part if acc is None else acc + part
        o_ref[0] = _merge_heads(acc[:, 0:HD] / acc[:, HD:HD + 1]).astype(BF16)

    is_lat = qi * TQD >= n_ctx
    pl.when(is_lat)(lambda: scores(n_lat))
    pl.when(jnp.logical_not(is_lat))(lambda: scores(0))
    pl.when(is_lat)(lambda: weighted(n_lat))
    pl.when(jnp.logical_not(is_lat))(lambda: weighted(0))


def _dense_attention(q, k, v, n_ctx):
    b, _, s_tot, _ = q.shape
    rows = GQ * TQD
    return pl.pallas_call(
        functools.partial(_dense_attn_kernel, n_ctx=n_ctx, s_tot=s_tot),
        grid=(b, NKV, s_tot // TQD),
        in_specs=[
            pl.BlockSpec((1, GQ, TQD, HD), lambda bi, j, i: (bi, j, i, 0)),
            pl.BlockSpec((1, 1, s_tot, HD), lambda bi, j, i: (bi, j, 0, 0)),
            pl.BlockSpec((1, 1, s_tot, 2 * HD), lambda bi, j, i: (bi, j, 0, 0)),
        ],
        out_specs=pl.BlockSpec((1, TQD, GQ * HD), lambda bi, j, i: (bi, i, j)),
        out_shape=jax.ShapeDtypeStruct((b, s_tot, QW), BF16),
        scratch_shapes=[
            pltpu.VMEM((rows, n_ctx), F32),
            pltpu.VMEM(((s_tot - n_ctx) // TKL, rows, TKL), F32),
            pltpu.VMEM(((s_tot - n_ctx) // TKL + 1, rows, 128), F32),
        ],
        compiler_params=_params(("arbitrary", "arbitrary", "arbitrary")),
        name="dense_attention",
    )(q, k, v)


def _window_attn_kernel(sink_ref, q_ref, k_ref, v_ref, o_ref, sc_sc, sw_sc, m_sc, e_sc, *, n_ctx, s_tot):
    qi = pl.program_id(1)
    span = TQ + 2 * WINDOW
    q0 = n_ctx + qi * TQ
    start = pl.multiple_of(jnp.clip(q0 - WINDOW, n_ctx, s_tot - span), 128)
    row = lax.broadcasted_iota(jnp.int32, (TQ, span), 0)
    col = lax.broadcasted_iota(jnp.int32, (TQ, span), 1)
    band = jnp.where(jnp.abs((q0 - start) + row - col) <= WINDOW, 0.0, -jnp.inf)
    band = jnp.concatenate([band] * GQ, axis=0)
    for j in range(NKV):
        q = q_ref[0, j * GQ:(j + 1) * GQ].reshape(GQ * TQ, HD)
        s_c = _dot_nt(q, k_ref[0, j, 0:n_ctx, :])
        s_w = _dot_nt(q, k_ref[0, j, pl.ds(start, span), :]) + band
        sc_sc[j] = s_c
        sw_sc[j] = s_w
        sink = jnp.concatenate([jnp.full((TQ, 128), sink_ref[j * GQ + g], F32) for g in range(GQ)], axis=0)
        m_lane = jnp.maximum(_lane_max(s_c), _lane_max(s_w))
        m_sc[j] = jnp.maximum(jnp.broadcast_to(jnp.max(m_lane, axis=1, keepdims=True), m_lane.shape), sink)
        e_sc[j] = jnp.exp(sink - m_sc[j])
    for j in range(NKV):
        m = m_sc[j]
        p_c = jnp.exp(sc_sc[j] - jnp.concatenate([m] * (n_ctx // 128), axis=1)).astype(BF16)
        p_w = jnp.exp(sw_sc[j] - jnp.concatenate([m] * (span // 128), axis=1)).astype(BF16)
        acc = _dot(p_c, v_ref[0, j, 0:n_ctx, :]) + _dot(p_w, v_ref[0, j, pl.ds(start, span), :])
        l = acc[:, HD:HD + 1] + e_sc[j][:, 0:1]
        o_ref[0, :, j * GQ * HD:(j + 1) * GQ * HD] = _merge_heads(acc[:, 0:HD] / l).astype(BF16)


def _window_attention(sink, q, k, v, n_ctx):
    b, _, s_tot, _ = q.shape
    s_lat = s_tot - n_ctx
    off = n_ctx // TQ
    return pl.pallas_call(
        functools.partial(_window_attn_kernel, n_ctx=n_ctx, s_tot=s_tot),
        grid_spec=pltpu.PrefetchScalarGridSpec(
            num_scalar_prefetch=1,
            grid=(b, s_lat // TQ),
            in_specs=[
                pl.BlockSpec((1, NH, TQ, HD), lambda bi, i, sk: (bi, 0, i + off, 0)),
                pl.BlockSpec((1, NKV, s_tot, HD), lambda bi, i, sk: (bi, 0, 0, 0)),
                pl.BlockSpec((1, NKV, s_tot, 2 * HD), lambda bi, i, sk: (bi, 0, 0, 0)),
            ],
            out_specs=pl.BlockSpec((1, TQ, QW), lambda bi, i, sk: (bi, i, 0)),
            scratch_shapes=[
                pltpu.VMEM((NKV, GQ * TQ, n_ctx), F32),
                pltpu.VMEM((NKV, GQ * TQ, TQ + 2 * WINDOW), F32),
                pltpu.VMEM((NKV, GQ * TQ, 128), F32),
                pltpu.VMEM((NKV, GQ * TQ, 128), F32),
            ],
        ),
        out_shape=jax.ShapeDtypeStruct((b, s_lat, QW), BF16),
        compiler_params=_params(("arbitrary", "arbitrary")),
        name="window_attention",
    )(sink, q, k, v)


def _conv_kernel(g_ref, cw_ref, cb_ref, lg_ref, lb_ref, o_ref, pad_sc, win_sc, *, n_ctx, s_tot):
    zeros = jnp.zeros((HALO, UW), F32)
    pad_sc[0:HALO, :] = zeros
    pad_sc[HALO + n_ctx:2 * HALO + n_ctx, :] = zeros
    pad_sc[2 * HALO + s_tot:3 * HALO + s_tot, :] = zeros

    def fill(i, carry):
        src = pl.multiple_of(i * TM, TM)
        dst = pl.multiple_of(src + HALO + jnp.where(src >= n_ctx, HALO, 0), 8)
        pad_sc[pl.ds(dst, TM), :] = g_ref[0, pl.ds(src, TM), :]
        return carry

    lax.fori_loop(0, s_tot // TM, fill, 0)
    half = CONV_K // 2

    def tile(i, carry):
        src = pl.multiple_of(i * SUB, SUB)
        base = pl.multiple_of(src + jnp.where(src >= n_ctx, HALO, 0), 8)
        win = pad_sc[pl.ds(base, SUB + 2 * HALO), :]
        keep = SUB + 2 * HALO - 8
        for r in range(8):
            win_sc[r, 0:keep, :] = win[r:r + keep, :]
        acc = jnp.zeros((SUB, UW), F32) + cb_ref[...]
        for t in range(CONV_K):
            a, r = divmod(HALO - half + t, 8)
            acc = acc + win_sc[r, 8 * a:8 * a + SUB, :] * cw_ref[t:t + 1, :]
        mu = jnp.mean(acc, axis=-1, keepdims=True)
        xc = acc - mu
        var = jnp.mean(xc * xc, axis=-1, keepdims=True)
        yn = xc * lax.rsqrt(var + EPS) * lg_ref[...] + lb_ref[...]
        o_ref[0, pl.ds(src, SUB), :] = (yn * _sigmoid(yn)).astype(BF16)
        return carry

    lax.fori_loop(0, s_tot // SUB, tile, 0)


def _conv_module(glu, conv_w, conv_b, ln_g, ln_b, n_ctx):
    b, s_tot, _ = glu.shape
    const = lambda bi: (0, 0)
    return pl.pallas_call(
        functools.partial(_conv_kernel, n_ctx=n_ctx, s_tot=s_tot),
        grid=(b,),
        in_specs=[
            pl.BlockSpec((1, s_tot, UW), lambda bi: (bi, 0, 0)),
            pl.BlockSpec((CONV_K, UW), const),
            pl.BlockSpec((1, UW), const),
            pl.BlockSpec((1, UW), const),
            pl.BlockSpec((1, UW), const),
        ],
        out_specs=pl.BlockSpec((1, s_tot, UW), lambda bi: (bi, 0, 0)),
        out_shape=jax.ShapeDtypeStruct((b, s_tot, UW), BF16),
        scratch_shapes=[pltpu.VMEM((s_tot + 3 * HALO, UW), F32), pltpu.VMEM((8, SUB + 2 * HALO, UW), F32)],
        compiler_params=_params(("arbitrary",)),
        name="conv_module",
    )(glu, conv_w, conv_b, ln_g, ln_b)


def _pool_kernel(u_ref, pw_ref, ps_ref, o_ref, pad_sc, *, n_ctx, s_lat):
    zeros = jnp.zeros((HALO, UW), F32)
    pad_sc[0:HALO, :] = zeros
    pad_sc[HALO + s_lat:2 * HALO + s_lat, :] = zeros

    def fill(i, carry):
        src = pl.multiple_of(i * TM, TM)
        pad_sc[pl.ds(pl.multiple_of(src + HALO, 8), TM), :] = u_ref[0, pl.ds(pl.multiple_of(src + n_ctx, 8), TM), :]
        return carry

    lax.fori_loop(0, s_lat // TM, fill, 0)
    gw = UW // len(POOL_SIZES)

    def tile(i, carry):
        src = pl.multiple_of(i * SUB, SUB)
        win = pad_sc[pl.ds(src, SUB + 2 * HALO), :]
        t = src + lax.broadcasted_iota(jnp.int32, (SUB, 1), 0)
        outs = []
        for gi, w in enumerate(POOL_SIZES):
            lanes = slice(gi * gw, (gi + 1) * gw)
            tot = jnp.zeros((SUB, gw), F32)
            for d in range(-(w // 2), w - w // 2):
                tot = tot + win[HALO + d:HALO + d + SUB, lanes]
            lo = jnp.clip(t - w // 2, 0, s_lat)
            hi = jnp.clip(t - w // 2 + w, 0, s_lat)
            p = tot / (hi - lo).astype(F32) - win[HALO:HALO + SUB, lanes]
            outs.append(_dot(p.astype(BF16), pw_ref[gi]))
        y = jnp.concatenate(outs, axis=1) * ps_ref[...]
        o_ref[0, pl.ds(src, SUB), :] = y.astype(BF16)
        return carry

    lax.fori_loop(0, s_lat // SUB, tile, 0)


def _pool_mixer(u, pool_w, pool_scale, n_ctx):
    b, s_tot, _ = u.shape
    s_lat = s_tot - n_ctx
    gw = UW // len(POOL_SIZES)
    return pl.pallas_call(
        functools.partial(_pool_kernel, n_ctx=n_ctx, s_lat=s_lat),
        grid=(b,),
        in_specs=[
            pl.BlockSpec((1, s_tot, UW), lambda bi: (bi, 0, 0)),
            pl.BlockSpec((len(POOL_SIZES), gw, gw), lambda bi: (0, 0, 0)),
            pl.BlockSpec((1, UW), lambda bi: (0, 0)),
        ],
        out_specs=pl.BlockSpec((1, s_lat, UW), lambda bi: (bi, 0, 0)),
        out_shape=jax.ShapeDtypeStruct((b, s_lat, UW), BF16),
        scratch_shapes=[pltpu.VMEM((s_lat + 2 * HALO, UW), F32)],
        compiler_params=_params(("arbitrary",)),
        name="pool_mixer",
    )(u, pool_w, pool_scale)


def _route(logits):
    shape = logits.shape
    lane = lax.broadcasted_iota(jnp.int32, shape, 1)
    lanef = lane.astype(F32)
    is_g = (lane >= N_EXP) & (lane < N_EXP + N_GROUPS)
    big = 1e9
    lg = jnp.where(is_g, logits, -jnp.inf)
    gmax = jnp.max(lg, axis=1, keepdims=True)
    gsel = jnp.min(jnp.where(is_g & (lg == gmax), lanef, big), axis=1, keepdims=True) - float(N_EXP)
    gsum = jnp.sum(jnp.exp(lg - gmax), axis=1, keepdims=True)
    g_w = 1.0 / gsum
    in_grp = (lane < N_EXP) & ((lane >> (PER_GROUP.bit_length() - 1)).astype(F32) == gsel)
    el = jnp.where(in_grp, logits, -jnp.inf)
    emax = jnp.max(el, axis=1, keepdims=True)
    ee = jnp.exp(el - emax)
    p = ee / jnp.sum(ee, axis=1, keepdims=True)
    p1 = jnp.max(jnp.where(in_grp, p, -1.0), axis=1, keepdims=True)
    i1 = jnp.min(jnp.where(in_grp & (p == p1), lanef, big), axis=1, keepdims=True)
    rest = in_grp & (lanef != i1)
    p2 = jnp.max(jnp.where(rest, p, -1.0), axis=1, keepdims=True)
    i2 = jnp.min(jnp.where(rest & (p == p2), lanef, big), axis=1, keepdims=True)
    w1 = g_w * p1 / (p1 + p2)
    w2 = g_w * p2 / (p1 + p2)
    onehot = jnp.where((lanef == i1) | (lanef == i2), 1.0, 0.0)
    return lanef, i1, i2, w1, w2, onehot


def _outproj_kernel(a_ref, b_ref, *refs, with_ctx):
    mod_ref, w_ref, lng_ref, rw_ref, rb_ref, tri_ref, upper_ref, x1_ref, h2_ref, r_ref, g8_ref = refs[-11:]
    a = jnp.concatenate([a_ref[bb] for bb in range(PB)], axis=0)
    bmix = jnp.concatenate([b_ref[bb] for bb in range(PB)], axis=0)
    y = _dot(a, w_ref[0:QW, :]) + _dot(bmix, w_ref[QW:QW + UW, :])
    h2_parts = []
    for bb in range(PB):
        x_in = _tile_rows(refs[0], refs[1], bb) if with_ctx else refs[0][bb]
        mod = mod_ref[bb, 0]
        x1 = x_in + mod[:, 2 * D:3 * D] * y[bb * TM:(bb + 1) * TM]
        x1_ref[bb] = x1
        h2_parts.append(_modulated(x1, mod, lng_ref[...], 3, 4))
        h2_ref[bb] = h2_parts[-1].astype(BF16)
    logits = _dot3(jnp.concatenate(h2_parts, axis=0), rw_ref[...]) + rb_ref[...]
    lanef, i1, i2, w1, w2, onehot = _route(logits)
    for bb in range(PB):
        rows = slice(bb * TM, (bb + 1) * TM)
        hot = onehot[rows]
        before = _dot(tri_ref[...], hot.astype(BF16))
        cnt = jnp.sum(hot, axis=0, keepdims=True)
        g8 = jnp.floor((cnt + float(GRP - 1)) * (1.0 / GRP))
        goff = _dot(jnp.broadcast_to(g8, (8, 128)).astype(BF16), upper_ref[...])[0:1, :]
        local = before + float(GRP) * goff
        lane = lax.broadcasted_iota(jnp.int32, (TM, 128), 1).astype(F32)
        e1, e2 = i1[rows], i2[rows]
        l1 = jnp.sum(jnp.where(lane == e1, local, 0.0), axis=1, keepdims=True)
        l2 = jnp.sum(jnp.where(lane == e2, local, 0.0), axis=1, keepdims=True)
        r_ref[bb] = jnp.where(lane == 0.0, e1, jnp.where(lane == 1.0, e2, jnp.where(lane == 2.0, w1[rows], jnp.where(
            lane == 3.0, w2[rows], jnp.where(lane == 4.0, l1, jnp.where(lane == 5.0, l2, 0.0))))))
        g8_ref[bb, 0] = g8


def _out_project(a, bmix, residual, modsel, w_out, ln_g, rw, rb, tri, upper):
    b, s_out, _ = a.shape
    nt = s_out // TM
    const = lambda bi, i: (0, 0)
    with_ctx = len(residual) == 2
    sel = (lambda i: jnp.minimum(i, 1)) if with_ctx else (lambda i: 1)
    tok = lambda bi, i: (bi, i, 0)
    first_lat = residual[0].shape[1] // TM - nt
    res_specs = _token_specs(0) if with_ctx else [pl.BlockSpec((PB, TM, D), lambda bi, i: (bi, i + first_lat, 0))]
    return pl.pallas_call(
        functools.partial(_outproj_kernel, with_ctx=with_ctx),
        grid=(b // PB, nt),
        in_specs=[
            pl.BlockSpec((PB, TM, QW), tok),
            pl.BlockSpec((PB, TM, UW), tok),
        ] + res_specs + [
            pl.BlockSpec((PB, 1, 1, 6 * D), lambda bi, i: (bi, sel(i), 0, 0)),
            pl.BlockSpec((D, D), const),
            pl.BlockSpec((1, D), const),
            pl.BlockSpec((D, 128), const),
            pl.BlockSpec((1, 128), const),
            pl.BlockSpec((TM, TM), const),
            pl.BlockSpec((128, 128), const),
        ],
        out_specs=[
            pl.BlockSpec((PB, TM, D), tok),
            pl.BlockSpec((PB, TM, D), tok),
            pl.BlockSpec((PB, TM, 128), tok),
            pl.BlockSpec((PB, 1, 1, 128), lambda bi, i: (bi, i, 0, 0)),
        ],
        out_shape=[
            jax.ShapeDtypeStruct((b, s_out, D), F32),
            jax.ShapeDtypeStruct((b, s_out, D), BF16),
            jax.ShapeDtypeStruct((b, s_out, 128), F32),
            jax.ShapeDtypeStruct((b, nt, 1, 128), F32),
        ],
        compiler_params=_params(("arbitrary", "arbitrary")),
        name="out_projection_router",
    )(a, bmix, *residual, modsel, w_out, ln_g, rw, rb, tri, upper)


def _pack_halves(x, exact=False):
    half = x.shape[1] // 2
    a, b = x[:, 0:half], x[:, half:]
    if not exact:
        a, b = a.astype(BF16).astype(F32), b.astype(BF16).astype(F32)
    return lax.bitcast_convert_type(a, U32) | (lax.bitcast_convert_type(b, U32) >> 16)


def _unpack_halves(w):
    a = lax.bitcast_convert_type(w & jnp.uint32(0xFFFF0000), F32)
    b = lax.bitcast_convert_type(w << 16, F32)
    return jnp.concatenate([a, b], axis=1).astype(BF16)


def _rows_copy(src_ref, src_row, dst_ref, dst_row, rows, sem):
    rows = pl.multiple_of(rows, GRP)
    return pltpu.make_async_copy(src_ref.at[pl.ds(pl.multiple_of(src_row, GRP), rows), :],
                                 dst_ref.at[pl.ds(pl.multiple_of(dst_row, GRP), rows), :], sem)


def _wait_rows(count, src_ref, dst_ref, sem):
    @pl.when(count > 0)
    def _():
        _rows_copy(src_ref, 0, dst_ref, 0, count * GRP, sem).wait()


def _start_runs(runs, tile, copy):
    dst_ref, off_ref, len_ref = runs
    for e in range(N_EXP):
        idx = tile * N_EXP + e
        rows = len_ref[idx]

        @pl.when(rows > 0)
        def _():
            copy(dst_ref[idx], off_ref[idx], rows).start(priority=1)


def _dispatch_kernel(dst_ref, off_ref, len_ref, ng_ref, zdst_ref, zlen_ref, h_ref, r_ref, xs_ref, loc, zbuf, sems,
                     zsem):
    runs = (dst_ref, off_ref, len_ref)
    i = pl.program_id(0)
    n = pl.num_programs(0)

    @pl.when(i == 0)
    def _():
        zbuf[...] = jnp.zeros(zbuf.shape, U32)
        for e in range(N_EXP):
            pl.when(zlen_ref[e] > 0)(lambda e=e: _rows_copy(zbuf, 0, xs_ref, zdst_ref[e], zlen_ref[e], zsem).start())
        for e in range(N_EXP):
            pl.when(zlen_ref[e] > 0)(lambda e=e: _rows_copy(zbuf, 0, xs_ref, zdst_ref[e], zlen_ref[e], zsem).wait())

    def run(slot):
        buf = loc.at[slot]
        sem = sems.at[slot]

        @pl.when(i >= 2)
        def _():
            _wait_rows(ng_ref[i - 2], buf, xs_ref, sem)

        rt = r_ref[...].T
        rows = lax.broadcasted_iota(jnp.int32, (LR, TM), 0).astype(F32)
        perm = jnp.where((rows == rt[4:5, :]) | (rows == rt[5:6, :]), 1.0, 0.0).astype(BF16)
        buf[...] = _pack_halves(_dot(perm, h_ref[...]), exact=True)

        _start_runs(runs, i, lambda dst, off, rows: _rows_copy(buf, off, xs_ref, dst, rows, sem))

        @pl.when(i == n - 1)
        def _():
            _wait_rows(ng_ref[i], buf, xs_ref, sem)

            @pl.when(i >= 1)
            def _():
                _wait_rows(ng_ref[i - 1], loc.at[1 - slot], xs_ref, sems.at[1 - slot])

    pl.when(i % 2 == 0)(lambda: run(0))
    pl.when(i % 2 == 1)(lambda: run(1))


def _dispatch(runs, ng, zdst, zlen, h2, rinfo, cap):
    t = h2.shape[0]
    tok = lambda i, *_: (i, 0)
    return pl.pallas_call(
        _dispatch_kernel,
        grid_spec=pltpu.PrefetchScalarGridSpec(
            num_scalar_prefetch=6,
            grid=(t // TM,),
            in_specs=[
                pl.BlockSpec((TM, D), tok),
                pl.BlockSpec((TM, 128), tok),
            ],
            out_specs=pl.BlockSpec(memory_space=pl.ANY),
            scratch_shapes=[
                pltpu.VMEM((2, LR, D // 2), U32),
                pltpu.VMEM((BLK, D // 2), U32),
                pltpu.SemaphoreType.DMA((2,)),
                pltpu.SemaphoreType.DMA,
            ],
        ),
        out_shape=jax.ShapeDtypeStruct((cap, D // 2), U32),
        compiler_params=_params(("arbitrary",)),
        name="moe_dispatch",
    )(*runs, ng, zdst, zlen, h2, rinfo)


def _expert_kernel(be_ref, nu_ref, nxt_ref, par_ref, x_ref, wg_ref, wu_ref, wd_ref, y_ref, wg_sc, wu_sc, wd_sc,
                   wg_in, wu_in, wd_in, sems, *, layer):
    i = pl.program_id(0)
    used = i < nu_ref[0]
    fresh = (i == 0) | (be_ref[i] != be_ref[jnp.maximum(i - 1, 0)])

    def fetch(e, slot):
        return [pltpu.make_async_copy(src.at[layer, e], dst.at[slot], sems.at[slot, k])
                for k, (src, dst) in enumerate(((wg_ref, wg_in), (wu_ref, wu_in), (wd_ref, wd_in)))]

    @pl.when(used & (i == 0))
    def _():
        for c in fetch(be_ref[0], 0):
            c.start()

    def load(slot):
        for c in fetch(be_ref[i], slot):
            c.wait()
        wg_sc[...] = wg_in[slot].astype(BF16)
        wu_sc[...] = wu_in[slot].astype(BF16)
        wd_sc[...] = wd_in[slot].astype(BF16)

        @pl.when(nxt_ref[i] >= 0)
        def _():
            for c in fetch(nxt_ref[i], 1 - slot):
                c.start()

    pl.when(used & fresh & (par_ref[i] == 0))(lambda: load(0))
    pl.when(used & fresh & (par_ref[i] == 1))(lambda: load(1))

    @pl.when(used)
    def _():
        xb = _unpack_halves(x_ref[...])
        g = _dot(xb, wg_sc[...])
        u = _dot(xb, wu_sc[...])
        hid = (g * _sigmoid(g)) * u
        y_ref[...] = _pack_halves(_dot(hid.astype(BF16), wd_sc[...]))

    @pl.when(jnp.logical_not(used))
    def _():
        y_ref[...] = jnp.zeros(y_ref.shape, U32)


def _experts(blk_e, n_used, xs, wg, wu, wd, layer):
    cap = xs.shape[0]
    n_blk = cap // BLK
    i32 = jnp.int32
    idx = jnp.arange(n_blk, dtype=i32)
    in_use = idx < n_used[0]
    later = (blk_e[None, :] > blk_e[:, None]) & in_use[None, :]
    nxt = jnp.min(jnp.where(later, blk_e[None, :], N_EXP), axis=1)
    nxt = jnp.where(nxt < N_EXP, nxt, -1).astype(i32)
    changes = jnp.concatenate([jnp.zeros((1,), i32), (blk_e[1:] != blk_e[:-1]).astype(i32)])
    parity = (jnp.cumsum(changes) % 2).astype(i32)
    row = lambda i, be, nu, nx, pa: (jnp.minimum(i, nu[0] - 1), 0)
    return pl.pallas_call(
        functools.partial(_expert_kernel, layer=layer),
        grid_spec=pltpu.PrefetchScalarGridSpec(
            num_scalar_prefetch=4,
            grid=(n_blk,),
            in_specs=[
                pl.BlockSpec((BLK, D // 2), row),
                pl.BlockSpec(memory_space=pl.ANY),
                pl.BlockSpec(memory_space=pl.ANY),
                pl.BlockSpec(memory_space=pl.ANY),
            ],
            out_specs=pl.BlockSpec((BLK, D // 2), lambda i, be, nu, nx, pa: (i, 0)),
            scratch_shapes=[
                pltpu.VMEM((D, D_EXP), BF16),
                pltpu.VMEM((D, D_EXP), BF16),
                pltpu.VMEM((D_EXP, D), BF16),
                pltpu.VMEM((2, D, D_EXP), F32),
                pltpu.VMEM((2, D, D_EXP), F32),
                pltpu.VMEM((2, D_EXP, D), F32),
                pltpu.SemaphoreType.DMA((2, 3)),
            ],
        ),
        out_shape=jax.ShapeDtypeStruct((cap, D // 2), U32),
        compiler_params=_params(("arbitrary",)),
        name="moe_experts",
    )(blk_e, n_used, nxt, parity, xs, wg, wu, wd)


def _combine_kernel(dst_ref, off_ref, len_ref, ng_ref, y_ref, x1_ref, r_ref, mod_ref, o_ref, loc, sems):
    runs = (dst_ref, off_ref, len_ref)
    i = pl.program_id(0)
    n = pl.num_programs(0)

    def fetch(tile, slot):
        _start_runs(runs, tile, lambda dst, off, rows: _rows_copy(y_ref, dst, loc.at[slot], off, rows, sems.at[slot]))

    @pl.when(i == 0)
    def _():
        loc[...] = jnp.zeros(loc.shape, U32)
        fetch(0, 0)

    def run(slot):
        @pl.when(i + 1 < n)
        def _():
            fetch(i + 1, 1 - slot)

        _wait_rows(ng_ref[i], y_ref, loc.at[slot], sems.at[slot])
        o_ref[...] = _unsort_add(loc[slot], r_ref[...], x1_ref[...], mod_ref[0, 0][:, 5 * D:6 * D])

    pl.when(i % 2 == 0)(lambda: run(0))
    pl.when(i % 2 == 1)(lambda: run(1))


def _combine(runs, ng, ys, x1, rinfo, modsel, *, tiles_per_batch, with_ctx):
    t = x1.shape[0]
    if with_ctx:
        msel = lambda i, *_: (i // tiles_per_batch, jnp.minimum(i % tiles_per_batch, 1), 0, 0)
    else:
        msel = lambda i, *_: (i // tiles_per_batch, 1, 0, 0)
    tok = lambda i, *_: (i, 0)
    return pl.pallas_call(
        _combine_kernel,
        grid_spec=pltpu.PrefetchScalarGridSpec(
            num_scalar_prefetch=4,
            grid=(t // TM,),
            in_specs=[
                pl.BlockSpec(memory_space=pl.ANY),
                pl.BlockSpec((TM, D), tok),
                pl.BlockSpec((TM, 128), tok),
                pl.BlockSpec((1, 1, 1, 6 * D), msel),
            ],
            out_specs=pl.BlockSpec((TM, D), tok),
            scratch_shapes=[pltpu.VMEM((2, LR, D // 2), U32), pltpu.SemaphoreType.DMA((2,))],
        ),
        out_shape=jax.ShapeDtypeStruct((t, D), F32),
        compiler_params=_params(("arbitrary",)),
        name="moe_combine",
    )(*runs, ng, ys, x1, rinfo, modsel)


def _unsort_add(ysw, info, x1, g2):
    col = lax.broadcasted_iota(jnp.int32, (TM, LR), 1).astype(F32)
    pick = jnp.concatenate([jnp.where(col == info[:, 4:5], 1.0, 0.0).astype(BF16),
                            jnp.where(col == info[:, 5:6], 1.0, 0.0).astype(BF16)], axis=0)
    y12 = _dot(pick, _unpack_halves(ysw))
    return x1 + g2 * (info[:, 2:3] * y12[0:TM] + info[:, 3:4] * y12[TM:2 * TM])


def _combine_proj_kernel(dst_ref, off_ref, len_ref, ng_ref, y_ref, x1_ref, r_ref, mod0_ref, mod_ref, lng_ref, w_ref,
                         cos_ref, sin_ref, qg_ref, kg_ref, bdq_ref, bdk_ref,
                         xc_ref, q_ref, k_ref, v_ref, u_ref, loc, sems, *, nt):
    runs = (dst_ref, off_ref, len_ref)
    bp = pl.program_id(0)
    i = pl.program_id(1)
    step = bp * nt + i
    slot = step % 2

    def fetch(bp_, i_, slot_):
        for bb in range(PB):
            tile = (bp_ * PB + bb) * nt + i_
            _start_runs(runs, tile, lambda dst, off, rows, bb=bb: _rows_copy(
                y_ref, dst, loc.at[slot_, bb], off, rows, sems.at[slot_, bb]))

    @pl.when(step == 0)
    def _():
        loc[...] = jnp.zeros(loc.shape, U32)
        fetch(0, 0, 0)

    @pl.when(step + 1 < pl.num_programs(0) * nt)
    def _():
        wrap = i + 1 == nt
        fetch(jnp.where(wrap, bp + 1, bp), jnp.where(wrap, 0, i + 1), 1 - slot)

    @pl.when(step >= 0)
    def _():
        for bb in range(PB):
            tile = (bp * PB + bb) * nt + i
            _wait_rows(ng_ref[tile], y_ref, loc.at[slot, bb], sems.at[slot, bb])
            xc_ref[bb] = _unsort_add(loc[slot, bb], r_ref[bb], x1_ref[bb], mod0_ref[bb, 0][:, 5 * D:6 * D])

    @pl.when(step >= -1)
    def _():
        h = jnp.concatenate([_modulated(xc_ref[bb], mod_ref[bb, 0], lng_ref[...], 0, 1) for bb in range(PB)], axis=0)
        px = _dot(h.astype(BF16), w_ref[...])
        _proj_epilogue(px, cos_ref, sin_ref, qg_ref, kg_ref, bdq_ref, bdk_ref, q_ref, k_ref, v_ref, u_ref, False)


def _combine_project(runs, ng, ys, x1, rinfo, modsel0, modsel, ln_g, w_in, cos, sin, q_g, k_g, bdq, bdk):
    b, s_tot, _ = x1.shape
    nt = s_tot // TM
    n_in = w_in.shape[1]
    const = lambda bi, i, *_: (0, 0)
    tok = lambda bi, i, *_: (bi, i, 0)
    msel = lambda bi, i, *_: (bi, jnp.minimum(i, 1), 0, 0)
    head = lambda bi, i, *_: (bi, 0, i, 0)
    return pl.pallas_call(
        functools.partial(_combine_proj_kernel, nt=nt),
        grid_spec=pltpu.PrefetchScalarGridSpec(
            num_scalar_prefetch=4,
            grid=(b // PB, nt),
            in_specs=[
                pl.BlockSpec(memory_space=pl.ANY),
                pl.BlockSpec((PB, TM, D), tok),
                pl.BlockSpec((PB, TM, 128), tok),
                pl.BlockSpec((PB, 1, 1, 6 * D), msel),
                pl.BlockSpec((PB, 1, 1, 6 * D), msel),
                pl.BlockSpec((1, D), const),
                pl.BlockSpec((D, n_in), const),
                pl.BlockSpec((TM, 128), lambda bi, i, *_: (i, 0)),
                pl.BlockSpec((TM, 128), lambda bi, i, *_: (i, 0)),
                pl.BlockSpec((1, QW), const),
                pl.BlockSpec((1, KW), const),
                pl.BlockSpec((QW, QW), const),
                pl.BlockSpec((KW, KW), const),
            ],
            out_specs=[
                pl.BlockSpec((PB, TM, D), tok),
                pl.BlockSpec((PB, NH, TM, HD), head),
                pl.BlockSpec((PB, NKV, TM, HD), head),
                pl.BlockSpec((PB, NKV, TM, 2 * HD), head),
                pl.BlockSpec((PB, TM, UW), tok),
            ],
            scratch_shapes=[pltpu.VMEM((2, PB, LR, D // 2), U32), pltpu.SemaphoreType.DMA((2, PB))],
        ),
        out_shape=[
            jax.ShapeDtypeStruct((b, s_tot, D), F32),
            jax.ShapeDtypeStruct((b, NH, s_tot, HD), BF16),
            jax.ShapeDtypeStruct((b, NKV, s_tot, HD), BF16),
            jax.ShapeDtypeStruct((b, NKV, s_tot, 2 * HD), BF16),
            jax.ShapeDtypeStruct((b, s_tot, UW), F32),
        ],
        compiler_params=_params(("arbitrary", "arbitrary")),
        name="moe_combine_in_projection",
    )(*runs, ng, ys, x1, rinfo, modsel0, modsel, ln_g, w_in, cos, sin, q_g, k_g, bdq, bdk)


def _moe(x1, h2, rinfo, g8, modsel, wg, wu, wd, layer, *, with_ctx, combine=True):
    b, s, _ = x1.shape
    t = b * s
    n_tiles = t // TM
    i32 = jnp.int32
    grp = g8.reshape(n_tiles, 128)[:, 0:N_EXP].astype(i32)
    n_blk = -(-(2 * t + n_tiles * N_EXP * (GRP - 1) + N_EXP * (BLK - 1)) // BLK)
    cap = n_blk * BLK
    goff_end = jnp.cumsum(grp, axis=1)
    ng = goff_end[:, N_EXP - 1].astype(i32)
    rows_e = GRP * jnp.sum(grp, axis=0)
    padded = (rows_e + BLK - 1) // BLK * BLK
    pend = jnp.cumsum(padded)
    pstart = pend - padded
    run_dst = (pstart[None, :] + GRP * (jnp.cumsum(grp, axis=0) - grp)).reshape(-1).astype(i32)
    run_off = (GRP * (goff_end - grp)).reshape(-1).astype(i32)
    run_len = (GRP * grp).reshape(-1).astype(i32)
    runs = (run_dst, run_off, run_len)
    zdst = (pstart + rows_e).astype(i32)
    zlen = (padded - rows_e).astype(i32)
    blk_e = jnp.minimum(jnp.sum((pend[None, :] <= (jnp.arange(n_blk, dtype=i32) * BLK)[:, None]).astype(i32), axis=1),
                        N_EXP - 1).astype(i32)
    n_used = (pend[N_EXP - 1:N_EXP] // BLK).astype(i32)
    info = rinfo.reshape(t, 128)
    xs = _dispatch(runs, ng, zdst, zlen, h2.reshape(t, D), info, cap)
    ys = _experts(blk_e, n_used, xs, wg, wu, wd, layer)
    if not combine:
        return runs, ng, ys
    out = _combine(runs, ng, ys, x1.reshape(t, D), info, modsel, tiles_per_batch=s // TM, with_ctx=with_ctx)
    return out.reshape(b, s, D)


def _rope_tables(n_ctx, s_lat):
    rows = s_lat // GRID_W
    row = jnp.repeat(jnp.arange(rows, dtype=F32), GRID_W)
    col = jnp.tile(jnp.arange(GRID_W, dtype=F32), rows)
    n_freq = HD // 4
    inv = ROPE_THETA ** (-jnp.arange(n_freq, dtype=F32) / n_freq)
    ang = jnp.concatenate([row[:, None] * inv, col[:, None] * inv], axis=-1)
    cos = jnp.concatenate([jnp.ones((n_ctx, HD // 2), F32), jnp.cos(ang)], axis=0)
    sin = jnp.concatenate([jnp.zeros((n_ctx, HD // 2), F32), jnp.sin(ang)], axis=0)
    cos = jnp.concatenate([cos, cos, cos, cos], axis=1)
    sin = jnp.concatenate([-sin, sin, -sin, sin], axis=1)
    return cos, sin


def _block_diag_ones(n):
    i = jnp.arange(n) // HD
    return (i[:, None] == i[None, :]).astype(BF16)


def kernel(x, c, ctx, c_ctx, mod_w, mod_b, ln1_g, ln2_g, w_in_ab, w_out_ab, q_norm_a, k_norm_a, conv_w, conv_b,
           conv_ln_g, conv_ln_b, w_in_cd, w_out_cd, q_norm_c, k_norm_c, sink_c, pool_w, pool_scale,
           rt_grp_w, rt_grp_b, rt_exp_w, rt_exp_b, ex_gate, ex_up, ex_down):
    b, s_lat, _ = x.shape
    n_ctx = ctx.shape[1]
    assert n_ctx == TM and s_lat % TM == 0 and b <= 8 and b % PB == 0

    cc = jnp.zeros((16, D), F32).at[0:b].set(c).at[8].set(c_ctx)
    mod = _modulation(cc, mod_w, mod_b)

    def modsel(l):
        return jnp.stack([jnp.broadcast_to(mod[l, 8], (b, 6 * D)), mod[l, 0:b]], axis=1).reshape(b, 2, 1, 6 * D)

    cos, sin = _rope_tables(n_ctx, s_lat)
    bdq = _block_diag_ones(QW)
    bdk = _block_diag_ones(KW)
    tri = (jnp.arange(TM)[:, None] > jnp.arange(TM)[None, :]).astype(BF16)
    upper = (jnp.arange(128)[:, None] < jnp.arange(128)[None, :]).astype(BF16)

    def router(l):
        rw = jnp.zeros((D, 128), F32).at[:, 0:N_EXP].set(rt_exp_w[l]).at[:, N_EXP:N_EXP + N_GROUPS].set(rt_grp_w[l])
        rb = jnp.zeros((1, 128), F32).at[0, 0:N_EXP].set(rt_exp_b[l]).at[0, N_EXP:N_EXP + N_GROUPS].set(rt_grp_b[l])
        return rw, rb

    def tile_gain(g, n):
        return jnp.tile(g, n).reshape(1, n * HD)

    s_tot = n_ctx + s_lat

    ms = modsel(0)
    q, k, v, glu = _project(ctx, x, 0, s_tot, ms, ln1_g[0:1], w_in_ab[0].astype(BF16), cos, sin,
                            tile_gain(q_norm_a[0], NH), tile_gain(k_norm_a[0], NKV), bdq, bdk, glu=True)
    att = _dense_attention(q, k, v, n_ctx)
    cv = _conv_module(glu, conv_w[0], conv_b[0:1], conv_ln_g[0:1], conv_ln_b[0:1], n_ctx)
    rw, rb = router(0)
    x1, h2, rinfo, g8 = _out_project(att, cv, (ctx, x), ms, w_out_ab[0].astype(BF16), ln2_g[0:1], rw, rb, tri, upper)
    runs, ng, ys = _moe(x1, h2, rinfo, g8, ms, ex_gate, ex_up, ex_down, 0, with_ctx=True, combine=False)

    ms0, ms = ms, modsel(1)
    xc, q, k, v, u = _combine_project(runs, ng, ys, x1, rinfo, ms0, ms, ln1_g[1:2], w_in_cd[0].astype(BF16), cos, sin,
                                      tile_gain(q_norm_c[0], NH), tile_gain(k_norm_c[0], NKV), bdq, bdk)
    att = _window_attention(sink_c[0], q, k, v, n_ctx)
    pm = _pool_mixer(u, pool_w[0].astype(BF16), pool_scale[0:1], n_ctx)
    rw, rb = router(1)
    x1, h2, rinfo, g8 = _out_project(att, pm, (xc,), ms, w_out_cd[0].astype(BF16), ln2_g[1:2], rw, rb, tri, upper)
    return _moe(x1, h2, rinfo, g8, ms, ex_gate, ex_up, ex_down, 1, with_ctx=False)
```

```python
import functools

import jax
import jax.numpy as jnp
from jax import lax
from jax.experimental import pallas as pl
from jax.experimental.pallas import tpu as pltpu

F32 = jnp.float32
BF16 = jnp.bfloat16
U32 = jnp.uint32

D = 1024
HD = 64
NH = 8
NKV = 2
GQ = NH // NKV
QW = NH * HD
KW = NKV * HD
UW = 512
EPS = 1e-6
ROPE_THETA = 10000.0
GRID_W = 64
CONV_K = 31
WINDOW = 128
POOL_SIZES = (2, 4, 8, 16)
N_GROUPS = 4
PER_GROUP = 8
N_EXP = N_GROUPS * PER_GROUP
D_EXP = D // 2

TM = 256
PB = 4
TQ = 256
TQD = 256
TKL = 512
SUB = 64
HALO = 16
BLK = 1024
GRP = 8
NG = 96
LR = NG * GRP
VMEM_LIMIT = 56 * 1024 * 1024


def _sigmoid(x):
    return 1.0 / (1.0 + jnp.exp(-x))


def _dot(a, b):
    return jnp.dot(a, b, preferred_element_type=F32)


def _dot_nt(a, b):
    return lax.dot_general(a, b, (((1,), (1,)), ((), ())), preferred_element_type=F32)


def _split(a):
    hi = a.astype(BF16)
    lo = (a - hi.astype(F32)).astype(BF16)
    return hi, lo


def _dot3(a, w):
    a_hi, a_lo = _split(a)
    w_hi, w_lo = _split(w)
    return _dot(a_hi, w_hi) + _dot(a_lo, w_hi) + _dot(a_hi, w_lo)


def _rmsnorm(x, g):
    return x * lax.rsqrt(jnp.mean(x * x, axis=-1, keepdims=True) + EPS) * g


def _params(sem, vmem=VMEM_LIMIT):
    return pltpu.CompilerParams(dimension_semantics=sem, vmem_limit_bytes=vmem)


def _mod_kernel(c_ref, w_ref, b_ref, o_ref):
    a = c_ref[...]
    a = a * _sigmoid(a)
    o_ref[0] = _dot3(a, w_ref[0]) + b_ref[0]


def _modulation(cc, mod_w, mod_b):
    depth = mod_w.shape[0]
    tn = 1024
    return pl.pallas_call(
        _mod_kernel,
        grid=(depth, 6 * D // tn),
        in_specs=[
            pl.BlockSpec((16, D), lambda l, j: (0, 0)),
            pl.BlockSpec((1, D, tn), lambda l, j: (l, 0, j)),
            pl.BlockSpec((1, 1, tn), lambda l, j: (l, 0, j)),
        ],
        out_specs=pl.BlockSpec((1, 16, tn), lambda l, j: (l, 0, j)),
        out_shape=jax.ShapeDtypeStruct((depth, 16, 6 * D), F32),
        compiler_params=_params(("arbitrary", "arbitrary")),
        name="modulation",
    )(cc, mod_w, mod_b.reshape(depth, 1, 6 * D))


def _swap_halves(t):
    w = t.shape[1]
    lane = lax.broadcasted_iota(jnp.int32, t.shape, 1)
    first = (lane & (HD - 1)) < (HD // 2)
    return jnp.where(first, pltpu.roll(t, w - HD // 2, 1), pltpu.roll(t, HD // 2, 1))


def _head_norm_rope(t, bd, g, cos, sin, scale):
    ssq = _dot((t * t).astype(BF16), bd)
    tn = t * lax.rsqrt(ssq * (1.0 / HD) + EPS) * g
    n = t.shape[1] // 128
    if n > 1:
        cos = jnp.concatenate([cos] * n, axis=1)
        sin = jnp.concatenate([sin] * n, axis=1)
    out = tn * cos + _swap_halves(tn) * sin
    return out * scale if scale != 1.0 else out


def _tile_rows(c_ref, x_ref, bb):
    return jnp.where(pl.program_id(1) == 0, c_ref[bb], x_ref[bb])


def _modulated(x, mod, lng, shift_at, scale_at):
    return _rmsnorm(x, lng) * (1.0 + mod[:, scale_at * D:(scale_at + 1) * D]) + mod[:, shift_at * D:(shift_at + 1) * D]


def _proj_kernel(c_ref, x_ref, mod_ref, lng_ref, w_ref, cos_ref, sin_ref, qg_ref, kg_ref, bdq_ref, bdk_ref,
                 q_ref, k_ref, v_ref, u_ref, *, glu):
    h = jnp.concatenate([_modulated(_tile_rows(c_ref, x_ref, bb), mod_ref[bb, 0], lng_ref[...], 0, 1)
                         for bb in range(PB)], axis=0)
    px = _dot(h.astype(BF16), w_ref[...])
    _proj_epilogue(px, cos_ref, sin_ref, qg_ref, kg_ref, bdq_ref, bdk_ref, q_ref, k_ref, v_ref, u_ref, glu)


def _proj_epilogue(px, cos_ref, sin_ref, qg_ref, kg_ref, bdq_ref, bdk_ref, q_ref, k_ref, v_ref, u_ref, glu):
    cos = jnp.concatenate([cos_ref[...]] * PB, axis=0)
    sin = jnp.concatenate([sin_ref[...]] * PB, axis=0)
    q = _head_norm_rope(px[:, 0:QW], bdq_ref[...], qg_ref[...], cos, sin, HD ** -0.5)
    k = _head_norm_rope(px[:, QW:QW + KW], bdk_ref[...], kg_ref[...], cos, sin, 1.0)
    v = px[:, QW + KW:QW + 2 * KW]
    lane = lax.broadcasted_iota(jnp.int32, (TM, KW), 1)
    ones_col = jnp.where(lane == HD, 1.0, 0.0)
    o_u = QW + 2 * KW
    if glu:
        u = px[:, o_u:o_u + UW] * _sigmoid(px[:, o_u + UW:o_u + 2 * UW])
    else:
        u = px[:, o_u:o_u + UW]
    for bb in range(PB):
        rows = slice(bb * TM, (bb + 1) * TM)
        for hh in range(NH):
            q_ref[bb, hh] = q[rows, hh * HD:(hh + 1) * HD].astype(BF16)
        for j in range(NKV):
            k_ref[bb, j] = k[rows, j * HD:(j + 1) * HD].astype(BF16)
            vj = v[rows] if j == 0 else pltpu.roll(v[rows], KW - j * HD, 1)
            v_ref[bb, j] = jnp.where(lane < HD, vj, ones_col).astype(BF16)
        u_ref[bb] = u[rows]


def _token_specs(lat_first):
    return [pl.BlockSpec((PB, TM, D), lambda bi, i: (bi, 0, 0)),
            pl.BlockSpec((PB, TM, D), lambda bi, i: (bi, jnp.maximum(i - 1, 0) + lat_first, 0))]


def _project(ctx_rows, lat_rows, lat_first, s_tot, modsel, ln_g, w_in, cos, sin, q_g, k_g, bdq, bdk, *, glu):
    b = ctx_rows.shape[0]
    n_in = w_in.shape[1]
    nt = s_tot // TM
    const = lambda bi, i: (0, 0)
    return pl.pallas_call(
        functools.partial(_proj_kernel, glu=glu),
        grid=(b // PB, nt),
        in_specs=_token_specs(lat_first) + [
            pl.BlockSpec((PB, 1, 1, 6 * D), lambda bi, i: (bi, jnp.minimum(i, 1), 0, 0)),
            pl.BlockSpec((1, D), const),
            pl.BlockSpec((D, n_in), const),
            pl.BlockSpec((TM, 128), lambda bi, i: (i, 0)),
            pl.BlockSpec((TM, 128), lambda bi, i: (i, 0)),
            pl.BlockSpec((1, QW), const),
            pl.BlockSpec((1, KW), const),
            pl.BlockSpec((QW, QW), const),
            pl.BlockSpec((KW, KW), const),
        ],
        out_specs=[
            pl.BlockSpec((PB, NH, TM, HD), lambda bi, i: (bi, 0, i, 0)),
            pl.BlockSpec((PB, NKV, TM, HD), lambda bi, i: (bi, 0, i, 0)),
            pl.BlockSpec((PB, NKV, TM, 2 * HD), lambda bi, i: (bi, 0, i, 0)),
            pl.BlockSpec((PB, TM, UW), lambda bi, i: (bi, i, 0)),
        ],
        out_shape=[
            jax.ShapeDtypeStruct((b, NH, s_tot, HD), BF16),
            jax.ShapeDtypeStruct((b, NKV, s_tot, HD), BF16),
            jax.ShapeDtypeStruct((b, NKV, s_tot, 2 * HD), BF16),
            jax.ShapeDtypeStruct((b, s_tot, UW), F32),
        ],
        compiler_params=_params(("arbitrary", "arbitrary")),
        name="in_projection",
    )(ctx_rows, lat_rows, modsel, ln_g, w_in, cos, sin, q_g, k_g, bdq, bdk)


def _merge_heads(o):
    tq = o.shape[0] // GQ
    return jnp.concatenate([o[g * tq:(g + 1) * tq] for g in range(GQ)], axis=1)


def _lane_max(s):
    return functools.reduce(jnp.maximum, [s[:, j * 128:(j + 1) * 128] for j in range(s.shape[1] // 128)])


def _dense_attn_kernel(q_ref, k_ref, v_ref, o_ref, sc_sc, sl_sc, m_sc, *, n_ctx, s_tot):
    qi = pl.program_id(2)
    q = q_ref[0].reshape(GQ * TQD, HD)
    n_lat = (s_tot - n_ctx) // TKL

    def chunks(n):
        spans = [(0, n_ctx)] + [(n_ctx + c * TKL, n_ctx + (c + 1) * TKL) for c in range(n)]
        return list(zip(spans, [sc_sc] + [sl_sc.at[c] for c in range(n)]))

    def scores(n):
        for i, ((lo, hi), slot) in enumerate(chunks(n)):
            s = _dot_nt(q, k_ref[0, 0, lo:hi, :])
            slot[...] = s
            m_sc[i] = _lane_max(s)
        m_lane = functools.reduce(jnp.maximum, [m_sc[i] for i in range(n + 1)])
        m_sc[0] = jnp.broadcast_to(jnp.max(m_lane, axis=1, keepdims=True), m_lane.shape)

    def weighted(n):
        acc = None
        for (lo, hi), slot in chunks(n):
            m = jnp.concatenate([m_sc[0]] * ((hi - lo) // 128), axis=1)
            part = _dot(jnp.exp(slot[...] - m).astype(BF16), v_ref[0, 0, lo:hi, :])
            acc = part if acc is None else acc + part
        o_ref[0] = _merge_heads(acc[:, 0:HD] / acc[:, HD:HD + 1]).astype(BF16)

    is_lat = qi * TQD >= n_ctx
    pl.when(is_lat)(lambda: scores(n_lat))
    pl.when(jnp.logical_not(is_lat))(lambda: scores(0))
    pl.when(is_lat)(lambda: weighted(n_lat))
    pl.when(jnp.logical_not(is_lat))(lambda: weighted(0))


def _dense_attention(q, k, v, n_ctx):
    b, _, s_tot, _ = q.shape
    rows = GQ * TQD
    return pl.pallas_call(
        functools.partial(_dense_attn_kernel, n_ctx=n_ctx, s_tot=s_tot),
        grid=(b, NKV, s_tot // TQD),
        in_specs=[
            pl.BlockSpec((1, GQ, TQD, HD), lambda bi, j, i: (bi, j, i, 0)),
            pl.BlockSpec((1, 1, s_tot, HD), lambda bi, j, i: (bi, j, 0, 0)),
            pl.BlockSpec((1, 1, s_tot, 2 * HD), lambda bi, j, i: (bi, j, 0, 0)),
        ],
        out_specs=pl.BlockSpec((1, TQD, GQ * HD), lambda bi, j, i: (bi, i, j)),
        out_shape=jax.ShapeDtypeStruct((b, s_tot, QW), BF16),
        scratch_shapes=[
            pltpu.VMEM((rows, n_ctx), F32),
            pltpu.VMEM(((s_tot - n_ctx) // TKL, rows, TKL), F32),
            pltpu.VMEM(((s_tot - n_ctx) // TKL + 1, rows, 128), F32),
        ],
        compiler_params=_params(("arbitrary", "arbitrary", "arbitrary")),
        name="dense_attention",
    )(q, k, v)


def _window_attn_kernel(sink_ref, q_ref, k_ref, v_ref, o_ref, sc_sc, sw_sc, m_sc, e_sc, *, n_ctx, s_tot):
    qi = pl.program_id(1)
    span = TQ + 2 * WINDOW
    q0 = n_ctx + qi * TQ
    start = pl.multiple_of(jnp.clip(q0 - WINDOW, n_ctx, s_tot - span), 128)
    row = lax.broadcasted_iota(jnp.int32, (TQ, span), 0)
    col = lax.broadcasted_iota(jnp.int32, (TQ, span), 1)
    band = jnp.where(jnp.abs((q0 - start) + row - col) <= WINDOW, 0.0, -jnp.inf)
    band = jnp.concatenate([band] * GQ, axis=0)
    for j in range(NKV):
        q = q_ref[0, j * GQ:(j + 1) * GQ].reshape(GQ * TQ, HD)
        s_c = _dot_nt(q, k_ref[0, j, 0:n_ctx, :])
        s_w = _dot_nt(q, k_ref[0, j, pl.ds(start, span), :]) + band
        sc_sc[j] = s_c
        sw_sc[j] = s_w
        sink = jnp.concatenate([jnp.full((TQ, 128), sink_ref[j * GQ + g], F32) for g in range(GQ)], axis=0)
        m_lane = jnp.maximum(_lane_max(s_c), _lane_max(s_w))
        m_sc[j] = jnp.maximum(jnp.broadcast_to(jnp.max(m_lane, axis=1, keepdims=True), m_lane.shape), sink)
        e_sc[j] = jnp.exp(sink - m_sc[j])
    for j in range(NKV):
        m = m_sc[j]
        p_c = jnp.exp(sc_sc[j] - jnp.concatenate([m] * (n_ctx // 128), axis=1)).astype(BF16)
        p_w = jnp.exp(sw_sc[j] - jnp.concatenate([m] * (span // 128), axis=1)).astype(BF16)
        acc = _dot(p_c, v_ref[0, j, 0:n_ctx, :]) + _dot(p_w, v_ref[0, j, pl.ds(start, span), :])
        l = acc[:, HD:HD + 1] + e_sc[j][:, 0:1]
        o_ref[0, :, j * GQ * HD:(j + 1) * GQ * HD] = _merge_heads(acc[:, 0:HD] / l).astype(BF16)


def _window_attention(sink, q, k, v, n_ctx):
    b, _, s_tot, _ = q.shape
    s_lat = s_tot - n_ctx
    off = n_ctx // TQ
    return pl.pallas_call(
        functools.partial(_window_attn_kernel, n_ctx=n_ctx, s_tot=s_tot),
        grid_spec=pltpu.PrefetchScalarGridSpec(
            num_scalar_prefetch=1,
            grid=(b, s_lat // TQ),
            in_specs=[
                pl.BlockSpec((1, NH, TQ, HD), lambda bi, i, sk: (bi, 0, i + off, 0)),
                pl.BlockSpec((1, NKV, s_tot, HD), lambda bi, i, sk: (bi, 0, 0, 0)),
                pl.BlockSpec((1, NKV, s_tot, 2 * HD), lambda bi, i, sk: (bi, 0, 0, 0)),
            ],
            out_specs=pl.BlockSpec((1, TQ, QW), lambda bi, i, sk: (bi, i, 0)),
            scratch_shapes=[
                pltpu.VMEM((NKV, GQ * TQ, n_ctx), F32),
                pltpu.VMEM((NKV, GQ * TQ, TQ + 2 * WINDOW), F32),
                pltpu.VMEM((NKV, GQ * TQ, 128), F32),
                pltpu.VMEM((NKV, GQ * TQ, 128), F32),
            ],
        ),
        out_shape=jax.ShapeDtypeStruct((b, s_lat, QW), BF16),
        compiler_params=_params(("arbitrary", "arbitrary")),
        name="window_attention",
    )(sink, q, k, v)


def _conv_kernel(g_ref, cw_ref, cb_ref, lg_ref, lb_ref, o_ref, pad_sc, win_sc, *, n_ctx, s_tot):
    zeros = jnp.zeros((HALO, UW), F32)
    pad_sc[0:HALO, :] = zeros
    pad_sc[HALO + n_ctx:2 * HALO + n_ctx, :] = zeros
    pad_sc[2 * HALO + s_tot:3 * HALO + s_tot, :] = zeros

    def fill(i, carry):
        src = pl.multiple_of(i * TM, TM)
        dst = pl.multiple_of(src + HALO + jnp.where(src >= n_ctx, HALO, 0), 8)
        pad_sc[pl.ds(dst, TM), :] = g_ref[0, pl.ds(src, TM), :]
        return carry

    lax.fori_loop(0, s_tot // TM, fill, 0)
    half = CONV_K // 2

    def tile(i, carry):
        src = pl.multiple_of(i * SUB, SUB)
        base = pl.multiple_of(src + jnp.where(src >= n_ctx, HALO, 0), 8)
        win = pad_sc[pl.ds(base, SUB + 2 * HALO), :]
        keep = SUB + 2 * HALO - 8
        for r in range(8):
            win_sc[r, 0:keep, :] = win[r:r + keep, :]
        acc = jnp.zeros((SUB, UW), F32) + cb_ref[...]
        for t in range(CONV_K):
            a, r = divmod(HALO - half + t, 8)
            acc = acc + win_sc[r, 8 * a:8 * a + SUB, :] * cw_ref[t:t + 1, :]
        mu = jnp.mean(acc, axis=-1, keepdims=True)
        xc = acc - mu
        var = jnp.mean(xc * xc, axis=-1, keepdims=True)
        yn = xc * lax.rsqrt(var + EPS) * lg_ref[...] + lb_ref[...]
        o_ref[0, pl.ds(src, SUB), :] = (yn * _sigmoid(yn)).astype(BF16)
        return carry

    lax.fori_loop(0, s_tot // SUB, tile, 0)


def _conv_module(glu, conv_w, conv_b, ln_g, ln_b, n_ctx):
    b, s_tot, _ = glu.shape
    const = lambda bi: (0, 0)
    return pl.pallas_call(
        functools.partial(_conv_kernel, n_ctx=n_ctx, s_tot=s_tot),
        grid=(b,),
        in_specs=[
            pl.BlockSpec((1, s_tot, UW), lambda bi: (bi, 0, 0)),
            pl.BlockSpec((CONV_K, UW), const),
            pl.BlockSpec((1, UW), const),
            pl.BlockSpec((1, UW), const),
            pl.BlockSpec((1, UW), const),
        ],
        out_specs=pl.BlockSpec((1, s_tot, UW), lambda bi: (bi, 0, 0)),
        out_shape=jax.ShapeDtypeStruct((b, s_tot, UW), BF16),
        scratch_shapes=[pltpu.VMEM((s_tot + 3 * HALO, UW), F32), pltpu.VMEM((8, SUB + 2 * HALO, UW), F32)],
        compiler_params=_params(("arbitrary",)),
        name="conv_module",
    )(glu, conv_w, conv_b, ln_g, ln_b)


def _pool_kernel(u_ref, pw_ref, ps_ref, o_ref, pad_sc, *, n_ctx, s_lat):
    zeros = jnp.zeros((HALO, UW), F32)
    pad_sc[0:HALO, :] = zeros
    pad_sc[HALO + s_lat:2 * HALO + s_lat, :] = zeros

    def fill(i, carry):
        src = pl.multiple_of(i * TM, TM)
        pad_sc[pl.ds(pl.multiple_of(src + HALO, 8), TM), :] = u_ref[0, pl.ds(pl.multiple_of(src + n_ctx, 8), TM), :]
        return carry

    lax.fori_loop(0, s_lat // TM, fill, 0)
    gw = UW // len(POOL_SIZES)

    def tile(i, carry):
        src = pl.multiple_of(i * SUB, SUB)
        win = pad_sc[pl.ds(src, SUB + 2 * HALO), :]
        t = src + lax.broadcasted_iota(jnp.int32, (SUB, 1), 0)
        outs = []
        for gi, w in enumerate(POOL_SIZES):
            lanes = slice(gi * gw, (gi + 1) * gw)
            tot = jnp.zeros((SUB, gw), F32)
            for d in range(-(w // 2), w - w // 2):
                tot = tot + win[HALO + d:HALO + d + SUB, lanes]
            lo = jnp.clip(t - w // 2, 0, s_lat)
            hi = jnp.clip(t - w // 2 + w, 0, s_lat)
            p = tot / (hi - lo).astype(F32) - win[HALO:HALO + SUB, lanes]
            outs.append(_dot(p.astype(BF16), pw_ref[gi]))
        y = jnp.concatenate(outs, axis=1) * ps_ref[...]
        o_ref[0, pl.ds(src, SUB), :] = y.astype(BF16)
        return carry

    lax.fori_loop(0, s_lat // SUB, tile, 0)


def _pool_mixer(u, pool_w, pool_scale, n_ctx):
    b, s_tot, _ = u.shape
    s_lat = s_tot - n_ctx
    gw = UW // len(POOL_SIZES)
    return pl.pallas_call(
        functools.partial(_pool_kernel, n_ctx=n_ctx, s_lat=s_lat),
        grid=(b,),
        in_specs=[
            pl.BlockSpec((1, s_tot, UW), lambda bi: (bi, 0, 0)),
            pl.BlockSpec((len(POOL_SIZES), gw, gw), lambda bi: (0, 0, 0)),
            pl.BlockSpec((1, UW), lambda bi: (0, 0)),
        ],
        out_specs=pl.BlockSpec((1, s_lat, UW), lambda bi: (bi, 0, 0)),
        out_shape=jax.ShapeDtypeStruct((b, s_lat, UW), BF16),
        scratch_shapes=[pltpu.VMEM((s_lat + 2 * HALO, UW), F32)],
        compiler_params=_params(("arbitrary",)),
        name="pool_mixer",
    )(u, pool_w, pool_scale)


def _route(logits):
    shape = logits.shape
    lane = lax.broadcasted_iota(jnp.int32, shape, 1)
    lanef = lane.astype(F32)
    is_g = (lane >= N_EXP) & (lane < N_EXP + N_GROUPS)
    big = 1e9
    lg = jnp.where(is_g, logits, -jnp.inf)
    gmax = jnp.max(lg, axis=1, keepdims=True)
    gsel = jnp.min(jnp.where(is_g & (lg == gmax), lanef, big), axis=1, keepdims=True) - float(N_EXP)
    gsum = jnp.sum(jnp.exp(lg - gmax), axis=1, keepdims=True)
    g_w = 1.0 / gsum
    in_grp = (lane < N_EXP) & ((lane >> (PER_GROUP.bit_length() - 1)).astype(F32) == gsel)
    el = jnp.where(in_grp, logits, -jnp.inf)
    emax = jnp.max(el, axis=1, keepdims=True)
    ee = jnp.exp(el - emax)
    p = ee / jnp.sum(ee, axis=1, keepdims=True)
    p1 = jnp.max(jnp.where(in_grp, p, -1.0), axis=1, keepdims=True)
    i1 = jnp.min(jnp.where(in_grp & (p == p1), lanef, big), axis=1, keepdims=True)
    rest = in_grp & (lanef != i1)
    p2 = jnp.max(jnp.where(rest, p, -1.0), axis=1, keepdims=True)
    i2 = jnp.min(jnp.where(rest & (p == p2), lanef, big), axis=1, keepdims=True)
    w1 = g_w * p1 / (p1 + p2)
    w2 = g_w * p2 / (p1 + p2)
    onehot = jnp.where((lanef == i1) | (lanef == i2), 1.0, 0.0)
    return lanef, i1, i2, w1, w2, onehot


def _outproj_kernel(a_ref, b_ref, *refs, with_ctx):
    mod_ref, w_ref, lng_ref, rw_ref, rb_ref, tri_ref, upper_ref, x1_ref, h2_ref, r_ref, g8_ref = refs[-11:]
    a = jnp.concatenate([a_ref[bb] for bb in range(PB)], axis=0)
    bmix = jnp.concatenate([b_ref[bb] for bb in range(PB)], axis=0)
    y = _dot(a, w_ref[0:QW, :]) + _dot(bmix, w_ref[QW:QW + UW, :])
    h2_parts = []
    for bb in range(PB):
        x_in = _tile_rows(refs[0], refs[1], bb) if with_ctx else refs[0][bb]
        mod = mod_ref[bb, 0]
        x1 = x_in + mod[:, 2 * D:3 * D] * y[bb * TM:(bb + 1) * TM]
        x1_ref[bb] = x1
        h2_parts.append(_modulated(x1, mod, lng_ref[...], 3, 4))
        h2_ref[bb] = h2_parts[-1].astype(BF16)
    logits = _dot3(jnp.concatenate(h2_parts, axis=0), rw_ref[...]) + rb_ref[...]
    lanef, i1, i2, w1, w2, onehot = _route(logits)
    for bb in range(PB):
        rows = slice(bb * TM, (bb + 1) * TM)
        hot = onehot[rows]
        before = _dot(tri_ref[...], hot.astype(BF16))
        cnt = jnp.sum(hot, axis=0, keepdims=True)
        g8 = jnp.floor((cnt + float(GRP - 1)) * (1.0 / GRP))
        goff = _dot(jnp.broadcast_to(g8, (8, 128)).astype(BF16), upper_ref[...])[0:1, :]
        local = before + float(GRP) * goff
        lane = lax.broadcasted_iota(jnp.int32, (TM, 128), 1).astype(F32)
        e1, e2 = i1[rows], i2[rows]
        l1 = jnp.sum(jnp.where(lane == e1, local, 0.0), axis=1, keepdims=True)
        l2 = jnp.sum(jnp.where(lane == e2, local, 0.0), axis=1, keepdims=True)
        r_ref[bb] = jnp.where(lane == 0.0, e1, jnp.where(lane == 1.0, e2, jnp.where(lane == 2.0, w1[rows], jnp.where(
            lane == 3.0, w2[rows], jnp.where(lane == 4.0, l1, jnp.where(lane == 5.0, l2, 0.0))))))
        g8_ref[bb, 0] = g8


def _out_project(a, bmix, residual, modsel, w_out, ln_g, rw, rb, tri, upper):
    b, s_out, _ = a.shape
    nt = s_out // TM
    const = lambda bi, i: (0, 0)
    with_ctx = len(residual) == 2
    sel = (lambda i: jnp.minimum(i, 1)) if with_ctx else (lambda i: 1)
    tok = lambda bi, i: (bi, i, 0)
    first_lat = residual[0].shape[1] // TM - nt
    res_specs = _token_specs(0) if with_ctx else [pl.BlockSpec((PB, TM, D), lambda bi, i: (bi, i + first_lat, 0))]
    return pl.pallas_call(
        functools.partial(_outproj_kernel, with_ctx=with_ctx),
        grid=(b // PB, nt),
        in_specs=[
            pl.BlockSpec((PB, TM, QW), tok),
            pl.BlockSpec((PB, TM, UW), tok),
        ] + res_specs + [
            pl.BlockSpec((PB, 1, 1, 6 * D), lambda bi, i: (bi, sel(i), 0, 0)),
            pl.BlockSpec((D, D), const),
            pl.BlockSpec((1, D), const),
            pl.BlockSpec((D, 128), const),
            pl.BlockSpec((1, 128), const),
            pl.BlockSpec((TM, TM), const),
            pl.BlockSpec((128, 128), const),
        ],
        out_specs=[
            pl.BlockSpec((PB, TM, D), tok),
            pl.BlockSpec((PB, TM, D), tok),
            pl.BlockSpec((PB, TM, 128), tok),
            pl.BlockSpec((PB, 1, 1, 128), lambda bi, i: (bi, i, 0, 0)),
        ],
        out_shape=[
            jax.ShapeDtypeStruct((b, s_out, D), F32),
            jax.ShapeDtypeStruct((b, s_out, D), BF16),
            jax.ShapeDtypeStruct((b, s_out, 128), F32),
            jax.ShapeDtypeStruct((b, nt, 1, 128), F32),
        ],
        compiler_params=_params(("arbitrary", "arbitrary")),
        name="out_projection_router",
    )(a, bmix, *residual, modsel, w_out, ln_g, rw, rb, tri, upper)


def _pack_halves(x, exact=False):
    half = x.shape[1] // 2
    a, b = x[:, 0:half], x[:, half:]
    if not exact:
        a, b = a.astype(BF16).astype(F32), b.astype(BF16).astype(F32)
    return lax.bitcast_convert_type(a, U32) | (lax.bitcast_convert_type(b, U32) >> 16)


def _unpack_halves(w):
    a = lax.bitcast_convert_type(w & jnp.uint32(0xFFFF0000), F32)
    b = lax.bitcast_convert_type(w << 16, F32)
    return jnp.concatenate([a, b], axis=1).astype(BF16)


def _rows_copy(src_ref, src_row, dst_ref, dst_row, rows, sem):
    rows = pl.multiple_of(rows, GRP)
    return pltpu.make_async_copy(src_ref.at[pl.ds(pl.multiple_of(src_row, GRP), rows), :],
                                 dst_ref.at[pl.ds(pl.multiple_of(dst_row, GRP), rows), :], sem)


def _wait_rows(count, src_ref, dst_ref, sem):
    @pl.when(count > 0)
    def _():
        _rows_copy(src_ref, 0, dst_ref, 0, count * GRP, sem).wait()


def _start_runs(runs, tile, copy):
    dst_ref, off_ref, len_ref = runs
    for e in range(N_EXP):
        idx = tile * N_EXP + e
        rows = len_ref[idx]

        @pl.when(rows > 0)
        def _():
            copy(dst_ref[idx], off_ref[idx], rows).start(priority=1)


def _dispatch_kernel(dst_ref, off_ref, len_ref, ng_ref, zdst_ref, zlen_ref, h_ref, r_ref, xs_ref, loc, zbuf, sems,
                     zsem):
    runs = (dst_ref, off_ref, len_ref)
    i = pl.program_id(0)
    n = pl.num_programs(0)

    @pl.when(i == 0)
    def _():
        zbuf[...] = jnp.zeros(zbuf.shape, U32)
        for e in range(N_EXP):
            pl.when(zlen_ref[e] > 0)(lambda e=e: _rows_copy(zbuf, 0, xs_ref, zdst_ref[e], zlen_ref[e], zsem).start())
        for e in range(N_EXP):
            pl.when(zlen_ref[e] > 0)(lambda e=e: _rows_copy(zbuf, 0, xs_ref, zdst_ref[e], zlen_ref[e], zsem).wait())

    def run(slot):
        buf = loc.at[slot]
        sem = sems.at[slot]

        @pl.when(i >= 2)
        def _():
            _wait_rows(ng_ref[i - 2], buf, xs_ref, sem)

        rt = r_ref[...].T
        rows = lax.broadcasted_iota(jnp.int32, (LR, TM), 0).astype(F32)
        perm = jnp.where((rows == rt[4:5, :]) | (rows == rt[5:6, :]), 1.0, 0.0).astype(BF16)
        buf[...] = _pack_halves(_dot(perm, h_ref[...]), exact=True)

        _start_runs(runs, i, lambda dst, off, rows: _rows_copy(buf, off, xs_ref, dst, rows, sem))

        @pl.when(i == n - 1)
        def _():
            _wait_rows(ng_ref[i], buf, xs_ref, sem)

            @pl.when(i >= 1)
            def _():
                _wait_rows(ng_ref[i - 1], loc.at[1 - slot], xs_ref, sems.at[1 - slot])

    pl.when(i % 2 == 0)(lambda: run(0))
    pl.when(i % 2 == 1)(lambda: run(1))


def _dispatch(runs, ng, zdst, zlen, h2, rinfo, cap):
    t = h2.shape[0]
    tok = lambda i, *_: (i, 0)
    return pl.pallas_call(
        _dispatch_kernel,
        grid_spec=pltpu.PrefetchScalarGridSpec(
            num_scalar_prefetch=6,
            grid=(t // TM,),
            in_specs=[
                pl.BlockSpec((TM, D), tok),
                pl.BlockSpec((TM, 128), tok),
            ],
            out_specs=pl.BlockSpec(memory_space=pl.ANY),
            scratch_shapes=[
                pltpu.VMEM((2, LR, D // 2), U32),
                pltpu.VMEM((BLK, D // 2), U32),
                pltpu.SemaphoreType.DMA((2,)),
                pltpu.SemaphoreType.DMA,
            ],
        ),
        out_shape=jax.ShapeDtypeStruct((cap, D // 2), U32),
        compiler_params=_params(("arbitrary",)),
        name="moe_dispatch",
    )(*runs, ng, zdst, zlen, h2, rinfo)


def _expert_kernel(be_ref, nu_ref, nxt_ref, par_ref, x_ref, wg_ref, wu_ref, wd_ref, y_ref, wg_sc, wu_sc, wd_sc,
                   wg_in, wu_in, wd_in, sems, *, layer):
    i = pl.program_id(0)
    used = i < nu_ref[0]
    fresh = (i == 0) | (be_ref[i] != be_ref[jnp.maximum(i - 1, 0)])

    def fetch(e, slot):
        return [pltpu.make_async_copy(src.at[layer, e], dst.at[slot], sems.at[slot, k])
                for k, (src, dst) in enumerate(((wg_ref, wg_in), (wu_ref, wu_in), (wd_ref, wd_in)))]

    @pl.when(used & (i == 0))
    def _():
        for c in fetch(be_ref[0], 0):
            c.start()

    def load(slot):
        @pl.when(nxt_ref[i] >= 0)
        def _():
            for c in fetch(nxt_ref[i], 1 - slot):
                c.start()

        for c in fetch(be_ref[i], slot):
            c.wait()
        wg_sc[...] = wg_in[slot].astype(BF16)
        wu_sc[...] = wu_in[slot].astype(BF16)
        wd_sc[...] = wd_in[slot].astype(BF16)

    pl.when(used & fresh & (par_ref[i] == 0))(lambda: load(0))
    pl.when(used & fresh & (par_ref[i] == 1))(lambda: load(1))

    @pl.when(used)
    def _():
        xb = _unpack_halves(x_ref[...])
        g = _dot(xb, wg_sc[...])
        u = _dot(xb, wu_sc[...])
        hid = (g * _sigmoid(g)) * u
        y_ref[...] = _pack_halves(_dot(hid.astype(BF16), wd_sc[...]))

    @pl.when(jnp.logical_not(used))
    def _():
        y_ref[...] = jnp.zeros(y_ref.shape, U32)


def _experts(blk_e, n_used, xs, wg, wu, wd, layer):
    cap = xs.shape[0]
    n_blk = cap // BLK
    i32 = jnp.int32
    idx = jnp.arange(n_blk, dtype=i32)
    in_use = idx < n_used[0]
    later = (blk_e[None, :] > blk_e[:, None]) & in_use[None, :]
    nxt = jnp.min(jnp.where(later, blk_e[None, :], N_EXP), axis=1)
    nxt = jnp.where(nxt < N_EXP, nxt, -1).astype(i32)
    changes = jnp.concatenate([jnp.zeros((1,), i32), (blk_e[1:] != blk_e[:-1]).astype(i32)])
    parity = (jnp.cumsum(changes) % 2).astype(i32)
    row = lambda i, be, nu, nx, pa: (jnp.minimum(i, nu[0] - 1), 0)
    return pl.pallas_call(
        functools.partial(_expert_kernel, layer=layer),
        grid_spec=pltpu.PrefetchScalarGridSpec(
            num_scalar_prefetch=4,
            grid=(n_blk,),
            in_specs=[
                pl.BlockSpec((BLK, D // 2), row),
                pl.BlockSpec(memory_space=pl.ANY),
                pl.BlockSpec(memory_space=pl.ANY),
                pl.BlockSpec(memory_space=pl.ANY),
            ],
            out_specs=pl.BlockSpec((BLK, D // 2), lambda i, be, nu, nx, pa: (i, 0)),
            scratch_shapes=[
                pltpu.VMEM((D, D_EXP), BF16),
                pltpu.VMEM((D, D_EXP), BF16),
                pltpu.VMEM((D_EXP, D), BF16),
                pltpu.VMEM((2, D, D_EXP), F32),
                pltpu.VMEM((2, D, D_EXP), F32),
                pltpu.VMEM((2, D_EXP, D), F32),
                pltpu.SemaphoreType.DMA((2, 3)),
            ],
        ),
        out_shape=jax.ShapeDtypeStruct((cap, D // 2), U32),
        compiler_params=_params(("arbitrary",)),
        name="moe_experts",
    )(blk_e, n_used, nxt, parity, xs, wg, wu, wd)


def _combine_kernel(dst_ref, off_ref, len_ref, ng_ref, y_ref, x1_ref, r_ref, mod_ref, o_ref, loc, sems):
    runs = (dst_ref, off_ref, len_ref)
    i = pl.program_id(0)
    n = pl.num_programs(0)

    def fetch(tile, slot):
        _start_runs(runs, tile, lambda dst, off, rows: _rows_copy(y_ref, dst, loc.at[slot], off, rows, sems.at[slot]))

    @pl.when(i == 0)
    def _():
        loc[...] = jnp.zeros(loc.shape, U32)
        fetch(0, 0)

    def run(slot):
        @pl.when(i + 1 < n)
        def _():
            fetch(i + 1, 1 - slot)

        _wait_rows(ng_ref[i], y_ref, loc.at[slot], sems.at[slot])
        o_ref[...] = _unsort_add(loc[slot], r_ref[...], x1_ref[...], mod_ref[0, 0][:, 5 * D:6 * D])

    pl.when(i % 2 == 0)(lambda: run(0))
    pl.when(i % 2 == 1)(lambda: run(1))


def _combine(runs, ng, ys, x1, rinfo, modsel, *, tiles_per_batch, with_ctx):
    t = x1.shape[0]
    if with_ctx:
        msel = lambda i, *_: (i // tiles_per_batch, jnp.minimum(i % tiles_per_batch, 1), 0, 0)
    else:
        msel = lambda i, *_: (i // tiles_per_batch, 1, 0, 0)
    tok = lambda i, *_: (i, 0)
    return pl.pallas_call(
        _combine_kernel,
        grid_spec=pltpu.PrefetchScalarGridSpec(
            num_scalar_prefetch=4,
            grid=(t // TM,),
            in_specs=[
                pl.BlockSpec(memory_space=pl.ANY),
                pl.BlockSpec((TM, D), tok),
                pl.BlockSpec((TM, 128), tok),
                pl.BlockSpec((1, 1, 1, 6 * D), msel),
            ],
            out_specs=pl.BlockSpec((TM, D), tok),
            scratch_shapes=[pltpu.VMEM((2, LR, D // 2), U32), pltpu.SemaphoreType.DMA((2,))],
        ),
        out_shape=jax.ShapeDtypeStruct((t, D), F32),
        compiler_params=_params(("arbitrary",)),
        name="moe_combine",
    )(*runs, ng, ys, x1, rinfo, modsel)


def _unsort_add(ysw, info, x1, g2):
    col = lax.broadcasted_iota(jnp.int32, (TM, LR), 1).astype(F32)
    pick = jnp.concatenate([jnp.where(col == info[:, 4:5], 1.0, 0.0).astype(BF16),
                            jnp.where(col == info[:, 5:6], 1.0, 0.0).astype(BF16)], axis=0)
    y12 = _dot(pick, _unpack_halves(ysw))
    return x1 + g2 * (info[:, 2:3] * y12[0:TM] + info[:, 3:4] * y12[TM:2 * TM])


def _combine_proj_kernel(dst_ref, off_ref, len_ref, ng_ref, y_ref, x1_ref, r_ref, mod0_ref, mod_ref, lng_ref, w_ref,
                         cos_ref, sin_ref, qg_ref, kg_ref, bdq_ref, bdk_ref,
                         xc_ref, q_ref, k_ref, v_ref, u_ref, loc, sems, *, nt):
    runs = (dst_ref, off_ref, len_ref)
    bp = pl.program_id(0)
    i = pl.program_id(1)
    step = bp * nt + i
    slot = step % 2

    def fetch(bp_, i_, slot_):
        for bb in range(PB):
            tile = (bp_ * PB + bb) * nt + i_
            _start_runs(runs, tile, lambda dst, off, rows, bb=bb: _rows_copy(
                y_ref, dst, loc.at[slot_, bb], off, rows, sems.at[slot_, bb]))

    @pl.when(step == 0)
    def _():
        loc[...] = jnp.zeros(loc.shape, U32)
        fetch(0, 0, 0)

    @pl.when(step + 1 < pl.num_programs(0) * nt)
    def _():
        wrap = i + 1 == nt
        fetch(jnp.where(wrap, bp + 1, bp), jnp.where(wrap, 0, i + 1), 1 - slot)

    @pl.when(step >= 0)
    def _():
        for bb in range(PB):
            tile = (bp * PB + bb) * nt + i
            _wait_rows(ng_ref[tile], y_ref, loc.at[slot, bb], sems.at[slot, bb])
            xc_ref[bb] = _unsort_add(loc[slot, bb], r_ref[bb], x1_ref[bb], mod0_ref[bb, 0][:, 5 * D:6 * D])

    @pl.when(step >= -1)
    def _():
        h = jnp.concatenate([_modulated(xc_ref[bb], mod_ref[bb, 0], lng_ref[...], 0, 1) for bb in range(PB)], axis=0)
        px = _dot(h.astype(BF16), w_ref[...])
        _proj_epilogue(px, cos_ref, sin_ref, qg_ref, kg_ref, bdq_ref, bdk_ref, q_ref, k_ref, v_ref, u_ref, False)


def _combine_project(runs, ng, ys, x1, rinfo, modsel0, modsel, ln_g, w_in, cos, sin, q_g, k_g, bdq, bdk):
    b, s_tot, _ = x1.shape
    nt = s_tot // TM
    n_in = w_in.shape[1]
    const = lambda bi, i, *_: (0, 0)
    tok = lambda bi, i, *_: (bi, i, 0)
    msel = lambda bi, i, *_: (bi, jnp.minimum(i, 1), 0, 0)
    head = lambda bi, i, *_: (bi, 0, i, 0)
    return pl.pallas_call(
        functools.partial(_combine_proj_kernel, nt=nt),
        grid_spec=pltpu.PrefetchScalarGridSpec(
            num_scalar_prefetch=4,
            grid=(b // PB, nt),
            in_specs=[
                pl.BlockSpec(memory_space=pl.ANY),
                pl.BlockSpec((PB, TM, D), tok),
                pl.BlockSpec((PB, TM, 128), tok),
                pl.BlockSpec((PB, 1, 1, 6 * D), msel),
                pl.BlockSpec((PB, 1, 1, 6 * D), msel),
                pl.BlockSpec((1, D), const),
                pl.BlockSpec((D, n_in), const),
                pl.BlockSpec((TM, 128), lambda bi, i, *_: (i, 0)),
                pl.BlockSpec((TM, 128), lambda bi, i, *_: (i, 0)),
                pl.BlockSpec((1, QW), const),
                pl.BlockSpec((1, KW), const),
                pl.BlockSpec((QW, QW), const),
                pl.BlockSpec((KW, KW), const),
            ],
            out_specs=[
                pl.BlockSpec((PB, TM, D), tok),
                pl.BlockSpec((PB, NH, TM, HD), head),
                pl.BlockSpec((PB, NKV, TM, HD), head),
                pl.BlockSpec((PB, NKV, TM, 2 * HD), head),
                pl.BlockSpec((PB, TM, UW), tok),
            ],
            scratch_shapes=[pltpu.VMEM((2, PB, LR, D // 2), U32), pltpu.SemaphoreType.DMA((2, PB))],
        ),
        out_shape=[
            jax.ShapeDtypeStruct((b, s_tot, D), F32),
            jax.ShapeDtypeStruct((b, NH, s_tot, HD), BF16),
            jax.ShapeDtypeStruct((b, NKV, s_tot, HD), BF16),
            jax.ShapeDtypeStruct((b, NKV, s_tot, 2 * HD), BF16),
            jax.ShapeDtypeStruct((b, s_tot, UW), F32),
        ],
        compiler_params=_params(("arbitrary", "arbitrary")),
        name="moe_combine_in_projection",
    )(*runs, ng, ys, x1, rinfo, modsel0, modsel, ln_g, w_in, cos, sin, q_g, k_g, bdq, bdk)


def _moe(x1, h2, rinfo, g8, modsel, wg, wu, wd, layer, *, with_ctx, combine=True):
    b, s, _ = x1.shape
    t = b * s
    n_tiles = t // TM
    i32 = jnp.int32
    grp = g8.reshape(n_tiles, 128)[:, 0:N_EXP].astype(i32)
    n_blk = -(-(2 * t + n_tiles * N_EXP * (GRP - 1) + N_EXP * (BLK - 1)) // BLK)
    cap = n_blk * BLK
    goff_end = jnp.cumsum(grp, axis=1)
    ng = goff_end[:, N_EXP - 1].astype(i32)
    rows_e = GRP * jnp.sum(grp, axis=0)
    padded = (rows_e + BLK - 1) // BLK * BLK
    pend = jnp.cumsum(padded)
    pstart = pend - padded
    run_dst = (pstart[None, :] + GRP * (jnp.cumsum(grp, axis=0) - grp)).reshape(-1).astype(i32)
    run_off = (GRP * (goff_end - grp)).reshape(-1).astype(i32)
    run_len = (GRP * grp).reshape(-1).astype(i32)
    runs = (run_dst, run_off, run_len)
    zdst = (pstart + rows_e).astype(i32)
    zlen = (padded - rows_e).astype(i32)
    blk_e = jnp.minimum(jnp.sum((pend[None, :] <= (jnp.arange(n_blk, dtype=i32) * BLK)[:, None]).astype(i32), axis=1),
                        N_EXP - 1).astype(i32)
    n_used = (pend[N_EXP - 1:N_EXP] // BLK).astype(i32)
    info = rinfo.reshape(t, 128)
    xs = _dispatch(runs, ng, zdst, zlen, h2.reshape(t, D), info, cap)
    ys = _experts(blk_e, n_used, xs, wg, wu, wd, layer)
    if not combine:
        return runs, ng, ys
    out = _combine(runs, ng, ys, x1.reshape(t, D), info, modsel, tiles_per_batch=s // TM, with_ctx=with_ctx)
    return out.reshape(b, s, D)


def _rope_tables(n_ctx, s_lat):
    rows = s_lat // GRID_W
    row = jnp.repeat(jnp.arange(rows, dtype=F32), GRID_W)
    col = jnp.tile(jnp.arange(GRID_W, dtype=F32), rows)
    n_freq = HD // 4
    inv = ROPE_THETA ** (-jnp.arange(n_freq, dtype=F32) / n_freq)
    ang = jnp.concatenate([row[:, None] * inv, col[:, None] * inv], axis=-1)
    cos = jnp.concatenate([jnp.ones((n_ctx, HD // 2), F32), jnp.cos(ang)], axis=0)
    sin = jnp.concatenate([jnp.zeros((n_ctx, HD // 2), F32), jnp.sin(ang)], axis=0)
    cos = jnp.concatenate([cos, cos, cos, cos], axis=1)
    sin = jnp.concatenate([-sin, sin, -sin, sin], axis=1)
    return cos, sin


def _block_diag_ones(n):
    i = jnp.arange(n) // HD
    return (i[:, None] == i[None, :]).astype(BF16)


def kernel(x, c, ctx, c_ctx, mod_w, mod_b, ln1_g, ln2_g, w_in_ab, w_out_ab, q_norm_a, k_norm_a, conv_w, conv_b,
           conv_ln_g, conv_ln_b, w_in_cd, w_out_cd, q_norm_c, k_norm_c, sink_c, pool_w, pool_scale,
           rt_grp_w, rt_grp_b, rt_exp_w, rt_exp_b, ex_gate, ex_up, ex_down):
    b, s_lat, _ = x.shape
    n_ctx = ctx.shape[1]
    assert n_ctx == TM and s_lat % TM == 0 and b <= 8 and b % PB == 0

    cc = jnp.zeros((16, D), F32).at[0:b].set(c).at[8].set(c_ctx)
    mod = _modulation(cc, mod_w, mod_b)

    def modsel(l):
        return jnp.stack([jnp.broadcast_to(mod[l, 8], (b, 6 * D)), mod[l, 0:b]], axis=1).reshape(b, 2, 1, 6 * D)

    cos, sin = _rope_tables(n_ctx, s_lat)
    bdq = _block_diag_ones(QW)
    bdk = _block_diag_ones(KW)
    tri = (jnp.arange(TM)[:, None] > jnp.arange(TM)[None, :]).astype(BF16)
    upper = (jnp.arange(128)[:, None] < jnp.arange(128)[None, :]).astype(BF16)

    def router(l):
        rw = jnp.zeros((D, 128), F32).at[:, 0:N_EXP].set(rt_exp_w[l]).at[:, N_EXP:N_EXP + N_GROUPS].set(rt_grp_w[l])
        rb = jnp.zeros((1, 128), F32).at[0, 0:N_EXP].set(rt_exp_b[l]).at[0, N_EXP:N_EXP + N_GROUPS].set(rt_grp_b[l])
        return rw, rb

    def tile_gain(g, n):
        return jnp.tile(g, n).reshape(1, n * HD)

    s_tot = n_ctx + s_lat

    ms = modsel(0)
    q, k, v, glu = _project(ctx, x, 0, s_tot, ms, ln1_g[0:1], w_in_ab[0].astype(BF16), cos, sin,
                            tile_gain(q_norm_a[0], NH), tile_gain(k_norm_a[0], NKV), bdq, bdk, glu=True)
    att = _dense_attention(q, k, v, n_ctx)
    cv = _conv_module(glu, conv_w[0], conv_b[0:1], conv_ln_g[0:1], conv_ln_b[0:1], n_ctx)
    rw, rb = router(0)
    x1, h2, rinfo, g8 = _out_project(att, cv, (ctx, x), ms, w_out_ab[0].astype(BF16), ln2_g[0:1], rw, rb, tri, upper)
    runs, ng, ys = _moe(x1, h2, rinfo, g8, ms, ex_gate, ex_up, ex_down, 0, with_ctx=True, combine=False)

    ms0, ms = ms, modsel(1)
    xc, q, k, v, u = _combine_project(runs, ng, ys, x1, rinfo, ms0, ms, ln1_g[1:2], w_in_cd[0].astype(BF16), cos, sin,
                                      tile_gain(q_norm_c[0], NH), tile_gain(k_norm_c[0], NKV), bdq, bdk)
    att = _window_attention(sink_c[0], q, k, v, n_ctx)
    pm = _pool_mixer(u, pool_w[0].astype(BF16), pool_scale[0:1], n_ctx)
    rw, rb = router(1)
    x1, h2, rinfo, g8 = _out_project(att, pm, (xc,), ms, w_out_cd[0].astype(BF16), ln2_g[1:2], rw, rb, tri, upper)
    return _moe(x1, h2, rinfo, g8, ms, ex_gate, ex_up, ex_down, 1, with_ctx=False)
```
